```python
import math
import jax
import jax.numpy as jnp
from jax import lax
import numpy as np

D_MODEL = 1024
BATCH = 8
SEQ = 2048
DEPTH = 2

CTX_LEN = 256
GRID_W = 64
N_BRANCH = 4
BRANCH_W = D_MODEL // N_BRANCH

MLA_HEADS = BRANCH_W // 64
MLA_NOPE = 64
MLA_ROPE = 32
MLA_V = BRANCH_W // MLA_HEADS
MLA_Q_RANK = D_MODEL // 4
MLA_KV_RANK = D_MODEL // 8

NA_HEADS = BRANCH_W // 64
NA_DH = BRANCH_W // NA_HEADS
NA_WIN_R = 8
NA_WIN_C = 16

S5_GROUP = 16
S5_GROUPS = BRANCH_W // S5_GROUP
S5_STATE = 64

M2_HEADS = BRANCH_W // 64
M2_HEADDIM = BRANCH_W // M2_HEADS
M2_STATE = 128
M2_GROUPS = 2
M2_CONV = 4
M2_CHUNK = 128
M2_CONV_CH = BRANCH_W + 2 * M2_GROUPS * M2_STATE

N_EXPERTS = 16
EXPERT_FF = D_MODEL
CAPACITY_FACTOR = 2

Q_BLOCK = 128
ROPE_THETA = 10000.0
EPS = 1e-6

IN_SPLITS = (MLA_Q_RANK, MLA_KV_RANK, MLA_ROPE, 3 * BRANCH_W, BRANCH_W, BRANCH_W, M2_CONV_CH, 2 * M2_HEADS, N_BRANCH * D_MODEL)
N_IN = sum(IN_SPLITS)

kernel_name = 'hybrid_mla_na_s5_ssd_ecmoe_dit'


def rmsnorm(x, g):
    x32 = x.astype(jnp.float32)
    y = x32 * lax.rsqrt(jnp.mean(x32 * x32, axis=-1, keepdims=True) + EPS)
    return (y * g.astype(jnp.float32)).astype(x.dtype)


def modulate(h, shift, scale):
    return h * (1.0 + scale) + shift


def split_cols(p, sizes):
    return jnp.split(p, np.cumsum(sizes)[:-1].tolist(), axis=-1)


def rope_1d(x, pos):
    d = x.shape[-1]
    inv_freq = 1.0 / (ROPE_THETA ** (jnp.arange(0, d, 2, dtype=jnp.float32) / d))
    ang = pos.astype(jnp.float32)[:, None] * inv_freq[None, :]
    cos = jnp.concatenate([jnp.cos(ang), jnp.cos(ang)], axis=-1)
    sin = jnp.concatenate([jnp.sin(ang), jnp.sin(ang)], axis=-1)
    x32 = x.astype(jnp.float32)
    x1, x2 = jnp.split(x32, 2, axis=-1)
    rot = jnp.concatenate([-x2, x1], axis=-1)
    return (x32 * cos + rot * sin).astype(x.dtype)


def rope_2d(x, row, col):
    half = x.shape[-1] // 2
    return jnp.concatenate([rope_1d(x[..., :half], row), rope_1d(x[..., half:], col)], axis=-1)


def split_heads(t, n_heads):
    b, s, _ = t.shape
    return t.reshape(b, s, n_heads, -1).transpose(0, 2, 1, 3)


def merge_heads(t):
    b, h, s, d = t.shape
    return t.transpose(0, 2, 1, 3).reshape(b, s, h * d)


def block_attention(q, k, v, scale):
    b, h, t, d = q.shape
    nb = t // Q_BLOCK
    q_blocks = q.reshape(b, h, nb, Q_BLOCK, d).transpose(2, 0, 1, 3, 4)

    def one_block(qb):
        s = jnp.einsum('bhqd,bhkd->bhqk', qb, k).astype(jnp.float32) * scale
        p = jax.nn.softmax(s, axis=-1).astype(v.dtype)
        return jnp.einsum('bhqk,bhkd->bhqd', p, v)

    o = lax.map(one_block, q_blocks)
    return o.transpose(1, 2, 0, 3, 4).reshape(b, h, t, v.shape[-1])


def mla_mixer(cq_lat, ckv_lat, kr_lat, cq_ctx, ckv_ctx, kr_ctx, g_cq, g_ckv, w_uq, w_ukv, row, col, ctx_out):
    def keys_values(ckv, kr):
        b, t, _ = ckv.shape
        kvh = (rmsnorm(ckv, g_ckv) @ w_ukv).reshape(b, t, MLA_HEADS, MLA_NOPE + MLA_V)
        k_rope = jnp.broadcast_to(kr[:, :, None, :], (b, t, MLA_HEADS, MLA_ROPE))
        k = jnp.concatenate([kvh[..., :MLA_NOPE], k_rope], axis=-1)
        return k.transpose(0, 2, 1, 3), kvh[..., MLA_NOPE:].transpose(0, 2, 1, 3)

    def queries(cq):
        return split_heads(rmsnorm(cq, g_cq) @ w_uq, MLA_HEADS)

    scale = (MLA_NOPE + MLA_ROPE) ** -0.5
    k_lat, v_lat = keys_values(ckv_lat, rope_2d(kr_lat, row, col))
    k_ctx, v_ctx = keys_values(ckv_ctx, kr_ctx)
    q_lat = queries(cq_lat)
    q_lat = jnp.concatenate([q_lat[..., :MLA_NOPE], rope_2d(q_lat[..., MLA_NOPE:], row, col)], axis=-1)
    o_lat = block_attention(q_lat, jnp.concatenate([k_lat, k_ctx], axis=2),
                            jnp.concatenate([v_lat, v_ctx], axis=2), scale)
    o_ctx = merge_heads(block_attention(queries(cq_ctx), k_ctx, v_ctx, scale)) if ctx_out else None
    return merge_heads(o_lat), o_ctx


def na_mixer(qkv_lat, qkv_ctx, rpb, ctx_out):
    b, s, _ = qkv_lat.shape
    rows = s // GRID_W
    wr = min(NA_WIN_R, rows)
    scale = NA_DH ** -0.5
    q, k, v = [split_heads(t, NA_HEADS) for t in jnp.split(qkv_lat, 3, axis=-1)]
    qc, kc, vc = [split_heads(t, NA_HEADS) for t in jnp.split(qkv_ctx, 3, axis=-1)]

    def to_grid(t):
        return t.reshape(b, NA_HEADS, rows, GRID_W, NA_DH)

    kg, vg = to_grid(k), to_grid(v)
    q_rows = to_grid(q).transpose(2, 0, 1, 3, 4)
    row_start = jnp.clip(jnp.arange(rows) - wr // 2, 0, rows - wr)
    qcol = jnp.arange(GRID_W)
    col_idx = jnp.clip(qcol - NA_WIN_C // 2, 0, GRID_W - NA_WIN_C)[:, None] + jnp.arange(NA_WIN_C)[None, :]
    col_off = col_idx - qcol[:, None] + (NA_WIN_C - 1)
    rpb32 = rpb.astype(jnp.float32)
    n_win = wr * NA_WIN_C

    def one_row(args):
        r, qr = args
        rs = row_start[r]
        kw = lax.dynamic_slice_in_dim(kg, rs, wr, axis=2)[:, :, :, col_idx]
        vw = lax.dynamic_slice_in_dim(vg, rs, wr, axis=2)[:, :, :, col_idx]
        row_off = rs + jnp.arange(wr) - r + (NA_WIN_R - 1)
        bias = rpb32[:, row_off[None, :, None], col_off[:, None, :]]
        s_win = jnp.einsum('bhqd,bhrqcd->bhqrc', qr, kw).astype(jnp.float32) * scale + bias[None]
        s_ctx = jnp.einsum('bhqd,bhkd->bhqk', qr, kc).astype(jnp.float32) * scale
        s_all = jnp.concatenate([s_win.reshape(b, NA_HEADS, GRID_W, n_win), s_ctx], axis=-1)
        p = jax.nn.softmax(s_all, axis=-1).astype(vg.dtype)
        p_win = p[..., :n_win].reshape(b, NA_HEADS, GRID_W, wr, NA_WIN_C)
        return (jnp.einsum('bhqrc,bhrqcd->bhqd', p_win, vw)
                + jnp.einsum('bhqk,bhkd->bhqd', p[..., n_win:], vc))

    o = lax.map(one_row, (jnp.arange(rows), q_rows))
    o_lat = o.transpose(1, 0, 3, 2, 4).reshape(b, s, NA_HEADS * NA_DH)
    o_ctx = merge_heads(block_attention(qc, kc, vc, scale)) if ctx_out else None
    return o_lat, o_ctx


def _linear_combine(e1, e2):
    a1, b1 = e1
    a2, b2 = e2
    return a1 * a2, a2 * b1 + b2


def s5_scan(abar, bbar, u, h0, reverse):
    bu = jnp.einsum('gpc,btgc->btgp', bbar, u.astype(jnp.complex64))
    a = jnp.broadcast_to(abar, bu.shape)
    a_cum, h = lax.associative_scan(_linear_combine, (a, bu), reverse=reverse, axis=1)
    return h + a_cum * h0[:, None]


def s5_mixer(u_lat, u_ctx, a_re, a_im, log_step, b_re, b_im, c_re, c_im, d_skip, w_glu, b_glu, ctx_out):
    f32 = jnp.float32

    def groups(u):
        return u.reshape(u.shape[0], u.shape[1], S5_GROUPS, S5_GROUP)

    def readout(cm, h):
        y = jnp.real(jnp.einsum('gcp,btgp->btgc', cm, h))
        return y.reshape(h.shape[0], h.shape[1], BRANCH_W)

    def glu(y):
        z = jax.nn.gelu(y)
        return z * jax.nn.sigmoid(z @ w_glu + b_glu)

    ul, uc = groups(u_lat), groups(u_ctx)
    d32 = d_skip.astype(f32)
    y_lat = d32 * u_lat.astype(f32)
    y_ctx = d32 * u_ctx.astype(f32) if ctx_out else None
    h_zero = jnp.zeros((u_ctx.shape[0], S5_GROUPS, S5_STATE), jnp.complex64)
    for d, rev in ((0, False), (1, True)):
        a = lax.complex(a_re[d].astype(f32), a_im[d].astype(f32))
        abar = jnp.exp(jnp.exp(log_step[d].astype(f32))[:, None] * a)
        bbar = ((abar - 1.0) / a)[:, :, None] * lax.complex(b_re[d].astype(f32), b_im[d].astype(f32))
        cm = lax.complex(c_re[d].astype(f32), c_im[d].astype(f32))
        h_ctx = s5_scan(abar, bbar, uc, h_zero, rev)
        h_lat = s5_scan(abar, bbar, ul, h_ctx[:, 0] if rev else h_ctx[:, -1], rev)
        y_lat = y_lat + readout(cm, h_lat)
        if ctx_out:
            y_ctx = y_ctx + readout(cm, h_ctx)
    o_ctx = glu(y_ctx).astype(u_ctx.dtype) if ctx_out else None
    return glu(y_lat).astype(u_lat.dtype), o_ctx


def dw_conv_centred(x, w, b):
    k = w.shape[0]
    y = lax.conv_general_dilated(x, w[:, None, :].astype(x.dtype), window_strides=(1,),
                                 padding=[(k // 2, k - 1 - k // 2)],
                                 dimension_numbers=('NWC', 'WIO', 'NWC'),
                                 feature_group_count=x.shape[-1])
    return y + b


def ssd_chunked(x, dt, a, bm, cm, h0):
    b, t, h, p = x.shape
    nc = t // M2_CHUNK
    rep = h // bm.shape[2]
    bh = jnp.repeat(bm, rep, axis=2).reshape(b, nc, M2_CHUNK, h, -1)
    ch = jnp.repeat(cm, rep, axis=2).reshape(b, nc, M2_CHUNK, h, -1)
    xdt = (x * dt[..., None]).reshape(b, nc, M2_CHUNK, h, p)
    a_cum = jnp.cumsum((dt * a).reshape(b, nc, M2_CHUNK, h), axis=2)
    causal = jnp.tril(jnp.ones((M2_CHUNK, M2_CHUNK), dtype=bool))[None, None, :, :, None]
    seg = a_cum[:, :, :, None, :] - a_cum[:, :, None, :, :]
    decay = jnp.exp(jnp.where(causal, seg, -jnp.inf))
    y_diag = jnp.einsum('bclsh,bcshp->bclhp', jnp.einsum('bclhn,bcshn->bclsh', ch, bh) * decay, xdt)
    states = jnp.einsum('bcsh,bcshn,bcshp->bchpn', jnp.exp(a_cum[:, :, -1:, :] - a_cum), bh, xdt)
    chunk_decay = jnp.exp(a_cum[:, :, -1, :])

    def step(state, inp):
        s_c, d_c = inp
        return state * d_c[:, :, None, None] + s_c, state

    h_last, h_in = lax.scan(step, h0, (jnp.moveaxis(states, 1, 0), jnp.moveaxis(chunk_decay, 1, 0)))
    y_off = jnp.einsum('bclhn,cbhpn,bclh->bclhp', ch, h_in, jnp.exp(a_cum))
    return (y_diag + y_off).reshape(b, t, h, p), h_last


def ssd_final_state(x, dt, a, bm):
    rep = x.shape[2] // bm.shape[2]
    a_cum = jnp.cumsum(dt * a, axis=1)
    w = jnp.exp(a_cum[:, -1:, :] - a_cum) * dt
    return jnp.einsum('bth,bthn,bthp->bhpn', w, jnp.repeat(bm, rep, axis=2), x)


def _flip(t, rev):
    return jnp.flip(t, axis=1) if rev else t


def m2_mixer(z_lat, xbc_lat, dt_lat, z_ctx, xbc_ctx, dt_ctx, conv_w, conv_b, a_log, dt_bias, d_skip, g_norm, ctx_out):
    f32 = jnp.float32

    def prep(xbc, dt_raw):
        b, t, _ = xbc.shape
        xbc = jax.nn.silu(dw_conv_centred(xbc, conv_w, conv_b)).astype(f32)
        xs, bm, cm = split_cols(xbc, (BRANCH_W, M2_GROUPS * M2_STATE, M2_GROUPS * M2_STATE))
        dt = jax.nn.softplus(dt_raw.astype(f32).reshape(b, t, 2, M2_HEADS) + dt_bias.astype(f32))
        return (xs.reshape(b, t, M2_HEADS, M2_HEADDIM), bm.reshape(b, t, M2_GROUPS, M2_STATE),
                cm.reshape(b, t, M2_GROUPS, M2_STATE), dt)

    def gated_norm(y, z):
        b, t = z.shape[:2]
        return rmsnorm(y.reshape(b, t, BRANCH_W) * jax.nn.silu(z.astype(f32)), g_norm).astype(z.dtype)

    xl, bl, cl, dtl = prep(xbc_lat, dt_lat)
    xc, bc, cc, dtc = prep(xbc_ctx, dt_ctx)
    d32 = d_skip.astype(f32)[:, None]
    y_lat = d32 * xl
    y_ctx = d32 * xc if ctx_out else None
    h_zero = jnp.zeros((xc.shape[0], M2_HEADS, M2_HEADDIM, M2_STATE), f32)
    for d in (0, 1):
        rev = d == 1
        a = -jnp.exp(a_log[d].astype(f32))
        if ctx_out:
            yc, h_ctx = ssd_chunked(_flip(xc, rev), _flip(dtc[:, :, d], rev), a, _flip(bc, rev), _flip(cc, rev), h_zero)
            y_ctx = y_ctx + _flip(yc, rev)
        else:
            h_ctx = ssd_final_state(_flip(xc, rev), _flip(dtc[:, :, d], rev), a, _flip(bc, rev))
        yl, _ = ssd_chunked(_flip(xl, rev), _flip(dtl[:, :, d], rev), a, _flip(bl, rev), _flip(cl, rev), h_ctx)
        y_lat = y_lat + _flip(yl, rev)
    o_ctx = gated_norm(y_ctx, z_ctx) if ctx_out else None
    return gated_norm(y_lat, z_lat), o_ctx


def token_mixer(h_lat, h_ctx, w_in, mla_g_cq, mla_g_ckv, mla_w_uq, mla_w_ukv, na_rpb,
                s5_a_re, s5_a_im, s5_log_step, s5_b_re, s5_b_im, s5_c_re, s5_c_im, s5_d, s5_w_glu, s5_b_glu,
                m2_conv_w, m2_conv_b, m2_a_log, m2_dt_bias, m2_d, m2_g_norm, w_branch, w_out, ctx_out):
    s = h_lat.shape[1]
    pos = jnp.arange(s)
    row, col = pos // GRID_W, pos % GRID_W
    pl = split_cols(h_lat @ w_in, IN_SPLITS)
    if ctx_out:
        pc = split_cols(h_ctx @ w_in, IN_SPLITS)
    else:
        pc = split_cols(h_ctx @ w_in[:, :N_IN - N_BRANCH * D_MODEL], IN_SPLITS[:-1])
    o_mla, oc_mla = mla_mixer(pl[0], pl[1], pl[2], pc[0], pc[1], pc[2], mla_g_cq, mla_g_ckv,
                              mla_w_uq, mla_w_ukv, row, col, ctx_out)
    o_na, oc_na = na_mixer(pl[3], pc[3], na_rpb, ctx_out)
    o_s5, oc_s5 = s5_mixer(pl[4], pc[4], s5_a_re, s5_a_im, s5_log_step, s5_b_re, s5_b_im,
                           s5_c_re, s5_c_im, s5_d, s5_w_glu, s5_b_glu, ctx_out)
    o_m2, oc_m2 = m2_mixer(pl[5], pl[6], pl[7], pc[5], pc[6], pc[7], m2_conv_w, m2_conv_b,
                           m2_a_log, m2_dt_bias, m2_d, m2_g_norm, ctx_out)

    def merge(outs, gate_logits):
        gl = gate_logits.reshape(gate_logits.shape[:-1] + (N_BRANCH, D_MODEL))
        y = jax.nn.sigmoid(gl[..., 0, :]) * (outs[0] @ w_branch[0])
        for j in range(1, N_BRANCH):
            y = y + jax.nn.sigmoid(gl[..., j, :]) * (outs[j] @ w_branch[j])
        return y @ w_out

    y_lat = merge((o_mla, o_na, o_s5, o_m2), pl[8])
    y_ctx = merge((oc_mla, oc_na, oc_s5, oc_m2), pc[8]) if ctx_out else None
    return y_lat, y_ctx


def expert_choice_ffn(h, w_router, w_gate, w_up, w_down):
    b, t, _ = h.shape
    cap = CAPACITY_FACTOR * t // N_EXPERTS
    aff = jax.nn.softmax(jnp.einsum('btd,de->bte', h, w_router).astype(jnp.float32), axis=-1)
    g, idx = lax.top_k(jnp.swapaxes(aff, 1, 2), cap)
    bidx = jnp.arange(b)[:, None, None]
    xs = h[bidx, idx]
    act = jax.nn.silu(jnp.einsum('becd,edf->becf', xs, w_gate)) * jnp.einsum('becd,edf->becf', xs, w_up)
    ys = jnp.einsum('becf,efd->becd', act, w_down) * g[..., None].astype(h.dtype)
    return jnp.zeros_like(h).at[bidx, idx].add(ys)


def setup_inputs(seed: int = 0) -> dict:
    key = jax.random.key(seed)
    keys = iter(jax.random.split(key, 48))

    def normal(shape, std):
        return jax.random.normal(next(keys), shape, jnp.float32) * std

    def gain(shape):
        return 1.0 + normal(shape, 0.02)

    def log_uniform(shape, lo, hi):
        return jax.random.uniform(next(keys), shape, jnp.float32, minval=math.log(lo), maxval=math.log(hi))

    L = DEPTH
    W = BRANCH_W
    dt0 = jnp.exp(log_uniform((L, 2, M2_HEADS), 1e-3, 1e-1))
    return {
        'x': normal((BATCH, SEQ, D_MODEL), 1.0),
        'c': normal((BATCH, D_MODEL), 1.0),
        'ctx': normal((BATCH, CTX_LEN, D_MODEL), 1.0),
        'c_ctx': normal((D_MODEL,), 1.0),
        'w_ada': normal((L, D_MODEL, 6 * D_MODEL), 0.2 * D_MODEL ** -0.5),
        'b_ada': normal((L, 6 * D_MODEL), 0.02),
        'g_pre_mix': gain((L, D_MODEL)),
        'g_post_mix': gain((L, D_MODEL)),
        'g_pre_ffn': gain((L, D_MODEL)),
        'g_post_ffn': gain((L, D_MODEL)),
        'w_in': normal((L, D_MODEL, N_IN), D_MODEL ** -0.5),
        'mla_g_cq': gain((L, MLA_Q_RANK)),
        'mla_g_ckv': gain((L, MLA_KV_RANK)),
        'mla_w_uq': normal((L, MLA_Q_RANK, MLA_HEADS * (MLA_NOPE + MLA_ROPE)), MLA_Q_RANK ** -0.5),
        'mla_w_ukv': normal((L, MLA_KV_RANK, MLA_HEADS * (MLA_NOPE + MLA_V)), MLA_KV_RANK ** -0.5),
        'na_rpb': normal((L, NA_HEADS, 2 * NA_WIN_R - 1, 2 * NA_WIN_C - 1), 0.1),
        's5_a_re': -0.5 + normal((L, 2, S5_GROUPS, S5_STATE), 0.01),
        's5_a_im': jnp.pi * jnp.arange(S5_STATE, dtype=jnp.float32) + normal((L, 2, S5_GROUPS, S5_STATE), 0.01),
        's5_log_step': log_uniform((L, 2, S5_GROUPS), 1e-3, 1e-1),
        's5_b_re': normal((L, 2, S5_GROUPS, S5_STATE, S5_GROUP), (2 * S5_GROUP) ** -0.5),
        's5_b_im': normal((L, 2, S5_GROUPS, S5_STATE, S5_GROUP), (2 * S5_GROUP) ** -0.5),
        's5_c_re': normal((L, 2, S5_GROUPS, S5_GROUP, S5_STATE), (2 * S5_STATE) ** -0.5),
        's5_c_im': normal((L, 2, S5_GROUPS, S5_GROUP, S5_STATE), (2 * S5_STATE) ** -0.5),
        's5_d': normal((L, W), 1.0),
        's5_w_glu': normal((L, W, W), W ** -0.5),
        's5_b_glu': normal((L, W), 0.02),
        'm2_conv_w': normal((L, M2_CONV, M2_CONV_CH), M2_CONV ** -0.5),
        'm2_conv_b': normal((L, M2_CONV_CH), 0.02),
        'm2_a_log': jnp.log(jax.random.uniform(next(keys), (L, 2, M2_HEADS), jnp.float32, 1.0, 16.0)),
        'm2_dt_bias': dt0 + jnp.log(-jnp.expm1(-dt0)),
        'm2_d': 1.0 + normal((L, M2_HEADS), 0.1),
        'm2_g_norm': gain((L, W)),
        'w_branch': normal((L, N_BRANCH, W, D_MODEL), W ** -0.5),
        'w_out': normal((L, D_MODEL, D_MODEL), D_MODEL ** -0.5),
        'w_router': normal((L, D_MODEL, N_EXPERTS), D_MODEL ** -0.5),
        'w_gate': normal((L, N_EXPERTS, D_MODEL, EXPERT_FF), D_MODEL ** -0.5),
        'w_up': normal((L, N_EXPERTS, D_MODEL, EXPERT_FF), D_MODEL ** -0.5),
        'w_down': normal((L, N_EXPERTS, EXPERT_FF, D_MODEL), EXPERT_FF ** -0.5),
    }


def reference(x, c, ctx, c_ctx, w_ada, b_ada, g_pre_mix, g_post_mix, g_pre_ffn, g_post_ffn, w_in,
              mla_g_cq, mla_g_ckv, mla_w_uq, mla_w_ukv, na_rpb,
              s5_a_re, s5_a_im, s5_log_step, s5_b_re, s5_b_im, s5_c_re, s5_c_im, s5_d, s5_w_glu, s5_b_glu,
              m2_conv_w, m2_conv_b, m2_a_log, m2_dt_bias, m2_d, m2_g_norm,
              w_branch, w_out, w_router, w_gate, w_up, w_down):
    xc = ctx
    silu_c = jax.nn.silu(c)
    silu_cc = jax.nn.silu(c_ctx)
    for l in range(DEPTH):
        ctx_out = l < DEPTH - 1
        mod = jnp.split(silu_c @ w_ada[l] + b_ada[l], 6, axis=-1)
        shift1, scale1, gate1, shift2, scale2, gate2 = [m[:, None, :] for m in mod]
        cshift1, cscale1, cgate1, cshift2, cscale2, cgate2 = jnp.split(silu_cc @ w_ada[l] + b_ada[l], 6)

        h = modulate(rmsnorm(x, g_pre_mix[l]), shift1, scale1)
        hc = modulate(rmsnorm(xc, g_pre_mix[l]), cshift1, cscale1)
        y, yc = token_mixer(h, hc, w_in[l], mla_g_cq[l], mla_g_ckv[l], mla_w_uq[l], mla_w_ukv[l], na_rpb[l],
                            s5_a_re[l], s5_a_im[l], s5_log_step[l], s5_b_re[l], s5_b_im[l], s5_c_re[l],
                            s5_c_im[l], s5_d[l], s5_w_glu[l], s5_b_glu[l], m2_conv_w[l], m2_conv_b[l],
                            m2_a_log[l], m2_dt_bias[l], m2_d[l], m2_g_norm[l], w_branch[l], w_out[l], ctx_out)
        x = x + gate1 * rmsnorm(y, g_post_mix[l])
        h = modulate(rmsnorm(x, g_pre_ffn[l]), shift2, scale2)
        x = x + gate2 * rmsnorm(expert_choice_ffn(h, w_router[l], w_gate[l], w_up[l], w_down[l]), g_post_ffn[l])

        if ctx_out:
            xc = xc + cgate1 * rmsnorm(yc, g_post_mix[l])
            hc = modulate(rmsnorm(xc, g_pre_ffn[l]), cshift2, cscale2)
            xc = xc + cgate2 * rmsnorm(expert_choice_ffn(hc, w_router[l], w_gate[l], w_up[l], w_down[l]), g_post_ffn[l])
    return x
```

```python
import functools
import math

import numpy as np
import jax
import jax.numpy as jnp
from jax import lax
from jax.experimental import pallas as pl
from jax.experimental.pallas import tpu as pltpu

F32 = jnp.float32
BF16 = jnp.bfloat16

D = 1024
T = 2048
LC = 256
N = T + LC
GRID_W = 64
ROW_BLK = 256
NBLK = N // ROW_BLK
LAT_BLKS = T // ROW_BLK
EPS = 1e-6
N_EXPERTS = 16
CAP_LAT = 2 * T // N_EXPERTS
CAP_CTX = 2 * LC // N_EXPERTS
ROPE_THETA = 10000.0
NEG = -1e30

MLA_W = 512
NA_W = 768
S5_W = 256
M2_W = 1152
GATE_W = 4096

S5_CHUNK = 64
SSD_CHUNK = 128

VMEM_LIMIT = 56 * 1024 * 1024


def _cp(*sem):
    return pltpu.CompilerParams(dimension_semantics=sem, vmem_limit_bytes=VMEM_LIMIT)


def _dot(a, b):
    return jnp.dot(a.astype(BF16), b.astype(BF16), preferred_element_type=F32)


def _dot_nt(a, b):
    return lax.dot_general(a.astype(BF16), b.astype(BF16), (((1,), (1,)), ((), ())),
                           preferred_element_type=F32)


def _split3(a):
    hi = a.astype(BF16)
    r = a - hi.astype(F32)
    mid = r.astype(BF16)
    lo = (r - mid.astype(F32)).astype(BF16)
    return hi, mid, lo


def _dot_hi(a, b):
    ah, am, _ = _split3(a)
    bh, bm, _ = _split3(b)
    f = lambda x, y: jnp.dot(x, y, preferred_element_type=F32)
    return f(ah, bh) + (f(ah, bm) + f(am, bh))


def _dot_hi_nt(a, b):
    ah, am, _ = _split3(a)
    bh, bm, _ = _split3(b)
    f = lambda x, y: lax.dot_general(x, y, (((1,), (1,)), ((), ())), preferred_element_type=F32)
    return f(ah, bh) + (f(ah, bm) + f(am, bh))


def _dot_exact_lhs(m_bf16, a):
    h, m, l = _split3(a)
    f = lambda y: jnp.dot(m_bf16, y, preferred_element_type=F32)
    return f(h) + (f(m) + f(l))


def _dot_exact_rhs(a, m_bf16):
    h, m, l = _split3(a)
    f = lambda y: jnp.dot(y, m_bf16, preferred_element_type=F32)
    return f(h) + (f(m) + f(l))


def _sigmoid(x):
    return 1.0 / (1.0 + jnp.exp(-x))


def _silu(x):
    return x * _sigmoid(x)


def _rms(x, g):
    return x * lax.rsqrt(jnp.mean(x * x, axis=-1, keepdims=True) + EPS) * g


def _mod_row(nb):
    def f(i, nbatch):
        return jnp.where(i % nb == LAT_BLKS, nbatch, i // nb)
    return f


def _flat_blk(i, nb):
    return (i // nb) * NBLK + i % nb


def _ada_kernel(c_ref, w_ref, b_ref, o_ref):
    c = c_ref[...]
    o_ref[...] = _dot_hi(_silu(c), w_ref[0]) + b_ref[0]


def _ada(cvec, w_ada, b_ada3, l):
    rows = cvec.shape[0]
    return pl.pallas_call(
        _ada_kernel,
        grid=(6,),
        in_specs=[pl.BlockSpec((rows, D), lambda k: (0, 0)),
                  pl.BlockSpec((1, D, D), lambda k: (l, 0, k)),
                  pl.BlockSpec((1, 1, D), lambda k: (l, 0, k))],
        out_specs=pl.BlockSpec((rows, D), lambda k: (0, k)),
        out_shape=jax.ShapeDtypeStruct((rows, 6 * D), F32),
        compiler_params=_cp("arbitrary"),
        name="ada",
    )(cvec, w_ada, b_ada3)


def _inproj_kernel(x_ref, g_ref, sh_ref, sc_ref, wm, wn, ws, w2, wg, om, on, os_, o2, og):
    h = (_rms(x_ref[...], g_ref[...]) * (1.0 + sc_ref[0]) + sh_ref[0]).astype(BF16)
    om[...] = jnp.dot(h, wm[...], preferred_element_type=F32)
    on[...] = jnp.dot(h, wn[...], preferred_element_type=F32).astype(BF16)
    os_[...] = jnp.dot(h, ws[...], preferred_element_type=F32)
    o2[...] = jnp.dot(h, w2[...], preferred_element_type=F32)
    og[...] = jnp.dot(h, wg[...], preferred_element_type=F32)


def _inproj(x2d, g, mod3, ws, nbatch):
    rows = x2d.shape[0]
    nblk = rows // ROW_BLK
    mrow = _mod_row(NBLK)
    full = lambda w: pl.BlockSpec(w.shape, lambda i: (0, 0))
    widths = (MLA_W, NA_W, S5_W, M2_W, GATE_W)
    dts = (F32, BF16, F32, F32, F32)
    return pl.pallas_call(
        _inproj_kernel,
        grid=(nblk,),
        in_specs=[pl.BlockSpec((ROW_BLK, D), lambda i: (i, 0)),
                  pl.BlockSpec((1, D), lambda i: (0, 0)),
                  pl.BlockSpec((1, 1, D), lambda i: (mrow(i, nbatch), 0, 0)),
                  pl.BlockSpec((1, 1, D), lambda i: (mrow(i, nbatch), 0, 1))]
                 + [full(w) for w in ws],
        out_specs=[pl.BlockSpec((ROW_BLK, w), lambda i: (i, 0)) for w in widths],
        out_shape=[jax.ShapeDtypeStruct((rows, w), dt) for w, dt in zip(widths, dts)],
        compiler_params=_cp("arbitrary"),
        name="inproj",
    )(x2d, g, mod3, mod3, *ws)


def _mla_prep_kernel(p_ref, gq_ref, gkv_ref, wq_ref, wqr_ref, wk_ref, wv_ref, e_ref,
                     cos_ref, sin_ref, ck_ref, sk_ref, q_out, k_out, v_out):
    p = p_ref[...]
    cqn = _rms(p[:, :256], gq_ref[...]).astype(BF16)
    q = jnp.dot(cqn, wq_ref[...], preferred_element_type=F32)
    qr = jnp.dot(cqn, wqr_ref[...], preferred_element_type=F32)
    q_out[...] = (q * cos_ref[...] + qr * sin_ref[...]).astype(BF16)
    ckvn = _rms(p[:, 256:384], gkv_ref[...]).astype(BF16)
    kro = (p[:, 384:416] * ck_ref[...] + p[:, 416:448] * sk_ref[...]).astype(BF16)
    k = jnp.dot(ckvn, wk_ref[...], preferred_element_type=F32) + jnp.dot(kro, e_ref[...], preferred_element_type=F32)
    k_out[...] = k.astype(BF16)
    v_out[...] = jnp.dot(ckvn, wv_ref[...], preferred_element_type=F32).astype(BF16)


def _mla_prep(p_mla, gq, gkv, wq, wqr, wk, wv, e, cos_q, sin_q, cos_k, sin_k):
    rows = p_mla.shape[0]
    full = lambda w: pl.BlockSpec(w.shape, lambda i: (0, 0))
    tab = lambda w: pl.BlockSpec((ROW_BLK, w), lambda i: (i % NBLK, 0))
    return pl.pallas_call(
        _mla_prep_kernel,
        grid=(rows // ROW_BLK,),
        in_specs=[pl.BlockSpec((ROW_BLK, MLA_W), lambda i: (i, 0)),
                  full(gq), full(gkv), full(wq), full(wqr), full(wk), full(wv), full(e),
                  tab(512), tab(512), tab(32), tab(32)],
        out_specs=[pl.BlockSpec((ROW_BLK, 512), lambda i: (i, 0)),
                   pl.BlockSpec((ROW_BLK, 512), lambda i: (i, 0)),
                   pl.BlockSpec((ROW_BLK, 256), lambda i: (i, 0))],
        out_shape=[jax.ShapeDtypeStruct((rows, 512), BF16),
                   jax.ShapeDtypeStruct((rows, 512), BF16),
                   jax.ShapeDtypeStruct((rows, 256), BF16)],
        compiler_params=_cp("arbitrary"),
        name="mla_prep",
    )(p_mla, gq, gkv, wq, wqr, wk, wv, e, cos_q, sin_q, cos_k, sin_k)


def _mla_attn_kernel(q_ref, k_ref, v_ref, o_ref, *, has_ctx):
    def run(k0, nk):
        for h in range(4):
            qh = q_ref[0, :, 128 * h:128 * h + 128]
            kh = k_ref[0, k0:k0 + nk, 128 * h:128 * h + 128]
            s = _dot_nt(qh, kh)
            m = jnp.max(s, axis=-1, keepdims=True)
            p = jnp.exp(s - m)
            l = jnp.sum(p, axis=-1, keepdims=True)
            o = _dot(p, v_ref[0, k0:k0 + nk, 64 * h:64 * h + 64]) / l
            o_ref[0, :, 64 * h:64 * h + 64] = o.astype(o_ref.dtype)

    if has_ctx:
        j = pl.program_id(1)

        @pl.when(j < LAT_BLKS)
        def _():
            run(0, N)

        @pl.when(j == LAT_BLKS)
        def _():
            run(T, LC)
    else:
        run(0, N)


def _mla_attn(q, k, v, has_ctx):
    nbatch = q.shape[0]
    nb = NBLK if has_ctx else LAT_BLKS
    return pl.pallas_call(
        functools.partial(_mla_attn_kernel, has_ctx=has_ctx),
        grid=(nbatch, nb),
        in_specs=[pl.BlockSpec((1, ROW_BLK, 512), lambda b, j: (b, j, 0)),
                  pl.BlockSpec((1, N, 512), lambda b, j: (b, 0, 0)),
                  pl.BlockSpec((1, N, 256), lambda b, j: (b, 0, 0))],
        out_specs=pl.BlockSpec((1, ROW_BLK, 256), lambda b, j: (b, j, 0)),
        out_shape=jax.ShapeDtypeStruct((nbatch, N, 256), BF16),
        compiler_params=_cp("arbitrary", "arbitrary"),
        name="mla_attn",
    )(q, k, v)


def _na_kernel(q_ref, kc_ref, k0_ref, k1_ref, k2_ref, vc_ref, v0_ref, v1_ref, v2_ref, b_ref, o_ref, *, has_ctx):
    scale = jnp.asarray(0.125, BF16)

    def heads(win):
        kws = (k0_ref, k1_ref, k2_ref)
        vws = (v0_ref, v1_ref, v2_ref)
        for h in range(4):
            sl = slice(64 * h, 64 * h + 64)
            qh = q_ref[0, :, sl] * scale
            s_c = _dot_nt(qh, kc_ref[0, :, sl])
            m = jnp.max(s_c, axis=-1, keepdims=True)
            s_w = []
            if win:
                for i in range(3):
                    s = _dot_nt(qh, kws[i][0, :, sl]) + b_ref[0, h, :, 256 * i:256 * i + 256]
                    s_w.append(s)
                    m = jnp.maximum(m, jnp.max(s, axis=-1, keepdims=True))
            p = jnp.exp(s_c - m)
            l = jnp.sum(p, axis=-1, keepdims=True)
            o = _dot(p, vc_ref[0, :, sl])
            for i, s in enumerate(s_w):
                p = jnp.exp(s - m)
                l = l + jnp.sum(p, axis=-1, keepdims=True)
                o = o + _dot(p, vws[i][0, :, sl])
            o_ref[0, :, sl] = (o / l).astype(o_ref.dtype)

    if has_ctx:
        g = pl.program_id(0)

        @pl.when(g < LAT_BLKS)
        def _():
            heads(True)

        @pl.when(g == LAT_BLKS)
        def _():
            heads(False)
    else:
        heads(True)


def _na_attn(qkv, bias, has_ctx):
    nbatch = qkv.shape[0]
    ng = NBLK if has_ctx else LAT_BLKS
    j0 = lambda g: jnp.clip(g - 1, 0, LAT_BLKS - 3)
    blk = lambda f: pl.BlockSpec((1, ROW_BLK, 256), f)
    return pl.pallas_call(
        functools.partial(_na_kernel, has_ctx=has_ctx),
        grid=(ng, nbatch),
        in_specs=[blk(lambda g, b: (b, g, 0)),
                  blk(lambda g, b: (b, LAT_BLKS, 1)),
                  blk(lambda g, b: (b, j0(g), 1)),
                  blk(lambda g, b: (b, j0(g) + 1, 1)),
                  blk(lambda g, b: (b, j0(g) + 2, 1)),
                  blk(lambda g, b: (b, LAT_BLKS, 2)),
                  blk(lambda g, b: (b, j0(g), 2)),
                  blk(lambda g, b: (b, j0(g) + 1, 2)),
                  blk(lambda g, b: (b, j0(g) + 2, 2)),
                  pl.BlockSpec((1, 4, ROW_BLK, 768), lambda g, b: (jnp.minimum(g, LAT_BLKS - 1), 0, 0, 0))],
        out_specs=blk(lambda g, b: (b, g, 0)),
        out_shape=jax.ShapeDtypeStruct((nbatch, N, 256), BF16),
        compiler_params=_cp("arbitrary", "arbitrary"),
        name="na_attn",
    )(qkv, qkv, qkv, qkv, qkv, qkv, qkv, qkv, qkv, bias)


def _na_structure():
    ar = np.zeros((8, 4, 12, 15), np.float32)
    rowmask = np.zeros((8, 4, 12), bool)
    for g in range(8):
        k0 = 4 * min(max(g - 1, 0), 5)
        for qr in range(4):
            r = 4 * g + qr
            rs = min(max(r - 4, 0), 24)
            for kk in range(12):
                kr = k0 + kk
                if rs <= kr < rs + 8:
                    ar[g, qr, kk, kr - r + 7] = 1.0
                    rowmask[g, qr, kk] = True
    ac = np.zeros((64, 64, 31), np.float32)
    colmask = np.zeros((64, 64), bool)
    for c in range(64):
        cs = min(max(c - 8, 0), 48)
        for kc in range(cs, cs + 16):
            ac[c, kc, kc - c + 15] = 1.0
            colmask[c, kc] = True
    valid = rowmask[:, :, None, :, None] & colmask[None, None, :, None, :]
    negmask = np.where(valid, 0.0, NEG).astype(np.float32)
    return ar, ac, negmask


_NA_AR, _NA_AC, _NA_NEG = _na_structure()


def _na_bias(rpb):
    hp = lax.Precision.HIGHEST
    tz = jnp.einsum('hij,cdj->hicd', rpb.astype(F32), _NA_AC, precision=hp)
    bias = jnp.einsum('gqki,hicd->ghqckd', _NA_AR, tz, precision=hp) + _NA_NEG[:, None]
    return bias.reshape(8, 4, 256, 768)


def _s5_kernel(uf_ref, ub_ref, bf_ref, bb_ref, af_ref, ab_ref, cf_ref, cb_ref, yf_ref, yb_ref,
               hf, hb, buf_f, buf_b):
    i = pl.program_id(0)
    half = 1024

    @pl.when(i == 0)
    def _():
        hf[...] = jnp.zeros_like(hf)
        hb[...] = jnp.zeros_like(hb)

    buf_f[...] = _dot(uf_ref[...], bf_ref[...])
    buf_b[...] = _dot(ub_ref[...], bb_ref[...])

    def one(buf, a_ref, row, hr, hi):
        bu_r = buf[pl.ds(row, 8), :half]
        bu_i = buf[pl.ds(row, 8), half:]
        ar = a_ref[:, :half]
        ai = a_ref[:, half:]
        nr = ar * hr - ai * hi + bu_r
        ni = ar * hi + ai * hr + bu_i
        buf[pl.ds(row, 8), :half] = nr
        buf[pl.ds(row, 8), half:] = ni
        return nr, ni

    def step(k, carry):
        fr, fi, br, bi = carry
        rf = pl.multiple_of(k * 8, 8)
        rb = pl.multiple_of((S5_CHUNK - 1 - k) * 8, 8)
        fr, fi = one(buf_f, af_ref, rf, fr, fi)
        br, bi = one(buf_b, ab_ref, rb, br, bi)
        return fr, fi, br, bi

    init = (hf[:, :half], hf[:, half:], hb[:, :half], hb[:, half:])
    fr, fi, br, bi = lax.fori_loop(0, S5_CHUNK, step, init)
    hf[:, :half] = fr
    hf[:, half:] = fi
    hb[:, :half] = br
    hb[:, half:] = bi
    yf_ref[...] = _dot(buf_f[...], cf_ref[...])
    yb_ref[...] = _dot(buf_b[...], cb_ref[...])


def _s5_scan(u_tb, bmats, avecs, cmats):
    rows = u_tb.shape[0]
    cr = S5_CHUNK * 8
    nch = rows // cr
    nctx = LC // S5_CHUNK
    fidx = lambda i: jnp.where(i < nctx, nch - nctx + i, i - nctx)
    bidx = lambda i: nch - 1 - i
    full = lambda w: pl.BlockSpec(w.shape, lambda i: (0, 0))
    return pl.pallas_call(
        _s5_kernel,
        grid=(nch,),
        in_specs=[pl.BlockSpec((cr, 256), lambda i: (fidx(i), 0)),
                  pl.BlockSpec((cr, 256), lambda i: (bidx(i), 0)),
                  full(bmats[0]), full(bmats[1]), full(avecs[0]), full(avecs[1]),
                  full(cmats[0]), full(cmats[1])],
        out_specs=[pl.BlockSpec((cr, 256), lambda i: (fidx(i), 0)),
                   pl.BlockSpec((cr, 256), lambda i: (bidx(i), 0))],
        out_shape=[jax.ShapeDtypeStruct((rows, 256), F32)] * 2,
        scratch_shapes=[pltpu.VMEM((8, 2048), F32), pltpu.VMEM((8, 2048), F32),
                        pltpu.VMEM((cr, 2048), F32), pltpu.VMEM((cr, 2048), F32)],
        compiler_params=_cp("arbitrary"),
        name="s5_scan",
    )(u_tb, u_tb, bmats[0], bmats[1], avecs[0], avecs[1], cmats[0], cmats[1])


def _s5_glu_kernel(u_ref, yf_ref, yb_ref, d_ref, w_ref, b_ref, o_ref):
    y = d_ref[...] * u_ref[...] + yf_ref[...] + yb_ref[...]
    z = y * (0.5 * (1.0 + jnp.tanh(math.sqrt(2.0 / math.pi) * (y + 0.044715 * (y * y * y)))))
    o_ref[...] = (z * _sigmoid(_dot(z, w_ref[...]) + b_ref[...])).astype(o_ref.dtype)


def _s5_glu(u_tb, yf, yb, d, w, b):
    rows = u_tb.shape[0]
    rb = 512
    blk = pl.BlockSpec((rb, 256), lambda i: (i, 0))
    full = lambda a: pl.BlockSpec(a.shape, lambda i: (0, 0))
    return pl.pallas_call(
        _s5_glu_kernel,
        grid=(rows // rb,),
        in_specs=[blk, blk, blk, full(d), full(w), full(b)],
        out_specs=blk,
        out_shape=jax.ShapeDtypeStruct((rows, 256), BF16),
        compiler_params=_cp("arbitrary"),
        name="s5_glu",
    )(u_tb, yf, yb, d, w, b)


def _s5_params(a_re, a_im, log_step, b_re, b_im, c_re, c_im):
    eye = jnp.eye(16, dtype=F32)
    bmats, avecs, cmats = [], [], []
    for d in range(2):
        a = lax.complex(a_re[d].astype(F32), a_im[d].astype(F32))
        abar = jnp.exp(jnp.exp(log_step[d].astype(F32))[:, None] * a)
        bbar = ((abar - 1.0) / a)[:, :, None] * lax.complex(b_re[d].astype(F32), b_im[d].astype(F32))
        blk_in = lambda m: (eye[:, None, :, None] * jnp.transpose(m, (0, 2, 1))[:, :, None, :]).reshape(256, 1024)
        bmats.append(jnp.concatenate([blk_in(jnp.real(bbar)), blk_in(jnp.imag(bbar))], axis=1).astype(BF16))
        avec = jnp.concatenate([jnp.real(abar).reshape(1, 1024), jnp.imag(abar).reshape(1, 1024)], axis=1)
        avecs.append(jnp.broadcast_to(avec, (8, 2048)))
        blk_out = lambda m: (eye[:, None, :, None] * jnp.transpose(m, (0, 2, 1))[:, :, None, :]).reshape(1024, 256)
        cmats.append(jnp.concatenate([blk_out(c_re[d].astype(F32)), -blk_out(c_im[d].astype(F32))], axis=0).astype(BF16))
    return bmats, avecs, cmats


def _m2_prep_kernel(x_ref, dt_ref, w_ref, b_ref, dtb_ref, xo_ref, dto_ref):
    x = x_ref[0]
    t = lax.broadcasted_iota(jnp.int32, x.shape, 0)
    m2 = ((t >= 2) & (t < T)) | (t >= T + 2)
    m1 = ((t >= 1) & (t < T)) | (t >= T + 1)
    p1 = (t <= T - 2) | ((t >= T) & (t <= N - 2))
    w = w_ref[...]
    y = (w[0:1] * jnp.where(m2, pltpu.roll(x, 2, 0), 0.0)
         + w[1:2] * jnp.where(m1, pltpu.roll(x, 1, 0), 0.0)
         + w[2:3] * x
         + w[3:4] * jnp.where(p1, pltpu.roll(x, N - 1, 0), 0.0)) + b_ref[...]
    xo_ref[0] = _silu(y)
    v = dt_ref[0] + dtb_ref[...]
    dto_ref[0] = jnp.maximum(v, 0.0) + jnp.log1p(jnp.exp(-jnp.abs(v)))


def _m2_prep(p_m2, conv_w, conv_b, dtb):
    nbatch = p_m2.shape[0]
    return pl.pallas_call(
        _m2_prep_kernel,
        grid=(nbatch, 3),
        in_specs=[pl.BlockSpec((1, N, 256), lambda b, c: (b, 0, 1 + c)),
                  pl.BlockSpec((1, N, 128), lambda b, c: (b, 0, 8)),
                  pl.BlockSpec((4, 256), lambda b, c: (0, c)),
                  pl.BlockSpec((1, 256), lambda b, c: (0, c)),
                  pl.BlockSpec((1, 128), lambda b, c: (0, 0))],
        out_specs=[pl.BlockSpec((1, N, 256), lambda b, c: (b, 0, c)),
                   pl.BlockSpec((1, N, 128), lambda b, c: (b, 0, 0))],
        out_shape=[jax.ShapeDtypeStruct((nbatch, N, 768), F32),
                   jax.ShapeDtypeStruct((nbatch, N, 128), F32)],
        compiler_params=_cp("arbitrary", "arbitrary"),
        name="m2_prep",
    )(p_m2, p_m2, conv_w, conv_b, dtb)


def _ssd_kernel(xf_ref, dtf_ref, xb_ref, dtb_ref, a_ref, yf_ref, yb_ref, hs):
    i = pl.program_id(1)
    L = SSD_CHUNK

    @pl.when(i == 0)
    def _():
        hs[...] = jnp.zeros_like(hs)

    li = lax.broadcasted_iota(jnp.int32, (L, L), 0)
    si = lax.broadcasted_iota(jnp.int32, (L, L), 1)

    def direction(x_ref, dt_ref, y_ref, d, causal, causal_t, last):
        cz = jnp.where(causal, 1.0, 0.0).astype(BF16)
        czt = jnp.where(causal_t, 1.0, 0.0).astype(BF16)
        xbc = x_ref[0]
        dt = dt_ref[0]
        dta = dt * a_ref[...]
        cum = _dot_exact_lhs(cz, dta)
        dtt = dt.T
        cumt = _dot_exact_rhs(dta.T, czt)
        for g in range(2):
            bm = xbc[:, 256 + 128 * g:384 + 128 * g]
            cm = xbc[:, 512 + 128 * g:640 + 128 * g]
            gmat = _dot_nt(cm, bm)
            bmt = bm.T.astype(BF16)
            cmb = cm.astype(BF16)
            for hh in range(2):
                h = 2 * g + hh
                ln = 4 * d + h
                ccol = cum[:, ln:ln + 1]
                crow = cumt[ln:ln + 1, :]
                decay = jnp.where(causal, jnp.exp(ccol - crow), 0.0)
                mmat = gmat * decay * dtt[ln:ln + 1, :]
                xh = xbc[:, 64 * h:64 * h + 64]
                yd = _dot(mmat, xh)
                clast = cum[last:last + 1, ln:ln + 1]
                wcol = jnp.exp(clast - ccol) * dt[:, ln:ln + 1]
                st = jnp.dot(bmt, (xh * wcol).astype(BF16), preferred_element_type=F32)
                hprev = hs[ln]
                yo = jnp.dot(cmb, hprev.astype(BF16), preferred_element_type=F32) * jnp.exp(ccol)
                hs[ln] = hprev * jnp.exp(clast) + st
                y_ref[0, :, 64 * h:64 * h + 64] = yd + yo

    direction(xf_ref, dtf_ref, yf_ref, 0, si <= li, li <= si, L - 1)
    direction(xb_ref, dtb_ref, yb_ref, 1, si >= li, li >= si, 0)


def _ssd(xbc, dt, a_row):
    nbatch = xbc.shape[0]
    nch = N // SSD_CHUNK
    nctx = LC // SSD_CHUNK
    fidx = lambda i: jnp.where(i < nctx, nch - nctx + i, i - nctx)
    bidx = lambda i: nch - 1 - i
    return pl.pallas_call(
        _ssd_kernel,
        grid=(nbatch, nch),
        in_specs=[pl.BlockSpec((1, SSD_CHUNK, 768), lambda b, i: (b, fidx(i), 0)),
                  pl.BlockSpec((1, SSD_CHUNK, 128), lambda b, i: (b, fidx(i), 0)),
                  pl.BlockSpec((1, SSD_CHUNK, 768), lambda b, i: (b, bidx(i), 0)),
                  pl.BlockSpec((1, SSD_CHUNK, 128), lambda b, i: (b, bidx(i), 0)),
                  pl.BlockSpec((1, 128), lambda b, i: (0, 0))],
        out_specs=[pl.BlockSpec((1, SSD_CHUNK, 256), lambda b, i: (b, fidx(i), 0)),
                   pl.BlockSpec((1, SSD_CHUNK, 256), lambda b, i: (b, bidx(i), 0))],
        out_shape=[jax.ShapeDtypeStruct((nbatch, N, 256), F32)] * 2,
        scratch_shapes=[pltpu.VMEM((8, 128, 64), F32)],
        compiler_params=_cp("arbitrary", "arbitrary"),
        name="ssd",
    )(xbc, dt, xbc, dt, a_row)


def _m2_norm_kernel(x_ref, z_ref, yf_ref, yb_ref, d_ref, g_ref, o_ref):
    y = d_ref[...] * x_ref[...] + yf_ref[...] + yb_ref[...]
    o_ref[...] = _rms(y * _silu(z_ref[...]), g_ref[...]).astype(o_ref.dtype)


def _m2_norm(xbc2d, p_m2_2d, yf2d, yb2d, dvec, gn):
    rows = xbc2d.shape[0]
    blk = pl.BlockSpec((ROW_BLK, 256), lambda i: (i, 0))
    full = lambda a: pl.BlockSpec(a.shape, lambda i: (0, 0))
    return pl.pallas_call(
        _m2_norm_kernel,
        grid=(rows // ROW_BLK,),
        in_specs=[blk, blk, blk, blk, full(dvec), full(gn)],
        out_specs=blk,
        out_shape=jax.ShapeDtypeStruct((rows, 256), BF16),
        compiler_params=_cp("arbitrary"),
        name="m2_norm",
    )(xbc2d, p_m2_2d, yf2d, yb2d, dvec, gn)


def _merge_kernel(x_ref, oa_ref, on_ref, os_ref, om_ref, gl_ref, wb_ref, wo_ref, gpm_ref, gate1_ref,
                  gpf_ref, sh2_ref, sc2_ref, wr_ref, x1_ref, h2_ref, aff_ref):
    outs = (oa_ref, on_ref, os_ref, om_ref)
    y = None
    for j in range(4):
        t = _sigmoid(gl_ref[:, 1024 * j:1024 * j + 1024]) * jnp.dot(outs[j][...], wb_ref[j], preferred_element_type=F32)
        y = t if y is None else y + t
    y2 = _dot(y, wo_ref[...])
    x1 = x_ref[...] + gate1_ref[0] * _rms(y2, gpm_ref[...])
    x1_ref[...] = x1
    h2 = _rms(x1, gpf_ref[...]) * (1.0 + sc2_ref[0]) + sh2_ref[0]
    h2_ref[...] = h2.astype(BF16)
    logits = _dot_hi_nt(wr_ref[...], h2)
    m = jnp.max(logits, axis=0, keepdims=True)
    e = jnp.exp(logits - m)
    aff_ref[0, 0] = e / jnp.sum(e, axis=0, keepdims=True)


def _merge(x2d, o_mla, o_na, o_s5, o_m2, gl, wb, wo, gpm, mod3, gpf, wrt, nbatch, nb):
    rows = x2d.shape[0]
    mrow = _mod_row(nb)
    fb = lambda i: (_flat_blk(i, nb), 0)
    blk = lambda w: pl.BlockSpec((ROW_BLK, w), fb)
    full = lambda a: pl.BlockSpec(a.shape, lambda i: (0,) * a.ndim)
    modspec = lambda k: pl.BlockSpec((1, 1, D), lambda i: (mrow(i, nbatch), 0, k))
    return pl.pallas_call(
        _merge_kernel,
        grid=(nbatch * nb,),
        in_specs=[blk(D), blk(256), blk(256), blk(256), blk(256), blk(GATE_W),
                  full(wb), full(wo), full(gpm), modspec(2), full(gpf), modspec(3), modspec(4), full(wrt)],
        out_specs=[blk(D), blk(D),
                   pl.BlockSpec((1, 1, N_EXPERTS, ROW_BLK), lambda i: (i // nb, i % nb, 0, 0))],
        out_shape=[jax.ShapeDtypeStruct((rows, D), F32),
                   jax.ShapeDtypeStruct((rows, D), BF16),
                   jax.ShapeDtypeStruct((nbatch, NBLK, N_EXPERTS, ROW_BLK), F32)],
        compiler_params=_cp("arbitrary"),
        name="merge",
    )(x2d, o_mla, o_na, o_s5, o_m2, gl, wb, wo, gpm, mod3, gpf, mod3, mod3, wrt)


def _topk_kernel(aff_ref, slot_ref, *, nk, cap):
    a = aff_ref[0]
    bits = lax.bitcast_convert_type(a, jnp.int32)
    count = lambda m: jnp.sum(jnp.sum(jnp.where(m, 1.0, 0.0), axis=2, keepdims=True), axis=0, keepdims=True)
    thr = jnp.zeros((1, N_EXPERTS, 1), jnp.int32)
    for bit in range(30, -1, -1):
        cand = thr | (1 << bit)
        thr = jnp.where(count(bits >= cand) >= cap, cand, thr)
    gt = bits > thr
    eq = bits == thr
    need = cap - count(gt)

    tri = jnp.where(lax.broadcasted_iota(jnp.int32, (256, 256), 0) <= lax.broadcasted_iota(jnp.int32, (256, 256), 1),
                    1.0, 0.0).astype(BF16)

    def prefix_excl(m):
        incl = jnp.dot(m.reshape(nk * N_EXPERTS, 256).astype(BF16), tri,
                       preferred_element_type=F32).reshape(nk, N_EXPERTS, 256)
        offs = []
        run = jnp.zeros((1, N_EXPERTS, 1), F32)
        for k in range(nk):
            offs.append(run)
            run = run + incl[k:k + 1, :, 255:256]
        off = offs[0] if nk == 1 else jnp.concatenate(offs, axis=0)
        return incl - m + off

    eqf = jnp.where(eq, 1.0, 0.0)
    sel = jnp.where(gt, 1.0, jnp.where(eq & (prefix_excl(eqf) < need), 1.0, 0.0))
    slot = jnp.where(sel > 0.5, prefix_excl(sel), -1.0)
    slot_ref[0] = slot.astype(jnp.int32)


def _topk(aff, blk0, nk, cap):
    nbatch = aff.shape[0]
    return pl.pallas_call(
        functools.partial(_topk_kernel, nk=nk, cap=cap),
        grid=(nbatch,),
        in_specs=[pl.BlockSpec((1, nk, N_EXPERTS, ROW_BLK), lambda b: (b, blk0, 0, 0))],
        out_specs=pl.BlockSpec((1, nk, N_EXPERTS, ROW_BLK), lambda b: (b, 0, 0, 0)),
        out_shape=jax.ShapeDtypeStruct((nbatch, nk, N_EXPERTS, ROW_BLK), jnp.int32),
        compiler_params=_cp("arbitrary"),
        name="topk",
    )(aff)


def _expert_kernel(*refs, has_ctx):
    if has_ctx:
        (hl_ref, sl_ref, al_ref, hc_ref, sc_ref, ac_ref, wg_ref, wu_ref, wd_ref,
         yl_ref, yc_ref, wgb, wub, wdb) = refs
    else:
        hl_ref, sl_ref, al_ref, wg_ref, wu_ref, wd_ref, yl_ref, wgb, wub, wdb = refs
    e = pl.program_id(0)
    b = pl.program_id(1)

    @pl.when(b == 0)
    def _():
        wgb[...] = wg_ref[0, 0].astype(BF16)
        wub[...] = wu_ref[0, 0].astype(BF16)
        wdb[...] = wd_ref[0, 0].astype(BF16)

    def run_set(h_ref, slot_ref, aff_ref, nk, cap, out_ref):
        r_iota = lax.broadcasted_iota(jnp.int32, (cap, ROW_BLK), 0)
        xs = jnp.zeros((cap, D), F32)
        gs = jnp.zeros((cap, 1), F32)
        for k in range(nk):
            pm = slot_ref[0, k, pl.ds(e, 1), :] == r_iota
            xs = xs + jnp.dot(jnp.where(pm, 1.0, 0.0).astype(BF16), h_ref[0, ROW_BLK * k:ROW_BLK * k + ROW_BLK, :],
                              preferred_element_type=F32)
            gs = gs + jnp.sum(jnp.where(pm, aff_ref[0, k, pl.ds(e, 1), :], 0.0), axis=1, keepdims=True)
        xsb = xs.astype(BF16)
        gt = jnp.dot(xsb, wgb[...], preferred_element_type=F32)
        up = jnp.dot(xsb, wub[...], preferred_element_type=F32)
        y = jnp.dot((_silu(gt) * up).astype(BF16), wdb[...], preferred_element_type=F32)
        out_ref[0, 0] = (y * gs).astype(out_ref.dtype)

    run_set(hl_ref, sl_ref, al_ref, LAT_BLKS, CAP_LAT, yl_ref)
    if has_ctx:
        run_set(hc_ref, sc_ref, ac_ref, 1, CAP_CTX, yc_ref)


def _experts(h2, slot_lat, slot_ctx, aff, w_gate, w_up, w_down, l, has_ctx):
    nbatch = h2.shape[0]
    idx4 = lambda e, b: (b, 0, 0, 0)
    in_specs = [pl.BlockSpec((1, T, D), lambda e, b: (b, 0, 0)),
                pl.BlockSpec((1, LAT_BLKS, N_EXPERTS, ROW_BLK), idx4),
                pl.BlockSpec((1, LAT_BLKS, N_EXPERTS, ROW_BLK), idx4)]
    args = [h2, slot_lat, aff]
    out_specs = [pl.BlockSpec((1, 1, CAP_LAT, D), lambda e, b: (b, e, 0, 0))]
    out_shape = [jax.ShapeDtypeStruct((nbatch, N_EXPERTS, CAP_LAT, D), BF16)]
    if has_ctx:
        in_specs += [pl.BlockSpec((1, LC, D), lambda e, b: (b, LAT_BLKS, 0)),
                     pl.BlockSpec((1, 1, N_EXPERTS, ROW_BLK), idx4),
                     pl.BlockSpec((1, 1, N_EXPERTS, ROW_BLK), lambda e, b: (b, LAT_BLKS, 0, 0))]
        args += [h2, slot_ctx, aff]
        out_specs.append(pl.BlockSpec((1, 1, CAP_CTX, D), lambda e, b: (b, e, 0, 0)))
        out_shape.append(jax.ShapeDtypeStruct((nbatch, N_EXPERTS, CAP_CTX, D), BF16))
    wspec = pl.BlockSpec((1, 1, D, D), lambda e, b: (l, e, 0, 0))
    in_specs += [wspec, wspec, wspec]
    args += [w_gate, w_up, w_down]
    return pl.pallas_call(
        functools.partial(_expert_kernel, has_ctx=has_ctx),
        grid=(N_EXPERTS, nbatch),
        in_specs=in_specs,
        out_specs=out_specs,
        out_shape=out_shape,
        scratch_shapes=[pltpu.VMEM((D, D), BF16)] * 3,
        compiler_params=_cp("arbitrary", "arbitrary"),
        name="experts",
    )(*args)


def _combine_kernel(*refs, has_ctx):
    if has_ctx:
        x1_ref, yl_ref, scl_ref, yc_ref, scc_ref, gate2_ref, g_ref, o_ref = refs
    else:
        x1_ref, yl_ref, scl_ref, gate2_ref, g_ref, o_ref = refs

    def comb(sc_ref, y_ref, cap):
        lane = lax.broadcasted_iota(jnp.int32, (ROW_BLK, cap), 1)
        sc = sc_ref[0]
        acc = jnp.zeros((ROW_BLK, D), F32)
        for e in range(N_EXPERTS):
            pt = jnp.where(sc[:, e:e + 1] == lane, 1.0, 0.0).astype(BF16)
            acc = acc + jnp.dot(pt, y_ref[0, e], preferred_element_type=F32)
        o_ref[0] = x1_ref[0] + gate2_ref[0] * _rms(acc, g_ref[...])

    if has_ctx:
        j = pl.program_id(1)

        @pl.when(j < LAT_BLKS)
        def _():
            comb(scl_ref, yl_ref, CAP_LAT)

        @pl.when(j == LAT_BLKS)
        def _():
            comb(scc_ref, yc_ref, CAP_CTX)
    else:
        comb(scl_ref, yl_ref, CAP_LAT)


def _combine(x1, y_lat, scol_lat, y_ctx, scol_ctx, mod3, g, has_ctx):
    nbatch = x1.shape[0]
    nb = NBLK if has_ctx else LAT_BLKS
    ntok = N if has_ctx else T
    in_specs = [pl.BlockSpec((1, ROW_BLK, D), lambda b, j: (b, j, 0)),
                pl.BlockSpec((1, N_EXPERTS, CAP_LAT, D), lambda b, j: (b, 0, 0, 0)),
                pl.BlockSpec((1, ROW_BLK, N_EXPERTS), lambda b, j: (b, jnp.minimum(j, LAT_BLKS - 1), 0))]
    args = [x1, y_lat, scol_lat]
    if has_ctx:
        in_specs += [pl.BlockSpec((1, N_EXPERTS, CAP_CTX, D), lambda b, j: (b, 0, 0, 0)),
                     pl.BlockSpec((1, ROW_BLK, N_EXPERTS), lambda b, j: (b, 0, 0))]
        args += [y_ctx, scol_ctx]
    in_specs += [pl.BlockSpec((1, 1, D), lambda b, j: (jnp.where(j == LAT_BLKS, nbatch, b), 0, 5)),
                 pl.BlockSpec((1, D), lambda b, j: (0, 0))]
    args += [mod3, g]
    return pl.pallas_call(
        functools.partial(_combine_kernel, has_ctx=has_ctx),
        grid=(nbatch, nb),
        in_specs=in_specs,
        out_specs=pl.BlockSpec((1, ROW_BLK, D), lambda b, j: (b, j, 0)),
        out_shape=jax.ShapeDtypeStruct((nbatch, ntok, D), F32),
        compiler_params=_cp("arbitrary", "arbitrary"),
        name="combine",
    )(*args)


def _rope_tables():
    f = 1.0 / (ROPE_THETA ** (jnp.arange(0, 16, 2, dtype=F32) / 16))
    pos = jnp.arange(T)
    row, col = pos // GRID_W, pos % GRID_W
    ar = row.astype(F32)[:, None] * f[None, :]
    ac = col.astype(F32)[:, None] * f[None, :]
    cos32 = jnp.concatenate([jnp.cos(ar), jnp.cos(ar), jnp.cos(ac), jnp.cos(ac)], axis=-1)
    sin32 = jnp.concatenate([jnp.sin(ar), jnp.sin(ar), jnp.sin(ac), jnp.sin(ac)], axis=-1)
    cos32 = jnp.concatenate([cos32, jnp.ones((LC, 32), F32)], axis=0)
    sin32 = jnp.concatenate([sin32, jnp.zeros((LC, 32), F32)], axis=0)
    return cos32, sin32


def _rot_cols(w):
    a, b, c, d = w[..., 0:8], w[..., 8:16], w[..., 16:24], w[..., 24:32]
    return jnp.concatenate([-b, a, -d, c], axis=-1)


def _mla_weights(w_uq, w_ukv):
    r = w_uq.shape[0]
    wq3 = w_uq.reshape(r, 4, 96)
    z32 = jnp.zeros((r, 4, 32), F32)
    wq = jnp.concatenate([wq3, z32], axis=-1).reshape(r, 512)
    wqr = jnp.concatenate([jnp.zeros((r, 4, 64), F32), _rot_cols(wq3[..., 64:96]), z32], axis=-1).reshape(r, 512)
    rk = w_ukv.shape[0]
    wkv3 = w_ukv.reshape(rk, 4, 128)
    wk = jnp.concatenate([wkv3[..., :64], jnp.zeros((rk, 4, 64), F32)], axis=-1).reshape(rk, 512)
    wv = wkv3[..., 64:].reshape(rk, 256)
    e = np.zeros((32, 512), np.float32)
    for h in range(4):
        e[np.arange(32), 128 * h + 64 + np.arange(32)] = 1.0
    return wq.astype(BF16), wqr.astype(BF16), wk.astype(BF16), wv.astype(BF16), jnp.asarray(e, BF16)


def _inproj_weights(w):
    o = np.cumsum([0, 256, 128, 32, 768, 256, 256, 768, 8, 4096])
    seg = lambda i: w[:, o[i]:o[i + 1]]
    kr = seg(2)
    wm = jnp.concatenate([seg(0), seg(1), kr, _rot_cols(kr), jnp.zeros((D, 64), F32)], axis=1)
    w2 = jnp.concatenate([seg(5), seg(6), seg(7), jnp.zeros((D, 120), F32)], axis=1)
    return [a.astype(BF16) for a in (wm, seg(3), seg(4), w2, seg(8))]


def kernel(x, c, ctx, c_ctx, w_ada, b_ada, g_pre_mix, g_post_mix, g_pre_ffn, g_post_ffn, w_in, mla_g_cq, mla_g_ckv, mla_w_uq, mla_w_ukv, na_rpb, s5_a_re, s5_a_im, s5_log_step, s5_b_re, s5_b_im, s5_c_re, s5_c_im, s5_d, s5_w_glu, s5_b_glu, m2_conv_w, m2_conv_b, m2_a_log, m2_dt_bias, m2_d, m2_g_norm, w_branch, w_out, w_router, w_gate, w_up, w_down):
    nbatch = x.shape[0]
    depth = w_ada.shape[0]
    row2 = lambda v: v.reshape(1, -1).astype(F32)

    xs = jnp.concatenate([x, ctx], axis=1)
    cvec = jnp.concatenate([c, c_ctx[None, :], jnp.zeros((7, D), F32)], axis=0)
    cvec = cvec[: ((nbatch + 1 + 7) // 8) * 8]
    b_ada3 = b_ada.reshape(depth, 1, 6 * D)

    cos32, sin32 = _rope_tables()
    qscale = (64 + 32) ** -0.5
    lane_is_rope = np.tile(np.concatenate([np.zeros(64, bool), np.ones(32, bool), np.zeros(32, bool)]), 4)
    pick = lambda t32, fill: jnp.where(lane_is_rope[None, :], jnp.tile(jnp.pad(t32, ((0, 0), (64, 32))), (1, 4)), fill)
    cos_q = pick(cos32, 1.0) * qscale
    sin_q = pick(sin32, 0.0) * qscale

    for l in range(depth):
        has_ctx = l < depth - 1
        nb = NBLK if has_ctx else LAT_BLKS
        mod = _ada(cvec, w_ada, b_ada3, l)
        mod3 = mod.reshape(mod.shape[0], 1, 6 * D)
        x2d = xs.reshape(nbatch * N, D)

        p_mla, p_na, p_s5, p_m2, p_gate = _inproj(x2d, row2(g_pre_mix[l]), mod3, _inproj_weights(w_in[l]), nbatch)

        wq, wqr, wk, wv, e_mat = _mla_weights(mla_w_uq[l], mla_w_ukv[l])
        q, k, v = _mla_prep(p_mla, row2(mla_g_cq[l]), row2(mla_g_ckv[l]), wq, wqr, wk, wv, e_mat,
                            cos_q, sin_q, cos32, sin32)
        o_mla = _mla_attn(q.reshape(nbatch, N, 512), k.reshape(nbatch, N, 512), v.reshape(nbatch, N, 256), has_ctx)

        o_na = _na_attn(p_na.reshape(nbatch, N, NA_W), _na_bias(na_rpb[l]), has_ctx)

        u_tb = jnp.transpose(p_s5.reshape(nbatch, N, S5_W), (1, 0, 2)).reshape(N * nbatch, S5_W)
        bmats, avecs, cmats = _s5_params(s5_a_re[l], s5_a_im[l], s5_log_step[l], s5_b_re[l], s5_b_im[l],
                                         s5_c_re[l], s5_c_im[l])
        yf, yb = _s5_scan(u_tb, bmats, avecs, cmats)
        o_s5_tb = _s5_glu(u_tb, yf, yb, row2(s5_d[l]), s5_w_glu[l].astype(BF16), row2(s5_b_glu[l]))
        o_s5 = jnp.transpose(o_s5_tb.reshape(N, nbatch, S5_W), (1, 0, 2)).reshape(nbatch * N, S5_W)

        dtb = jnp.pad(m2_dt_bias[l].reshape(1, 8).astype(F32), ((0, 0), (0, 120)))
        xbc, dt = _m2_prep(p_m2.reshape(nbatch, N, M2_W), m2_conv_w[l].astype(F32), row2(m2_conv_b[l]), dtb)
        a_row = jnp.pad(-jnp.exp(m2_a_log[l].astype(F32)).reshape(1, 8), ((0, 0), (0, 120)))
        ssd_f, ssd_b = _ssd(xbc, dt, a_row)
        o_m2 = _m2_norm(xbc.reshape(nbatch * N, 768), p_m2, ssd_f.reshape(nbatch * N, 256),
                        ssd_b.reshape(nbatch * N, 256), row2(jnp.repeat(m2_d[l], 64)), row2(m2_g_norm[l]))

        x1, h2, aff = _merge(x2d, o_mla.reshape(nbatch * N, 256), o_na.reshape(nbatch * N, 256), o_s5, o_m2, p_gate,
                             w_branch[l].astype(BF16), w_out[l].astype(BF16), row2(g_post_mix[l]), mod3,
                             row2(g_pre_ffn[l]), jnp.transpose(w_router[l]).astype(F32), nbatch, nb)

        slot_lat = _topk(aff, 0, LAT_BLKS, CAP_LAT)
        slot_ctx = _topk(aff, LAT_BLKS, 1, CAP_CTX) if has_ctx else None
        ys = _experts(h2.reshape(nbatch, N, D), slot_lat, slot_ctx, aff, w_gate, w_up, w_down, l, has_ctx)
        scol_lat = jnp.transpose(slot_lat, (0, 1, 3, 2)).reshape(nbatch, T, N_EXPERTS)
        scol_ctx = jnp.transpose(slot_ctx, (0, 1, 3, 2)).reshape(nbatch, LC, N_EXPERTS) if has_ctx else None
        xs = _combine(x1.reshape(nbatch, N, D), ys[0], scol_lat, ys[1] if has_ctx else None, scol_ctx,
                      mod3, row2(g_post_ffn[l]), has_ctx)
    return xs
```

```python
import functools
import math

import numpy as np
import jax
import jax.numpy as jnp
from jax import lax
from jax.experimental import pallas as pl
from jax.experimental.pallas import tpu as pltpu

F32 = jnp.float32
BF16 = jnp.bfloat16

D = 1024
T = 2048
LC = 256
N = T + LC
GRID_W = 64
ROW_BLK = 256
NBLK = N // ROW_BLK
LAT_BLKS = T // ROW_BLK
EPS = 1e-6
N_EXPERTS = 16
CAP_LAT = 2 * T // N_EXPERTS
CAP_CTX = 2 * LC // N_EXPERTS
ROPE_THETA = 10000.0
NEG = -1e30

MLA_W = 512
NA_W = 768
S5_W = 256
M2_W = 1152
GATE_W = 4096

S5_CHUNK = 64
SSD_CHUNK = 128

VMEM_LIMIT = 56 * 1024 * 1024


def _cp(*sem):
    return pltpu.CompilerParams(dimension_semantics=sem, vmem_limit_bytes=VMEM_LIMIT)


def _dot(a, b):
    return jnp.dot(a.astype(BF16), b.astype(BF16), preferred_element_type=F32)


def _dot_nt(a, b):
    return lax.dot_general(a.astype(BF16), b.astype(BF16), (((1,), (1,)), ((), ())),
                           preferred_element_type=F32)


def _split3(a):
    hi = a.astype(BF16)
    r = a - hi.astype(F32)
    mid = r.astype(BF16)
    lo = (r - mid.astype(F32)).astype(BF16)
    return hi, mid, lo


def _dot_hi(a, b):
    ah, am, _ = _split3(a)
    bh, bm, _ = _split3(b)
    f = lambda x, y: jnp.dot(x, y, preferred_element_type=F32)
    return f(ah, bh) + (f(ah, bm) + f(am, bh))


def _dot_hi_nt(a, b):
    ah, am, _ = _split3(a)
    bh, bm, _ = _split3(b)
    f = lambda x, y: lax.dot_general(x, y, (((1,), (1,)), ((), ())), preferred_element_type=F32)
    return f(ah, bh) + (f(ah, bm) + f(am, bh))


def _dot_exact_lhs(m_bf16, a):
    h, m, l = _split3(a)
    f = lambda y: jnp.dot(m_bf16, y, preferred_element_type=F32)
    return f(h) + (f(m) + f(l))


def _dot_exact_rhs(a, m_bf16):
    h, m, l = _split3(a)
    f = lambda y: jnp.dot(y, m_bf16, preferred_element_type=F32)
    return f(h) + (f(m) + f(l))


def _sigmoid(x):
    return 0.5 * jnp.tanh(0.5 * x) + 0.5


def _silu(x):
    return x * _sigmoid(x)


def _rms(x, g):
    return x * lax.rsqrt(jnp.mean(x * x, axis=-1, keepdims=True) + EPS) * g


def _mod_row(nb):
    def f(i, nbatch):
        return jnp.where(i % nb == LAT_BLKS, nbatch, i // nb)
    return f


def _flat_blk(i, nb):
    return (i // nb) * NBLK + i % nb


def _ada_kernel(c_ref, w_ref, b_ref, o_ref):
    c = c_ref[...]
    o_ref[...] = _dot_hi(_silu(c), w_ref[0]) + b_ref[0]


def _ada(cvec, w_ada, b_ada3, l):
    rows = cvec.shape[0]
    return pl.pallas_call(
        _ada_kernel,
        grid=(6,),
        in_specs=[pl.BlockSpec((rows, D), lambda k: (0, 0)),
                  pl.BlockSpec((1, D, D), lambda k: (l, 0, k)),
                  pl.BlockSpec((1, 1, D), lambda k: (l, 0, k))],
        out_specs=pl.BlockSpec((rows, D), lambda k: (0, k)),
        out_shape=jax.ShapeDtypeStruct((rows, 6 * D), F32),
        compiler_params=_cp("arbitrary"),
        name="ada",
    )(cvec, w_ada, b_ada3)


def _inproj_kernel(x_ref, g_ref, sh_ref, sc_ref, wm, wn, ws, w2, wg, om, on, os_, o2, og):
    h = (_rms(x_ref[...], g_ref[...]) * (1.0 + sc_ref[0]) + sh_ref[0]).astype(BF16)
    om[...] = jnp.dot(h, wm[...], preferred_element_type=F32)
    on[...] = jnp.dot(h, wn[...], preferred_element_type=F32).astype(BF16)
    os_[...] = jnp.dot(h, ws[...], preferred_element_type=F32)
    o2[...] = jnp.dot(h, w2[...], preferred_element_type=F32)
    og[...] = jnp.dot(h, wg[...], preferred_element_type=F32)


def _inproj(x2d, g, mod3, ws, nbatch):
    rows = x2d.shape[0]
    nblk = rows // ROW_BLK
    mrow = _mod_row(NBLK)
    full = lambda w: pl.BlockSpec(w.shape, lambda i: (0, 0))
    widths = (MLA_W, NA_W, S5_W, M2_W, GATE_W)
    dts = (F32, BF16, F32, F32, F32)
    return pl.pallas_call(
        _inproj_kernel,
        grid=(nblk,),
        in_specs=[pl.BlockSpec((ROW_BLK, D), lambda i: (i, 0)),
                  pl.BlockSpec((1, D), lambda i: (0, 0)),
                  pl.BlockSpec((1, 1, D), lambda i: (mrow(i, nbatch), 0, 0)),
                  pl.BlockSpec((1, 1, D), lambda i: (mrow(i, nbatch), 0, 1))]
                 + [full(w) for w in ws],
        out_specs=[pl.BlockSpec((ROW_BLK, w), lambda i: (i, 0)) for w in widths],
        out_shape=[jax.ShapeDtypeStruct((rows, w), dt) for w, dt in zip(widths, dts)],
        compiler_params=_cp("arbitrary"),
        name="inproj",
    )(x2d, g, mod3, mod3, *ws)


def _mla_prep_kernel(p_ref, gq_ref, gkv_ref, wq_ref, wqr_ref, wk_ref, wv_ref, e_ref, vone_ref,
                     cos_ref, sin_ref, ck_ref, sk_ref, q_out, k_out, v_out):
    p = p_ref[...]
    cqn = _rms(p[:, :256], gq_ref[...]).astype(BF16)
    q = jnp.dot(cqn, wq_ref[...], preferred_element_type=F32)
    qr = jnp.dot(cqn, wqr_ref[...], preferred_element_type=F32)
    q_out[...] = (q * cos_ref[...] + qr * sin_ref[...]).astype(BF16)
    ckvn = _rms(p[:, 256:384], gkv_ref[...]).astype(BF16)
    kro = (p[:, 384:416] * ck_ref[...] + p[:, 416:448] * sk_ref[...]).astype(BF16)
    k = jnp.dot(ckvn, wk_ref[...], preferred_element_type=F32) + jnp.dot(kro, e_ref[...], preferred_element_type=F32)
    k_out[...] = k.astype(BF16)
    v_out[...] = (jnp.dot(ckvn, wv_ref[...], preferred_element_type=F32) + vone_ref[...]).astype(BF16)


def _mla_prep(p_mla, gq, gkv, wq, wqr, wk, wv, e, vone, cos_q, sin_q, cos_k, sin_k):
    rows = p_mla.shape[0]
    full = lambda w: pl.BlockSpec(w.shape, lambda i: (0, 0))
    tab = lambda w: pl.BlockSpec((ROW_BLK, w), lambda i: (i % NBLK, 0))
    return pl.pallas_call(
        _mla_prep_kernel,
        grid=(rows // ROW_BLK,),
        in_specs=[pl.BlockSpec((ROW_BLK, MLA_W), lambda i: (i, 0)),
                  full(gq), full(gkv), full(wq), full(wqr), full(wk), full(wv), full(e), full(vone),
                  tab(512), tab(512), tab(32), tab(32)],
        out_specs=[pl.BlockSpec((ROW_BLK, 512), lambda i: (i, 0))] * 3,
        out_shape=[jax.ShapeDtypeStruct((rows, 512), BF16)] * 3,
        compiler_params=_cp("arbitrary"),
        name="mla_prep",
    )(p_mla, gq, gkv, wq, wqr, wk, wv, e, vone, cos_q, sin_q, cos_k, sin_k)


def _mla_attn_kernel(q_ref, k_ref, v_ref, o_ref, *, has_ctx):
    lane = lax.broadcasted_iota(jnp.int32, (ROW_BLK, 128), 1)

    def run(k0, nk):
        for hp in range(2):
            pv = []
            for h in (2 * hp, 2 * hp + 1):
                qh = q_ref[0, :, 128 * h:128 * h + 128]
                kh = k_ref[0, k0:k0 + nk, 128 * h:128 * h + 128]
                s = _dot_nt(qh, kh)
                p = jnp.exp2(s - jnp.max(s, axis=-1, keepdims=True)).astype(BF16)
                pv.append(jnp.dot(p, v_ref[0, k0:k0 + nk, 128 * h:128 * h + 128], preferred_element_type=F32))
            oa, ob = pv
            o = jnp.where(lane < 64, oa * (1.0 / oa[:, 64:65]), ob * (1.0 / ob[:, 0:1]))
            o_ref[0, :, 128 * hp:128 * hp + 128] = o.astype(o_ref.dtype)

    if has_ctx:
        j = pl.program_id(1)

        @pl.when(j < LAT_BLKS)
        def _():
            run(0, N)

        @pl.when(j == LAT_BLKS)
        def _():
            run(T, LC)
    else:
        run(0, N)


def _mla_attn(q, k, v, has_ctx):
    nbatch = q.shape[0]
    nb = NBLK if has_ctx else LAT_BLKS
    return pl.pallas_call(
        functools.partial(_mla_attn_kernel, has_ctx=has_ctx),
        grid=(nbatch, nb),
        in_specs=[pl.BlockSpec((1, ROW_BLK, 512), lambda b, j: (b, j, 0)),
                  pl.BlockSpec((1, N, 512), lambda b, j: (b, 0, 0)),
                  pl.BlockSpec((1, N, 512), lambda b, j: (b, 0, 0))],
        out_specs=pl.BlockSpec((1, ROW_BLK, 256), lambda b, j: (b, j, 0)),
        out_shape=jax.ShapeDtypeStruct((nbatch, N, 256), BF16),
        compiler_params=_cp("arbitrary", "arbitrary"),
        name="mla_attn",
    )(q, k, v)


def _na_kernel(q_ref, kc_ref, k0_ref, k1_ref, k2_ref, vc_ref, v0_ref, v1_ref, v2_ref, b_ref, o_ref, *, has_ctx):
    scale = jnp.asarray(0.125, BF16)

    lane = lax.broadcasted_iota(jnp.int32, (ROW_BLK, 128), 1)

    def heads(win):
        kws = (k0_ref, k1_ref, k2_ref)
        vws = (v0_ref, v1_ref, v2_ref)
        for hp in range(2):
            sl = slice(128 * hp, 128 * hp + 128)
            qp = q_ref[0, :, sl] * scale
            outs = []
            for hh in range(2):
                h = 2 * hp + hh
                qh = jnp.where((lane < 64) if hh == 0 else (lane >= 64), qp, jnp.zeros_like(qp))
                s_c = _dot_nt(qh, kc_ref[0, :, sl])
                s_w = []
                smax = s_c
                if win:
                    for i in range(3):
                        s = _dot_nt(qh, kws[i][0, :, sl]) + b_ref[0, h, :, 256 * i:256 * i + 256]
                        s_w.append(s)
                        smax = jnp.maximum(smax, s)
                m = jnp.max(smax, axis=-1, keepdims=True)
                p = jnp.exp(s_c - m)
                psum = p
                o = _dot(p, vc_ref[0, :, sl])
                for i, s in enumerate(s_w):
                    p = jnp.exp(s - m)
                    psum = psum + p
                    o = o + _dot(p, vws[i][0, :, sl])
                outs.append(o * (1.0 / jnp.sum(psum, axis=-1, keepdims=True)))
            o_ref[0, :, sl] = jnp.where(lane < 64, outs[0], outs[1]).astype(o_ref.dtype)

    if has_ctx:
        g = pl.program_id(0)

        @pl.when(g < LAT_BLKS)
        def _():
            heads(True)

        @pl.when(g == LAT_BLKS)
        def _():
            heads(False)
    else:
        heads(True)


def _na_attn(qkv, bias, has_ctx):
    nbatch = qkv.shape[0]
    ng = NBLK if has_ctx else LAT_BLKS
    j0 = lambda g: jnp.clip(g - 1, 0, LAT_BLKS - 3)
    blk = lambda f: pl.BlockSpec((1, ROW_BLK, 256), f)
    return pl.pallas_call(
        functools.partial(_na_kernel, has_ctx=has_ctx),
        grid=(ng, nbatch),
        in_specs=[blk(lambda g, b: (b, g, 0)),
                  blk(lambda g, b: (b, LAT_BLKS, 1)),
                  blk(lambda g, b: (b, j0(g), 1)),
                  blk(lambda g, b: (b, j0(g) + 1, 1)),
                  blk(lambda g, b: (b, j0(g) + 2, 1)),
                  blk(lambda g, b: (b, LAT_BLKS, 2)),
                  blk(lambda g, b: (b, j0(g), 2)),
                  blk(lambda g, b: (b, j0(g) + 1, 2)),
                  blk(lambda g, b: (b, j0(g) + 2, 2)),
                  pl.BlockSpec((1, 4, ROW_BLK, 768), lambda g, b: (jnp.minimum(g, LAT_BLKS - 1), 0, 0, 0))],
        out_specs=blk(lambda g, b: (b, g, 0)),
        out_shape=jax.ShapeDtypeStruct((nbatch, N, 256), BF16),
        compiler_params=_cp("arbitrary", "arbitrary"),
        name="na_attn",
    )(qkv, qkv, qkv, qkv, qkv, qkv, qkv, qkv, qkv, bias)


def _na_structure():
    ar = np.zeros((8, 4, 12, 15), np.float32)
    rowmask = np.zeros((8, 4, 12), bool)
    for g in range(8):
        k0 = 4 * min(max(g - 1, 0), 5)
        for qr in range(4):
            r = 4 * g + qr
            rs = min(max(r - 4, 0), 24)
            for kk in range(12):
                kr = k0 + kk
                if rs <= kr < rs + 8:
                    ar[g, qr, kk, kr - r + 7] = 1.0
                    rowmask[g, qr, kk] = True
    ac = np.zeros((64, 64, 31), np.float32)
    colmask = np.zeros((64, 64), bool)
    for c in range(64):
        cs = min(max(c - 8, 0), 48)
        for kc in range(cs, cs + 16):
            ac[c, kc, kc - c + 15] = 1.0
            colmask[c, kc] = True
    valid = rowmask[:, :, None, :, None] & colmask[None, None, :, None, :]
    negmask = np.where(valid, 0.0, NEG).astype(np.float32).reshape(8, 1, 4, 64, 768)
    arx = np.repeat(ar, 64, axis=2)
    return arx, ac, negmask


_NA_ARX, _NA_AC, _NA_NEG = _na_structure()


def _na_bias(rpb):
    r = rpb.astype(F32)
    tz = sum(r[:, :, j][:, :, None, None] * _NA_AC[None, None, :, :, j] for j in range(31))
    tzx = jnp.tile(tz, (1, 1, 1, 12))
    bias = sum(_NA_ARX[:, None, :, None, :, i] * tzx[None, :, i, None] for i in range(15)) + _NA_NEG
    return bias.reshape(8, 4, 256, 768)


def _s5_kernel(uf_ref, ub_ref, bf_ref, bb_ref, af_ref, ab_ref, cf_ref, cb_ref, yf_ref, yb_ref,
               hf, hb, buf_f, buf_b, tm_f, tm_b):
    i = pl.program_id(0)
    half = 1024
    nb = uf_ref.shape[0]

    @pl.when(i == 0)
    def _():
        hf[...] = jnp.zeros_like(hf)
        hb[...] = jnp.zeros_like(hb)

    def expand(u_ref, tm, b_ref, buf):
        for b in range(nb):
            for c in range(2):
                tm[c, pl.ds(b, S5_CHUNK, stride=nb), :] = u_ref[b, :, 128 * c:128 * c + 128]
        buf[...] = _dot(jnp.concatenate([tm[0], tm[1]], axis=1), b_ref[...])

    expand(uf_ref, tm_f, bf_ref, buf_f)
    expand(ub_ref, tm_b, bb_ref, buf_b)

    def one(buf, a_ref, row, hr, hi):
        bu_r = buf[pl.ds(row, nb), :half]
        bu_i = buf[pl.ds(row, nb), half:]
        ar = a_ref[:, :half]
        ai = a_ref[:, half:]
        nr = ar * hr - ai * hi + bu_r
        ni = ar * hi + ai * hr + bu_i
        buf[pl.ds(row, nb), :half] = nr
        buf[pl.ds(row, nb), half:] = ni
        return nr, ni

    def step(k, carry):
        fr, fi, br, bi = carry
        rf = pl.multiple_of(k * nb, nb)
        rb = pl.multiple_of((S5_CHUNK - 1 - k) * nb, nb)
        fr, fi = one(buf_f, af_ref, rf, fr, fi)
        br, bi = one(buf_b, ab_ref, rb, br, bi)
        return fr, fi, br, bi

    init = (hf[:, :half], hf[:, half:], hb[:, :half], hb[:, half:])
    fr, fi, br, bi = lax.fori_loop(0, S5_CHUNK, step, init)
    hf[:, :half] = fr
    hf[:, half:] = fi
    hb[:, :half] = br
    hb[:, half:] = bi

    def readout(buf, c_ref, tm, y_ref):
        y = _dot(buf[...], c_ref[...])
        for c in range(2):
            tm[c] = y[:, 128 * c:128 * c + 128]
        for b in range(nb):
            for c in range(2):
                y_ref[b, :, 128 * c:128 * c + 128] = tm[c, pl.ds(b, S5_CHUNK, stride=nb), :]

    readout(buf_f, cf_ref, tm_f, yf_ref)
    readout(buf_b, cb_ref, tm_b, yb_ref)


def _s5_scan(u, bmats, avecs, cmats):
    nbatch = u.shape[0]
    cr = S5_CHUNK * nbatch
    nch = N // S5_CHUNK
    nctx = LC // S5_CHUNK
    fidx = lambda i: jnp.where(i < nctx, nch - nctx + i, i - nctx)
    bidx = lambda i: nch - 1 - i
    full = lambda w: pl.BlockSpec(w.shape, lambda i: (0, 0))
    ublk = lambda f: pl.BlockSpec((nbatch, S5_CHUNK, 256), lambda i: (0, f(i), 0))
    return pl.pallas_call(
        _s5_kernel,
        grid=(nch,),
        in_specs=[ublk(fidx), ublk(bidx),
                  full(bmats[0]), full(bmats[1]), full(avecs[0]), full(avecs[1]),
                  full(cmats[0]), full(cmats[1])],
        out_specs=[ublk(fidx), ublk(bidx)],
        out_shape=[jax.ShapeDtypeStruct((nbatch, N, 256), F32)] * 2,
        scratch_shapes=[pltpu.VMEM((nbatch, 2048), F32), pltpu.VMEM((nbatch, 2048), F32),
                        pltpu.VMEM((cr, 2048), F32), pltpu.VMEM((cr, 2048), F32),
                        pltpu.VMEM((2, cr, 128), F32), pltpu.VMEM((2, cr, 128), F32)],
        compiler_params=_cp("arbitrary"),
        name="s5_scan",
    )(u, u, bmats[0], bmats[1], avecs[0], avecs[1], cmats[0], cmats[1])


def _s5_glu_kernel(u_ref, yf_ref, yb_ref, d_ref, w_ref, b_ref, o_ref):
    y = d_ref[...] * u_ref[...] + yf_ref[...] + yb_ref[...]
    z = y * (0.5 * (1.0 + jnp.tanh(math.sqrt(2.0 / math.pi) * (y + 0.044715 * (y * y * y)))))
    o_ref[...] = (z * _sigmoid(_dot(z, w_ref[...]) + b_ref[...])).astype(o_ref.dtype)


def _s5_glu(u_tb, yf, yb, d, w, b):
    rows = u_tb.shape[0]
    rb = 512
    blk = pl.BlockSpec((rb, 256), lambda i: (i, 0))
    full = lambda a: pl.BlockSpec(a.shape, lambda i: (0, 0))
    return pl.pallas_call(
        _s5_glu_kernel,
        grid=(rows // rb,),
        in_specs=[blk, blk, blk, full(d), full(w), full(b)],
        out_specs=blk,
        out_shape=jax.ShapeDtypeStruct((rows, 256), BF16),
        compiler_params=_cp("arbitrary"),
        name="s5_glu",
    )(u_tb, yf, yb, d, w, b)


def _s5_params(a_re, a_im, log_step, b_re, b_im, c_re, c_im, nbatch):
    eye = jnp.eye(16, dtype=F32)
    bmats, avecs, cmats = [], [], []
    for d in range(2):
        a = lax.complex(a_re[d].astype(F32), a_im[d].astype(F32))
        abar = jnp.exp(jnp.exp(log_step[d].astype(F32))[:, None] * a)
        bbar = ((abar - 1.0) / a)[:, :, None] * lax.complex(b_re[d].astype(F32), b_im[d].astype(F32))
        blk_in = lambda m: (eye[:, None, :, None] * jnp.transpose(m, (0, 2, 1))[:, :, None, :]).reshape(256, 1024)
        bmats.append(jnp.concatenate([blk_in(jnp.real(bbar)), blk_in(jnp.imag(bbar))], axis=1).astype(BF16))
        avec = jnp.concatenate([jnp.real(abar).reshape(1, 1024), jnp.imag(abar).reshape(1, 1024)], axis=1)
        avecs.append(jnp.broadcast_to(avec, (nbatch, 2048)))
        blk_out = lambda m: (eye[:, None, :, None] * jnp.transpose(m, (0, 2, 1))[:, :, None, :]).reshape(1024, 256)
        cmats.append(jnp.concatenate([blk_out(c_re[d].astype(F32)), -blk_out(c_im[d].astype(F32))], axis=0).astype(BF16))
    return bmats, avecs, cmats


def _m2_prep_kernel(x_ref, dt_ref, w_ref, b_ref, dtb_ref, xo_ref, dto_ref, dtt_ref, bmt_ref):
    c = pl.program_id(1)
    x = x_ref[0]
    t = lax.broadcasted_iota(jnp.int32, x.shape, 0)
    m2 = ((t >= 2) & (t < T)) | (t >= T + 2)
    m1 = ((t >= 1) & (t < T)) | (t >= T + 1)
    p1 = (t <= T - 2) | ((t >= T) & (t <= N - 2))
    w = w_ref[...]
    y = (w[0:1] * jnp.where(m2, pltpu.roll(x, 2, 0), 0.0)
         + w[1:2] * jnp.where(m1, pltpu.roll(x, 1, 0), 0.0)
         + w[2:3] * x
         + w[3:4] * jnp.where(p1, pltpu.roll(x, N - 1, 0), 0.0)) + b_ref[...]
    act = _silu(y)
    xo_ref[0] = act

    @pl.when(c == 0)
    def _():
        v = dt_ref[0] + dtb_ref[...]
        dt = jnp.maximum(v, 0.0) + jnp.log1p(jnp.exp(-jnp.abs(v)))
        dto_ref[0] = dt
        dtt_ref[0] = dt.T

    @pl.when(c == 1)
    def _():
        bmt_ref[0] = act.T.astype(BF16)


def _m2_prep(p_m2, conv_w, conv_b, dtb):
    nbatch = p_m2.shape[0]
    return pl.pallas_call(
        _m2_prep_kernel,
        grid=(nbatch, 3),
        in_specs=[pl.BlockSpec((1, N, 256), lambda b, c: (b, 0, 1 + c)),
                  pl.BlockSpec((1, N, 128), lambda b, c: (b, 0, 8)),
                  pl.BlockSpec((4, 256), lambda b, c: (0, c)),
                  pl.BlockSpec((1, 256), lambda b, c: (0, c)),
                  pl.BlockSpec((1, 128), lambda b, c: (0, 0))],
        out_specs=[pl.BlockSpec((1, N, 256), lambda b, c: (b, 0, c)),
                   pl.BlockSpec((1, N, 128), lambda b, c: (b, 0, 0)),
                   pl.BlockSpec((1, 128, N), lambda b, c: (b, 0, 0)),
                   pl.BlockSpec((1, 256, N), lambda b, c: (b, 0, 0))],
        out_shape=[jax.ShapeDtypeStruct((nbatch, N, 768), F32),
                   jax.ShapeDtypeStruct((nbatch, N, 128), F32),
                   jax.ShapeDtypeStruct((nbatch, 128, N), F32),
                   jax.ShapeDtypeStruct((nbatch, 256, N), BF16)],
        compiler_params=_cp("arbitrary", "arbitrary"),
        name="m2_prep",
    )(p_m2, p_m2, conv_w, conv_b, dtb)


def _ssd_kernel(xf_ref, dtf_ref, dttf_ref, bmtf_ref, xb_ref, dtb_ref, dttb_ref, bmtb_ref, a_ref, acol_ref,
                yf_ref, yb_ref, hs):
    i = pl.program_id(1)
    L = SSD_CHUNK

    @pl.when(i == 0)
    def _():
        hs[...] = jnp.zeros_like(hs)

    li = lax.broadcasted_iota(jnp.int32, (L, L), 0)
    si = lax.broadcasted_iota(jnp.int32, (L, L), 1)

    def direction(x_ref, dt_ref, dtt_ref, bmt_ref, y_ref, d, causal, causal_t, last):
        cz = jnp.where(causal, 1.0, 0.0).astype(BF16)
        czt = jnp.where(causal_t, 1.0, 0.0).astype(BF16)
        xbc = x_ref[0]
        dt = dt_ref[0]
        cum = _dot_exact_lhs(cz, dt * a_ref[...])
        dtt = dtt_ref[0]
        cumt = _dot_exact_rhs(dtt * acol_ref[...], czt)
        for g in range(2):
            bm = xbc[:, 256 + 128 * g:384 + 128 * g]
            cm = xbc[:, 512 + 128 * g:640 + 128 * g]
            gmat = _dot_nt(cm, bm)
            bmt = bmt_ref[0, 128 * g:128 * g + 128, :]
            cmb = cm.astype(BF16)
            for hh in range(2):
                h = 2 * g + hh
                ln = 4 * d + h
                ccol = cum[:, ln:ln + 1]
                crow = cumt[ln:ln + 1, :]
                decay = jnp.where(causal, jnp.exp(ccol - crow), 0.0)
                mmat = gmat * decay * dtt[ln:ln + 1, :]
                xh = xbc[:, 64 * h:64 * h + 64]
                yd = _dot(mmat, xh)
                clast = cum[last:last + 1, ln:ln + 1]
                wcol = jnp.exp(clast - ccol) * dt[:, ln:ln + 1]
                st = jnp.dot(bmt, (xh * wcol).astype(BF16), preferred_element_type=F32)
                hprev = hs[ln]
                yo = jnp.dot(cmb, hprev.astype(BF16), preferred_element_type=F32) * jnp.exp(ccol)
                hs[ln] = hprev * jnp.exp(clast) + st
                y_ref[0, :, 64 * h:64 * h + 64] = yd + yo

    direction(xf_ref, dtf_ref, dttf_ref, bmtf_ref, yf_ref, 0, si <= li, li <= si, L - 1)
    direction(xb_ref, dtb_ref, dttb_ref, bmtb_ref, yb_ref, 1, si >= li, li >= si, 0)


def _ssd(xbc, dt, dtt, bmt, a_row):
    nbatch = xbc.shape[0]
    nch = N // SSD_CHUNK
    nctx = LC // SSD_CHUNK
    fidx = lambda i: jnp.where(i < nctx, nch - nctx + i, i - nctx)
    bidx = lambda i: nch - 1 - i
    ins = lambda f: [pl.BlockSpec((1, SSD_CHUNK, 768), lambda b, i: (b, f(i), 0)),
                     pl.BlockSpec((1, SSD_CHUNK, 128), lambda b, i: (b, f(i), 0)),
                     pl.BlockSpec((1, 128, SSD_CHUNK), lambda b, i: (b, 0, f(i))),
                     pl.BlockSpec((1, 256, SSD_CHUNK), lambda b, i: (b, 0, f(i)))]
    return pl.pallas_call(
        _ssd_kernel,
        grid=(nbatch, nch),
        in_specs=ins(fidx) + ins(bidx) + [pl.BlockSpec((1, 128), lambda b, i: (0, 0)),
                                          pl.BlockSpec((128, 1), lambda b, i: (0, 0))],
        out_specs=[pl.BlockSpec((1, SSD_CHUNK, 256), lambda b, i: (b, fidx(i), 0)),
                   pl.BlockSpec((1, SSD_CHUNK, 256), lambda b, i: (b, bidx(i), 0))],
        out_shape=[jax.ShapeDtypeStruct((nbatch, N, 256), F32)] * 2,
        scratch_shapes=[pltpu.VMEM((8, 128, 64), F32)],
        compiler_params=_cp("arbitrary", "arbitrary"),
        name="ssd",
    )(xbc, dt, dtt, bmt, xbc, dt, dtt, bmt, a_row, a_row.reshape(128, 1))


def _m2_norm_kernel(x_ref, z_ref, yf_ref, yb_ref, d_ref, g_ref, o_ref):
    y = d_ref[...] * x_ref[...] + yf_ref[...] + yb_ref[...]
    o_ref[...] = _rms(y * _silu(z_ref[...]), g_ref[...]).astype(o_ref.dtype)


def _m2_norm(xbc2d, p_m2_2d, yf2d, yb2d, dvec, gn):
    rows = xbc2d.shape[0]
    blk = pl.BlockSpec((ROW_BLK, 256), lambda i: (i, 0))
    full = lambda a: pl.BlockSpec(a.shape, lambda i: (0, 0))
    return pl.pallas_call(
        _m2_norm_kernel,
        grid=(rows // ROW_BLK,),
        in_specs=[blk, blk, blk, blk, full(dvec), full(gn)],
        out_specs=blk,
        out_shape=jax.ShapeDtypeStruct((rows, 256), BF16),
        compiler_params=_cp("arbitrary"),
        name="m2_norm",
    )(xbc2d, p_m2_2d, yf2d, yb2d, dvec, gn)


def _merge_kernel(x_ref, oa_ref, on_ref, os_ref, om_ref, gl_ref, wb_ref, wo_ref, gpm_ref, gate1_ref,
                  gpf_ref, sh2_ref, sc2_ref, wr_ref, x1_ref, h2_ref, aff_ref):
    outs = (oa_ref, on_ref, os_ref, om_ref)
    y = None
    for j in range(4):
        t = _sigmoid(gl_ref[:, 1024 * j:1024 * j + 1024]) * jnp.dot(outs[j][...], wb_ref[j], preferred_element_type=F32)
        y = t if y is None else y + t
    y2 = _dot(y, wo_ref[...])
    x1 = x_ref[...] + gate1_ref[0] * _rms(y2, gpm_ref[...])
    x1_ref[...] = x1
    h2 = _rms(x1, gpf_ref[...]) * (1.0 + sc2_ref[0]) + sh2_ref[0]
    h2_ref[...] = h2.astype(BF16)
    logits = _dot_hi_nt(wr_ref[...], h2)
    m = jnp.max(logits, axis=0, keepdims=True)
    e = jnp.exp(logits - m)
    aff_ref[0, 0] = e / jnp.sum(e, axis=0, keepdims=True)


def _merge(x2d, o_mla, o_na, o_s5, o_m2, gl, wb, wo, gpm, mod3, gpf, wrt, nbatch, nb):
    rows = x2d.shape[0]
    mrow = _mod_row(nb)
    fb = lambda i: (_flat_blk(i, nb), 0)
    blk = lambda w: pl.BlockSpec((ROW_BLK, w), fb)
    full = lambda a: pl.BlockSpec(a.shape, lambda i: (0,) * a.ndim)
    modspec = lambda k: pl.BlockSpec((1, 1, D), lambda i: (mrow(i, nbatch), 0, k))
    return pl.pallas_call(
        _merge_kernel,
        grid=(nbatch * nb,),
        in_specs=[blk(D), blk(256), blk(256), blk(256), blk(256), blk(GATE_W),
                  full(wb), full(wo), full(gpm), modspec(2), full(gpf), modspec(3), modspec(4), full(wrt)],
        out_specs=[blk(D), blk(D),
                   pl.BlockSpec((1, 1, N_EXPERTS, ROW_BLK), lambda i: (i // nb, i % nb, 0, 0))],
        out_shape=[jax.ShapeDtypeStruct((rows, D), F32),
                   jax.ShapeDtypeStruct((rows, D), BF16),
                   jax.ShapeDtypeStruct((nbatch, NBLK, N_EXPERTS, ROW_BLK), F32)],
        compiler_params=_cp("arbitrary"),
        name="merge",
    )(x2d, o_mla, o_na, o_s5, o_m2, gl, wb, wo, gpm, mod3, gpf, mod3, mod3, wrt)


def _topk_kernel(aff_ref, slot_ref, *, nk, cap):
    a = aff_ref[0]
    bits = lax.bitcast_convert_type(a, jnp.int32)
    count = lambda m: jnp.sum(jnp.sum(jnp.where(m, 1.0, 0.0), axis=2, keepdims=True), axis=0, keepdims=True)
    thr = jnp.zeros((1, N_EXPERTS, 1), jnp.int32)
    for bit in range(30, -1, -1):
        cand = thr | (1 << bit)
        thr = jnp.where(count(bits >= cand) >= cap, cand, thr)
    gt = bits > thr
    eq = bits == thr
    need = cap - count(gt)

    tri = jnp.where(lax.broadcasted_iota(jnp.int32, (256, 256), 0) <= lax.broadcasted_iota(jnp.int32, (256, 256), 1),
                    1.0, 0.0).astype(BF16)

    def prefix_excl(m):
        incl = jnp.dot(m.reshape(nk * N_EXPERTS, 256).astype(BF16), tri,
                       preferred_element_type=F32).reshape(nk, N_EXPERTS, 256)
        offs = []
        run = jnp.zeros((1, N_EXPERTS, 1), F32)
        for k in range(nk):
            offs.append(run)
            run = run + incl[k:k + 1, :, 255:256]
        off = offs[0] if nk == 1 else jnp.concatenate(offs, axis=0)
        return incl - m + off

    eqf = jnp.where(eq, 1.0, 0.0)
    sel = jnp.where(gt, 1.0, jnp.where(eq & (prefix_excl(eqf) < need), 1.0, 0.0))
    slot = jnp.where(sel > 0.5, prefix_excl(sel), -1.0)
    slot_ref[0] = slot.astype(jnp.int32)


def _topk(aff, blk0, nk, cap):
    nbatch = aff.shape[0]
    return pl.pallas_call(
        functools.partial(_topk_kernel, nk=nk, cap=cap),
        grid=(nbatch,),
        in_specs=[pl.BlockSpec((1, nk, N_EXPERTS, ROW_BLK), lambda b: (b, blk0, 0, 0))],
        out_specs=pl.BlockSpec((1, nk, N_EXPERTS, ROW_BLK), lambda b: (b, 0, 0, 0)),
        out_shape=jax.ShapeDtypeStruct((nbatch, nk, N_EXPERTS, ROW_BLK), jnp.int32),
        compiler_params=_cp("arbitrary"),
        name="topk",
    )(aff)


def _expert_kernel(*refs, has_ctx):
    if has_ctx:
        (hl_ref, sl_ref, al_ref, hc_ref, sc_ref, ac_ref, wg_ref, wu_ref, wd_ref,
         yl_ref, yc_ref, wgb, wub, wdb, xc_all, gc_all) = refs
    else:
        hl_ref, sl_ref, al_ref, wg_ref, wu_ref, wd_ref, yl_ref, wgb, wub, wdb = refs
    e = pl.program_id(0)
    b = pl.program_id(1)

    @pl.when(b == 0)
    def _():
        wgb[...] = wg_ref[0, 0].astype(BF16)
        wub[...] = wu_ref[0, 0].astype(BF16)
        wdb[...] = wd_ref[0, 0].astype(BF16)

    def gather(h_ref, slot_ref, aff_ref, nk, cap):
        r_iota = lax.broadcasted_iota(jnp.int32, (cap, ROW_BLK), 0)
        xs = jnp.zeros((cap, D), F32)
        gs = jnp.zeros((cap, 1), F32)
        for k in range(nk):
            pm = slot_ref[0, k, pl.ds(e, 1), :] == r_iota
            xs = xs + jnp.dot(jnp.where(pm, 1.0, 0.0).astype(BF16), h_ref[0, ROW_BLK * k:ROW_BLK * k + ROW_BLK, :],
                              preferred_element_type=F32)
            gs = gs + jnp.sum(jnp.where(pm, aff_ref[0, k, pl.ds(e, 1), :], 0.0), axis=1, keepdims=True)
        return xs.astype(BF16), gs

    def ffn(xsb):
        gt = jnp.dot(xsb, wgb[...], preferred_element_type=F32)
        up = jnp.dot(xsb, wub[...], preferred_element_type=F32)
        return jnp.dot((_silu(gt) * up).astype(BF16), wdb[...], preferred_element_type=F32)

    xs, gs = gather(hl_ref, sl_ref, al_ref, LAT_BLKS, CAP_LAT)
    yl_ref[0, 0] = (ffn(xs) * gs).astype(yl_ref.dtype)
    if has_ctx:
        xs, gs = gather(hc_ref, sc_ref, ac_ref, 1, CAP_CTX)
        row = pl.multiple_of(b * CAP_CTX, CAP_CTX)
        xc_all[pl.ds(row, CAP_CTX), :] = xs
        gc_all[pl.ds(row, CAP_CTX), :] = gs

        @pl.when(b == pl.num_programs(1) - 1)
        def _():
            yc_ref[0] = (ffn(xc_all[...]) * gc_all[...]).astype(yc_ref.dtype)


def _experts(h2, slot_lat, slot_ctx, aff, w_gate, w_up, w_down, l, has_ctx):
    nbatch = h2.shape[0]
    idx4 = lambda e, b: (b, 0, 0, 0)
    in_specs = [pl.BlockSpec((1, T, D), lambda e, b: (b, 0, 0)),
                pl.BlockSpec((1, LAT_BLKS, N_EXPERTS, ROW_BLK), idx4),
                pl.BlockSpec((1, LAT_BLKS, N_EXPERTS, ROW_BLK), idx4)]
    args = [h2, slot_lat, aff]
    out_specs = [pl.BlockSpec((1, 1, CAP_LAT, D), lambda e, b: (b, e, 0, 0))]
    out_shape = [jax.ShapeDtypeStruct((nbatch, N_EXPERTS, CAP_LAT, D), BF16)]
    if has_ctx:
        in_specs += [pl.BlockSpec((1, LC, D), lambda e, b: (b, LAT_BLKS, 0)),
                     pl.BlockSpec((1, 1, N_EXPERTS, ROW_BLK), idx4),
                     pl.BlockSpec((1, 1, N_EXPERTS, ROW_BLK), lambda e, b: (b, LAT_BLKS, 0, 0))]
        args += [h2, slot_ctx, aff]
        out_specs.append(pl.BlockSpec((1, nbatch * CAP_CTX, D), lambda e, b: (e, 0, 0)))
        out_shape.append(jax.ShapeDtypeStruct((N_EXPERTS, nbatch * CAP_CTX, D), BF16))
    wspec = pl.BlockSpec((1, 1, D, D), lambda e, b: (l, e, 0, 0))
    in_specs += [wspec, wspec, wspec]
    args += [w_gate, w_up, w_down]
    scratch = [pltpu.VMEM((D, D), BF16)] * 3
    if has_ctx:
        scratch += [pltpu.VMEM((nbatch * CAP_CTX, D), BF16), pltpu.VMEM((nbatch * CAP_CTX, 1), F32)]
    return pl.pallas_call(
        functools.partial(_expert_kernel, has_ctx=has_ctx),
        grid=(N_EXPERTS, nbatch),
        in_specs=in_specs,
        out_specs=out_specs,
        out_shape=out_shape,
        scratch_shapes=scratch,
        compiler_params=_cp("arbitrary", "arbitrary"),
        name="experts",
    )(*args)


def _combine_kernel(*refs, has_ctx):
    if has_ctx:
        x1_ref, yl_ref, scl_ref, yc_ref, scc_ref, gate2_ref, g_ref, o_ref = refs
    else:
        x1_ref, yl_ref, scl_ref, gate2_ref, g_ref, o_ref = refs

    def comb(sc_ref, y_of, cap):
        lane = lax.broadcasted_iota(jnp.int32, (ROW_BLK, cap), 1)
        sc = sc_ref[0]
        acc = jnp.zeros((ROW_BLK, D), F32)
        for e in range(N_EXPERTS):
            pt = jnp.where(sc[:, e:e + 1] == lane, 1.0, 0.0).astype(BF16)
            acc = acc + jnp.dot(pt, y_of(e), preferred_element_type=F32)
        o_ref[0] = x1_ref[0] + gate2_ref[0] * _rms(acc, g_ref[...])

    lat_y = lambda e: yl_ref[0, e]
    if has_ctx:
        j = pl.program_id(1)

        @pl.when(j < LAT_BLKS)
        def _():
            comb(scl_ref, lat_y, CAP_LAT)

        @pl.when(j == LAT_BLKS)
        def _():
            comb(scc_ref, lambda e: yc_ref[e], CAP_CTX)
    else:
        comb(scl_ref, lat_y, CAP_LAT)


def _combine(x1, y_lat, scol_lat, y_ctx, scol_ctx, mod3, g, has_ctx):
    nbatch = x1.shape[0]
    nb = NBLK if has_ctx else LAT_BLKS
    ntok = N if has_ctx else T
    in_specs = [pl.BlockSpec((1, ROW_BLK, D), lambda b, j: (b, j, 0)),
                pl.BlockSpec((1, N_EXPERTS, CAP_LAT, D), lambda b, j: (b, 0, 0, 0)),
                pl.BlockSpec((1, ROW_BLK, N_EXPERTS), lambda b, j: (b, jnp.minimum(j, LAT_BLKS - 1), 0))]
    args = [x1, y_lat, scol_lat]
    if has_ctx:
        in_specs += [pl.BlockSpec((N_EXPERTS, CAP_CTX, D), lambda b, j: (0, b, 0)),
                     pl.BlockSpec((1, ROW_BLK, N_EXPERTS), lambda b, j: (b, 0, 0))]
        args += [y_ctx, scol_ctx]
    in_specs += [pl.BlockSpec((1, 1, D), lambda b, j: (jnp.where(j == LAT_BLKS, nbatch, b), 0, 5)),
                 pl.BlockSpec((1, D), lambda b, j: (0, 0))]
    args += [mod3, g]
    return pl.pallas_call(
        functools.partial(_combine_kernel, has_ctx=has_ctx),
        grid=(nbatch, nb),
        in_specs=in_specs,
        out_specs=pl.BlockSpec((1, ROW_BLK, D), lambda b, j: (b, j, 0)),
        out_shape=jax.ShapeDtypeStruct((nbatch, ntok, D), F32),
        compiler_params=_cp("arbitrary", "arbitrary"),
        name="combine",
    )(*args)


def _rope_tables():
    f = 1.0 / (ROPE_THETA ** (jnp.arange(0, 16, 2, dtype=F32) / 16))
    pos = jnp.arange(T)
    row, col = pos // GRID_W, pos % GRID_W
    ar = row.astype(F32)[:, None] * f[None, :]
    ac = col.astype(F32)[:, None] * f[None, :]
    cos32 = jnp.concatenate([jnp.cos(ar), jnp.cos(ar), jnp.cos(ac), jnp.cos(ac)], axis=-1)
    sin32 = jnp.concatenate([jnp.sin(ar), jnp.sin(ar), jnp.sin(ac), jnp.sin(ac)], axis=-1)
    cos32 = jnp.concatenate([cos32, jnp.ones((LC, 32), F32)], axis=0)
    sin32 = jnp.concatenate([sin32, jnp.zeros((LC, 32), F32)], axis=0)
    return cos32, sin32


def _rot_cols(w):
    a, b, c, d = w[..., 0:8], w[..., 8:16], w[..., 16:24], w[..., 24:32]
    return jnp.concatenate([-b, a, -d, c], axis=-1)


def _mla_weights(w_uq, w_ukv):
    r = w_uq.shape[0]
    wq3 = w_uq.reshape(r, 4, 96)
    z32 = jnp.zeros((r, 4, 32), F32)
    wq = jnp.concatenate([wq3, z32], axis=-1).reshape(r, 512)
    wqr = jnp.concatenate([jnp.zeros((r, 4, 64), F32), _rot_cols(wq3[..., 64:96]), z32], axis=-1).reshape(r, 512)
    rk = w_ukv.shape[0]
    wkv3 = w_ukv.reshape(rk, 4, 128)
    wk = jnp.concatenate([wkv3[..., :64], jnp.zeros((rk, 4, 64), F32)], axis=-1).reshape(rk, 512)
    z64 = jnp.zeros((rk, 64), F32)
    vh = [wkv3[:, h, 64:] for h in range(4)]
    wv = jnp.concatenate([vh[0], z64, z64, vh[1], vh[2], z64, z64, vh[3]], axis=-1)
    vone = np.tile(np.repeat(np.array([0.0, 1.0, 1.0, 0.0], np.float32), 64), 2)[None, :]
    e = np.zeros((32, 512), np.float32)
    for h in range(4):
        e[np.arange(32), 128 * h + 64 + np.arange(32)] = 1.0
    return (wq.astype(BF16), wqr.astype(BF16), wk.astype(BF16), wv.astype(BF16), jnp.asarray(e, BF16),
            jnp.asarray(vone))


def _inproj_weights(w):
    o = np.cumsum([0, 256, 128, 32, 768, 256, 256, 768, 8, 4096])
    seg = lambda i: w[:, o[i]:o[i + 1]]
    kr = seg(2)
    wm = jnp.concatenate([seg(0), seg(1), kr, _rot_cols(kr), jnp.zeros((D, 64), F32)], axis=1)
    w2 = jnp.concatenate([seg(5), seg(6), seg(7), jnp.zeros((D, 120), F32)], axis=1)
    return [a.astype(BF16) for a in (wm, seg(3), seg(4), w2, seg(8))]


def kernel(x, c, ctx, c_ctx, w_ada, b_ada, g_pre_mix, g_post_mix, g_pre_ffn, g_post_ffn, w_in, mla_g_cq, mla_g_ckv, mla_w_uq, mla_w_ukv, na_rpb, s5_a_re, s5_a_im, s5_log_step, s5_b_re, s5_b_im, s5_c_re, s5_c_im, s5_d, s5_w_glu, s5_b_glu, m2_conv_w, m2_conv_b, m2_a_log, m2_dt_bias, m2_d, m2_g_norm, w_branch, w_out, w_router, w_gate, w_up, w_down):
    nbatch = x.shape[0]
    depth = w_ada.shape[0]
    row2 = lambda v: v.reshape(1, -1).astype(F32)

    xs = jnp.concatenate([x, ctx], axis=1)
    cvec = jnp.concatenate([c, c_ctx[None, :], jnp.zeros((7, D), F32)], axis=0)
    cvec = cvec[: ((nbatch + 1 + 7) // 8) * 8]
    b_ada3 = b_ada.reshape(depth, 1, 6 * D)

    cos32, sin32 = _rope_tables()
    qscale = (64 + 32) ** -0.5 * math.log2(math.e)
    lane_is_rope = np.tile(np.concatenate([np.zeros(64, bool), np.ones(32, bool), np.zeros(32, bool)]), 4)
    pick = lambda t32, fill: jnp.where(lane_is_rope[None, :], jnp.tile(jnp.pad(t32, ((0, 0), (64, 32))), (1, 4)), fill)
    cos_q = pick(cos32, 1.0) * qscale
    sin_q = pick(sin32, 0.0) * qscale

    for l in range(depth):
        has_ctx = l < depth - 1
        nb = NBLK if has_ctx else LAT_BLKS
        mod = _ada(cvec, w_ada, b_ada3, l)
        mod3 = mod.reshape(mod.shape[0], 1, 6 * D)
        x2d = xs.reshape(nbatch * N, D)

        p_mla, p_na, p_s5, p_m2, p_gate = _inproj(x2d, row2(g_pre_mix[l]), mod3, _inproj_weights(w_in[l]), nbatch)

        wq, wqr, wk, wv, e_mat, vone = _mla_weights(mla_w_uq[l], mla_w_ukv[l])
        q, k, v = _mla_prep(p_mla, row2(mla_g_cq[l]), row2(mla_g_ckv[l]), wq, wqr, wk, wv, e_mat, vone,
                            cos_q, sin_q, cos32, sin32)
        o_mla = _mla_attn(q.reshape(nbatch, N, 512), k.reshape(nbatch, N, 512), v.reshape(nbatch, N, 512), has_ctx)

        o_na = _na_attn(p_na.reshape(nbatch, N, NA_W), _na_bias(na_rpb[l]), has_ctx)

        bmats, avecs, cmats = _s5_params(s5_a_re[l], s5_a_im[l], s5_log_step[l], s5_b_re[l], s5_b_im[l],
                                         s5_c_re[l], s5_c_im[l], nbatch)
        yf, yb = _s5_scan(p_s5.reshape(nbatch, N, S5_W), bmats, avecs, cmats)
        o_s5 = _s5_glu(p_s5, yf.reshape(nbatch * N, S5_W), yb.reshape(nbatch * N, S5_W), row2(s5_d[l]),
                       s5_w_glu[l].astype(BF16), row2(s5_b_glu[l]))

        dtb = jnp.pad(m2_dt_bias[l].reshape(1, 8).astype(F32), ((0, 0), (0, 120)))
        xbc, dt, dtt, bmt = _m2_prep(p_m2.reshape(nbatch, N, M2_W), m2_conv_w[l].astype(F32), row2(m2_conv_b[l]), dtb)
        a_row = jnp.pad(-jnp.exp(m2_a_log[l].astype(F32)).reshape(1, 8), ((0, 0), (0, 120)))
        ssd_f, ssd_b = _ssd(xbc, dt, dtt, bmt, a_row)
        o_m2 = _m2_norm(xbc.reshape(nbatch * N, 768), p_m2, ssd_f.reshape(nbatch * N, 256),
                        ssd_b.reshape(nbatch * N, 256), row2(jnp.repeat(m2_d[l], 64)), row2(m2_g_norm[l]))

        x1, h2, aff = _merge(x2d, o_mla.reshape(nbatch * N, 256), o_na.reshape(nbatch * N, 256), o_s5, o_m2, p_gate,
                             w_branch[l].astype(BF16), w_out[l].astype(BF16), row2(g_post_mix[l]), mod3,
                             row2(g_pre_ffn[l]), jnp.transpose(w_router[l]).astype(F32), nbatch, nb)

        slot_lat = _topk(aff, 0, LAT_BLKS, CAP_LAT)
        slot_ctx = _topk(aff, LAT_BLKS, 1, CAP_CTX) if has_ctx else None
        ys = _experts(h2.reshape(nbatch, N, D), slot_lat, slot_ctx, aff, w_gate, w_up, w_down, l, has_ctx)
        scol_lat = jnp.transpose(slot_lat, (0, 1, 3, 2)).reshape(nbatch, T, N_EXPERTS)
        scol_ctx = jnp.transpose(slot_ctx, (0, 1, 3, 2)).reshape(nbatch, LC, N_EXPERTS) if has_ctx else None
        xs = _combine(x1.reshape(nbatch, N, D), ys[0], scol_lat, ys[1] if has_ctx else None, scol_ctx,
                      mod3, row2(g_post_ffn[l]), has_ctx)
    return xs
```

```python
import functools
import math

import numpy as np
import jax
import jax.numpy as jnp
from jax import lax
from jax.experimental import pallas as pl
from jax.experimental.pallas import tpu as pltpu

F32 = jnp.float32
BF16 = jnp.bfloat16

D = 1024
T = 2048
LC = 256
N = T + LC
GRID_W = 64
ROW_BLK = 256
NBLK = N // ROW_BLK
LAT_BLKS = T // ROW_BLK
EPS = 1e-6
N_EXPERTS = 16
CAP_LAT = 2 * T // N_EXPERTS
CAP_CTX = 2 * LC // N_EXPERTS
ROPE_THETA = 10000.0
NEG = -1e30

MLA_W = 512
NA_W = 768
S5_W = 256
M2_W = 1152
GATE_W = 4096

S5_CHUNK = 64
SSD_CHUNK = 128

VMEM_LIMIT = 56 * 1024 * 1024


def _cp(*sem):
    return pltpu.CompilerParams(dimension_semantics=sem, vmem_limit_bytes=VMEM_LIMIT)


def _lspec(a, l, *lead):
    nd = a.ndim - 1 - len(lead)
    return pl.BlockSpec((None,) * (1 + len(lead)) + a.shape[1 + len(lead):], lambda *_: (l,) + lead + (0,) * nd)


def _dot(a, b):
    return jnp.dot(a.astype(BF16), b.astype(BF16), preferred_element_type=F32)


def _dot_nt(a, b):
    return lax.dot_general(a.astype(BF16), b.astype(BF16), (((1,), (1,)), ((), ())),
                           preferred_element_type=F32)


def _split3(a):
    hi = a.astype(BF16)
    r = a - hi.astype(F32)
    mid = r.astype(BF16)
    lo = (r - mid.astype(F32)).astype(BF16)
    return hi, mid, lo


def _dot_hi(a, b):
    ah, am, _ = _split3(a)
    bh, bm, _ = _split3(b)
    f = lambda x, y: jnp.dot(x, y, preferred_element_type=F32)
    return f(ah, bh) + (f(ah, bm) + f(am, bh))


def _dot_hi_nt(a, b):
    ah, am, _ = _split3(a)
    bh, bm, _ = _split3(b)
    f = lambda x, y: lax.dot_general(x, y, (((1,), (1,)), ((), ())), preferred_element_type=F32)
    return f(ah, bh) + (f(ah, bm) + f(am, bh))


def _dot_exact_lhs(m_bf16, a):
    h, m, l = _split3(a)
    f = lambda y: jnp.dot(m_bf16, y, preferred_element_type=F32)
    return f(h) + (f(m) + f(l))


def _dot_exact_rhs(a, m_bf16):
    h, m, l = _split3(a)
    f = lambda y: jnp.dot(y, m_bf16, preferred_element_type=F32)
    return f(h) + (f(m) + f(l))


def _sigmoid(x):
    return 0.5 * jnp.tanh(0.5 * x) + 0.5


def _silu(x):
    return x * _sigmoid(x)


def _rms(x, g):
    return x * lax.rsqrt(jnp.mean(x * x, axis=-1, keepdims=True) + EPS) * g


def _mod_row(nb):
    def f(i, nbatch):
        return jnp.where(i % nb == LAT_BLKS, nbatch, i // nb)
    return f


def _flat_blk(i, nb):
    return (i // nb) * NBLK + i % nb


def _ada_kernel(c_ref, w_ref, b_ref, o_ref):
    c = c_ref[...]
    o_ref[...] = _dot_hi(_silu(c), w_ref[0]) + b_ref[0]


def _ada(cvec, w_ada, b_ada3, l):
    rows = cvec.shape[0]
    return pl.pallas_call(
        _ada_kernel,
        grid=(6,),
        in_specs=[pl.BlockSpec((rows, D), lambda k: (0, 0)),
                  pl.BlockSpec((1, D, D), lambda k: (l, 0, k)),
                  pl.BlockSpec((1, 1, D), lambda k: (l, 0, k))],
        out_specs=pl.BlockSpec((rows, D), lambda k: (0, k)),
        out_shape=jax.ShapeDtypeStruct((rows, 6 * D), F32),
        compiler_params=_cp("arbitrary"),
        name="ada",
    )(cvec, w_ada, b_ada3)


def _inproj_kernel(x_ref, g_ref, sh_ref, sc_ref, wm, wn, ws, w2, wg, om, on, os_, o2, og):
    h = (_rms(x_ref[...], g_ref[...]) * (1.0 + sc_ref[0]) + sh_ref[0]).astype(BF16)
    om[...] = jnp.dot(h, wm[...], preferred_element_type=F32)
    on[...] = jnp.dot(h, wn[...], preferred_element_type=F32).astype(BF16)
    os_[...] = jnp.dot(h, ws[...], preferred_element_type=F32)
    o2[...] = jnp.dot(h, w2[...], preferred_element_type=F32)
    og[...] = jnp.dot(h, wg[...], preferred_element_type=F32)


def _inproj(x2d, g, mod3, ws, nbatch, l):
    rows = x2d.shape[0]
    nblk = rows // ROW_BLK
    mrow = _mod_row(NBLK)
    full = lambda w: _lspec(w, l)
    widths = (MLA_W, NA_W, S5_W, M2_W, GATE_W)
    dts = (F32, BF16, F32, F32, F32)
    return pl.pallas_call(
        _inproj_kernel,
        grid=(nblk,),
        in_specs=[pl.BlockSpec((ROW_BLK, D), lambda i: (i, 0)),
                  full(g),
                  pl.BlockSpec((1, 1, D), lambda i: (mrow(i, nbatch), 0, 0)),
                  pl.BlockSpec((1, 1, D), lambda i: (mrow(i, nbatch), 0, 1))]
                 + [full(w) for w in ws],
        out_specs=[pl.BlockSpec((ROW_BLK, w), lambda i: (i, 0)) for w in widths],
        out_shape=[jax.ShapeDtypeStruct((rows, w), dt) for w, dt in zip(widths, dts)],
        compiler_params=_cp("arbitrary"),
        name="inproj",
    )(x2d, g, mod3, mod3, *ws)


def _mla_prep_kernel(p_ref, gq_ref, gkv_ref, wq_ref, wqr_ref, wk_ref, wv_ref, e_ref, vone_ref,
                     cos_ref, sin_ref, ck_ref, sk_ref, q_out, k_out, v_out):
    p = p_ref[...]
    cqn = _rms(p[:, :256], gq_ref[...]).astype(BF16)
    q = jnp.dot(cqn, wq_ref[...], preferred_element_type=F32)
    qr = jnp.dot(cqn, wqr_ref[...], preferred_element_type=F32)
    q_out[...] = (q * cos_ref[...] + qr * sin_ref[...]).astype(BF16)
    ckvn = _rms(p[:, 256:384], gkv_ref[...]).astype(BF16)
    kro = (p[:, 384:416] * ck_ref[...] + p[:, 416:448] * sk_ref[...]).astype(BF16)
    k = jnp.dot(ckvn, wk_ref[...], preferred_element_type=F32) + jnp.dot(kro, e_ref[...], preferred_element_type=F32)
    k_out[...] = k.astype(BF16)
    v_out[...] = (jnp.dot(ckvn, wv_ref[...], preferred_element_type=F32) + vone_ref[...]).astype(BF16)


def _mla_prep(p_mla, gq, gkv, wq, wqr, wk, wv, e, vone, cos_q, sin_q, cos_k, sin_k, l):
    rows = p_mla.shape[0]
    full = lambda w: _lspec(w, l)
    const = lambda w: pl.BlockSpec(w.shape, lambda i: (0, 0))
    tab = lambda w: pl.BlockSpec((ROW_BLK, w), lambda i: (i % NBLK, 0))
    return pl.pallas_call(
        _mla_prep_kernel,
        grid=(rows // ROW_BLK,),
        in_specs=[pl.BlockSpec((ROW_BLK, MLA_W), lambda i: (i, 0)),
                  full(gq), full(gkv), full(wq), full(wqr), full(wk), full(wv), const(e), const(vone),
                  tab(512), tab(512), tab(32), tab(32)],
        out_specs=[pl.BlockSpec((ROW_BLK, 512), lambda i: (i, 0))] * 3,
        out_shape=[jax.ShapeDtypeStruct((rows, 512), BF16)] * 3,
        compiler_params=_cp("arbitrary"),
        name="mla_prep",
    )(p_mla, gq, gkv, wq, wqr, wk, wv, e, vone, cos_q, sin_q, cos_k, sin_k)


def _mla_attn_kernel(q_ref, k_ref, v_ref, o_ref, *, has_ctx):
    lane = lax.broadcasted_iota(jnp.int32, (ROW_BLK, 128), 1)

    def run(k0, nk):
        for hp in range(2):
            pv = []
            for h in (2 * hp, 2 * hp + 1):
                qh = q_ref[0, :, 128 * h:128 * h + 128]
                kh = k_ref[0, k0:k0 + nk, 128 * h:128 * h + 128]
                s = _dot_nt(qh, kh)
                p = jnp.exp2(s - jnp.max(s, axis=-1, keepdims=True)).astype(BF16)
                pv.append(jnp.dot(p, v_ref[0, k0:k0 + nk, 128 * h:128 * h + 128], preferred_element_type=F32))
            oa, ob = pv
            o = jnp.where(lane < 64, oa * (1.0 / oa[:, 64:65]), ob * (1.0 / ob[:, 0:1]))
            o_ref[0, :, 128 * hp:128 * hp + 128] = o.astype(o_ref.dtype)

    if has_ctx:
        j = pl.program_id(1)

        @pl.when(j < LAT_BLKS)
        def _():
            run(0, N)

        @pl.when(j == LAT_BLKS)
        def _():
            run(T, LC)
    else:
        run(0, N)


def _mla_attn(q, k, v, has_ctx):
    nbatch = q.shape[0]
    nb = NBLK if has_ctx else LAT_BLKS
    return pl.pallas_call(
        functools.partial(_mla_attn_kernel, has_ctx=has_ctx),
        grid=(nbatch, nb),
        in_specs=[pl.BlockSpec((1, ROW_BLK, 512), lambda b, j: (b, j, 0)),
                  pl.BlockSpec((1, N, 512), lambda b, j: (b, 0, 0)),
                  pl.BlockSpec((1, N, 512), lambda b, j: (b, 0, 0))],
        out_specs=pl.BlockSpec((1, ROW_BLK, 256), lambda b, j: (b, j, 0)),
        out_shape=jax.ShapeDtypeStruct((nbatch, N, 256), BF16),
        compiler_params=_cp("arbitrary", "arbitrary"),
        name="mla_attn",
    )(q, k, v)


def _na_kernel(q_ref, kc_ref, k0_ref, k1_ref, k2_ref, vc_ref, v0_ref, v1_ref, v2_ref, b_ref, o_ref, *, has_ctx):
    scale = jnp.asarray(0.125, BF16)

    lane = lax.broadcasted_iota(jnp.int32, (ROW_BLK, 128), 1)
    lane64 = lax.broadcasted_iota(jnp.int32, (GRID_W, 128), 1) < 64
    grp = jnp.minimum(pl.program_id(0), LAT_BLKS - 1)
    krow0 = 4 * jnp.clip(grp - 1, 0, LAT_BLKS - 3)

    def bias_piece(h, i):
        rows = []
        for qr in range(4):
            r = 4 * grp + qr
            rs = jnp.clip(r - 4, 0, 24)
            tiles = []
            for pr in range(2):
                kr = krow0 + (4 * i + 2 * pr)
                t = b_ref[h, jnp.clip(kr - r + 8, 0, 15)]
                ok_a = jnp.logical_and(kr >= rs, kr < rs + 8).astype(jnp.int32)
                ok_b = jnp.logical_and(kr + 1 >= rs, kr + 1 < rs + 8).astype(jnp.int32)
                tiles.append(jnp.where(jnp.where(lane64, ok_a, ok_b) > 0, t, NEG))
            rows.append(jnp.concatenate(tiles, axis=1))
        return jnp.concatenate(rows, axis=0)

    def heads(win):
        kws = (k0_ref, k1_ref, k2_ref)
        vws = (v0_ref, v1_ref, v2_ref)
        for hp in range(2):
            sl = slice(128 * hp, 128 * hp + 128)
            qp = q_ref[0, :, sl] * scale
            outs = []
            for hh in range(2):
                h = 2 * hp + hh
                qh = jnp.where((lane < 64) if hh == 0 else (lane >= 64), qp, jnp.zeros_like(qp))
                s_c = _dot_nt(qh, kc_ref[0, :, sl])
                s_w = []
                smax = s_c
                if win:
                    for i in range(3):
                        s = _dot_nt(qh, kws[i][0, :, sl]) + bias_piece(h, i)
                        s_w.append(s)
                        smax = jnp.maximum(smax, s)
                m = jnp.max(smax, axis=-1, keepdims=True)
                p = jnp.exp(s_c - m)
                psum = p
                o = _dot(p, vc_ref[0, :, sl])
                for i, s in enumerate(s_w):
                    p = jnp.exp(s - m)
                    psum = psum + p
                    o = o + _dot(p, vws[i][0, :, sl])
                outs.append(o * (1.0 / jnp.sum(psum, axis=-1, keepdims=True)))
            o_ref[0, :, sl] = jnp.where(lane < 64, outs[0], outs[1]).astype(o_ref.dtype)

    if has_ctx:
        g = pl.program_id(0)

        @pl.when(g < LAT_BLKS)
        def _():
            heads(True)

        @pl.when(g == LAT_BLKS)
        def _():
            heads(False)
    else:
        heads(True)


def _na_attn(qkv, bias, has_ctx, l):
    nbatch = qkv.shape[0]
    ng = NBLK if has_ctx else LAT_BLKS
    j0 = lambda g: jnp.clip(g - 1, 0, LAT_BLKS - 3)
    blk = lambda f: pl.BlockSpec((1, ROW_BLK, 256), f)
    return pl.pallas_call(
        functools.partial(_na_kernel, has_ctx=has_ctx),
        grid=(ng, nbatch),
        in_specs=[blk(lambda g, b: (b, g, 0)),
                  blk(lambda g, b: (b, LAT_BLKS, 1)),
                  blk(lambda g, b: (b, j0(g), 1)),
                  blk(lambda g, b: (b, j0(g) + 1, 1)),
                  blk(lambda g, b: (b, j0(g) + 2, 1)),
                  blk(lambda g, b: (b, LAT_BLKS, 2)),
                  blk(lambda g, b: (b, j0(g), 2)),
                  blk(lambda g, b: (b, j0(g) + 1, 2)),
                  blk(lambda g, b: (b, j0(g) + 2, 2)),
                  _lspec(bias, l)],
        out_specs=blk(lambda g, b: (b, g, 0)),
        out_shape=jax.ShapeDtypeStruct((nbatch, N, 256), BF16),
        compiler_params=_cp("arbitrary", "arbitrary"),
        name="na_attn",
    )(qkv, qkv, qkv, qkv, qkv, qkv, qkv, qkv, qkv, bias)


def _na_col_structure():
    onehot = np.zeros((31, 64, 64), np.float32)
    colmask = np.zeros((64, 64), bool)
    for c in range(64):
        cs = min(max(c - 8, 0), 48)
        for kc in range(cs, cs + 16):
            onehot[kc - c + 15, c, kc] = 1.0
            colmask[c, kc] = True
    return np.tile(onehot.reshape(31, 4096), (3, 1)), colmask


_NA_COL_ONEHOT, _NA_COLMASK = _na_col_structure()


def _na_tables(rpb):
    nl = rpb.shape[0]
    r = rpb.astype(F32)
    hi = lax.reduce_precision(r, 8, 7)
    mid = lax.reduce_precision(r - hi, 8, 7)
    parts = jnp.stack([hi, mid, r - hi - mid], axis=3).reshape(nl * 60, 93)
    tz = jnp.dot(parts, _NA_COL_ONEHOT, preferred_element_type=F32).reshape(nl, 4, 15, 64, 64)
    tz = jnp.where(_NA_COLMASK, tz, NEG)
    edge = jnp.full((nl, 4, 1, 64, 64), NEG, F32)
    tz17 = jnp.concatenate([edge, tz, edge], axis=2)
    return jnp.concatenate([tz17[:, :, :16], tz17[:, :, 1:]], axis=-1)


def _s5_kernel(uf_ref, ub_ref, bf_ref, bb_ref, af_ref, ab_ref, cf_ref, cb_ref, yf_ref, yb_ref,
               hf, hb, buf_f, buf_b, tm_f, tm_b):
    i = pl.program_id(0)
    half = 1024
    nb = uf_ref.shape[0]

    @pl.when(i == 0)
    def _():
        hf[...] = jnp.zeros_like(hf)
        hb[...] = jnp.zeros_like(hb)

    def expand(u_ref, tm, b_ref, buf):
        for b in range(nb):
            for c in range(2):
                tm[c, pl.ds(b, S5_CHUNK, stride=nb), :] = u_ref[b, :, 128 * c:128 * c + 128]
        buf[...] = _dot(jnp.concatenate([tm[0], tm[1]], axis=1), b_ref[...])

    expand(uf_ref, tm_f, bf_ref, buf_f)
    expand(ub_ref, tm_b, bb_ref, buf_b)

    def one(buf, a_ref, row, hr, hi):
        bu_r = buf[pl.ds(row, nb), :half]
        bu_i = buf[pl.ds(row, nb), half:]
        ar = a_ref[:, :half]
        ai = a_ref[:, half:]
        nr = ar * hr - ai * hi + bu_r
        ni = ar * hi + ai * hr + bu_i
        buf[pl.ds(row, nb), :half] = nr
        buf[pl.ds(row, nb), half:] = ni
        return nr, ni

    def step(k, carry):
        fr, fi, br, bi = carry
        rf = pl.multiple_of(k * nb, nb)
        rb = pl.multiple_of((S5_CHUNK - 1 - k) * nb, nb)
        fr, fi = one(buf_f, af_ref, rf, fr, fi)
        br, bi = one(buf_b, ab_ref, rb, br, bi)
        return fr, fi, br, bi

    init = (hf[:, :half], hf[:, half:], hb[:, :half], hb[:, half:])
    fr, fi, br, bi = lax.fori_loop(0, S5_CHUNK, step, init)
    hf[:, :half] = fr
    hf[:, half:] = fi
    hb[:, :half] = br
    hb[:, half:] = bi

    def readout(buf, c_ref, tm, y_ref):
        y = _dot(buf[...], c_ref[...])
        for c in range(2):
            tm[c] = y[:, 128 * c:128 * c + 128]
        for b in range(nb):
            for c in range(2):
                y_ref[b, :, 128 * c:128 * c + 128] = tm[c, pl.ds(b, S5_CHUNK, stride=nb), :]

    readout(buf_f, cf_ref, tm_f, yf_ref)
    readout(buf_b, cb_ref, tm_b, yb_ref)


def _s5_scan(u, bmat, avec, cmat, l):
    nbatch = u.shape[0]
    cr = S5_CHUNK * nbatch
    nch = N // S5_CHUNK
    nctx = LC // S5_CHUNK
    fidx = lambda i: jnp.where(i < nctx, nch - nctx + i, i - nctx)
    bidx = lambda i: nch - 1 - i
    ublk = lambda f: pl.BlockSpec((nbatch, S5_CHUNK, 256), lambda i: (0, f(i), 0))
    return pl.pallas_call(
        _s5_kernel,
        grid=(nch,),
        in_specs=[ublk(fidx), ublk(bidx),
                  _lspec(bmat, l, 0), _lspec(bmat, l, 1), _lspec(avec, l, 0), _lspec(avec, l, 1),
                  _lspec(cmat, l, 0), _lspec(cmat, l, 1)],
        out_specs=[ublk(fidx), ublk(bidx)],
        out_shape=[jax.ShapeDtypeStruct((nbatch, N, 256), F32)] * 2,
        scratch_shapes=[pltpu.VMEM((nbatch, 2048), F32), pltpu.VMEM((nbatch, 2048), F32),
                        pltpu.VMEM((cr, 2048), F32), pltpu.VMEM((cr, 2048), F32),
                        pltpu.VMEM((2, cr, 128), F32), pltpu.VMEM((2, cr, 128), F32)],
        compiler_params=_cp("arbitrary"),
        name="s5_scan",
    )(u, u, bmat, bmat, avec, avec, cmat, cmat)


def _s5_glu_kernel(u_ref, yf_ref, yb_ref, d_ref, w_ref, b_ref, o_ref):
    y = d_ref[...] * u_ref[...] + yf_ref[...] + yb_ref[...]
    z = y * (0.5 * (1.0 + jnp.tanh(math.sqrt(2.0 / math.pi) * (y + 0.044715 * (y * y * y)))))
    o_ref[...] = (z * _sigmoid(_dot(z, w_ref[...]) + b_ref[...])).astype(o_ref.dtype)


def _s5_glu(u_tb, yf, yb, d, w, b, l):
    rows = u_tb.shape[0]
    rb = 512
    blk = pl.BlockSpec((rb, 256), lambda i: (i, 0))
    full = lambda a: _lspec(a, l)
    return pl.pallas_call(
        _s5_glu_kernel,
        grid=(rows // rb,),
        in_specs=[blk, blk, blk, full(d), full(w), full(b)],
        out_specs=blk,
        out_shape=jax.ShapeDtypeStruct((rows, 256), BF16),
        compiler_params=_cp("arbitrary"),
        name="s5_glu",
    )(u_tb, yf, yb, d, w, b)


def _s5_params(a_re, a_im, log_step, b_re, b_im, c_re, c_im, nbatch):
    nl = a_re.shape[0]
    eye = jnp.eye(16, dtype=F32)[:, None, :, None]
    a = lax.complex(a_re.astype(F32), a_im.astype(F32))
    abar = jnp.exp(jnp.exp(log_step.astype(F32))[..., None] * a)
    bbar = ((abar - 1.0) / a)[..., None] * lax.complex(b_re.astype(F32), b_im.astype(F32))
    blk_in = lambda m: (eye * jnp.swapaxes(m, -1, -2)[..., :, :, None, :]).reshape(nl, 2, 256, 1024)
    bmat = jnp.concatenate([blk_in(jnp.real(bbar)), blk_in(jnp.imag(bbar))], axis=-1).astype(BF16)
    avec = jnp.concatenate([jnp.real(abar).reshape(nl, 2, 1, 1024), jnp.imag(abar).reshape(nl, 2, 1, 1024)], axis=-1)
    avec = jnp.broadcast_to(avec, (nl, 2, nbatch, 2048))
    blk_out = lambda m: (eye * jnp.swapaxes(m, -1, -2)[..., :, :, None, :]).reshape(nl, 2, 1024, 256)
    cmat = jnp.concatenate([blk_out(c_re.astype(F32)), -blk_out(c_im.astype(F32))], axis=-2).astype(BF16)
    return bmat, avec, cmat


def _m2_prep_kernel(x_ref, dt_ref, w_ref, b_ref, dtb_ref, xo_ref, dto_ref, dtt_ref, bmt_ref):
    c = pl.program_id(1)
    x = x_ref[0]
    t = lax.broadcasted_iota(jnp.int32, x.shape, 0)
    m2 = ((t >= 2) & (t < T)) | (t >= T + 2)
    m1 = ((t >= 1) & (t < T)) | (t >= T + 1)
    p1 = (t <= T - 2) | ((t >= T) & (t <= N - 2))
    w = w_ref[...]
    y = (w[0:1] * jnp.where(m2, pltpu.roll(x, 2, 0), 0.0)
         + w[1:2] * jnp.where(m1, pltpu.roll(x, 1, 0), 0.0)
         + w[2:3] * x
         + w[3:4] * jnp.where(p1, pltpu.roll(x, N - 1, 0), 0.0)) + b_ref[...]
    act = _silu(y)
    xo_ref[0] = act

    @pl.when(c == 0)
    def _():
        v = dt_ref[0] + dtb_ref[...]
        dt = jnp.maximum(v, 0.0) + jnp.log1p(jnp.exp(-jnp.abs(v)))
        dto_ref[0] = dt
        dtt_ref[0] = dt.T

    @pl.when(c == 1)
    def _():
        bmt_ref[0] = act.T.astype(BF16)


def _m2_prep(p_m2, conv_w, conv_b, dtb, l):
    nbatch = p_m2.shape[0]
    return pl.pallas_call(
        _m2_prep_kernel,
        grid=(nbatch, 3),
        in_specs=[pl.BlockSpec((1, N, 256), lambda b, c: (b, 0, 1 + c)),
                  pl.BlockSpec((1, N, 128), lambda b, c: (b, 0, 8)),
                  pl.BlockSpec((None, 4, 256), lambda b, c: (l, 0, c)),
                  pl.BlockSpec((None, 1, 256), lambda b, c: (l, 0, c)),
                  _lspec(dtb, l)],
        out_specs=[pl.BlockSpec((1, N, 256), lambda b, c: (b, 0, c)),
                   pl.BlockSpec((1, N, 128), lambda b, c: (b, 0, 0)),
                   pl.BlockSpec((1, 128, N), lambda b, c: (b, 0, 0)),
                   pl.BlockSpec((1, 256, N), lambda b, c: (b, 0, 0))],
        out_shape=[jax.ShapeDtypeStruct((nbatch, N, 768), F32),
                   jax.ShapeDtypeStruct((nbatch, N, 128), F32),
                   jax.ShapeDtypeStruct((nbatch, 128, N), F32),
                   jax.ShapeDtypeStruct((nbatch, 256, N), BF16)],
        compiler_params=_cp("arbitrary", "arbitrary"),
        name="m2_prep",
    )(p_m2, p_m2, conv_w, conv_b, dtb)


def _ssd_kernel(xf_ref, dtf_ref, dttf_ref, bmtf_ref, xb_ref, dtb_ref, dttb_ref, bmtb_ref, a_ref, acol_ref,
                yf_ref, yb_ref, hs):
    i = pl.program_id(1)
    L = SSD_CHUNK

    @pl.when(i == 0)
    def _():
        hs[...] = jnp.zeros_like(hs)

    li = lax.broadcasted_iota(jnp.int32, (L, L), 0)
    si = lax.broadcasted_iota(jnp.int32, (L, L), 1)

    def direction(x_ref, dt_ref, dtt_ref, bmt_ref, y_ref, d, causal, causal_t, last):
        cz = jnp.where(causal, 1.0, 0.0).astype(BF16)
        czt = jnp.where(causal_t, 1.0, 0.0).astype(BF16)
        xbc = x_ref[0]
        dt = dt_ref[0]
        cum = _dot_exact_lhs(cz, dt * a_ref[...])
        dtt = dtt_ref[0]
        cumt = _dot_exact_rhs(dtt * acol_ref[...], czt)
        for g in range(2):
            bm = xbc[:, 256 + 128 * g:384 + 128 * g]
            cm = xbc[:, 512 + 128 * g:640 + 128 * g]
            gmat = _dot_nt(cm, bm)
            bmt = bmt_ref[0, 128 * g:128 * g + 128, :]
            cmb = cm.astype(BF16)
            for hh in range(2):
                h = 2 * g + hh
                ln = 4 * d + h
                ccol = cum[:, ln:ln + 1]
                crow = cumt[ln:ln + 1, :]
                decay = jnp.where(causal, jnp.exp(ccol - crow), 0.0)
                mmat = gmat * decay * dtt[ln:ln + 1, :]
                xh = xbc[:, 64 * h:64 * h + 64]
                yd = _dot(mmat, xh)
                clast = cum[last:last + 1, ln:ln + 1]
                wcol = jnp.exp(clast - ccol) * dt[:, ln:ln + 1]
                st = jnp.dot(bmt, (xh * wcol).astype(BF16), preferred_element_type=F32)
                hprev = hs[ln]
                yo = jnp.dot(cmb, hprev.astype(BF16), preferred_element_type=F32) * jnp.exp(ccol)
                hs[ln] = hprev * jnp.exp(clast) + st
                y_ref[0, :, 64 * h:64 * h + 64] = yd + yo

    direction(xf_ref, dtf_ref, dttf_ref, bmtf_ref, yf_ref, 0, si <= li, li <= si, L - 1)
    direction(xb_ref, dtb_ref, dttb_ref, bmtb_ref, yb_ref, 1, si >= li, li >= si, 0)


def _ssd(xbc, dt, dtt, bmt, a_row, a_col, l):
    nbatch = xbc.shape[0]
    nch = N // SSD_CHUNK
    nctx = LC // SSD_CHUNK
    fidx = lambda i: jnp.where(i < nctx, nch - nctx + i, i - nctx)
    bidx = lambda i: nch - 1 - i
    ins = lambda f: [pl.BlockSpec((1, SSD_CHUNK, 768), lambda b, i: (b, f(i), 0)),
                     pl.BlockSpec((1, SSD_CHUNK, 128), lambda b, i: (b, f(i), 0)),
                     pl.BlockSpec((1, 128, SSD_CHUNK), lambda b, i: (b, 0, f(i))),
                     pl.BlockSpec((1, 256, SSD_CHUNK), lambda b, i: (b, 0, f(i)))]
    return pl.pallas_call(
        _ssd_kernel,
        grid=(nbatch, nch),
        in_specs=ins(fidx) + ins(bidx) + [_lspec(a_row, l), _lspec(a_col, l)],
        out_specs=[pl.BlockSpec((1, SSD_CHUNK, 256), lambda b, i: (b, fidx(i), 0)),
                   pl.BlockSpec((1, SSD_CHUNK, 256), lambda b, i: (b, bidx(i), 0))],
        out_shape=[jax.ShapeDtypeStruct((nbatch, N, 256), F32)] * 2,
        scratch_shapes=[pltpu.VMEM((8, 128, 64), F32)],
        compiler_params=_cp("arbitrary", "arbitrary"),
        name="ssd",
    )(xbc, dt, dtt, bmt, xbc, dt, dtt, bmt, a_row, a_col)


def _m2_norm_kernel(x_ref, z_ref, yf_ref, yb_ref, d_ref, g_ref, o_ref):
    y = d_ref[...] * x_ref[...] + yf_ref[...] + yb_ref[...]
    o_ref[...] = _rms(y * _silu(z_ref[...]), g_ref[...]).astype(o_ref.dtype)


def _m2_norm(xbc2d, p_m2_2d, yf2d, yb2d, dvec, gn, l):
    rows = xbc2d.shape[0]
    blk = pl.BlockSpec((ROW_BLK, 256), lambda i: (i, 0))
    full = lambda a: _lspec(a, l)
    return pl.pallas_call(
        _m2_norm_kernel,
        grid=(rows // ROW_BLK,),
        in_specs=[blk, blk, blk, blk, full(dvec), full(gn)],
        out_specs=blk,
        out_shape=jax.ShapeDtypeStruct((rows, 256), BF16),
        compiler_params=_cp("arbitrary"),
        name="m2_norm",
    )(xbc2d, p_m2_2d, yf2d, yb2d, dvec, gn)


def _merge_kernel(x_ref, oa_ref, on_ref, os_ref, om_ref, gl_ref, wb_ref, wo_ref, gpm_ref, gate1_ref,
                  gpf_ref, sh2_ref, sc2_ref, wr_ref, x1_ref, h2_ref, aff_ref):
    outs = (oa_ref, on_ref, os_ref, om_ref)
    y = None
    for j in range(4):
        t = _sigmoid(gl_ref[:, 1024 * j:1024 * j + 1024]) * jnp.dot(outs[j][...], wb_ref[j], preferred_element_type=F32)
        y = t if y is None else y + t
    y2 = _dot(y, wo_ref[...])
    x1 = x_ref[...] + gate1_ref[0] * _rms(y2, gpm_ref[...])
    x1_ref[...] = x1
    h2 = _rms(x1, gpf_ref[...]) * (1.0 + sc2_ref[0]) + sh2_ref[0]
    h2_ref[...] = h2.astype(BF16)
    logits = _dot_hi_nt(wr_ref[...], h2)
    m = jnp.max(logits, axis=0, keepdims=True)
    e = jnp.exp(logits - m)
    aff_ref[0, 0] = e / jnp.sum(e, axis=0, keepdims=True)


def _merge(x2d, o_mla, o_na, o_s5, o_m2, gl, wb, wo, gpm, mod3, gpf, wrt, nbatch, nb, l):
    rows = x2d.shape[0]
    mrow = _mod_row(nb)
    fb = lambda i: (_flat_blk(i, nb), 0)
    blk = lambda w: pl.BlockSpec((ROW_BLK, w), fb)
    full = lambda a: _lspec(a, l)
    modspec = lambda k: pl.BlockSpec((1, 1, D), lambda i: (mrow(i, nbatch), 0, k))
    return pl.pallas_call(
        _merge_kernel,
        grid=(nbatch * nb,),
        in_specs=[blk(D), blk(256), blk(256), blk(256), blk(256), blk(GATE_W),
                  full(wb), full(wo), full(gpm), modspec(2), full(gpf), modspec(3), modspec(4), full(wrt)],
        out_specs=[blk(D), blk(D),
                   pl.BlockSpec((1, 1, N_EXPERTS, ROW_BLK), lambda i: (i // nb, i % nb, 0, 0))],
        out_shape=[jax.ShapeDtypeStruct((rows, D), F32),
                   jax.ShapeDtypeStruct((rows, D), BF16),
                   jax.ShapeDtypeStruct((nbatch, NBLK, N_EXPERTS, ROW_BLK), F32)],
        compiler_params=_cp("arbitrary"),
        name="merge",
    )(x2d, o_mla, o_na, o_s5, o_m2, gl, wb, wo, gpm, mod3, gpf, mod3, mod3, wrt)


def _topk_kernel(aff_ref, slot_ref, *, nk, cap):
    a = aff_ref[0]
    bits = lax.bitcast_convert_type(a, jnp.int32)
    count = lambda m: jnp.sum(jnp.sum(jnp.where(m, 1.0, 0.0), axis=2, keepdims=True), axis=0, keepdims=True)
    thr = jnp.zeros((1, N_EXPERTS, 1), jnp.int32)
    for bit in range(30, -1, -1):
        cand = thr | (1 << bit)
        thr = jnp.where(count(bits >= cand) >= cap, cand, thr)
    gt = bits > thr
    eq = bits == thr
    need = cap - count(gt)

    tri = jnp.where(lax.broadcasted_iota(jnp.int32, (256, 256), 0) <= lax.broadcasted_iota(jnp.int32, (256, 256), 1),
                    1.0, 0.0).astype(BF16)

    def prefix_excl(m):
        incl = jnp.dot(m.reshape(nk * N_EXPERTS, 256).astype(BF16), tri,
                       preferred_element_type=F32).reshape(nk, N_EXPERTS, 256)
        offs = []
        run = jnp.zeros((1, N_EXPERTS, 1), F32)
        for k in range(nk):
            offs.append(run)
            run = run + incl[k:k + 1, :, 255:256]
        off = offs[0] if nk == 1 else jnp.concatenate(offs, axis=0)
        return incl - m + off

    eqf = jnp.where(eq, 1.0, 0.0)
    sel = jnp.where(gt, 1.0, jnp.where(eq & (prefix_excl(eqf) < need), 1.0, 0.0))
    slot = jnp.where(sel > 0.5, prefix_excl(sel), -1.0)
    slot_ref[0] = slot.astype(jnp.int32)


def _topk(aff, blk0, nk, cap):
    nbatch = aff.shape[0]
    return pl.pallas_call(
        functools.partial(_topk_kernel, nk=nk, cap=cap),
        grid=(nbatch,),
        in_specs=[pl.BlockSpec((1, nk, N_EXPERTS, ROW_BLK), lambda b: (b, blk0, 0, 0))],
        out_specs=pl.BlockSpec((1, nk, N_EXPERTS, ROW_BLK), lambda b: (b, 0, 0, 0)),
        out_shape=jax.ShapeDtypeStruct((nbatch, nk, N_EXPERTS, ROW_BLK), jnp.int32),
        compiler_params=_cp("arbitrary"),
        name="topk",
    )(aff)


def _expert_kernel(*refs, has_ctx):
    if has_ctx:
        (hl_ref, sl_ref, al_ref, hc_ref, sc_ref, ac_ref, wg_ref, wu_ref, wd_ref,
         yl_ref, yc_ref, wgb, wub, wdb, xc_all, gc_all) = refs
    else:
        hl_ref, sl_ref, al_ref, wg_ref, wu_ref, wd_ref, yl_ref, wgb, wub, wdb = refs
    e = pl.program_id(0)
    b = pl.program_id(1)

    @pl.when(b == 0)
    def _():
        wgb[...] = wg_ref[0, 0].astype(BF16)
        wub[...] = wu_ref[0, 0].astype(BF16)
        wdb[...] = wd_ref[0, 0].astype(BF16)

    def gather(h_ref, slot_ref, aff_ref, nk, cap):
        r_iota = lax.broadcasted_iota(jnp.int32, (cap, ROW_BLK), 0)
        xs = jnp.zeros((cap, D), F32)
        gs = jnp.zeros((cap, 1), F32)
        for k in range(nk):
            pm = slot_ref[0, k, pl.ds(e, 1), :] == r_iota
            xs = xs + jnp.dot(jnp.where(pm, 1.0, 0.0).astype(BF16), h_ref[0, ROW_BLK * k:ROW_BLK * k + ROW_BLK, :],
                              preferred_element_type=F32)
            gs = gs + jnp.sum(jnp.where(pm, aff_ref[0, k, pl.ds(e, 1), :], 0.0), axis=1, keepdims=True)
        return xs.astype(BF16), gs

    def ffn(xsb):
        gt = jnp.dot(xsb, wgb[...], preferred_element_type=F32)
        up = jnp.dot(xsb, wub[...], preferred_element_type=F32)
        return jnp.dot((_silu(gt) * up).astype(BF16), wdb[...], preferred_element_type=F32)

    xs, gs = gather(hl_ref, sl_ref, al_ref, LAT_BLKS, CAP_LAT)
    yl_ref[0, 0] = (ffn(xs) * gs).astype(yl_ref.dtype)
    if has_ctx:
        xs, gs = gather(hc_ref, sc_ref, ac_ref, 1, CAP_CTX)
        row = pl.multiple_of(b * CAP_CTX, CAP_CTX)
        xc_all[pl.ds(row, CAP_CTX), :] = xs
        gc_all[pl.ds(row, CAP_CTX), :] = gs

        @pl.when(b == pl.num_programs(1) - 1)
        def _():
            yc_ref[0] = (ffn(xc_all[...]) * gc_all[...]).astype(yc_ref.dtype)


def _experts(h2, slot_lat, slot_ctx, aff, w_gate, w_up, w_down, l, has_ctx):
    nbatch = h2.shape[0]
    idx4 = lambda e, b: (b, 0, 0, 0)
    in_specs = [pl.BlockSpec((1, T, D), lambda e, b: (b, 0, 0)),
                pl.BlockSpec((1, LAT_BLKS, N_EXPERTS, ROW_BLK), idx4),
                pl.BlockSpec((1, LAT_BLKS, N_EXPERTS, ROW_BLK), idx4)]
    args = [h2, slot_lat, aff]
    out_specs = [pl.BlockSpec((1, 1, CAP_LAT, D), lambda e, b: (b, e, 0, 0))]
    out_shape = [jax.ShapeDtypeStruct((nbatch, N_EXPERTS, CAP_LAT, D), BF16)]
    if has_ctx:
        in_specs += [pl.BlockSpec((1, LC, D), lambda e, b: (b, LAT_BLKS, 0)),
                     pl.BlockSpec((1, 1, N_EXPERTS, ROW_BLK), idx4),
                     pl.BlockSpec((1, 1, N_EXPERTS, ROW_BLK), lambda e, b: (b, LAT_BLKS, 0, 0))]
        args += [h2, slot_ctx, aff]
        out_specs.append(pl.BlockSpec((1, nbatch * CAP_CTX, D), lambda e, b: (e, 0, 0)))
        out_shape.append(jax.ShapeDtypeStruct((N_EXPERTS, nbatch * CAP_CTX, D), BF16))
    wspec = pl.BlockSpec((1, 1, D, D), lambda e, b: (l, e, 0, 0))
    in_specs += [wspec, wspec, wspec]
    args += [w_gate, w_up, w_down]
    scratch = [pltpu.VMEM((D, D), BF16)] * 3
    if has_ctx:
        scratch += [pltpu.VMEM((nbatch * CAP_CTX, D), BF16), pltpu.VMEM((nbatch * CAP_CTX, 1), F32)]
    return pl.pallas_call(
        functools.partial(_expert_kernel, has_ctx=has_ctx),
        grid=(N_EXPERTS, nbatch),
        in_specs=in_specs,
        out_specs=out_specs,
        out_shape=out_shape,
        scratch_shapes=scratch,
        compiler_params=_cp("arbitrary", "arbitrary"),
        name="experts",
    )(*args)


def _combine_kernel(*refs, has_ctx):
    if has_ctx:
        x1_ref, yl_ref, scl_ref, yc_ref, scc_ref, gate2_ref, g_ref, o_ref = refs
    else:
        x1_ref, yl_ref, scl_ref, gate2_ref, g_ref, o_ref = refs

    def comb(sc_ref, y_of, cap):
        lane = lax.broadcasted_iota(jnp.int32, (ROW_BLK, cap), 1)
        sc = sc_ref[0]
        acc = jnp.zeros((ROW_BLK, D), F32)
        for e in range(N_EXPERTS):
            pt = jnp.where(sc[:, e:e + 1] == lane, 1.0, 0.0).astype(BF16)
            acc = acc + jnp.dot(pt, y_of(e), preferred_element_type=F32)
        o_ref[0] = x1_ref[0] + gate2_ref[0] * _rms(acc, g_ref[...])

    lat_y = lambda e: yl_ref[0, e]
    if has_ctx:
        j = pl.program_id(1)

        @pl.when(j < LAT_BLKS)
        def _():
            comb(scl_ref, lat_y, CAP_LAT)

        @pl.when(j == LAT_BLKS)
        def _():
            comb(scc_ref, lambda e: yc_ref[e], CAP_CTX)
    else:
        comb(scl_ref, lat_y, CAP_LAT)


def _combine(x1, y_lat, scol_lat, y_ctx, scol_ctx, mod3, g, has_ctx, l):
    nbatch = x1.shape[0]
    nb = NBLK if has_ctx else LAT_BLKS
    ntok = N if has_ctx else T
    in_specs = [pl.BlockSpec((1, ROW_BLK, D), lambda b, j: (b, j, 0)),
                pl.BlockSpec((1, N_EXPERTS, CAP_LAT, D), lambda b, j: (b, 0, 0, 0)),
                pl.BlockSpec((1, ROW_BLK, N_EXPERTS), lambda b, j: (b, jnp.minimum(j, LAT_BLKS - 1), 0))]
    args = [x1, y_lat, scol_lat]
    if has_ctx:
        in_specs += [pl.BlockSpec((N_EXPERTS, CAP_CTX, D), lambda b, j: (0, b, 0)),
                     pl.BlockSpec((1, ROW_BLK, N_EXPERTS), lambda b, j: (b, 0, 0))]
        args += [y_ctx, scol_ctx]
    in_specs += [pl.BlockSpec((1, 1, D), lambda b, j: (jnp.where(j == LAT_BLKS, nbatch, b), 0, 5)),
                 _lspec(g, l)]
    args += [mod3, g]
    return pl.pallas_call(
        functools.partial(_combine_kernel, has_ctx=has_ctx),
        grid=(nbatch, nb),
        in_specs=in_specs,
        out_specs=pl.BlockSpec((1, ROW_BLK, D), lambda b, j: (b, j, 0)),
        out_shape=jax.ShapeDtypeStruct((nbatch, ntok, D), F32),
        compiler_params=_cp("arbitrary", "arbitrary"),
        name="combine",
    )(*args)


def _rope_tables():
    f = 1.0 / (ROPE_THETA ** (jnp.arange(0, 16, 2, dtype=F32) / 16))
    pos = jnp.arange(T)
    row, col = pos // GRID_W, pos % GRID_W
    ar = row.astype(F32)[:, None] * f[None, :]
    ac = col.astype(F32)[:, None] * f[None, :]
    cos32 = jnp.concatenate([jnp.cos(ar), jnp.cos(ar), jnp.cos(ac), jnp.cos(ac)], axis=-1)
    sin32 = jnp.concatenate([jnp.sin(ar), jnp.sin(ar), jnp.sin(ac), jnp.sin(ac)], axis=-1)
    cos32 = jnp.concatenate([cos32, jnp.ones((LC, 32), F32)], axis=0)
    sin32 = jnp.concatenate([sin32, jnp.zeros((LC, 32), F32)], axis=0)
    return cos32, sin32


def _rot_cols(w):
    a, b, c, d = w[..., 0:8], w[..., 8:16], w[..., 16:24], w[..., 24:32]
    return jnp.concatenate([-b, a, -d, c], axis=-1)


def _mla_weights(w_uq, w_ukv):
    nl, r, _ = w_uq.shape
    wq3 = w_uq.reshape(nl, r, 4, 96)
    z32 = jnp.zeros((nl, r, 4, 32), F32)
    wq = jnp.concatenate([wq3, z32], axis=-1).reshape(nl, r, 512)
    wqr = jnp.concatenate([jnp.zeros((nl, r, 4, 64), F32), _rot_cols(wq3[..., 64:96]), z32], axis=-1).reshape(nl, r, 512)
    rk = w_ukv.shape[1]
    wkv3 = w_ukv.reshape(nl, rk, 4, 128)
    wk = jnp.concatenate([wkv3[..., :64], jnp.zeros((nl, rk, 4, 64), F32)], axis=-1).reshape(nl, rk, 512)
    z64 = jnp.zeros((nl, rk, 64), F32)
    vh = [wkv3[:, :, h, 64:] for h in range(4)]
    wv = jnp.concatenate([vh[0], z64, z64, vh[1], vh[2], z64, z64, vh[3]], axis=-1)
    vone = np.tile(np.repeat(np.array([0.0, 1.0, 1.0, 0.0], np.float32), 64), 2)[None, :]
    e = np.zeros((32, 512), np.float32)
    for h in range(4):
        e[np.arange(32), 128 * h + 64 + np.arange(32)] = 1.0
    return (wq.astype(BF16), wqr.astype(BF16), wk.astype(BF16), wv.astype(BF16), jnp.asarray(e, BF16),
            jnp.asarray(vone))


def _inproj_weights(w):
    o = np.cumsum([0, 256, 128, 32, 768, 256, 256, 768, 8, 4096])
    seg = lambda i: w[:, :, o[i]:o[i + 1]]
    nl = w.shape[0]
    kr = seg(2)
    wm = jnp.concatenate([seg(0), seg(1), kr, _rot_cols(kr), jnp.zeros((nl, D, 64), F32)], axis=-1)
    w2 = jnp.concatenate([seg(5), seg(6), seg(7), jnp.zeros((nl, D, 120), F32)], axis=-1)
    return [a.astype(BF16) for a in (wm, seg(3), seg(4), w2, seg(8))]


def kernel(x, c, ctx, c_ctx, w_ada, b_ada, g_pre_mix, g_post_mix, g_pre_ffn, g_post_ffn, w_in, mla_g_cq, mla_g_ckv, mla_w_uq, mla_w_ukv, na_rpb, s5_a_re, s5_a_im, s5_log_step, s5_b_re, s5_b_im, s5_c_re, s5_c_im, s5_d, s5_w_glu, s5_b_glu, m2_conv_w, m2_conv_b, m2_a_log, m2_dt_bias, m2_d, m2_g_norm, w_branch, w_out, w_router, w_gate, w_up, w_down):
    nbatch = x.shape[0]
    depth = w_ada.shape[0]
    vec = lambda v: v.reshape(depth, 1, -1).astype(F32)

    xs = jnp.concatenate([x, ctx], axis=1)
    cvec = jnp.concatenate([c, c_ctx[None, :], jnp.zeros((7, D), F32)], axis=0)
    cvec = cvec[: ((nbatch + 1 + 7) // 8) * 8]
    b_ada3 = b_ada.reshape(depth, 1, 6 * D)

    cos32, sin32 = _rope_tables()
    qscale = (64 + 32) ** -0.5 * math.log2(math.e)
    lane_is_rope = np.tile(np.concatenate([np.zeros(64, bool), np.ones(32, bool), np.zeros(32, bool)]), 4)
    pick = lambda t32, fill: jnp.where(lane_is_rope[None, :], jnp.tile(jnp.pad(t32, ((0, 0), (64, 32))), (1, 4)), fill)
    cos_q = pick(cos32, 1.0) * qscale
    sin_q = pick(sin32, 0.0) * qscale

    g_pre_mix, g_post_mix, g_pre_ffn, g_post_ffn = vec(g_pre_mix), vec(g_post_mix), vec(g_pre_ffn), vec(g_post_ffn)
    w_in_segs = _inproj_weights(w_in)
    wq, wqr, wk, wv, e_mat, vone = _mla_weights(mla_w_uq, mla_w_ukv)
    mla_g_cq, mla_g_ckv = vec(mla_g_cq), vec(mla_g_ckv)
    na_tabs = _na_tables(na_rpb)
    s5_bmat, s5_avec, s5_cmat = _s5_params(s5_a_re, s5_a_im, s5_log_step, s5_b_re, s5_b_im, s5_c_re, s5_c_im, nbatch)
    s5_d, s5_b_glu, s5_w_glu = vec(s5_d), vec(s5_b_glu), s5_w_glu.astype(BF16)
    m2_conv_w, m2_conv_b = m2_conv_w.astype(F32), vec(m2_conv_b)
    dt_bias = jnp.pad(m2_dt_bias.reshape(depth, 1, 8).astype(F32), ((0, 0), (0, 0), (0, 120)))
    a_row = jnp.pad(-jnp.exp(m2_a_log.astype(F32)).reshape(depth, 1, 8), ((0, 0), (0, 0), (0, 120)))
    a_col = a_row.reshape(depth, 128, 1)
    m2_dvec, m2_g_norm = vec(jnp.repeat(m2_d, 64, axis=-1)), vec(m2_g_norm)
    w_branch, w_out = w_branch.astype(BF16), w_out.astype(BF16)
    w_router_t = jnp.swapaxes(w_router, 1, 2).astype(F32)

    for l in range(depth):
        has_ctx = l < depth - 1
        nb = NBLK if has_ctx else LAT_BLKS
        mod = _ada(cvec, w_ada, b_ada3, l)
        mod3 = mod.reshape(mod.shape[0], 1, 6 * D)
        x2d = xs.reshape(nbatch * N, D)

        p_mla, p_na, p_s5, p_m2, p_gate = _inproj(x2d, g_pre_mix, mod3, w_in_segs, nbatch, l)

        q, k, v = _mla_prep(p_mla, mla_g_cq, mla_g_ckv, wq, wqr, wk, wv, e_mat, vone, cos_q, sin_q, cos32, sin32, l)
        o_mla = _mla_attn(q.reshape(nbatch, N, 512), k.reshape(nbatch, N, 512), v.reshape(nbatch, N, 512), has_ctx)

        o_na = _na_attn(p_na.reshape(nbatch, N, NA_W), na_tabs, has_ctx, l)

        yf, yb = _s5_scan(p_s5.reshape(nbatch, N, S5_W), s5_bmat, s5_avec, s5_cmat, l)
        o_s5 = _s5_glu(p_s5, yf.reshape(nbatch * N, S5_W), yb.reshape(nbatch * N, S5_W), s5_d, s5_w_glu, s5_b_glu, l)

        xbc, dt, dtt, bmt = _m2_prep(p_m2.reshape(nbatch, N, M2_W), m2_conv_w, m2_conv_b, dt_bias, l)
        ssd_f, ssd_b = _ssd(xbc, dt, dtt, bmt, a_row, a_col, l)
        o_m2 = _m2_norm(xbc.reshape(nbatch * N, 768), p_m2, ssd_f.reshape(nbatch * N, 256),
                        ssd_b.reshape(nbatch * N, 256), m2_dvec, m2_g_norm, l)

        x1, h2, aff = _merge(x2d, o_mla.reshape(nbatch * N, 256), o_na.reshape(nbatch * N, 256), o_s5, o_m2, p_gate,
                             w_branch, w_out, g_post_mix, mod3, g_pre_ffn, w_router_t, nbatch, nb, l)

        slot_lat = _topk(aff, 0, LAT_BLKS, CAP_LAT)
        slot_ctx = _topk(aff, LAT_BLKS, 1, CAP_CTX) if has_ctx else None
        ys = _experts(h2.reshape(nbatch, N, D), slot_lat, slot_ctx, aff, w_gate, w_up, w_down, l, has_ctx)
        scol_lat = jnp.transpose(slot_lat, (0, 1, 3, 2)).reshape(nbatch, T, N_EXPERTS)
        scol_ctx = jnp.transpose(slot_ctx, (0, 1, 3, 2)).reshape(nbatch, LC, N_EXPERTS) if has_ctx else None
        xs = _combine(x1.reshape(nbatch, N, D), ys[0], scol_lat, ys[1] if has_ctx else None, scol_ctx,
                      mod3, g_post_ffn, has_ctx, l)
    return xs
```

```python
import functools
import math

import numpy as np
import jax
import jax.numpy as jnp
from jax import lax
from jax.experimental import pallas as pl
from jax.experimental.pallas import tpu as pltpu

F32 = jnp.float32
BF16 = jnp.bfloat16

D = 1024
T = 2048
LC = 256
N = T + LC
GRID_W = 64
ROW_BLK = 256
NBLK = N // ROW_BLK
LAT_BLKS = T // ROW_BLK
EPS = 1e-6
N_EXPERTS = 16
CAP_LAT = 2 * T // N_EXPERTS
CAP_CTX = 2 * LC // N_EXPERTS
ROPE_THETA = 10000.0
NEG = -1e30

MLA_W = 512
NA_W = 768
S5_W = 256
M2_W = 1152
GATE_W = 4096

MOE_WIN = 80
S5_CHUNK = 64
SSD_CHUNK = 128

VMEM_LIMIT = 56 * 1024 * 1024


def _cp(*sem):
    return pltpu.CompilerParams(dimension_semantics=sem, vmem_limit_bytes=VMEM_LIMIT)


def _lspec(a, l, *lead):
    nd = a.ndim - 1 - len(lead)
    return pl.BlockSpec((None,) * (1 + len(lead)) + a.shape[1 + len(lead):], lambda *_: (l,) + lead + (0,) * nd)


def _dot(a, b):
    return jnp.dot(a.astype(BF16), b.astype(BF16), preferred_element_type=F32)


def _dot_nt(a, b):
    return lax.dot_general(a.astype(BF16), b.astype(BF16), (((1,), (1,)), ((), ())),
                           preferred_element_type=F32)


def _split3(a):
    hi = a.astype(BF16)
    r = a - hi.astype(F32)
    mid = r.astype(BF16)
    lo = (r - mid.astype(F32)).astype(BF16)
    return hi, mid, lo


def _dot_hi(a, b):
    ah, am, _ = _split3(a)
    bh, bm, _ = _split3(b)
    f = lambda x, y: jnp.dot(x, y, preferred_element_type=F32)
    return f(ah, bh) + (f(ah, bm) + f(am, bh))


def _dot_hi_nt(a, b):
    ah, am, _ = _split3(a)
    bh, bm, _ = _split3(b)
    f = lambda x, y: lax.dot_general(x, y, (((1,), (1,)), ((), ())), preferred_element_type=F32)
    return f(ah, bh) + (f(ah, bm) + f(am, bh))


def _dot_exact_lhs(m_bf16, a):
    h, m, l = _split3(a)
    f = lambda y: jnp.dot(m_bf16, y, preferred_element_type=F32)
    return f(h) + (f(m) + f(l))


def _dot_exact_rhs(a, m_bf16):
    h, m, l = _split3(a)
    f = lambda y: jnp.dot(y, m_bf16, preferred_element_type=F32)
    return f(h) + (f(m) + f(l))


def _sigmoid(x):
    return 0.5 * jnp.tanh(0.5 * x) + 0.5


def _silu(x):
    return x * _sigmoid(x)


def _rms(x, g):
    return x * lax.rsqrt(jnp.mean(x * x, axis=-1, keepdims=True) + EPS) * g


def _mod_row(nb):
    def f(i, nbatch):
        return jnp.where(i % nb == LAT_BLKS, nbatch, i // nb)
    return f


def _flat_blk(i, nb):
    return (i // nb) * NBLK + i % nb


def _ada_kernel(c_ref, w_ref, b_ref, o_ref):
    c = c_ref[...]
    o_ref[...] = _dot_hi(_silu(c), w_ref[0]) + b_ref[0]


def _ada(cvec, w_ada, b_ada3, l):
    rows = cvec.shape[0]
    return pl.pallas_call(
        _ada_kernel,
        grid=(6,),
        in_specs=[pl.BlockSpec((rows, D), lambda k: (0, 0)),
                  pl.BlockSpec((1, D, D), lambda k: (l, 0, k)),
                  pl.BlockSpec((1, 1, D), lambda k: (l, 0, k))],
        out_specs=pl.BlockSpec((rows, D), lambda k: (0, k)),
        out_shape=jax.ShapeDtypeStruct((rows, 6 * D), F32),
        compiler_params=_cp("arbitrary"),
        name="ada",
    )(cvec, w_ada, b_ada3)


def _inproj_kernel(x_ref, g_ref, sh_ref, sc_ref, wm, wn, ws, w2, wg, om, on, os_, o2, og):
    h = (_rms(x_ref[...], g_ref[...]) * (1.0 + sc_ref[0]) + sh_ref[0]).astype(BF16)
    om[...] = jnp.dot(h, wm[...], preferred_element_type=F32)
    on[...] = jnp.dot(h, wn[...], preferred_element_type=F32).astype(BF16)
    os_[...] = jnp.dot(h, ws[...], preferred_element_type=F32)
    o2[...] = jnp.dot(h, w2[...], preferred_element_type=F32)
    og[...] = jnp.dot(h, wg[...], preferred_element_type=F32)


def _inproj(x2d, g, mod3, ws, nbatch, l):
    rows = x2d.shape[0]
    nblk = rows // ROW_BLK
    mrow = _mod_row(NBLK)
    full = lambda w: _lspec(w, l)
    widths = (MLA_W, NA_W, S5_W, M2_W, GATE_W)
    dts = (F32, BF16, F32, F32, F32)
    return pl.pallas_call(
        _inproj_kernel,
        grid=(nblk,),
        in_specs=[pl.BlockSpec((ROW_BLK, D), lambda i: (i, 0)),
                  full(g),
                  pl.BlockSpec((1, 1, D), lambda i: (mrow(i, nbatch), 0, 0)),
                  pl.BlockSpec((1, 1, D), lambda i: (mrow(i, nbatch), 0, 1))]
                 + [full(w) for w in ws],
        out_specs=[pl.BlockSpec((ROW_BLK, w), lambda i: (i, 0)) for w in widths],
        out_shape=[jax.ShapeDtypeStruct((rows, w), dt) for w, dt in zip(widths, dts)],
        compiler_params=_cp("arbitrary"),
        name="inproj",
    )(x2d, g, mod3, mod3, *ws)


def _mla_prep_kernel(p_ref, gq_ref, gkv_ref, wq_ref, wqr_ref, wk_ref, wv_ref, e_ref, vone_ref,
                     cos_ref, sin_ref, ck_ref, sk_ref, q_out, k_out, v_out):
    p = p_ref[...]
    cqn = _rms(p[:, :256], gq_ref[...]).astype(BF16)
    q = jnp.dot(cqn, wq_ref[...], preferred_element_type=F32)
    qr = jnp.dot(cqn, wqr_ref[...], preferred_element_type=F32)
    q_out[...] = (q * cos_ref[...] + qr * sin_ref[...]).astype(BF16)
    ckvn = _rms(p[:, 256:384], gkv_ref[...]).astype(BF16)
    kro = (p[:, 384:416] * ck_ref[...] + p[:, 416:448] * sk_ref[...]).astype(BF16)
    k = jnp.dot(ckvn, wk_ref[...], preferred_element_type=F32) + jnp.dot(kro, e_ref[...], preferred_element_type=F32)
    k_out[...] = k.astype(BF16)
    v_out[...] = (jnp.dot(ckvn, wv_ref[...], preferred_element_type=F32) + vone_ref[...]).astype(BF16)


def _mla_prep(p_mla, gq, gkv, wq, wqr, wk, wv, e, vone, cos_q, sin_q, cos_k, sin_k, l):
    rows = p_mla.shape[0]
    full = lambda w: _lspec(w, l)
    const = lambda w: pl.BlockSpec(w.shape, lambda i: (0, 0))
    tab = lambda w: pl.BlockSpec((ROW_BLK, w), lambda i: (i % NBLK, 0))
    return pl.pallas_call(
        _mla_prep_kernel,
        grid=(rows // ROW_BLK,),
        in_specs=[pl.BlockSpec((ROW_BLK, MLA_W), lambda i: (i, 0)),
                  full(gq), full(gkv), full(wq), full(wqr), full(wk), full(wv), const(e), const(vone),
                  tab(512), tab(512), tab(32), tab(32)],
        out_specs=[pl.BlockSpec((ROW_BLK, 512), lambda i: (i, 0))] * 3,
        out_shape=[jax.ShapeDtypeStruct((rows, 512), BF16)] * 3,
        compiler_params=_cp("arbitrary"),
        name="mla_prep",
    )(p_mla, gq, gkv, wq, wqr, wk, wv, e, vone, cos_q, sin_q, cos_k, sin_k)


def _mla_attn_kernel(q_ref, k_ref, v_ref, o_ref, *, has_ctx):
    lane = lax.broadcasted_iota(jnp.int32, (ROW_BLK, 128), 1)

    def run(k0, nk):
        for hp in range(2):
            pv = []
            for h in (2 * hp, 2 * hp + 1):
                qh = q_ref[0, :, 128 * h:128 * h + 128]
                kh = k_ref[0, k0:k0 + nk, 128 * h:128 * h + 128]
                s = _dot_nt(qh, kh)
                p = jnp.exp2(s - jnp.max(s, axis=-1, keepdims=True)).astype(BF16)
                pv.append(jnp.dot(p, v_ref[0, k0:k0 + nk, 128 * h:128 * h + 128], preferred_element_type=F32))
            oa, ob = pv
            o = jnp.where(lane < 64, oa * (1.0 / oa[:, 64:65]), ob * (1.0 / ob[:, 0:1]))
            o_ref[0, :, 128 * hp:128 * hp + 128] = o.astype(o_ref.dtype)

    if has_ctx:
        j = pl.program_id(1)

        @pl.when(j < LAT_BLKS)
        def _():
            run(0, N)

        @pl.when(j == LAT_BLKS)
        def _():
            run(T, LC)
    else:
        run(0, N)


def _mla_attn(q, k, v, has_ctx):
    nbatch = q.shape[0]
    nb = NBLK if has_ctx else LAT_BLKS
    return pl.pallas_call(
        functools.partial(_mla_attn_kernel, has_ctx=has_ctx),
        grid=(nbatch, nb),
        in_specs=[pl.BlockSpec((1, ROW_BLK, 512), lambda b, j: (b, j, 0)),
                  pl.BlockSpec((1, N, 512), lambda b, j: (b, 0, 0)),
                  pl.BlockSpec((1, N, 512), lambda b, j: (b, 0, 0))],
        out_specs=pl.BlockSpec((1, ROW_BLK, 256), lambda b, j: (b, j, 0)),
        out_shape=jax.ShapeDtypeStruct((nbatch, nb * ROW_BLK, 256), BF16),
        compiler_params=_cp("arbitrary", "arbitrary"),
        name="mla_attn",
    )(q, k, v)


def _na_kernel(q_ref, kc_ref, k0_ref, k1_ref, k2_ref, vc_ref, v0_ref, v1_ref, v2_ref, b_ref, o_ref, *, has_ctx):
    scale = jnp.asarray(0.125, BF16)

    lane = lax.broadcasted_iota(jnp.int32, (ROW_BLK, 128), 1)
    lane64 = lax.broadcasted_iota(jnp.int32, (GRID_W, 128), 1) < 64
    grp = jnp.minimum(pl.program_id(0), LAT_BLKS - 1)
    krow0 = 4 * jnp.clip(grp - 1, 0, LAT_BLKS - 3)

    def bias_piece(h, i):
        rows = []
        for qr in range(4):
            r = 4 * grp + qr
            rs = jnp.clip(r - 4, 0, 24)
            tiles = []
            for pr in range(2):
                kr = krow0 + (4 * i + 2 * pr)
                t = b_ref[h, jnp.clip(kr - r + 8, 0, 15)]
                ok_a = jnp.logical_and(kr >= rs, kr < rs + 8).astype(jnp.int32)
                ok_b = jnp.logical_and(kr + 1 >= rs, kr + 1 < rs + 8).astype(jnp.int32)
                tiles.append(jnp.where(jnp.where(lane64, ok_a, ok_b) > 0, t, NEG))
            rows.append(jnp.concatenate(tiles, axis=1))
        return jnp.concatenate(rows, axis=0)

    def heads(win):
        kws = (k0_ref, k1_ref, k2_ref)
        vws = (v0_ref, v1_ref, v2_ref)
        for hp in range(2):
            sl = slice(128 * hp, 128 * hp + 128)
            qp = q_ref[0, :, sl] * scale
            outs = []
            for hh in range(2):
                h = 2 * hp + hh
                qh = jnp.where((lane < 64) if hh == 0 else (lane >= 64), qp, jnp.zeros_like(qp))
                s_c = _dot_nt(qh, kc_ref[0, :, sl])
                s_w = []
                smax = s_c
                if win:
                    for i in range(3):
                        s = _dot_nt(qh, kws[i][0, :, sl]) + bias_piece(h, i)
                        s_w.append(s)
                        smax = jnp.maximum(smax, s)
                m = jnp.max(smax, axis=-1, keepdims=True)
                p = jnp.exp(s_c - m)
                psum = p
                o = _dot(p, vc_ref[0, :, sl])
                for i, s in enumerate(s_w):
                    p = jnp.exp(s - m)
                    psum = psum + p
                    o = o + _dot(p, vws[i][0, :, sl])
                outs.append(o * (1.0 / jnp.sum(psum, axis=-1, keepdims=True)))
            o_ref[0, :, sl] = jnp.where(lane < 64, outs[0], outs[1]).astype(o_ref.dtype)

    if has_ctx:
        g = pl.program_id(0)

        @pl.when(g < LAT_BLKS)
        def _():
            heads(True)

        @pl.when(g == LAT_BLKS)
        def _():
            heads(False)
    else:
        heads(True)


def _na_attn(qkv, bias, has_ctx, l):
    nbatch = qkv.shape[0]
    ng = NBLK if has_ctx else LAT_BLKS
    j0 = lambda g: jnp.clip(g - 1, 0, LAT_BLKS - 3)
    blk = lambda f: pl.BlockSpec((1, ROW_BLK, 256), f)
    return pl.pallas_call(
        functools.partial(_na_kernel, has_ctx=has_ctx),
        grid=(ng, nbatch),
        in_specs=[blk(lambda g, b: (b, g, 0)),
                  blk(lambda g, b: (b, LAT_BLKS, 1)),
                  blk(lambda g, b: (b, j0(g), 1)),
                  blk(lambda g, b: (b, j0(g) + 1, 1)),
                  blk(lambda g, b: (b, j0(g) + 2, 1)),
                  blk(lambda g, b: (b, LAT_BLKS, 2)),
                  blk(lambda g, b: (b, j0(g), 2)),
                  blk(lambda g, b: (b, j0(g) + 1, 2)),
                  blk(lambda g, b: (b, j0(g) + 2, 2)),
                  _lspec(bias, l)],
        out_specs=blk(lambda g, b: (b, g, 0)),
        out_shape=jax.ShapeDtypeStruct((nbatch, ng * ROW_BLK, 256), BF16),
        compiler_params=_cp("arbitrary", "arbitrary"),
        name="na_attn",
    )(qkv, qkv, qkv, qkv, qkv, qkv, qkv, qkv, qkv, bias)


def _na_col_structure():
    onehot = np.zeros((31, 64, 64), np.float32)
    colmask = np.zeros((64, 64), bool)
    for c in range(64):
        cs = min(max(c - 8, 0), 48)
        for kc in range(cs, cs + 16):
            onehot[kc - c + 15, c, kc] = 1.0
            colmask[c, kc] = True
    return np.tile(onehot.reshape(31, 4096), (3, 1)), colmask


_NA_COL_ONEHOT, _NA_COLMASK = _na_col_structure()


def _na_tables(rpb):
    nl = rpb.shape[0]
    r = rpb.astype(F32)
    hi = lax.reduce_precision(r, 8, 7)
    mid = lax.reduce_precision(r - hi, 8, 7)
    parts = jnp.stack([hi, mid, r - hi - mid], axis=3).reshape(nl * 60, 93)
    tz = jnp.dot(parts, _NA_COL_ONEHOT, preferred_element_type=F32).reshape(nl, 4, 15, 64, 64)
    tz = jnp.where(_NA_COLMASK, tz, NEG)
    edge = jnp.full((nl, 4, 1, 64, 64), NEG, F32)
    tz17 = jnp.concatenate([edge, tz, edge], axis=2)
    return jnp.concatenate([tz17[:, :, :16], tz17[:, :, 1:]], axis=-1)


def _s5_kernel(uf_ref, ub_ref, bf_ref, bb_ref, af_ref, ab_ref, cf_ref, cb_ref, yf_ref, yb_ref,
               hf, hb, buf_f, buf_b, tm_f, tm_b):
    i = pl.program_id(0)
    half = 1024
    nb = uf_ref.shape[0]

    @pl.when(i == 0)
    def _():
        hf[...] = jnp.zeros_like(hf)
        hb[...] = jnp.zeros_like(hb)

    def expand(u_ref, tm, b_ref, buf):
        for b in range(nb):
            for c in range(2):
                tm[c, pl.ds(b, S5_CHUNK, stride=nb), :] = u_ref[b, :, 128 * c:128 * c + 128]
        buf[...] = _dot(jnp.concatenate([tm[0], tm[1]], axis=1), b_ref[...])

    expand(uf_ref, tm_f, bf_ref, buf_f)
    expand(ub_ref, tm_b, bb_ref, buf_b)

    def one(buf, a_ref, row, hr, hi):
        bu_r = buf[pl.ds(row, nb), :half]
        bu_i = buf[pl.ds(row, nb), half:]
        ar = a_ref[:, :half]
        ai = a_ref[:, half:]
        nr = ar * hr - ai * hi + bu_r
        ni = ar * hi + ai * hr + bu_i
        buf[pl.ds(row, nb), :half] = nr
        buf[pl.ds(row, nb), half:] = ni
        return nr, ni

    def step(k, carry):
        fr, fi, br, bi = carry
        rf = pl.multiple_of(k * nb, nb)
        rb = pl.multiple_of((S5_CHUNK - 1 - k) * nb, nb)
        fr, fi = one(buf_f, af_ref, rf, fr, fi)
        br, bi = one(buf_b, ab_ref, rb, br, bi)
        return fr, fi, br, bi

    init = (hf[:, :half], hf[:, half:], hb[:, :half], hb[:, half:])
    fr, fi, br, bi = lax.fori_loop(0, S5_CHUNK, step, init)
    hf[:, :half] = fr
    hf[:, half:] = fi
    hb[:, :half] = br
    hb[:, half:] = bi

    def readout(buf, c_ref, tm, y_ref):
        y = _dot(buf[...], c_ref[...])
        for c in range(2):
            tm[c] = y[:, 128 * c:128 * c + 128]
        for b in range(nb):
            for c in range(2):
                y_ref[b, :, 128 * c:128 * c + 128] = tm[c, pl.ds(b, S5_CHUNK, stride=nb), :]

    readout(buf_f, cf_ref, tm_f, yf_ref)
    readout(buf_b, cb_ref, tm_b, yb_ref)


def _s5_scan(u, bmat, avec, cmat, l):
    nbatch = u.shape[0]
    cr = S5_CHUNK * nbatch
    nch = N // S5_CHUNK
    nctx = LC // S5_CHUNK
    fidx = lambda i: jnp.where(i < nctx, nch - nctx + i, i - nctx)
    bidx = lambda i: nch - 1 - i
    ublk = lambda f: pl.BlockSpec((nbatch, S5_CHUNK, 256), lambda i: (0, f(i), 0))
    return pl.pallas_call(
        _s5_kernel,
        grid=(nch,),
        in_specs=[ublk(fidx), ublk(bidx),
                  _lspec(bmat, l, 0), _lspec(bmat, l, 1), _lspec(avec, l, 0), _lspec(avec, l, 1),
                  _lspec(cmat, l, 0), _lspec(cmat, l, 1)],
        out_specs=[ublk(fidx), ublk(bidx)],
        out_shape=[jax.ShapeDtypeStruct((nbatch, N, 256), F32)] * 2,
        scratch_shapes=[pltpu.VMEM((nbatch, 2048), F32), pltpu.VMEM((nbatch, 2048), F32),
                        pltpu.VMEM((cr, 2048), F32), pltpu.VMEM((cr, 2048), F32),
                        pltpu.VMEM((2, cr, 128), F32), pltpu.VMEM((2, cr, 128), F32)],
        compiler_params=_cp("arbitrary"),
        name="s5_scan",
    )(u, u, bmat, bmat, avec, avec, cmat, cmat)


def _s5_glu_kernel(u_ref, yf_ref, yb_ref, d_ref, w_ref, b_ref, o_ref):
    y = d_ref[...] * u_ref[...] + yf_ref[...] + yb_ref[...]
    z = y * (0.5 * (1.0 + jnp.tanh(math.sqrt(2.0 / math.pi) * (y + 0.044715 * (y * y * y)))))
    o_ref[...] = (z * _sigmoid(_dot(z, w_ref[...]) + b_ref[...])).astype(o_ref.dtype)


def _s5_glu(u_tb, yf, yb, d, w, b, l):
    rows = u_tb.shape[0]
    rb = 512
    blk = pl.BlockSpec((rb, 256), lambda i: (i, 0))
    full = lambda a: _lspec(a, l)
    return pl.pallas_call(
        _s5_glu_kernel,
        grid=(rows // rb,),
        in_specs=[blk, blk, blk, full(d), full(w), full(b)],
        out_specs=blk,
        out_shape=jax.ShapeDtypeStruct((rows, 256), BF16),
        compiler_params=_cp("arbitrary"),
        name="s5_glu",
    )(u_tb, yf, yb, d, w, b)


def _s5_params(a_re, a_im, log_step, b_re, b_im, c_re, c_im, nbatch):
    nl = a_re.shape[0]
    eye = jnp.eye(16, dtype=F32)[:, None, :, None]
    a = lax.complex(a_re.astype(F32), a_im.astype(F32))
    abar = jnp.exp(jnp.exp(log_step.astype(F32))[..., None] * a)
    bbar = ((abar - 1.0) / a)[..., None] * lax.complex(b_re.astype(F32), b_im.astype(F32))
    blk_in = lambda m: (eye * jnp.swapaxes(m, -1, -2)[..., :, :, None, :]).reshape(nl, 2, 256, 1024)
    bmat = jnp.concatenate([blk_in(jnp.real(bbar)), blk_in(jnp.imag(bbar))], axis=-1).astype(BF16)
    avec = jnp.concatenate([jnp.real(abar).reshape(nl, 2, 1, 1024), jnp.imag(abar).reshape(nl, 2, 1, 1024)], axis=-1)
    avec = jnp.broadcast_to(avec, (nl, 2, nbatch, 2048))
    blk_out = lambda m: (eye * jnp.swapaxes(m, -1, -2)[..., :, :, None, :]).reshape(nl, 2, 1024, 256)
    cmat = jnp.concatenate([blk_out(c_re.astype(F32)), -blk_out(c_im.astype(F32))], axis=-2).astype(BF16)
    return bmat, avec, cmat


def _m2_prep_kernel(x_ref, dt_ref, w_ref, b_ref, dtb_ref, xo_ref, dto_ref, dtt_ref, bmt_ref):
    c = pl.program_id(1)
    x = x_ref[0]
    t = lax.broadcasted_iota(jnp.int32, x.shape, 0)
    m2 = ((t >= 2) & (t < T)) | (t >= T + 2)
    m1 = ((t >= 1) & (t < T)) | (t >= T + 1)
    p1 = (t <= T - 2) | ((t >= T) & (t <= N - 2))
    w = w_ref[...]
    y = (w[0:1] * jnp.where(m2, pltpu.roll(x, 2, 0), 0.0)
         + w[1:2] * jnp.where(m1, pltpu.roll(x, 1, 0), 0.0)
         + w[2:3] * x
         + w[3:4] * jnp.where(p1, pltpu.roll(x, N - 1, 0), 0.0)) + b_ref[...]
    act = _silu(y)
    xo_ref[0] = act

    @pl.when(c == 0)
    def _():
        v = dt_ref[0] + dtb_ref[...]
        dt = jnp.maximum(v, 0.0) + jnp.log1p(jnp.exp(-jnp.abs(v)))
        dto_ref[0] = dt
        dtt_ref[0] = dt.T

    @pl.when(c == 1)
    def _():
        bmt_ref[0] = act.T.astype(BF16)


def _m2_prep(p_m2, conv_w, conv_b, dtb, l):
    nbatch = p_m2.shape[0]
    return pl.pallas_call(
        _m2_prep_kernel,
        grid=(nbatch, 3),
        in_specs=[pl.BlockSpec((1, N, 256), lambda b, c: (b, 0, 1 + c)),
                  pl.BlockSpec((1, N, 128), lambda b, c: (b, 0, 8)),
                  pl.BlockSpec((None, 4, 256), lambda b, c: (l, 0, c)),
                  pl.BlockSpec((None, 1, 256), lambda b, c: (l, 0, c)),
                  _lspec(dtb, l)],
        out_specs=[pl.BlockSpec((1, N, 256), lambda b, c: (b, 0, c)),
                   pl.BlockSpec((1, N, 128), lambda b, c: (b, 0, 0)),
                   pl.BlockSpec((1, 128, N), lambda b, c: (b, 0, 0)),
                   pl.BlockSpec((1, 256, N), lambda b, c: (b, 0, 0))],
        out_shape=[jax.ShapeDtypeStruct((nbatch, N, 768), F32),
                   jax.ShapeDtypeStruct((nbatch, N, 128), F32),
                   jax.ShapeDtypeStruct((nbatch, 128, N), F32),
                   jax.ShapeDtypeStruct((nbatch, 256, N), BF16)],
        compiler_params=_cp("arbitrary", "arbitrary"),
        name="m2_prep",
    )(p_m2, p_m2, conv_w, conv_b, dtb)


def _ssd_kernel(xf_ref, dtf_ref, dttf_ref, bmtf_ref, xb_ref, dtb_ref, dttb_ref, bmtb_ref, a_ref, acol_ref,
                yf_ref, yb_ref, hs):
    i = pl.program_id(1)
    L = SSD_CHUNK

    @pl.when(i == 0)
    def _():
        hs[...] = jnp.zeros_like(hs)

    li = lax.broadcasted_iota(jnp.int32, (L, L), 0)
    si = lax.broadcasted_iota(jnp.int32, (L, L), 1)

    def direction(x_ref, dt_ref, dtt_ref, bmt_ref, y_ref, d, causal, causal_t, last):
        cz = jnp.where(causal, 1.0, 0.0).astype(BF16)
        czt = jnp.where(causal_t, 1.0, 0.0).astype(BF16)
        xbc = x_ref[0]
        dt = dt_ref[0]
        cum = _dot_exact_lhs(cz, dt * a_ref[...])
        dtt = dtt_ref[0]
        cumt = _dot_exact_rhs(dtt * acol_ref[...], czt)
        for g in range(2):
            bm = xbc[:, 256 + 128 * g:384 + 128 * g]
            cm = xbc[:, 512 + 128 * g:640 + 128 * g]
            gmat = _dot_nt(cm, bm)
            bmt = bmt_ref[0, 128 * g:128 * g + 128, :]
            cmb = cm.astype(BF16)
            for hh in range(2):
                h = 2 * g + hh
                ln = 4 * d + h
                ccol = cum[:, ln:ln + 1]
                crow = cumt[ln:ln + 1, :]
                decay = jnp.where(causal, jnp.exp(ccol - crow), 0.0)
                mmat = gmat * decay * dtt[ln:ln + 1, :]
                xh = xbc[:, 64 * h:64 * h + 64]
                yd = _dot(mmat, xh)
                clast = cum[last:last + 1, ln:ln + 1]
                wcol = jnp.exp(clast - ccol) * dt[:, ln:ln + 1]
                st = jnp.dot(bmt, (xh * wcol).astype(BF16), preferred_element_type=F32)
                hprev = hs[ln]
                yo = jnp.dot(cmb, hprev.astype(BF16), preferred_element_type=F32) * jnp.exp(ccol)
                hs[ln] = hprev * jnp.exp(clast) + st
                y_ref[0, :, 64 * h:64 * h + 64] = yd + yo

    direction(xf_ref, dtf_ref, dttf_ref, bmtf_ref, yf_ref, 0, si <= li, li <= si, L - 1)
    direction(xb_ref, dtb_ref, dttb_ref, bmtb_ref, yb_ref, 1, si >= li, li >= si, 0)


def _ssd(xbc, dt, dtt, bmt, a_row, a_col, l):
    nbatch = xbc.shape[0]
    nch = N // SSD_CHUNK
    nctx = LC // SSD_CHUNK
    fidx = lambda i: jnp.where(i < nctx, nch - nctx + i, i - nctx)
    bidx = lambda i: nch - 1 - i
    ins = lambda f: [pl.BlockSpec((1, SSD_CHUNK, 768), lambda b, i: (b, f(i), 0)),
                     pl.BlockSpec((1, SSD_CHUNK, 128), lambda b, i: (b, f(i), 0)),
                     pl.BlockSpec((1, 128, SSD_CHUNK), lambda b, i: (b, 0, f(i))),
                     pl.BlockSpec((1, 256, SSD_CHUNK), lambda b, i: (b, 0, f(i)))]
    return pl.pallas_call(
        _ssd_kernel,
        grid=(nbatch, nch),
        in_specs=ins(fidx) + ins(bidx) + [_lspec(a_row, l), _lspec(a_col, l)],
        out_specs=[pl.BlockSpec((1, SSD_CHUNK, 256), lambda b, i: (b, fidx(i), 0)),
                   pl.BlockSpec((1, SSD_CHUNK, 256), lambda b, i: (b, bidx(i), 0))],
        out_shape=[jax.ShapeDtypeStruct((nbatch, N, 256), F32)] * 2,
        scratch_shapes=[pltpu.VMEM((8, 128, 64), F32)],
        compiler_params=_cp("arbitrary", "arbitrary"),
        name="ssd",
    )(xbc, dt, dtt, bmt, xbc, dt, dtt, bmt, a_row, a_col)


def _m2_norm_kernel(x_ref, z_ref, yf_ref, yb_ref, d_ref, g_ref, o_ref):
    y = d_ref[...] * x_ref[...] + yf_ref[...] + yb_ref[...]
    o_ref[...] = _rms(y * _silu(z_ref[...]), g_ref[...]).astype(o_ref.dtype)


def _m2_norm(xbc2d, p_m2_2d, yf2d, yb2d, dvec, gn, l):
    rows = xbc2d.shape[0]
    blk = pl.BlockSpec((ROW_BLK, 256), lambda i: (i, 0))
    full = lambda a: _lspec(a, l)
    return pl.pallas_call(
        _m2_norm_kernel,
        grid=(rows // ROW_BLK,),
        in_specs=[blk, blk, blk, blk, full(dvec), full(gn)],
        out_specs=blk,
        out_shape=jax.ShapeDtypeStruct((rows, 256), BF16),
        compiler_params=_cp("arbitrary"),
        name="m2_norm",
    )(xbc2d, p_m2_2d, yf2d, yb2d, dvec, gn)


def _merge_kernel(x_ref, oa_ref, on_ref, os_ref, om_ref, gl_ref, wb_ref, wo_ref, gpm_ref, gate1_ref,
                  gpf_ref, sh2_ref, sc2_ref, wr_ref, x1_ref, h2_ref, aff_ref):
    outs = (oa_ref, on_ref, os_ref, om_ref)
    y = None
    for j in range(4):
        t = _sigmoid(gl_ref[:, 1024 * j:1024 * j + 1024]) * jnp.dot(outs[j][...], wb_ref[j], preferred_element_type=F32)
        y = t if y is None else y + t
    y2 = _dot(y, wo_ref[...])
    x1 = x_ref[...] + gate1_ref[0] * _rms(y2, gpm_ref[...])
    x1_ref[...] = x1
    h2 = _rms(x1, gpf_ref[...]) * (1.0 + sc2_ref[0]) + sh2_ref[0]
    h2_ref[...] = h2.astype(BF16)
    logits = _dot_hi_nt(wr_ref[...], h2)
    m = jnp.max(logits, axis=0, keepdims=True)
    e = jnp.exp(logits - m)
    aff_ref[0, 0] = e / jnp.sum(e, axis=0, keepdims=True)


def _merge(x2d, o_mla, o_na, o_s5, o_m2, gl, wb, wo, gpm, mod3, gpf, wrt, nbatch, nb, l):
    mrow = _mod_row(nb)
    fb = lambda i: (_flat_blk(i, nb), 0)
    blk = lambda w: pl.BlockSpec((ROW_BLK, w), fb)
    oblk = lambda w: pl.BlockSpec((ROW_BLK, w), lambda i: (i, 0))
    full = lambda a: _lspec(a, l)
    modspec = lambda k: pl.BlockSpec((1, 1, D), lambda i: (mrow(i, nbatch), 0, k))
    orows = nbatch * nb * ROW_BLK
    return pl.pallas_call(
        _merge_kernel,
        grid=(nbatch * nb,),
        in_specs=[blk(D), oblk(256), oblk(256), blk(256), blk(256), blk(GATE_W),
                  full(wb), full(wo), full(gpm), modspec(2), full(gpf), modspec(3), modspec(4), full(wrt)],
        out_specs=[oblk(D), oblk(D),
                   pl.BlockSpec((1, 1, N_EXPERTS, ROW_BLK), lambda i: (i // nb, i % nb, 0, 0))],
        out_shape=[jax.ShapeDtypeStruct((orows, D), F32),
                   jax.ShapeDtypeStruct((orows, D), BF16),
                   jax.ShapeDtypeStruct((nbatch, nb, N_EXPERTS, ROW_BLK), F32)],
        compiler_params=_cp("arbitrary"),
        name="merge",
    )(x2d, o_mla, o_na, o_s5, o_m2, gl, wb, wo, gpm, mod3, gpf, mod3, mod3, wrt)


def _topk_kernel(aff_ref, slot_ref, *, nk, cap):
    a = aff_ref[0]
    bits = lax.bitcast_convert_type(a, jnp.int32)
    count = lambda m: jnp.sum(jnp.sum(jnp.where(m, 1.0, 0.0), axis=2, keepdims=True), axis=0, keepdims=True)
    thr = jnp.zeros((1, N_EXPERTS, 1), jnp.int32)
    for bit in range(30, -1, -1):
        cand = thr | (1 << bit)
        thr = jnp.where(count(bits >= cand) >= cap, cand, thr)
    gt = bits > thr
    eq = bits == thr
    need = cap - count(gt)

    tri = jnp.where(lax.broadcasted_iota(jnp.int32, (256, 256), 0) <= lax.broadcasted_iota(jnp.int32, (256, 256), 1),
                    1.0, 0.0).astype(BF16)

    def prefix_excl(m):
        incl = jnp.dot(m.reshape(nk * N_EXPERTS, 256).astype(BF16), tri,
                       preferred_element_type=F32).reshape(nk, N_EXPERTS, 256)
        offs = []
        run = jnp.zeros((1, N_EXPERTS, 1), F32)
        for k in range(nk):
            offs.append(run)
            run = run + incl[k:k + 1, :, 255:256]
        off = offs[0] if nk == 1 else jnp.concatenate(offs, axis=0)
        return incl - m + off

    eqf = jnp.where(eq, 1.0, 0.0)
    sel = jnp.where(gt, 1.0, jnp.where(eq & (prefix_excl(eqf) < need), 1.0, 0.0))
    slot = jnp.where(sel > 0.5, prefix_excl(sel), -1.0)
    slot_ref[0] = slot.astype(jnp.int32)


def _topk(aff, blk0, nk, cap):
    nbatch = aff.shape[0]
    return pl.pallas_call(
        functools.partial(_topk_kernel, nk=nk, cap=cap),
        grid=(nbatch,),
        in_specs=[pl.BlockSpec((1, nk, N_EXPERTS, ROW_BLK), lambda b: (b, blk0, 0, 0))],
        out_specs=pl.BlockSpec((1, nk, N_EXPERTS, ROW_BLK), lambda b: (b, 0, 0, 0)),
        out_shape=jax.ShapeDtypeStruct((nbatch, nk, N_EXPERTS, ROW_BLK), jnp.int32),
        compiler_params=_cp("arbitrary"),
        name="topk",
    )(aff)


def _expert_kernel(*refs, has_ctx):
    if has_ctx:
        (w0_ref, fits_ref, hl_ref, sl_ref, al_ref, hc_ref, sc_ref, ac_ref, wg_ref, wu_ref, wd_ref,
         yl_ref, yc_ref, wgb, wub, wdb, xs_acc, gs_acc, xc_all, gc_all) = refs
    else:
        (w0_ref, fits_ref, hl_ref, sl_ref, al_ref, wg_ref, wu_ref, wd_ref, yl_ref,
         wgb, wub, wdb, xs_acc, gs_acc) = refs
    e = pl.program_id(0)
    b = pl.program_id(1)

    @pl.when(b == 0)
    def _():
        wgb[...] = wg_ref[0, 0].astype(BF16)
        wub[...] = wu_ref[0, 0].astype(BF16)
        wdb[...] = wd_ref[0, 0].astype(BF16)

    def gather(h_ref, slot_ref, aff_ref, nk, cap):
        r_iota = lax.broadcasted_iota(jnp.int32, (cap, ROW_BLK), 0)
        xs = jnp.zeros((cap, D), F32)
        gs = jnp.zeros((cap, 1), F32)
        for k in range(nk):
            pm = slot_ref[0, k, pl.ds(e, 1), :] == r_iota
            xs = xs + jnp.dot(jnp.where(pm, 1.0, 0.0).astype(BF16), h_ref[0, ROW_BLK * k:ROW_BLK * k + ROW_BLK, :],
                              preferred_element_type=F32)
            gs = gs + jnp.sum(jnp.where(pm, aff_ref[0, k, pl.ds(e, 1), :], 0.0), axis=1, keepdims=True)
        return xs.astype(BF16), gs

    def ffn(xsb):
        gt = jnp.dot(xsb, wgb[...], preferred_element_type=F32)
        up = jnp.dot(xsb, wub[...], preferred_element_type=F32)
        return jnp.dot((_silu(gt) * up).astype(BF16), wdb[...], preferred_element_type=F32)

    @pl.when(fits_ref[b] == 1)
    def _():
        xs_acc[...] = jnp.zeros_like(xs_acc)
        gs_acc[...] = jnp.zeros_like(gs_acc)
        r_iota = lax.broadcasted_iota(jnp.int32, (MOE_WIN, ROW_BLK), 0)
        for k in range(LAT_BLKS):
            w0 = pl.multiple_of(w0_ref[(b * LAT_BLKS + k) * N_EXPERTS + e], 16)
            pm = (sl_ref[0, k, pl.ds(e, 1), :] - w0) == r_iota
            xs_acc[pl.ds(w0, MOE_WIN), :] += jnp.dot(jnp.where(pm, 1.0, 0.0).astype(BF16),
                                                     hl_ref[0, ROW_BLK * k:ROW_BLK * k + ROW_BLK, :],
                                                     preferred_element_type=F32)
            gs_acc[pl.ds(w0, MOE_WIN), :] += jnp.sum(jnp.where(pm, al_ref[0, k, pl.ds(e, 1), :], 0.0),
                                                     axis=1, keepdims=True)

    @pl.when(fits_ref[b] != 1)
    def _():
        xs, gs = gather(hl_ref, sl_ref, al_ref, LAT_BLKS, CAP_LAT)
        xs_acc[...] = xs.astype(F32)
        gs_acc[...] = gs

    yl_ref[0, 0] = (ffn(xs_acc[...].astype(BF16)) * gs_acc[...]).astype(yl_ref.dtype)
    if has_ctx:
        xs, gs = gather(hc_ref, sc_ref, ac_ref, 1, CAP_CTX)
        row = pl.multiple_of(b * CAP_CTX, CAP_CTX)
        xc_all[pl.ds(row, CAP_CTX), :] = xs
        gc_all[pl.ds(row, CAP_CTX), :] = gs

        @pl.when(b == pl.num_programs(1) - 1)
        def _():
            yc_ref[0] = (ffn(xc_all[...]) * gc_all[...]).astype(yc_ref.dtype)


def _experts(win0, fits, h2, slot_lat, slot_ctx, aff, w_gate, w_up, w_down, l, has_ctx):
    nbatch = h2.shape[0]
    idx4 = lambda e, b: (b, 0, 0, 0)
    smem = pl.BlockSpec(memory_space=pltpu.SMEM)
    in_specs = [smem, smem,
                pl.BlockSpec((1, T, D), lambda e, b: (b, 0, 0)),
                pl.BlockSpec((1, LAT_BLKS, N_EXPERTS, ROW_BLK), idx4),
                pl.BlockSpec((1, LAT_BLKS, N_EXPERTS, ROW_BLK), idx4)]
    args = [win0, fits, h2, slot_lat, aff]
    out_specs = [pl.BlockSpec((1, 1, CAP_LAT, D), lambda e, b: (b, e, 0, 0))]
    out_shape = [jax.ShapeDtypeStruct((nbatch, N_EXPERTS, CAP_LAT, D), BF16)]
    if has_ctx:
        in_specs += [pl.BlockSpec((1, LC, D), lambda e, b: (b, LAT_BLKS, 0)),
                     pl.BlockSpec((1, 1, N_EXPERTS, ROW_BLK), idx4),
                     pl.BlockSpec((1, 1, N_EXPERTS, ROW_BLK), lambda e, b: (b, LAT_BLKS, 0, 0))]
        args += [h2, slot_ctx, aff]
        out_specs.append(pl.BlockSpec((1, nbatch * CAP_CTX, D), lambda e, b: (e, 0, 0)))
        out_shape.append(jax.ShapeDtypeStruct((N_EXPERTS, nbatch * CAP_CTX, D), BF16))
    wspec = pl.BlockSpec((1, 1, D, D), lambda e, b: (l, e, 0, 0))
    in_specs += [wspec, wspec, wspec]
    args += [w_gate, w_up, w_down]
    scratch = [pltpu.VMEM((D, D), BF16)] * 3 + [pltpu.VMEM((CAP_LAT, D), F32), pltpu.VMEM((CAP_LAT, 1), F32)]
    if has_ctx:
        scratch += [pltpu.VMEM((nbatch * CAP_CTX, D), BF16), pltpu.VMEM((nbatch * CAP_CTX, 1), F32)]
    return pl.pallas_call(
        functools.partial(_expert_kernel, has_ctx=has_ctx),
        grid=(N_EXPERTS, nbatch),
        in_specs=in_specs,
        out_specs=out_specs,
        out_shape=out_shape,
        scratch_shapes=scratch,
        compiler_params=_cp("arbitrary", "arbitrary"),
        name="experts",
    )(*args)


def _combine_kernel(*refs, has_ctx):
    if has_ctx:
        w0_ref, fits_ref, x1_ref, yl_ref, scl_ref, yc_ref, scc_ref, gate2_ref, g_ref, o_ref, ycat = refs
    else:
        w0_ref, fits_ref, x1_ref, yl_ref, scl_ref, gate2_ref, g_ref, o_ref, ycat = refs
    b = pl.program_id(0)
    j = pl.program_id(1)

    def finish(acc):
        o_ref[0] = x1_ref[0] + gate2_ref[0] * _rms(acc, g_ref[...])

    def comb(sc_ref, y_of, cap):
        lane = lax.broadcasted_iota(jnp.int32, (ROW_BLK, cap), 1)
        sc = sc_ref[0]
        acc = jnp.zeros((ROW_BLK, D), F32)
        for e in range(N_EXPERTS):
            pt = jnp.where(sc[:, e:e + 1] == lane, 1.0, 0.0).astype(BF16)
            acc = acc + jnp.dot(pt, y_of(e), preferred_element_type=F32)
        finish(acc)

    def comb_windowed():
        lane = lax.broadcasted_iota(jnp.int32, (ROW_BLK, ROW_BLK), 1)
        sc = scl_ref[0]
        per = ROW_BLK // MOE_WIN
        if per * MOE_WIN < ROW_BLK:
            ycat[:, per * MOE_WIN:, :] = jnp.zeros((2, ROW_BLK - per * MOE_WIN, D), BF16)
        acc = jnp.zeros((ROW_BLK, D), F32)
        for gi, e0 in enumerate(range(0, N_EXPERTS, per)):
            hit = None
            for q, e in enumerate(range(e0, min(e0 + per, N_EXPERTS))):
                w0 = pl.multiple_of(w0_ref[(b * LAT_BLKS + j) * N_EXPERTS + e], 16)
                d = sc[:, e:e + 1] - w0
                d = jnp.where(jnp.logical_and(d >= 0, d < MOE_WIN), d + MOE_WIN * q, -1)
                hit = (d == lane) if hit is None else jnp.logical_or(hit, d == lane)
                ycat[gi % 2, MOE_WIN * q:MOE_WIN * q + MOE_WIN, :] = yl_ref[0, e, pl.ds(w0, MOE_WIN), :]
            acc = acc + jnp.dot(jnp.where(hit, 1.0, 0.0).astype(BF16), ycat[gi % 2], preferred_element_type=F32)
        finish(acc)

    lat_y = lambda e: yl_ref[0, e]
    fits = fits_ref[b] == 1

    @pl.when(jnp.logical_and(j < LAT_BLKS, fits))
    def _():
        comb_windowed()

    @pl.when(jnp.logical_and(j < LAT_BLKS, jnp.logical_not(fits)))
    def _():
        comb(scl_ref, lat_y, CAP_LAT)

    if has_ctx:
        @pl.when(j == LAT_BLKS)
        def _():
            comb(scc_ref, lambda e: yc_ref[e], CAP_CTX)


def _combine(win0, fits, x1, y_lat, scol_lat, y_ctx, scol_ctx, mod3, g, has_ctx, l):
    nbatch = x1.shape[0]
    nb = NBLK if has_ctx else LAT_BLKS
    ntok = N if has_ctx else T
    smem = pl.BlockSpec(memory_space=pltpu.SMEM)
    in_specs = [smem, smem,
                pl.BlockSpec((1, ROW_BLK, D), lambda b, j: (b, j, 0)),
                pl.BlockSpec((1, N_EXPERTS, CAP_LAT, D), lambda b, j: (b, 0, 0, 0)),
                pl.BlockSpec((1, ROW_BLK, N_EXPERTS), lambda b, j: (b, jnp.minimum(j, LAT_BLKS - 1), 0))]
    args = [win0, fits, x1, y_lat, scol_lat]
    if has_ctx:
        in_specs += [pl.BlockSpec((N_EXPERTS, CAP_CTX, D), lambda b, j: (0, b, 0)),
                     pl.BlockSpec((1, ROW_BLK, N_EXPERTS), lambda b, j: (b, 0, 0))]
        args += [y_ctx, scol_ctx]
    in_specs += [pl.BlockSpec((1, 1, D), lambda b, j: (jnp.where(j == LAT_BLKS, nbatch, b), 0, 5)),
                 _lspec(g, l)]
    args += [mod3, g]
    return pl.pallas_call(
        functools.partial(_combine_kernel, has_ctx=has_ctx),
        grid=(nbatch, nb),
        in_specs=in_specs,
        out_specs=pl.BlockSpec((1, ROW_BLK, D), lambda b, j: (b, j, 0)),
        out_shape=jax.ShapeDtypeStruct((nbatch, ntok, D), F32),
        scratch_shapes=[pltpu.VMEM((2, ROW_BLK, D), BF16)],
        compiler_params=_cp("arbitrary", "arbitrary"),
        name="combine",
    )(*args)


def _rope_tables():
    f = 1.0 / (ROPE_THETA ** (jnp.arange(0, 16, 2, dtype=F32) / 16))
    pos = jnp.arange(T)
    row, col = pos // GRID_W, pos % GRID_W
    ar = row.astype(F32)[:, None] * f[None, :]
    ac = col.astype(F32)[:, None] * f[None, :]
    cos32 = jnp.concatenate([jnp.cos(ar), jnp.cos(ar), jnp.cos(ac), jnp.cos(ac)], axis=-1)
    sin32 = jnp.concatenate([jnp.sin(ar), jnp.sin(ar), jnp.sin(ac), jnp.sin(ac)], axis=-1)
    cos32 = jnp.concatenate([cos32, jnp.ones((LC, 32), F32)], axis=0)
    sin32 = jnp.concatenate([sin32, jnp.zeros((LC, 32), F32)], axis=0)
    return cos32, sin32


def _rot_cols(w):
    a, b, c, d = w[..., 0:8], w[..., 8:16], w[..., 16:24], w[..., 24:32]
    return jnp.concatenate([-b, a, -d, c], axis=-1)


def _mla_weights(w_uq, w_ukv):
    nl, r, _ = w_uq.shape
    wq3 = w_uq.reshape(nl, r, 4, 96)
    z32 = jnp.zeros((nl, r, 4, 32), F32)
    wq = jnp.concatenate([wq3, z32], axis=-1).reshape(nl, r, 512)
    wqr = jnp.concatenate([jnp.zeros((nl, r, 4, 64), F32), _rot_cols(wq3[..., 64:96]), z32], axis=-1).reshape(nl, r, 512)
    rk = w_ukv.shape[1]
    wkv3 = w_ukv.reshape(nl, rk, 4, 128)
    wk = jnp.concatenate([wkv3[..., :64], jnp.zeros((nl, rk, 4, 64), F32)], axis=-1).reshape(nl, rk, 512)
    z64 = jnp.zeros((nl, rk, 64), F32)
    vh = [wkv3[:, :, h, 64:] for h in range(4)]
    wv = jnp.concatenate([vh[0], z64, z64, vh[1], vh[2], z64, z64, vh[3]], axis=-1)
    vone = np.tile(np.repeat(np.array([0.0, 1.0, 1.0, 0.0], np.float32), 64), 2)[None, :]
    e = np.zeros((32, 512), np.float32)
    for h in range(4):
        e[np.arange(32), 128 * h + 64 + np.arange(32)] = 1.0
    return (wq.astype(BF16), wqr.astype(BF16), wk.astype(BF16), wv.astype(BF16), jnp.asarray(e, BF16),
            jnp.asarray(vone))


def _inproj_weights(w):
    o = np.cumsum([0, 256, 128, 32, 768, 256, 256, 768, 8, 4096])
    seg = lambda i: w[:, :, o[i]:o[i + 1]]
    nl = w.shape[0]
    kr = seg(2)
    wm = jnp.concatenate([seg(0), seg(1), kr, _rot_cols(kr), jnp.zeros((nl, D, 64), F32)], axis=-1)
    w2 = jnp.concatenate([seg(5), seg(6), seg(7), jnp.zeros((nl, D, 120), F32)], axis=-1)
    return [a.astype(BF16) for a in (wm, seg(3), seg(4), w2, seg(8))]


def kernel(x, c, ctx, c_ctx, w_ada, b_ada, g_pre_mix, g_post_mix, g_pre_ffn, g_post_ffn, w_in, mla_g_cq, mla_g_ckv, mla_w_uq, mla_w_ukv, na_rpb, s5_a_re, s5_a_im, s5_log_step, s5_b_re, s5_b_im, s5_c_re, s5_c_im, s5_d, s5_w_glu, s5_b_glu, m2_conv_w, m2_conv_b, m2_a_log, m2_dt_bias, m2_d, m2_g_norm, w_branch, w_out, w_router, w_gate, w_up, w_down):
    nbatch = x.shape[0]
    depth = w_ada.shape[0]
    vec = lambda v: v.reshape(depth, 1, -1).astype(F32)

    xs = jnp.concatenate([x, ctx], axis=1)
    cvec = jnp.concatenate([c, c_ctx[None, :], jnp.zeros((7, D), F32)], axis=0)
    cvec = cvec[: ((nbatch + 1 + 7) // 8) * 8]
    b_ada3 = b_ada.reshape(depth, 1, 6 * D)

    cos32, sin32 = _rope_tables()
    qscale = (64 + 32) ** -0.5 * math.log2(math.e)
    lane_is_rope = np.tile(np.concatenate([np.zeros(64, bool), np.ones(32, bool), np.zeros(32, bool)]), 4)
    pick = lambda t32, fill: jnp.where(lane_is_rope[None, :], jnp.tile(jnp.pad(t32, ((0, 0), (64, 32))), (1, 4)), fill)
    cos_q = pick(cos32, 1.0) * qscale
    sin_q = pick(sin32, 0.0) * qscale

    g_pre_mix, g_post_mix, g_pre_ffn, g_post_ffn = vec(g_pre_mix), vec(g_post_mix), vec(g_pre_ffn), vec(g_post_ffn)
    w_in_segs = _inproj_weights(w_in)
    wq, wqr, wk, wv, e_mat, vone = _mla_weights(mla_w_uq, mla_w_ukv)
    mla_g_cq, mla_g_ckv = vec(mla_g_cq), vec(mla_g_ckv)
    na_tabs = _na_tables(na_rpb)
    s5_bmat, s5_avec, s5_cmat = _s5_params(s5_a_re, s5_a_im, s5_log_step, s5_b_re, s5_b_im, s5_c_re, s5_c_im, nbatch)
    s5_d, s5_b_glu, s5_w_glu = vec(s5_d), vec(s5_b_glu), s5_w_glu.astype(BF16)
    m2_conv_w, m2_conv_b = m2_conv_w.astype(F32), vec(m2_conv_b)
    dt_bias = jnp.pad(m2_dt_bias.reshape(depth, 1, 8).astype(F32), ((0, 0), (0, 0), (0, 120)))
    a_row = jnp.pad(-jnp.exp(m2_a_log.astype(F32)).reshape(depth, 1, 8), ((0, 0), (0, 0), (0, 120)))
    a_col = a_row.reshape(depth, 128, 1)
    m2_dvec, m2_g_norm = vec(jnp.repeat(m2_d, 64, axis=-1)), vec(m2_g_norm)
    w_branch, w_out = w_branch.astype(BF16), w_out.astype(BF16)
    w_router_t = jnp.swapaxes(w_router, 1, 2).astype(F32)

    for l in range(depth):
        has_ctx = l < depth - 1
        nb = NBLK if has_ctx else LAT_BLKS
        mod = _ada(cvec, w_ada, b_ada3, l)
        mod3 = mod.reshape(mod.shape[0], 1, 6 * D)
        x2d = xs.reshape(nbatch * N, D)

        p_mla, p_na, p_s5, p_m2, p_gate = _inproj(x2d, g_pre_mix, mod3, w_in_segs, nbatch, l)

        q, k, v = _mla_prep(p_mla, mla_g_cq, mla_g_ckv, wq, wqr, wk, wv, e_mat, vone, cos_q, sin_q, cos32, sin32, l)
        o_mla = _mla_attn(q.reshape(nbatch, N, 512), k.reshape(nbatch, N, 512), v.reshape(nbatch, N, 512), has_ctx)

        o_na = _na_attn(p_na.reshape(nbatch, N, NA_W), na_tabs, has_ctx, l)

        yf, yb = _s5_scan(p_s5.reshape(nbatch, N, S5_W), s5_bmat, s5_avec, s5_cmat, l)
        o_s5 = _s5_glu(p_s5, yf.reshape(nbatch * N, S5_W), yb.reshape(nbatch * N, S5_W), s5_d, s5_w_glu, s5_b_glu, l)

        xbc, dt, dtt, bmt = _m2_prep(p_m2.reshape(nbatch, N, M2_W), m2_conv_w, m2_conv_b, dt_bias, l)
        ssd_f, ssd_b = _ssd(xbc, dt, dtt, bmt, a_row, a_col, l)
        o_m2 = _m2_norm(xbc.reshape(nbatch * N, 768), p_m2, ssd_f.reshape(nbatch * N, 256),
                        ssd_b.reshape(nbatch * N, 256), m2_dvec, m2_g_norm, l)

        x1, h2, aff = _merge(x2d, o_mla.reshape(-1, 256), o_na.reshape(-1, 256), o_s5, o_m2, p_gate,
                             w_branch, w_out, g_post_mix, mod3, g_pre_ffn, w_router_t, nbatch, nb, l)

        slot_lat = _topk(aff, 0, LAT_BLKS, CAP_LAT)
        slot_ctx = _topk(aff, LAT_BLKS, 1, CAP_CTX) if has_ctx else None
        cnt = jnp.sum((slot_lat >= 0).astype(jnp.int32), axis=-1)
        first = jnp.cumsum(cnt, axis=1) - cnt
        win0 = jnp.minimum((first // 16) * 16, CAP_LAT - MOE_WIN)
        fits = jnp.all(first + cnt <= win0 + MOE_WIN, axis=(1, 2)).astype(jnp.int32)
        win0 = win0.reshape(-1)
        ys = _experts(win0, fits, h2.reshape(nbatch, nb * ROW_BLK, D), slot_lat, slot_ctx, aff, w_gate, w_up, w_down,
                      l, has_ctx)
        scol_lat = jnp.transpose(slot_lat, (0, 1, 3, 2)).reshape(nbatch, T, N_EXPERTS)
        scol_ctx = jnp.transpose(slot_ctx, (0, 1, 3, 2)).reshape(nbatch, LC, N_EXPERTS) if has_ctx else None
        xs = _combine(win0, fits, x1.reshape(nbatch, nb * ROW_BLK, D), ys[0], scol_lat, ys[1] if has_ctx else None,
                      scol_ctx, mod3, g_post_ffn, has_ctx, l)
    return xs
```

```python
import functools
import math

import numpy as np
import jax
import jax.numpy as jnp
from jax import lax
from jax.experimental import pallas as pl
from jax.experimental.pallas import tpu as pltpu

F32 = jnp.float32
BF16 = jnp.bfloat16

D = 1024
T = 2048
LC = 256
N = T + LC
GRID_W = 64
ROW_BLK = 256
NBLK = N // ROW_BLK
LAT_BLKS = T // ROW_BLK
EPS = 1e-6
N_EXPERTS = 16
CAP_LAT = 2 * T // N_EXPERTS
CAP_CTX = 2 * LC // N_EXPERTS
ROPE_THETA = 10000.0
NEG = -1e30

MLA_W = 512
NA_W = 768
S5_W = 256
M2_W = 1152
GATE_W = 4096

MOE_WIN = 80
S5_CHUNK = 64
SSD_CHUNK = 128

VMEM_LIMIT = 56 * 1024 * 1024


def _cp(*sem):
    return pltpu.CompilerParams(dimension_semantics=sem, vmem_limit_bytes=VMEM_LIMIT)


def _lspec(a, l, *lead):
    nd = a.ndim - 1 - len(lead)
    return pl.BlockSpec((None,) * (1 + len(lead)) + a.shape[1 + len(lead):], lambda *_: (l,) + lead + (0,) * nd)


def _dot(a, b):
    return jnp.dot(a.astype(BF16), b.astype(BF16), preferred_element_type=F32)


def _dot_nt(a, b):
    return lax.dot_general(a.astype(BF16), b.astype(BF16), (((1,), (1,)), ((), ())),
                           preferred_element_type=F32)


def _split3(a):
    hi = a.astype(BF16)
    r = a - hi.astype(F32)
    mid = r.astype(BF16)
    lo = (r - mid.astype(F32)).astype(BF16)
    return hi, mid, lo


def _dot_hi(a, b):
    ah, am, _ = _split3(a)
    bh, bm, _ = _split3(b)
    f = lambda x, y: jnp.dot(x, y, preferred_element_type=F32)
    return f(ah, bh) + (f(ah, bm) + f(am, bh))


def _dot_hi_nt(a, b):
    ah, am, _ = _split3(a)
    bh, bm, _ = _split3(b)
    f = lambda x, y: lax.dot_general(x, y, (((1,), (1,)), ((), ())), preferred_element_type=F32)
    return f(ah, bh) + (f(ah, bm) + f(am, bh))


def _dot_exact_lhs(m_bf16, a):
    h, m, l = _split3(a)
    f = lambda y: jnp.dot(m_bf16, y, preferred_element_type=F32)
    return f(h) + (f(m) + f(l))


def _dot_exact_rhs(a, m_bf16):
    h, m, l = _split3(a)
    f = lambda y: jnp.dot(y, m_bf16, preferred_element_type=F32)
    return f(h) + (f(m) + f(l))


def _sigmoid(x):
    return 0.5 * jnp.tanh(0.5 * x) + 0.5


def _silu(x):
    return x * _sigmoid(x)


def _rms(x, g):
    return x * lax.rsqrt(jnp.mean(x * x, axis=-1, keepdims=True) + EPS) * g


def _mod_row(nb):
    def f(i, nbatch):
        return jnp.where(i % nb == LAT_BLKS, nbatch, i // nb)
    return f


def _flat_blk(i, nb):
    return (i // nb) * NBLK + i % nb


def _ada_kernel(c_ref, w_ref, b_ref, o_ref):
    c = c_ref[...]
    o_ref[...] = _dot_hi(_silu(c), w_ref[0]) + b_ref[0]


def _ada(cvec, w_ada, b_ada3, l):
    rows = cvec.shape[0]
    return pl.pallas_call(
        _ada_kernel,
        grid=(6,),
        in_specs=[pl.BlockSpec((rows, D), lambda k: (0, 0)),
                  pl.BlockSpec((1, D, D), lambda k: (l, 0, k)),
                  pl.BlockSpec((1, 1, D), lambda k: (l, 0, k))],
        out_specs=pl.BlockSpec((rows, D), lambda k: (0, k)),
        out_shape=jax.ShapeDtypeStruct((rows, 6 * D), F32),
        compiler_params=_cp("arbitrary"),
        name="ada",
    )(cvec, w_ada, b_ada3)


def _inproj_kernel(x_ref, g_ref, sh_ref, sc_ref, wm, wn, ws, w2, wg, om, on, os_, o2, og):
    h = (_rms(x_ref[...], g_ref[...]) * (1.0 + sc_ref[0]) + sh_ref[0]).astype(BF16)
    om[...] = jnp.dot(h, wm[...], preferred_element_type=F32)
    on[...] = jnp.dot(h, wn[...], preferred_element_type=F32).astype(BF16)
    os_[...] = jnp.dot(h, ws[...], preferred_element_type=F32)
    o2[...] = jnp.dot(h, w2[...], preferred_element_type=F32)
    og[...] = jnp.dot(h, wg[...], preferred_element_type=F32)


def _inproj(x2d, g, mod3, ws, nbatch, l):
    rows = x2d.shape[0]
    nblk = rows // ROW_BLK
    mrow = _mod_row(NBLK)
    full = lambda w: _lspec(w, l)
    widths = (MLA_W, NA_W, S5_W, M2_W, GATE_W)
    dts = (F32, BF16, F32, F32, F32)
    return pl.pallas_call(
        _inproj_kernel,
        grid=(nblk,),
        in_specs=[pl.BlockSpec((ROW_BLK, D), lambda i: (i, 0)),
                  full(g),
                  pl.BlockSpec((1, 1, D), lambda i: (mrow(i, nbatch), 0, 0)),
                  pl.BlockSpec((1, 1, D), lambda i: (mrow(i, nbatch), 0, 1))]
                 + [full(w) for w in ws],
        out_specs=[pl.BlockSpec((ROW_BLK, w), lambda i: (i, 0)) for w in widths],
        out_shape=[jax.ShapeDtypeStruct((rows, w), dt) for w, dt in zip(widths, dts)],
        compiler_params=_cp("arbitrary"),
        name="inproj",
    )(x2d, g, mod3, mod3, *ws)


def _mla_prep_kernel(p_ref, gq_ref, gkv_ref, wq_ref, wqr_ref, wk_ref, wv_ref, e_ref, vone_ref,
                     cos_ref, sin_ref, ck_ref, sk_ref, q_out, k_out, v_out):
    p = p_ref[...]
    cqn = _rms(p[:, :256], gq_ref[...]).astype(BF16)
    q = jnp.dot(cqn, wq_ref[...], preferred_element_type=F32)
    qr = jnp.dot(cqn, wqr_ref[...], preferred_element_type=F32)
    q_out[...] = (q * cos_ref[...] + qr * sin_ref[...]).astype(BF16)
    ckvn = _rms(p[:, 256:384], gkv_ref[...]).astype(BF16)
    kro = (p[:, 384:416] * ck_ref[...] + p[:, 416:448] * sk_ref[...]).astype(BF16)
    k = jnp.dot(ckvn, wk_ref[...], preferred_element_type=F32) + jnp.dot(kro, e_ref[...], preferred_element_type=F32)
    k_out[...] = k.astype(BF16)
    v_out[...] = (jnp.dot(ckvn, wv_ref[...], preferred_element_type=F32) + vone_ref[...]).astype(BF16)


def _mla_prep(p_mla, gq, gkv, wq, wqr, wk, wv, e, vone, cos_q, sin_q, cos_k, sin_k, l):
    rows = p_mla.shape[0]
    full = lambda w: _lspec(w, l)
    const = lambda w: pl.BlockSpec(w.shape, lambda i: (0, 0))
    tab = lambda w: pl.BlockSpec((ROW_BLK, w), lambda i: (i % NBLK, 0))
    return pl.pallas_call(
        _mla_prep_kernel,
        grid=(rows // ROW_BLK,),
        in_specs=[pl.BlockSpec((ROW_BLK, MLA_W), lambda i: (i, 0)),
                  full(gq), full(gkv), full(wq), full(wqr), full(wk), full(wv), const(e), const(vone),
                  tab(512), tab(512), tab(32), tab(32)],
        out_specs=[pl.BlockSpec((ROW_BLK, 512), lambda i: (i, 0))] * 3,
        out_shape=[jax.ShapeDtypeStruct((rows, 512), BF16)] * 3,
        compiler_params=_cp("arbitrary"),
        name="mla_prep",
    )(p_mla, gq, gkv, wq, wqr, wk, wv, e, vone, cos_q, sin_q, cos_k, sin_k)


def _mla_attn_kernel(q_ref, k_ref, v_ref, o_ref, *, has_ctx):
    lane = lax.broadcasted_iota(jnp.int32, (ROW_BLK, 128), 1)

    def run(k0, nk):
        for hp in range(2):
            pv = []
            for h in (2 * hp, 2 * hp + 1):
                qh = q_ref[0, :, 128 * h:128 * h + 128]
                kh = k_ref[0, k0:k0 + nk, 128 * h:128 * h + 128]
                s = _dot_nt(qh, kh)
                p = jnp.exp2(s - jnp.max(s, axis=-1, keepdims=True)).astype(BF16)
                pv.append(jnp.dot(p, v_ref[0, k0:k0 + nk, 128 * h:128 * h + 128], preferred_element_type=F32))
            oa, ob = pv
            o = jnp.where(lane < 64, oa * (1.0 / oa[:, 64:65]), ob * (1.0 / ob[:, 0:1]))
            o_ref[0, :, 128 * hp:128 * hp + 128] = o.astype(o_ref.dtype)

    if has_ctx:
        j = pl.program_id(1)

        @pl.when(j < LAT_BLKS)
        def _():
            run(0, N)

        @pl.when(j == LAT_BLKS)
        def _():
            run(T, LC)
    else:
        run(0, N)


def _mla_attn(q, k, v, has_ctx):
    nbatch = q.shape[0]
    nb = NBLK if has_ctx else LAT_BLKS
    return pl.pallas_call(
        functools.partial(_mla_attn_kernel, has_ctx=has_ctx),
        grid=(nbatch, nb),
        in_specs=[pl.BlockSpec((1, ROW_BLK, 512), lambda b, j: (b, j, 0)),
                  pl.BlockSpec((1, N, 512), lambda b, j: (b, 0, 0)),
                  pl.BlockSpec((1, N, 512), lambda b, j: (b, 0, 0))],
        out_specs=pl.BlockSpec((1, ROW_BLK, 256), lambda b, j: (b, j, 0)),
        out_shape=jax.ShapeDtypeStruct((nbatch, nb * ROW_BLK, 256), BF16),
        compiler_params=_cp("arbitrary", "arbitrary"),
        name="mla_attn",
    )(q, k, v)


def _na_kernel(q_ref, kc_ref, k0_ref, k1_ref, k2_ref, vc_ref, v0_ref, v1_ref, v2_ref, b_ref, o_ref, *, has_ctx):
    scale = jnp.asarray(0.125, BF16)

    lane = lax.broadcasted_iota(jnp.int32, (ROW_BLK, 128), 1)
    lane64 = lax.broadcasted_iota(jnp.int32, (GRID_W, 128), 1) < 64
    grp = jnp.minimum(pl.program_id(0), LAT_BLKS - 1)
    krow0 = 4 * jnp.clip(grp - 1, 0, LAT_BLKS - 3)

    def bias_piece(h, i):
        rows = []
        for qr in range(4):
            r = 4 * grp + qr
            rs = jnp.clip(r - 4, 0, 24)
            tiles = []
            for pr in range(2):
                kr = krow0 + (4 * i + 2 * pr)
                t = b_ref[h, jnp.clip(kr - r + 8, 0, 15)]
                ok_a = jnp.logical_and(kr >= rs, kr < rs + 8).astype(jnp.int32)
                ok_b = jnp.logical_and(kr + 1 >= rs, kr + 1 < rs + 8).astype(jnp.int32)
                tiles.append(jnp.where(jnp.where(lane64, ok_a, ok_b) > 0, t, NEG))
            rows.append(jnp.concatenate(tiles, axis=1))
        return jnp.concatenate(rows, axis=0)

    def heads(win):
        kws = (k0_ref, k1_ref, k2_ref)
        vws = (v0_ref, v1_ref, v2_ref)
        for hp in range(2):
            sl = slice(128 * hp, 128 * hp + 128)
            qp = q_ref[0, :, sl] * scale
            outs = []
            for hh in range(2):
                h = 2 * hp + hh
                qh = jnp.where((lane < 64) if hh == 0 else (lane >= 64), qp, jnp.zeros_like(qp))
                s_c = _dot_nt(qh, kc_ref[0, :, sl])
                s_w = []
                smax = s_c
                if win:
                    for i in range(3):
                        s = _dot_nt(qh, kws[i][0, :, sl]) + bias_piece(h, i)
                        s_w.append(s)
                        smax = jnp.maximum(smax, s)
                m = jnp.max(smax, axis=-1, keepdims=True)
                p = jnp.exp(s_c - m)
                psum = p
                o = _dot(p, vc_ref[0, :, sl])
                for i, s in enumerate(s_w):
                    p = jnp.exp(s - m)
                    psum = psum + p
                    o = o + _dot(p, vws[i][0, :, sl])
                outs.append(o * (1.0 / jnp.sum(psum, axis=-1, keepdims=True)))
            o_ref[0, :, sl] = jnp.where(lane < 64, outs[0], outs[1]).astype(o_ref.dtype)

    if has_ctx:
        g = pl.program_id(0)

        @pl.when(g < LAT_BLKS)
        def _():
            heads(True)

        @pl.when(g == LAT_BLKS)
        def _():
            heads(False)
    else:
        heads(True)


def _na_attn(qkv, bias, has_ctx, l):
    nbatch = qkv.shape[0]
    ng = NBLK if has_ctx else LAT_BLKS
    j0 = lambda g: jnp.clip(g - 1, 0, LAT_BLKS - 3)
    blk = lambda f: pl.BlockSpec((1, ROW_BLK, 256), f)
    return pl.pallas_call(
        functools.partial(_na_kernel, has_ctx=has_ctx),
        grid=(ng, nbatch),
        in_specs=[blk(lambda g, b: (b, g, 0)),
                  blk(lambda g, b: (b, LAT_BLKS, 1)),
                  blk(lambda g, b: (b, j0(g), 1)),
                  blk(lambda g, b: (b, j0(g) + 1, 1)),
                  blk(lambda g, b: (b, j0(g) + 2, 1)),
                  blk(lambda g, b: (b, LAT_BLKS, 2)),
                  blk(lambda g, b: (b, j0(g), 2)),
                  blk(lambda g, b: (b, j0(g) + 1, 2)),
                  blk(lambda g, b: (b, j0(g) + 2, 2)),
                  _lspec(bias, l)],
        out_specs=blk(lambda g, b: (b, g, 0)),
        out_shape=jax.ShapeDtypeStruct((nbatch, ng * ROW_BLK, 256), BF16),
        compiler_params=_cp("arbitrary", "arbitrary"),
        name="na_attn",
    )(qkv, qkv, qkv, qkv, qkv, qkv, qkv, qkv, qkv, bias)


def _na_col_structure():
    onehot = np.zeros((31, 64, 64), np.float32)
    colmask = np.zeros((64, 64), bool)
    for c in range(64):
        cs = min(max(c - 8, 0), 48)
        for kc in range(cs, cs + 16):
            onehot[kc - c + 15, c, kc] = 1.0
            colmask[c, kc] = True
    return np.tile(onehot.reshape(31, 4096), (3, 1)), colmask


_NA_COL_ONEHOT, _NA_COLMASK = _na_col_structure()


def _na_tables(rpb):
    nl = rpb.shape[0]
    r = rpb.astype(F32)
    hi = lax.reduce_precision(r, 8, 7)
    mid = lax.reduce_precision(r - hi, 8, 7)
    parts = jnp.stack([hi, mid, r - hi - mid], axis=3).reshape(nl * 60, 93)
    tz = jnp.dot(parts, _NA_COL_ONEHOT, preferred_element_type=F32).reshape(nl, 4, 15, 64, 64)
    tz = jnp.where(_NA_COLMASK, tz, NEG)
    edge = jnp.full((nl, 4, 1, 64, 64), NEG, F32)
    tz17 = jnp.concatenate([edge, tz, edge], axis=2)
    return jnp.concatenate([tz17[:, :, :16], tz17[:, :, 1:]], axis=-1)


def _s5_kernel(uf_ref, ub_ref, bf_ref, bb_ref, af_ref, ab_ref, cf_ref, cb_ref, yf_ref, yb_ref,
               hf, hb, buf_f, buf_b, tm_f, tm_b):
    i = pl.program_id(0)
    half = 1024
    nb = uf_ref.shape[0]

    @pl.when(i == 0)
    def _():
        hf[...] = jnp.zeros_like(hf)
        hb[...] = jnp.zeros_like(hb)

    def expand(u_ref, tm, b_ref, buf):
        for b in range(nb):
            for c in range(2):
                tm[c, pl.ds(b, S5_CHUNK, stride=nb), :] = u_ref[b, :, 128 * c:128 * c + 128]
        buf[...] = _dot(jnp.concatenate([tm[0], tm[1]], axis=1), b_ref[...])

    def scan(buf, a_ref, h, order):
        hr, hi = h[:, :half], h[:, half:]
        ar, ai = a_ref[:, :half], a_ref[:, half:]
        for k in order:
            rows = slice(k * nb, (k + 1) * nb)
            nr = ar * hr - ai * hi + buf[rows, :half]
            ni = ar * hi + ai * hr + buf[rows, half:]
            buf[rows, :half] = nr
            buf[rows, half:] = ni
            hr, hi = nr, ni
        h[:, :half] = hr
        h[:, half:] = hi

    def readout(buf, c_ref, tm, y_ref):
        y = _dot(buf[...], c_ref[...])
        for c in range(2):
            tm[c] = y[:, 128 * c:128 * c + 128]
        for b in range(nb):
            for c in range(2):
                y_ref[b, :, 128 * c:128 * c + 128] = tm[c, pl.ds(b, S5_CHUNK, stride=nb), :]

    expand(uf_ref, tm_f, bf_ref, buf_f)
    expand(ub_ref, tm_b, bb_ref, buf_b)
    scan(buf_f, af_ref, hf, range(S5_CHUNK))
    readout(buf_f, cf_ref, tm_f, yf_ref)
    scan(buf_b, ab_ref, hb, range(S5_CHUNK - 1, -1, -1))
    readout(buf_b, cb_ref, tm_b, yb_ref)


def _s5_scan(u, bmat, avec, cmat, l):
    nbatch = u.shape[0]
    cr = S5_CHUNK * nbatch
    nch = N // S5_CHUNK
    nctx = LC // S5_CHUNK
    fidx = lambda i: jnp.where(i < nctx, nch - nctx + i, i - nctx)
    bidx = lambda i: nch - 1 - i
    ublk = lambda f: pl.BlockSpec((nbatch, S5_CHUNK, 256), lambda i: (0, f(i), 0))
    return pl.pallas_call(
        _s5_kernel,
        grid=(nch,),
        in_specs=[ublk(fidx), ublk(bidx),
                  _lspec(bmat, l, 0), _lspec(bmat, l, 1), _lspec(avec, l, 0), _lspec(avec, l, 1),
                  _lspec(cmat, l, 0), _lspec(cmat, l, 1)],
        out_specs=[ublk(fidx), ublk(bidx)],
        out_shape=[jax.ShapeDtypeStruct((nbatch, N, 256), F32)] * 2,
        scratch_shapes=[pltpu.VMEM((nbatch, 2048), F32), pltpu.VMEM((nbatch, 2048), F32),
                        pltpu.VMEM((cr, 2048), F32), pltpu.VMEM((cr, 2048), F32),
                        pltpu.VMEM((2, cr, 128), F32), pltpu.VMEM((2, cr, 128), F32)],
        compiler_params=_cp("arbitrary"),
        name="s5_scan",
    )(u, u, bmat, bmat, avec, avec, cmat, cmat)


def _s5_glu_kernel(u_ref, yf_ref, yb_ref, d_ref, w_ref, b_ref, o_ref):
    y = d_ref[...] * u_ref[...] + yf_ref[...] + yb_ref[...]
    z = y * (0.5 * (1.0 + jnp.tanh(math.sqrt(2.0 / math.pi) * (y + 0.044715 * (y * y * y)))))
    o_ref[...] = (z * _sigmoid(_dot(z, w_ref[...]) + b_ref[...])).astype(o_ref.dtype)


def _s5_glu(u_tb, yf, yb, d, w, b, l):
    rows = u_tb.shape[0]
    rb = 512
    blk = pl.BlockSpec((rb, 256), lambda i: (i, 0))
    full = lambda a: _lspec(a, l)
    return pl.pallas_call(
        _s5_glu_kernel,
        grid=(rows // rb,),
        in_specs=[blk, blk, blk, full(d), full(w), full(b)],
        out_specs=blk,
        out_shape=jax.ShapeDtypeStruct((rows, 256), BF16),
        compiler_params=_cp("arbitrary"),
        name="s5_glu",
    )(u_tb, yf, yb, d, w, b)


def _s5_params(a_re, a_im, log_step, b_re, b_im, c_re, c_im, nbatch):
    nl = a_re.shape[0]
    eye = jnp.eye(16, dtype=F32)[:, None, :, None]
    a = lax.complex(a_re.astype(F32), a_im.astype(F32))
    abar = jnp.exp(jnp.exp(log_step.astype(F32))[..., None] * a)
    bbar = ((abar - 1.0) / a)[..., None] * lax.complex(b_re.astype(F32), b_im.astype(F32))
    blk_in = lambda m: (eye * jnp.swapaxes(m, -1, -2)[..., :, :, None, :]).reshape(nl, 2, 256, 1024)
    bmat = jnp.concatenate([blk_in(jnp.real(bbar)), blk_in(jnp.imag(bbar))], axis=-1).astype(BF16)
    avec = jnp.concatenate([jnp.real(abar).reshape(nl, 2, 1, 1024), jnp.imag(abar).reshape(nl, 2, 1, 1024)], axis=-1)
    avec = jnp.broadcast_to(avec, (nl, 2, nbatch, 2048))
    blk_out = lambda m: (eye * jnp.swapaxes(m, -1, -2)[..., :, :, None, :]).reshape(nl, 2, 1024, 256)
    cmat = jnp.concatenate([blk_out(c_re.astype(F32)), -blk_out(c_im.astype(F32))], axis=-2).astype(BF16)
    return bmat, avec, cmat


def _m2_prep_kernel(x_ref, dt_ref, w_ref, b_ref, dtb_ref, xo_ref, dtt_ref, bmt_ref):
    c = pl.program_id(1)
    x = x_ref[0]
    t = lax.broadcasted_iota(jnp.int32, x.shape, 0)
    m2 = ((t >= 2) & (t < T)) | (t >= T + 2)
    m1 = ((t >= 1) & (t < T)) | (t >= T + 1)
    p1 = (t <= T - 2) | ((t >= T) & (t <= N - 2))
    w = w_ref[...]
    y = (w[0:1] * jnp.where(m2, pltpu.roll(x, 2, 0), 0.0)
         + w[1:2] * jnp.where(m1, pltpu.roll(x, 1, 0), 0.0)
         + w[2:3] * x
         + w[3:4] * jnp.where(p1, pltpu.roll(x, N - 1, 0), 0.0)) + b_ref[...]
    act = _silu(y)
    xo_ref[0] = act

    @pl.when(c == 0)
    def _():
        v = dt_ref[0] + dtb_ref[...]
        dtt_ref[0] = (jnp.maximum(v, 0.0) + jnp.log1p(jnp.exp(-jnp.abs(v)))).T

    @pl.when(c == 1)
    def _():
        bmt_ref[0] = act.T.astype(BF16)


def _m2_prep(p_m2, conv_w, conv_b, dtb, l):
    nbatch = p_m2.shape[0]
    return pl.pallas_call(
        _m2_prep_kernel,
        grid=(nbatch, 3),
        in_specs=[pl.BlockSpec((1, N, 256), lambda b, c: (b, 0, 1 + c)),
                  pl.BlockSpec((1, N, 128), lambda b, c: (b, 0, 8)),
                  pl.BlockSpec((None, 4, 256), lambda b, c: (l, 0, c)),
                  pl.BlockSpec((None, 1, 256), lambda b, c: (l, 0, c)),
                  _lspec(dtb, l)],
        out_specs=[pl.BlockSpec((1, N, 256), lambda b, c: (b, 0, c)),
                   pl.BlockSpec((1, 128, N), lambda b, c: (b, 0, 0)),
                   pl.BlockSpec((1, 256, N), lambda b, c: (b, 0, 0))],
        out_shape=[jax.ShapeDtypeStruct((nbatch, N, 768), F32),
                   jax.ShapeDtypeStruct((nbatch, 128, N), F32),
                   jax.ShapeDtypeStruct((nbatch, 256, N), BF16)],
        compiler_params=_cp("arbitrary", "arbitrary"),
        name="m2_prep",
    )(p_m2, p_m2, conv_w, conv_b, dtb)


def _ssd_kernel(xf_ref, dttf_ref, bmtf_ref, xb_ref, dttb_ref, bmtb_ref, acol_ref, yf_ref, yb_ref, hs):
    i = pl.program_id(1)
    L = SSD_CHUNK

    @pl.when(i == 0)
    def _():
        hs[...] = jnp.zeros_like(hs)

    li = lax.broadcasted_iota(jnp.int32, (L, L), 0)
    si = lax.broadcasted_iota(jnp.int32, (L, L), 1)
    left = si < 64
    pick = lambda a, b: jnp.where(left, a, b)

    def direction(x_ref, dtt_ref, bmt_ref, y_ref, d, causal, causal_t, last):
        cz = jnp.where(causal, 1.0, 0.0).astype(BF16)
        czt = jnp.where(causal_t, 1.0, 0.0).astype(BF16)
        xbc = x_ref[0]
        dtt = dtt_ref[0]
        parts = jnp.concatenate([p.astype(F32) for p in _split3(dtt * acol_ref[...])], axis=1)
        cumt = jnp.dot(parts.astype(BF16), jnp.concatenate([czt, czt, czt], axis=0),
                       preferred_element_type=F32)
        reps = jnp.concatenate([jnp.broadcast_to(parts[4 * d + h:4 * d + h + 1, :], (L, 3 * L)) for h in range(4)],
                               axis=0).astype(BF16)
        cumb = lax.dot_general(jnp.concatenate([cz, cz, cz], axis=1), reps, (((1,), (1,)), ((), ())),
                               preferred_element_type=F32)
        for g in range(2):
            xp = xbc[:, 128 * g:128 * g + 128].astype(BF16)
            cm = xbc[:, 512 + 128 * g:640 + 128 * g].astype(BF16)
            bmt = bmt_ref[0, 128 * g:128 * g + 128, :]
            hprev = hs[d, g]
            go = jnp.dot(cm, jnp.concatenate([bmt, hprev.astype(BF16)], axis=1), preferred_element_type=F32)
            gmat, yo = go[:, :L], go[:, L:]
            bmf = bmt.astype(F32)
            lhs, cbs, cls = [], [], []
            for hh in range(2):
                h = 2 * g + hh
                cb = cumb[:, L * h:L * h + L]
                crow = cumt[4 * d + h:4 * d + h + 1, :]
                dtrow = dtt[4 * d + h:4 * d + h + 1, :]
                cl = cb[last:last + 1, :]
                lhs.append((gmat * jnp.where(causal, jnp.exp(cb - crow), 0.0) * dtrow).astype(BF16))
                lhs.append((bmf * (jnp.exp(cl - crow) * dtrow)).astype(BF16))
                cbs.append(cb)
                cls.append(cl)
            big = jnp.dot(jnp.concatenate(lhs, axis=0), xp, preferred_element_type=F32)
            y_ref[0, :, 128 * g:128 * g + 128] = pick(big[0:L], big[2 * L:3 * L]) + yo * jnp.exp(pick(cbs[0], cbs[1]))
            hs[d, g] = hprev * jnp.exp(pick(cls[0], cls[1])) + pick(big[L:2 * L], big[3 * L:4 * L])

    direction(xf_ref, dttf_ref, bmtf_ref, yf_ref, 0, si <= li, li <= si, L - 1)
    direction(xb_ref, dttb_ref, bmtb_ref, yb_ref, 1, si >= li, li >= si, 0)


def _ssd(xbc, dtt, bmt, a_col, l):
    nbatch = xbc.shape[0]
    nch = N // SSD_CHUNK
    nctx = LC // SSD_CHUNK
    fidx = lambda i: jnp.where(i < nctx, nch - nctx + i, i - nctx)
    bidx = lambda i: nch - 1 - i
    ins = lambda f: [pl.BlockSpec((1, SSD_CHUNK, 768), lambda b, i: (b, f(i), 0)),
                     pl.BlockSpec((1, 128, SSD_CHUNK), lambda b, i: (b, 0, f(i))),
                     pl.BlockSpec((1, 256, SSD_CHUNK), lambda b, i: (b, 0, f(i)))]
    return pl.pallas_call(
        _ssd_kernel,
        grid=(nbatch, nch),
        in_specs=ins(fidx) + ins(bidx) + [_lspec(a_col, l)],
        out_specs=[pl.BlockSpec((1, SSD_CHUNK, 256), lambda b, i: (b, fidx(i), 0)),
                   pl.BlockSpec((1, SSD_CHUNK, 256), lambda b, i: (b, bidx(i), 0))],
        out_shape=[jax.ShapeDtypeStruct((nbatch, N, 256), F32)] * 2,
        scratch_shapes=[pltpu.VMEM((2, 2, 128, 128), F32)],
        compiler_params=_cp("arbitrary", "arbitrary"),
        name="ssd",
    )(xbc, dtt, bmt, xbc, dtt, bmt, a_col)


def _m2_norm_kernel(x_ref, z_ref, yf_ref, yb_ref, d_ref, g_ref, o_ref):
    y = d_ref[...] * x_ref[...] + yf_ref[...] + yb_ref[...]
    o_ref[...] = _rms(y * _silu(z_ref[...]), g_ref[...]).astype(o_ref.dtype)


def _m2_norm(xbc2d, p_m2_2d, yf2d, yb2d, dvec, gn, l):
    rows = xbc2d.shape[0]
    blk = pl.BlockSpec((ROW_BLK, 256), lambda i: (i, 0))
    full = lambda a: _lspec(a, l)
    return pl.pallas_call(
        _m2_norm_kernel,
        grid=(rows // ROW_BLK,),
        in_specs=[blk, blk, blk, blk, full(dvec), full(gn)],
        out_specs=blk,
        out_shape=jax.ShapeDtypeStruct((rows, 256), BF16),
        compiler_params=_cp("arbitrary"),
        name="m2_norm",
    )(xbc2d, p_m2_2d, yf2d, yb2d, dvec, gn)


def _merge_kernel(x_ref, oa_ref, on_ref, os_ref, om_ref, gl_ref, wb_ref, wo_ref, gpm_ref, gate1_ref,
                  gpf_ref, sh2_ref, sc2_ref, wr_ref, x1_ref, h2_ref, aff_ref):
    outs = (oa_ref, on_ref, os_ref, om_ref)
    y = None
    for j in range(4):
        d = jnp.dot(outs[j][...], wb_ref[j], preferred_element_type=F32)
        t = d * jnp.tanh(gl_ref[:, 1024 * j:1024 * j + 1024]) + d
        y = t if y is None else y + t
    y2 = _dot(y, wo_ref[...])
    x1 = x_ref[...] + gate1_ref[0] * _rms(y2, gpm_ref[...])
    x1_ref[...] = x1
    h2 = _rms(x1, gpf_ref[...]) * (1.0 + sc2_ref[0]) + sh2_ref[0]
    h2_ref[...] = h2.astype(BF16)
    logits = _dot_hi_nt(wr_ref[...], h2)
    m = jnp.max(logits, axis=0, keepdims=True)
    e = jnp.exp(logits - m)
    aff_ref[0, 0] = e / jnp.sum(e, axis=0, keepdims=True)


def _merge(x2d, o_mla, o_na, o_s5, o_m2, gl, wb, wo, gpm, mod3, gpf, wrt, nbatch, nb, l):
    mrow = _mod_row(nb)
    fb = lambda i: (_flat_blk(i, nb), 0)
    blk = lambda w: pl.BlockSpec((ROW_BLK, w), fb)
    oblk = lambda w: pl.BlockSpec((ROW_BLK, w), lambda i: (i, 0))
    full = lambda a: _lspec(a, l)
    modspec = lambda k: pl.BlockSpec((1, 1, D), lambda i: (mrow(i, nbatch), 0, k))
    orows = nbatch * nb * ROW_BLK
    return pl.pallas_call(
        _merge_kernel,
        grid=(nbatch * nb,),
        in_specs=[blk(D), oblk(256), oblk(256), blk(256), blk(256), blk(GATE_W),
                  full(wb), full(wo), full(gpm), modspec(2), full(gpf), modspec(3), modspec(4), full(wrt)],
        out_specs=[oblk(D), oblk(D),
                   pl.BlockSpec((1, 1, N_EXPERTS, ROW_BLK), lambda i: (i // nb, i % nb, 0, 0))],
        out_shape=[jax.ShapeDtypeStruct((orows, D), F32),
                   jax.ShapeDtypeStruct((orows, D), BF16),
                   jax.ShapeDtypeStruct((nbatch, nb, N_EXPERTS, ROW_BLK), F32)],
        compiler_params=_cp("arbitrary"),
        name="merge",
    )(x2d, o_mla, o_na, o_s5, o_m2, gl, wb, wo, gpm, mod3, gpf, mod3, mod3, wrt)


def _topk_kernel(aff_ref, slot_ref, *, nk, cap):
    a = aff_ref[0]
    bits = lax.bitcast_convert_type(a, jnp.int32)
    count = lambda m: jnp.sum(jnp.sum(jnp.where(m, 1.0, 0.0), axis=2, keepdims=True), axis=0, keepdims=True)
    thr = jnp.zeros((1, N_EXPERTS, 1), jnp.int32)
    for bit in range(30, -1, -1):
        cand = thr | (1 << bit)
        thr = jnp.where(count(bits >= cand) >= cap, cand, thr)
    gt = bits > thr
    eq = bits == thr
    need = cap - count(gt)

    tri = jnp.where(lax.broadcasted_iota(jnp.int32, (256, 256), 0) <= lax.broadcasted_iota(jnp.int32, (256, 256), 1),
                    1.0, 0.0).astype(BF16)

    def prefix_excl(m):
        incl = jnp.dot(m.reshape(nk * N_EXPERTS, 256).astype(BF16), tri,
                       preferred_element_type=F32).reshape(nk, N_EXPERTS, 256)
        offs = []
        run = jnp.zeros((1, N_EXPERTS, 1), F32)
        for k in range(nk):
            offs.append(run)
            run = run + incl[k:k + 1, :, 255:256]
        off = offs[0] if nk == 1 else jnp.concatenate(offs, axis=0)
        return incl - m + off

    eqf = jnp.where(eq, 1.0, 0.0)
    sel = jnp.where(gt, 1.0, jnp.where(eq & (prefix_excl(eqf) < need), 1.0, 0.0))
    slot = jnp.where(sel > 0.5, prefix_excl(sel), -1.0)
    slot_ref[0] = slot.astype(jnp.int32)


def _topk(aff, blk0, nk, cap):
    nbatch = aff.shape[0]
    return pl.pallas_call(
        functools.partial(_topk_kernel, nk=nk, cap=cap),
        grid=(nbatch,),
        in_specs=[pl.BlockSpec((1, nk, N_EXPERTS, ROW_BLK), lambda b: (b, blk0, 0, 0))],
        out_specs=pl.BlockSpec((1, nk, N_EXPERTS, ROW_BLK), lambda b: (b, 0, 0, 0)),
        out_shape=jax.ShapeDtypeStruct((nbatch, nk, N_EXPERTS, ROW_BLK), jnp.int32),
        compiler_params=_cp("arbitrary"),
        name="topk",
    )(aff)


def _expert_kernel(*refs, has_ctx):
    if has_ctx:
        (w0_ref, fits_ref, hl_ref, sl_ref, al_ref, hc_ref, sc_ref, ac_ref, wg_ref, wu_ref, wd_ref,
         yl_ref, yc_ref, wgb, wub, wdb, xs_acc, gs_acc, xc_all, gc_all) = refs
    else:
        (w0_ref, fits_ref, hl_ref, sl_ref, al_ref, wg_ref, wu_ref, wd_ref, yl_ref,
         wgb, wub, wdb, xs_acc, gs_acc) = refs
    e = pl.program_id(0)
    b = pl.program_id(1)

    @pl.when(b == 0)
    def _():
        wgb[...] = wg_ref[0, 0].astype(BF16)
        wub[...] = wu_ref[0, 0].astype(BF16)
        wdb[...] = wd_ref[0, 0].astype(BF16)

    def gather(h_ref, slot_ref, aff_ref, nk, cap):
        r_iota = lax.broadcasted_iota(jnp.int32, (cap, ROW_BLK), 0)
        xs = jnp.zeros((cap, D), F32)
        gs = jnp.zeros((cap, 1), F32)
        for k in range(nk):
            pm = slot_ref[0, k, pl.ds(e, 1), :] == r_iota
            xs = xs + jnp.dot(jnp.where(pm, 1.0, 0.0).astype(BF16), h_ref[0, ROW_BLK * k:ROW_BLK * k + ROW_BLK, :],
                              preferred_element_type=F32)
            gs = gs + jnp.sum(jnp.where(pm, aff_ref[0, k, pl.ds(e, 1), :], 0.0), axis=1, keepdims=True)
        return xs.astype(BF16), gs

    def ffn(xsb):
        gt = jnp.dot(xsb, wgb[...], preferred_element_type=F32)
        up = jnp.dot(xsb, wub[...], preferred_element_type=F32)
        return jnp.dot((_silu(gt) * up).astype(BF16), wdb[...], preferred_element_type=F32)

    @pl.when(fits_ref[b] == 1)
    def _():
        xs_acc[...] = jnp.zeros_like(xs_acc)
        gs_acc[...] = jnp.zeros_like(gs_acc)
        r_iota = lax.broadcasted_iota(jnp.int32, (MOE_WIN, ROW_BLK), 0)
        for k in range(LAT_BLKS):
            w0 = pl.multiple_of(w0_ref[(b * LAT_BLKS + k) * N_EXPERTS + e], 16)
            pm = (sl_ref[0, k, pl.ds(e, 1), :] - w0) == r_iota
            xs_acc[pl.ds(w0, MOE_WIN), :] += jnp.dot(jnp.where(pm, 1.0, 0.0).astype(BF16),
                                                     hl_ref[0, ROW_BLK * k:ROW_BLK * k + ROW_BLK, :],
                                                     preferred_element_type=F32)
            gs_acc[pl.ds(w0, MOE_WIN), :] += jnp.sum(jnp.where(pm, al_ref[0, k, pl.ds(e, 1), :], 0.0),
                                                     axis=1, keepdims=True)

    @pl.when(fits_ref[b] != 1)
    def _():
        xs, gs = gather(hl_ref, sl_ref, al_ref, LAT_BLKS, CAP_LAT)
        xs_acc[...] = xs.astype(F32)
        gs_acc[...] = gs

    yl_ref[0, 0] = (ffn(xs_acc[...].astype(BF16)) * gs_acc[...]).astype(yl_ref.dtype)
    if has_ctx:
        xs, gs = gather(hc_ref, sc_ref, ac_ref, 1, CAP_CTX)
        row = pl.multiple_of(b * CAP_CTX, CAP_CTX)
        xc_all[pl.ds(row, CAP_CTX), :] = xs
        gc_all[pl.ds(row, CAP_CTX), :] = gs

        @pl.when(b == pl.num_programs(1) - 1)
        def _():
            yc_ref[0] = (ffn(xc_all[...]) * gc_all[...]).astype(yc_ref.dtype)


def _experts(win0, fits, h2, slot_lat, slot_ctx, aff, w_gate, w_up, w_down, l, has_ctx):
    nbatch = h2.shape[0]
    idx4 = lambda e, b: (b, 0, 0, 0)
    smem = pl.BlockSpec(memory_space=pltpu.SMEM)
    in_specs = [smem, smem,
                pl.BlockSpec((1, T, D), lambda e, b: (b, 0, 0)),
                pl.BlockSpec((1, LAT_BLKS, N_EXPERTS, ROW_BLK), idx4),
                pl.BlockSpec((1, LAT_BLKS, N_EXPERTS, ROW_BLK), idx4)]
    args = [win0, fits, h2, slot_lat, aff]
    out_specs = [pl.BlockSpec((1, 1, CAP_LAT, D), lambda e, b: (b, e, 0, 0))]
    out_shape = [jax.ShapeDtypeStruct((nbatch, N_EXPERTS, CAP_LAT, D), BF16)]
    if has_ctx:
        in_specs += [pl.BlockSpec((1, LC, D), lambda e, b: (b, LAT_BLKS, 0)),
                     pl.BlockSpec((1, 1, N_EXPERTS, ROW_BLK), idx4),
                     pl.BlockSpec((1, 1, N_EXPERTS, ROW_BLK), lambda e, b: (b, LAT_BLKS, 0, 0))]
        args += [h2, slot_ctx, aff]
        out_specs.append(pl.BlockSpec((1, nbatch * CAP_CTX, D), lambda e, b: (e, 0, 0)))
        out_shape.append(jax.ShapeDtypeStruct((N_EXPERTS, nbatch * CAP_CTX, D), BF16))
    wspec = pl.BlockSpec((1, 1, D, D), lambda e, b: (l, e, 0, 0))
    in_specs += [wspec, wspec, wspec]
    args += [w_gate, w_up, w_down]
    scratch = [pltpu.VMEM((D, D), BF16)] * 3 + [pltpu.VMEM((CAP_LAT, D), F32), pltpu.VMEM((CAP_LAT, 1), F32)]
    if has_ctx:
        scratch += [pltpu.VMEM((nbatch * CAP_CTX, D), BF16), pltpu.VMEM((nbatch * CAP_CTX, 1), F32)]
    return pl.pallas_call(
        functools.partial(_expert_kernel, has_ctx=has_ctx),
        grid=(N_EXPERTS, nbatch),
        in_specs=in_specs,
        out_specs=out_specs,
        out_shape=out_shape,
        scratch_shapes=scratch,
        compiler_params=_cp("arbitrary", "arbitrary"),
        name="experts",
    )(*args)


def _combine_kernel(*refs, has_ctx):
    if has_ctx:
        w0_ref, fits_ref, x1_ref, yl_ref, scl_ref, yc_ref, scc_ref, gate2_ref, g_ref, o_ref, ycat = refs
    else:
        w0_ref, fits_ref, x1_ref, yl_ref, scl_ref, gate2_ref, g_ref, o_ref, ycat = refs
    b = pl.program_id(0)
    j = pl.program_id(1)

    def finish(acc):
        o_ref[0] = x1_ref[0] + gate2_ref[0] * _rms(acc, g_ref[...])

    def comb(sc_ref, y_of, cap):
        lane = lax.broadcasted_iota(jnp.int32, (ROW_BLK, cap), 1)
        sc = sc_ref[0]
        acc = jnp.zeros((ROW_BLK, D), F32)
        for e in range(N_EXPERTS):
            pt = jnp.where(sc[:, e:e + 1] == lane, 1.0, 0.0).astype(BF16)
            acc = acc + jnp.dot(pt, y_of(e), preferred_element_type=F32)
        finish(acc)

    def comb_windowed():
        lane = lax.broadcasted_iota(jnp.int32, (ROW_BLK, ROW_BLK), 1)
        sc = scl_ref[0]
        per = ROW_BLK // MOE_WIN
        if per * MOE_WIN < ROW_BLK:
            ycat[:, per * MOE_WIN:, :] = jnp.zeros((2, ROW_BLK - per * MOE_WIN, D), BF16)
        acc = jnp.zeros((ROW_BLK, D), F32)
        for gi, e0 in enumerate(range(0, N_EXPERTS, per)):
            hit = None
            for q, e in enumerate(range(e0, min(e0 + per, N_EXPERTS))):
                w0 = pl.multiple_of(w0_ref[(b * LAT_BLKS + j) * N_EXPERTS + e], 16)
                d = sc[:, e:e + 1] - w0
                d = jnp.where(jnp.logical_and(d >= 0, d < MOE_WIN), d + MOE_WIN * q, -1)
                hit = (d == lane) if hit is None else jnp.logical_or(hit, d == lane)
                ycat[gi % 2, MOE_WIN * q:MOE_WIN * q + MOE_WIN, :] = yl_ref[0, e, pl.ds(w0, MOE_WIN), :]
            acc = acc + jnp.dot(jnp.where(hit, 1.0, 0.0).astype(BF16), ycat[gi % 2], preferred_element_type=F32)
        finish(acc)

    lat_y = lambda e: yl_ref[0, e]
    fits = fits_ref[b] == 1

    @pl.when(jnp.logical_and(j < LAT_BLKS, fits))
    def _():
        comb_windowed()

    @pl.when(jnp.logical_and(j < LAT_BLKS, jnp.logical_not(fits)))
    def _():
        comb(scl_ref, lat_y, CAP_LAT)

    if has_ctx:
        @pl.when(j == LAT_BLKS)
        def _():
            comb(scc_ref, lambda e: yc_ref[e], CAP_CTX)


def _combine(win0, fits, x1, y_lat, scol_lat, y_ctx, scol_ctx, mod3, g, has_ctx, l):
    nbatch = x1.shape[0]
    nb = NBLK if has_ctx else LAT_BLKS
    ntok = N if has_ctx else T
    smem = pl.BlockSpec(memory_space=pltpu.SMEM)
    in_specs = [smem, smem,
                pl.BlockSpec((1, ROW_BLK, D), lambda b, j: (b, j, 0)),
                pl.BlockSpec((1, N_EXPERTS, CAP_LAT, D), lambda b, j: (b, 0, 0, 0)),
                pl.BlockSpec((1, ROW_BLK, N_EXPERTS), lambda b, j: (b, jnp.minimum(j, LAT_BLKS - 1), 0))]
    args = [win0, fits, x1, y_lat, scol_lat]
    if has_ctx:
        in_specs += [pl.BlockSpec((N_EXPERTS, CAP_CTX, D), lambda b, j: (0, b, 0)),
                     pl.BlockSpec((1, ROW_BLK, N_EXPERTS), lambda b, j: (b, 0, 0))]
        args += [y_ctx, scol_ctx]
    in_specs += [pl.BlockSpec((1, 1, D), lambda b, j: (jnp.where(j == LAT_BLKS, nbatch, b), 0, 5)),
                 _lspec(g, l)]
    args += [mod3, g]
    return pl.pallas_call(
        functools.partial(_combine_kernel, has_ctx=has_ctx),
        grid=(nbatch, nb),
        in_specs=in_specs,
        out_specs=pl.BlockSpec((1, ROW_BLK, D), lambda b, j: (b, j, 0)),
        out_shape=jax.ShapeDtypeStruct((nbatch, ntok, D), F32),
        scratch_shapes=[pltpu.VMEM((2, ROW_BLK, D), BF16)],
        compiler_params=_cp("arbitrary", "arbitrary"),
        name="combine",
    )(*args)


def _rope_tables():
    f = 1.0 / (ROPE_THETA ** (jnp.arange(0, 16, 2, dtype=F32) / 16))
    pos = jnp.arange(T)
    row, col = pos // GRID_W, pos % GRID_W
    ar = row.astype(F32)[:, None] * f[None, :]
    ac = col.astype(F32)[:, None] * f[None, :]
    cos32 = jnp.concatenate([jnp.cos(ar), jnp.cos(ar), jnp.cos(ac), jnp.cos(ac)], axis=-1)
    sin32 = jnp.concatenate([jnp.sin(ar), jnp.sin(ar), jnp.sin(ac), jnp.sin(ac)], axis=-1)
    cos32 = jnp.concatenate([cos32, jnp.ones((LC, 32), F32)], axis=0)
    sin32 = jnp.concatenate([sin32, jnp.zeros((LC, 32), F32)], axis=0)
    return cos32, sin32


def _rot_cols(w):
    a, b, c, d = w[..., 0:8], w[..., 8:16], w[..., 16:24], w[..., 24:32]
    return jnp.concatenate([-b, a, -d, c], axis=-1)


def _mla_weights(w_uq, w_ukv):
    nl, r, _ = w_uq.shape
    wq3 = w_uq.reshape(nl, r, 4, 96)
    z32 = jnp.zeros((nl, r, 4, 32), F32)
    wq = jnp.concatenate([wq3, z32], axis=-1).reshape(nl, r, 512)
    wqr = jnp.concatenate([jnp.zeros((nl, r, 4, 64), F32), _rot_cols(wq3[..., 64:96]), z32], axis=-1).reshape(nl, r, 512)
    rk = w_ukv.shape[1]
    wkv3 = w_ukv.reshape(nl, rk, 4, 128)
    wk = jnp.concatenate([wkv3[..., :64], jnp.zeros((nl, rk, 4, 64), F32)], axis=-1).reshape(nl, rk, 512)
    z64 = jnp.zeros((nl, rk, 64), F32)
    vh = [wkv3[:, :, h, 64:] for h in range(4)]
    wv = jnp.concatenate([vh[0], z64, z64, vh[1], vh[2], z64, z64, vh[3]], axis=-1)
    vone = np.tile(np.repeat(np.array([0.0, 1.0, 1.0, 0.0], np.float32), 64), 2)[None, :]
    e = np.zeros((32, 512), np.float32)
    for h in range(4):
        e[np.arange(32), 128 * h + 64 + np.arange(32)] = 1.0
    return (wq.astype(BF16), wqr.astype(BF16), wk.astype(BF16), wv.astype(BF16), jnp.asarray(e, BF16),
            jnp.asarray(vone))


def _inproj_weights(w):
    o = np.cumsum([0, 256, 128, 32, 768, 256, 256, 768, 8, 4096])
    seg = lambda i: w[:, :, o[i]:o[i + 1]]
    nl = w.shape[0]
    kr = seg(2)
    wm = jnp.concatenate([seg(0), seg(1), kr, _rot_cols(kr), jnp.zeros((nl, D, 64), F32)], axis=-1)
    w2 = jnp.concatenate([seg(5), seg(6), seg(7), jnp.zeros((nl, D, 120), F32)], axis=-1)
    return [a.astype(BF16) for a in (wm, seg(3), seg(4), w2, 0.5 * seg(8))]


def kernel(x, c, ctx, c_ctx, w_ada, b_ada, g_pre_mix, g_post_mix, g_pre_ffn, g_post_ffn, w_in, mla_g_cq, mla_g_ckv, mla_w_uq, mla_w_ukv, na_rpb, s5_a_re, s5_a_im, s5_log_step, s5_b_re, s5_b_im, s5_c_re, s5_c_im, s5_d, s5_w_glu, s5_b_glu, m2_conv_w, m2_conv_b, m2_a_log, m2_dt_bias, m2_d, m2_g_norm, w_branch, w_out, w_router, w_gate, w_up, w_down):
    nbatch = x.shape[0]
    depth = w_ada.shape[0]
    vec = lambda v: v.reshape(depth, 1, -1).astype(F32)

    xs = jnp.concatenate([x, ctx], axis=1)
    cvec = jnp.concatenate([c, c_ctx[None, :], jnp.zeros((7, D), F32)], axis=0)
    cvec = cvec[: ((nbatch + 1 + 7) // 8) * 8]
    b_ada3 = b_ada.reshape(depth, 1, 6 * D)

    cos32, sin32 = _rope_tables()
    qscale = (64 + 32) ** -0.5 * math.log2(math.e)
    lane_is_rope = np.tile(np.concatenate([np.zeros(64, bool), np.ones(32, bool), np.zeros(32, bool)]), 4)
    pick = lambda t32, fill: jnp.where(lane_is_rope[None, :], jnp.tile(jnp.pad(t32, ((0, 0), (64, 32))), (1, 4)), fill)
    cos_q = pick(cos32, 1.0) * qscale
    sin_q = pick(sin32, 0.0) * qscale

    g_pre_mix, g_post_mix, g_pre_ffn, g_post_ffn = vec(g_pre_mix), vec(g_post_mix), vec(g_pre_ffn), vec(g_post_ffn)
    w_in_segs = _inproj_weights(w_in)
    wq, wqr, wk, wv, e_mat, vone = _mla_weights(mla_w_uq, mla_w_ukv)
    mla_g_cq, mla_g_ckv = vec(mla_g_cq), vec(mla_g_ckv)
    na_tabs = _na_tables(na_rpb)
    s5_bmat, s5_avec, s5_cmat = _s5_params(s5_a_re, s5_a_im, s5_log_step, s5_b_re, s5_b_im, s5_c_re, s5_c_im, nbatch)
    s5_d, s5_b_glu, s5_w_glu = vec(s5_d), vec(s5_b_glu), s5_w_glu.astype(BF16)
    m2_conv_w, m2_conv_b = m2_conv_w.astype(F32), vec(m2_conv_b)
    dt_bias = jnp.pad(m2_dt_bias.reshape(depth, 1, 8).astype(F32), ((0, 0), (0, 0), (0, 120)))
    a_col = jnp.pad(-jnp.exp(m2_a_log.astype(F32)).reshape(depth, 8, 1), ((0, 0), (0, 120), (0, 0)))
    m2_dvec, m2_g_norm = vec(jnp.repeat(m2_d, 64, axis=-1)), vec(m2_g_norm)
    w_branch, w_out = (0.5 * w_branch).astype(BF16), w_out.astype(BF16)
    w_router_t = jnp.swapaxes(w_router, 1, 2).astype(F32)

    for l in range(depth):
        has_ctx = l < depth - 1
        nb = NBLK if has_ctx else LAT_BLKS
        mod = _ada(cvec, w_ada, b_ada3, l)
        mod3 = mod.reshape(mod.shape[0], 1, 6 * D)
        x2d = xs.reshape(nbatch * N, D)

        p_mla, p_na, p_s5, p_m2, p_gate = _inproj(x2d, g_pre_mix, mod3, w_in_segs, nbatch, l)

        q, k, v = _mla_prep(p_mla, mla_g_cq, mla_g_ckv, wq, wqr, wk, wv, e_mat, vone, cos_q, sin_q, cos32, sin32, l)
        o_mla = _mla_attn(q.reshape(nbatch, N, 512), k.reshape(nbatch, N, 512), v.reshape(nbatch, N, 512), has_ctx)

        o_na = _na_attn(p_na.reshape(nbatch, N, NA_W), na_tabs, has_ctx, l)

        yf, yb = _s5_scan(p_s5.reshape(nbatch, N, S5_W), s5_bmat, s5_avec, s5_cmat, l)
        o_s5 = _s5_glu(p_s5, yf.reshape(nbatch * N, S5_W), yb.reshape(nbatch * N, S5_W), s5_d, s5_w_glu, s5_b_glu, l)

        xbc, dtt, bmt = _m2_prep(p_m2.reshape(nbatch, N, M2_W), m2_conv_w, m2_conv_b, dt_bias, l)
        ssd_f, ssd_b = _ssd(xbc, dtt, bmt, a_col, l)
        o_m2 = _m2_norm(xbc.reshape(nbatch * N, 768), p_m2, ssd_f.reshape(nbatch * N, 256),
                        ssd_b.reshape(nbatch * N, 256), m2_dvec, m2_g_norm, l)

        x1, h2, aff = _merge(x2d, o_mla.reshape(-1, 256), o_na.reshape(-1, 256), o_s5, o_m2, p_gate,
                             w_branch, w_out, g_post_mix, mod3, g_pre_ffn, w_router_t, nbatch, nb, l)

        slot_lat = _topk(aff, 0, LAT_BLKS, CAP_LAT)
        slot_ctx = _topk(aff, LAT_BLKS, 1, CAP_CTX) if has_ctx else None
        cnt = jnp.sum((slot_lat >= 0).astype(jnp.int32), axis=-1)
        first = jnp.cumsum(cnt, axis=1) - cnt
        win0 = jnp.minimum((first // 16) * 16, CAP_LAT - MOE_WIN)
        fits = jnp.all(first + cnt <= win0 + MOE_WIN, axis=(1, 2)).astype(jnp.int32)
        win0 = win0.reshape(-1)
        ys = _experts(win0, fits, h2.reshape(nbatch, nb * ROW_BLK, D), slot_lat, slot_ctx, aff, w_gate, w_up, w_down,
                      l, has_ctx)
        scol_lat = jnp.transpose(slot_lat, (0, 1, 3, 2)).reshape(nbatch, T, N_EXPERTS)
        scol_ctx = jnp.transpose(slot_ctx, (0, 1, 3, 2)).reshape(nbatch, LC, N_EXPERTS) if has_ctx else None
        xs = _combine(win0, fits, x1.reshape(nbatch, nb * ROW_BLK, D), ys[0], scol_lat, ys[1] if has_ctx else None,
                      scol_ctx, mod3, g_post_ffn, has_ctx, l)
    return xs
```

```python
import functools
import math

import numpy as np
import jax
import jax.numpy as jnp
from jax import lax
from jax.experimental import pallas as pl
from jax.experimental.pallas import tpu as pltpu

F32 = jnp.float32
BF16 = jnp.bfloat16

D = 1024
T = 2048
LC = 256
N = T + LC
GRID_W = 64
ROW_BLK = 256
NBLK = N // ROW_BLK
LAT_BLKS = T // ROW_BLK
EPS = 1e-6
N_EXPERTS = 16
CAP_LAT = 2 * T // N_EXPERTS
CAP_CTX = 2 * LC // N_EXPERTS
ROPE_THETA = 10000.0
NEG = -1e30

MLA_W = 512
NA_W = 768
S5_W = 256
M2_W = 1152
GATE_W = 4096

MOE_WIN = 80
S5_CHUNK = 64
SSD_CHUNK = 128

VMEM_LIMIT = 56 * 1024 * 1024


def _cp(*sem):
    return pltpu.CompilerParams(dimension_semantics=sem, vmem_limit_bytes=VMEM_LIMIT)


def _lspec(a, l, *lead):
    nd = a.ndim - 1 - len(lead)
    return pl.BlockSpec((None,) * (1 + len(lead)) + a.shape[1 + len(lead):], lambda *_: (l,) + lead + (0,) * nd)


def _dot(a, b):
    return jnp.dot(a.astype(BF16), b.astype(BF16), preferred_element_type=F32)


def _dot_nt(a, b):
    return lax.dot_general(a.astype(BF16), b.astype(BF16), (((1,), (1,)), ((), ())),
                           preferred_element_type=F32)


def _split3(a):
    hi = a.astype(BF16)
    r = a - hi.astype(F32)
    mid = r.astype(BF16)
    lo = (r - mid.astype(F32)).astype(BF16)
    return hi, mid, lo


def _dot_hi(a, b):
    ah, am, _ = _split3(a)
    bh, bm, _ = _split3(b)
    f = lambda x, y: jnp.dot(x, y, preferred_element_type=F32)
    return f(ah, bh) + (f(ah, bm) + f(am, bh))


def _dot_hi_nt(a, b):
    ah, am, _ = _split3(a)
    bh, bm, _ = _split3(b)
    f = lambda x, y: lax.dot_general(x, y, (((1,), (1,)), ((), ())), preferred_element_type=F32)
    return f(ah, bh) + (f(ah, bm) + f(am, bh))


def _dot_exact_lhs(m_bf16, a):
    h, m, l = _split3(a)
    f = lambda y: jnp.dot(m_bf16, y, preferred_element_type=F32)
    return f(h) + (f(m) + f(l))


def _dot_exact_rhs(a, m_bf16):
    h, m, l = _split3(a)
    f = lambda y: jnp.dot(y, m_bf16, preferred_element_type=F32)
    return f(h) + (f(m) + f(l))


def _sigmoid(x):
    return 0.5 * jnp.tanh(0.5 * x) + 0.5


def _silu(x):
    return x * _sigmoid(x)


def _rms(x, g):
    return x * lax.rsqrt(jnp.mean(x * x, axis=-1, keepdims=True) + EPS) * g


def _mod_row(nb):
    def f(i, nbatch):
        return jnp.where(i % nb == LAT_BLKS, nbatch, i // nb)
    return f


def _flat_blk(i, nb):
    return (i // nb) * NBLK + i % nb


def _stream_specs(stream, nb):
    lat, ctxa, cblk = stream
    return [lat, ctxa], [pl.BlockSpec((1, ROW_BLK, D), lambda i: (i // nb, jnp.minimum(i % nb, LAT_BLKS - 1), 0)),
                         pl.BlockSpec((1, ROW_BLK, D), lambda i: (i // nb, cblk, 0))]


def _stream_block(xl_ref, xc_ref, nb):
    return jnp.where(pl.program_id(0) % nb == LAT_BLKS, xc_ref[0], xl_ref[0])


def _ada_kernel(c_ref, w_ref, b_ref, o_ref):
    c = c_ref[...]
    o_ref[...] = _dot_hi(_silu(c), w_ref[0]) + b_ref[0]


def _ada(cvec, w_ada, b_ada3, l):
    rows = cvec.shape[0]
    return pl.pallas_call(
        _ada_kernel,
        grid=(6,),
        in_specs=[pl.BlockSpec((rows, D), lambda k: (0, 0)),
                  pl.BlockSpec((1, D, D), lambda k: (l, 0, k)),
                  pl.BlockSpec((1, 1, D), lambda k: (l, 0, k))],
        out_specs=pl.BlockSpec((rows, D), lambda k: (0, k)),
        out_shape=jax.ShapeDtypeStruct((rows, 6 * D), F32),
        compiler_params=_cp("arbitrary"),
        name="ada",
    )(cvec, w_ada, b_ada3)


def _inproj_kernel(xl_ref, xc_ref, g_ref, sh_ref, sc_ref, wm, wn, ws, w2, wg, om, on, os_, o2, og):
    x = _stream_block(xl_ref, xc_ref, NBLK)
    h = (_rms(x, g_ref[...]) * (1.0 + sc_ref[0]) + sh_ref[0]).astype(BF16)
    om[...] = jnp.dot(h, wm[...], preferred_element_type=F32)
    on[...] = jnp.dot(h, wn[...], preferred_element_type=F32).astype(BF16)
    os_[...] = jnp.dot(h, ws[...], preferred_element_type=F32)
    o2[...] = jnp.dot(h, w2[...], preferred_element_type=F32)
    og[...] = jnp.dot(h, wg[...], preferred_element_type=F32)


def _inproj(stream, g, mod3, ws, nbatch, l):
    rows = nbatch * N
    nblk = rows // ROW_BLK
    mrow = _mod_row(NBLK)
    xargs, xspecs = _stream_specs(stream, NBLK)
    full = lambda w: _lspec(w, l)
    widths = (MLA_W, NA_W, S5_W, M2_W, GATE_W)
    dts = (F32, BF16, F32, F32, F32)
    return pl.pallas_call(
        _inproj_kernel,
        grid=(nblk,),
        in_specs=xspecs + [
                  full(g),
                  pl.BlockSpec((1, 1, D), lambda i: (mrow(i, nbatch), 0, 0)),
                  pl.BlockSpec((1, 1, D), lambda i: (mrow(i, nbatch), 0, 1))]
                 + [full(w) for w in ws],
        out_specs=[pl.BlockSpec((ROW_BLK, w), lambda i: (i, 0)) for w in widths],
        out_shape=[jax.ShapeDtypeStruct((rows, w), dt) for w, dt in zip(widths, dts)],
        compiler_params=_cp("arbitrary"),
        name="inproj",
    )(*xargs, g, mod3, mod3, *ws)


def _mla_prep_kernel(p_ref, gq_ref, gkv_ref, wq_ref, wqr_ref, wk_ref, wv_ref, e_ref, vone_ref,
                     cos_ref, sin_ref, ck_ref, sk_ref, q_out, k_out, v_out):
    p = p_ref[...]
    cqn = _rms(p[:, :256], gq_ref[...]).astype(BF16)
    q = jnp.dot(cqn, wq_ref[...], preferred_element_type=F32)
    qr = jnp.dot(cqn, wqr_ref[...], preferred_element_type=F32)
    q_out[...] = (q * cos_ref[...] + qr * sin_ref[...]).astype(BF16)
    ckvn = _rms(p[:, 256:384], gkv_ref[...]).astype(BF16)
    kro = (p[:, 384:416] * ck_ref[...] + p[:, 416:448] * sk_ref[...]).astype(BF16)
    k = jnp.dot(ckvn, wk_ref[...], preferred_element_type=F32) + jnp.dot(kro, e_ref[...], preferred_element_type=F32)
    k_out[...] = k.astype(BF16)
    v_out[...] = (jnp.dot(ckvn, wv_ref[...], preferred_element_type=F32) + vone_ref[...]).astype(BF16)


def _mla_prep(p_mla, gq, gkv, wq, wqr, wk, wv, e, vone, cos_q, sin_q, cos_k, sin_k, l):
    rows = p_mla.shape[0]
    full = lambda w: _lspec(w, l)
    const = lambda w: pl.BlockSpec(w.shape, lambda i: (0, 0))
    tab = lambda w: pl.BlockSpec((ROW_BLK, w), lambda i: (i % NBLK, 0))
    return pl.pallas_call(
        _mla_prep_kernel,
        grid=(rows // ROW_BLK,),
        in_specs=[pl.BlockSpec((ROW_BLK, MLA_W), lambda i: (i, 0)),
                  full(gq), full(gkv), full(wq), full(wqr), full(wk), full(wv), const(e), const(vone),
                  tab(512), tab(512), tab(32), tab(32)],
        out_specs=[pl.BlockSpec((ROW_BLK, 512), lambda i: (i, 0))] * 3,
        out_shape=[jax.ShapeDtypeStruct((rows, 512), BF16)] * 3,
        compiler_params=_cp("arbitrary"),
        name="mla_prep",
    )(p_mla, gq, gkv, wq, wqr, wk, wv, e, vone, cos_q, sin_q, cos_k, sin_k)


def _mla_attn_kernel(q_ref, k_ref, v_ref, o_ref, *, has_ctx):
    lane = lax.broadcasted_iota(jnp.int32, (q_ref.shape[1], 128), 1)

    def run(k0, nk):
        for hp in range(2):
            pv = []
            for h in (2 * hp, 2 * hp + 1):
                qh = q_ref[0, :, 128 * h:128 * h + 128]
                kh = k_ref[0, k0:k0 + nk, 128 * h:128 * h + 128]
                s = _dot_nt(qh, kh)
                p = jnp.exp2(s - jnp.max(s, axis=-1, keepdims=True)).astype(BF16)
                pv.append(jnp.dot(p, v_ref[0, k0:k0 + nk, 128 * h:128 * h + 128], preferred_element_type=F32))
            oa, ob = pv
            o = jnp.where(lane < 64, oa * (1.0 / oa[:, 64:65]), ob * (1.0 / ob[:, 0:1]))
            o_ref[0, :, 128 * hp:128 * hp + 128] = o.astype(o_ref.dtype)

    if has_ctx:
        j = pl.program_id(1)

        @pl.when(j < LAT_BLKS)
        def _():
            run(0, N)

        @pl.when(j == LAT_BLKS)
        def _():
            run(T, LC)
    else:
        run(0, N)


def _mla_attn(q, k, v, has_ctx):
    nbatch = q.shape[0]
    nb = NBLK if has_ctx else LAT_BLKS
    qb = ROW_BLK if has_ctx else 2 * ROW_BLK
    return pl.pallas_call(
        functools.partial(_mla_attn_kernel, has_ctx=has_ctx),
        grid=(nbatch, nb * ROW_BLK // qb),
        in_specs=[pl.BlockSpec((1, qb, 512), lambda b, j: (b, j, 0)),
                  pl.BlockSpec((1, N, 512), lambda b, j: (b, 0, 0)),
                  pl.BlockSpec((1, N, 512), lambda b, j: (b, 0, 0))],
        out_specs=pl.BlockSpec((1, qb, 256), lambda b, j: (b, j, 0)),
        out_shape=jax.ShapeDtypeStruct((nbatch, nb * ROW_BLK, 256), BF16),
        compiler_params=_cp("arbitrary", "arbitrary"),
        name="mla_attn",
    )(q, k, v)


def _na_kernel(q_ref, kc_ref, k0_ref, k1_ref, k2_ref, vc_ref, v0_ref, v1_ref, v2_ref, b_ref, o_ref, *, has_ctx):
    scale = jnp.asarray(0.125, BF16)

    lane = lax.broadcasted_iota(jnp.int32, (ROW_BLK, 128), 1)
    lane64 = lax.broadcasted_iota(jnp.int32, (GRID_W, 128), 1) < 64
    grp = jnp.minimum(pl.program_id(0), LAT_BLKS - 1)
    krow0 = 4 * jnp.clip(grp - 1, 0, LAT_BLKS - 3)

    def bias_piece(h, i):
        rows = []
        for qr in range(4):
            r = 4 * grp + qr
            rs = jnp.clip(r - 4, 0, 24)
            tiles = []
            for pr in range(2):
                kr = krow0 + (4 * i + 2 * pr)
                t = b_ref[h, jnp.clip(kr - r + 8, 0, 15)]
                ok_a = jnp.logical_and(kr >= rs, kr < rs + 8).astype(jnp.int32)
                ok_b = jnp.logical_and(kr + 1 >= rs, kr + 1 < rs + 8).astype(jnp.int32)
                tiles.append(jnp.where(jnp.where(lane64, ok_a, ok_b) > 0, t, NEG))
            rows.append(jnp.concatenate(tiles, axis=1))
        return jnp.concatenate(rows, axis=0)

    def heads(win):
        kws = (k0_ref, k1_ref, k2_ref)
        vws = (v0_ref, v1_ref, v2_ref)
        for hp in range(2):
            sl = slice(128 * hp, 128 * hp + 128)
            qp = q_ref[0, :, sl] * scale
            outs = []
            for hh in range(2):
                h = 2 * hp + hh
                qh = jnp.where((lane < 64) if hh == 0 else (lane >= 64), qp, jnp.zeros_like(qp))
                s_c = _dot_nt(qh, kc_ref[0, :, sl])
                s_w = []
                smax = s_c
                if win:
                    for i in range(3):
                        s = _dot_nt(qh, kws[i][0, :, sl]) + bias_piece(h, i)
                        s_w.append(s)
                        smax = jnp.maximum(smax, s)
                m = jnp.max(smax, axis=-1, keepdims=True)
                p = jnp.exp(s_c - m)
                psum = p
                o = _dot(p, vc_ref[0, :, sl])
                for i, s in enumerate(s_w):
                    p = jnp.exp(s - m)
                    psum = psum + p
                    o = o + _dot(p, vws[i][0, :, sl])
                outs.append(o * (1.0 / jnp.sum(psum, axis=-1, keepdims=True)))
            o_ref[0, :, sl] = jnp.where(lane < 64, outs[0], outs[1]).astype(o_ref.dtype)

    if has_ctx:
        g = pl.program_id(0)

        @pl.when(g < LAT_BLKS)
        def _():
            heads(True)

        @pl.when(g == LAT_BLKS)
        def _():
            heads(False)
    else:
        heads(True)


def _na_attn(qkv, bias, has_ctx, l):
    nbatch = qkv.shape[0]
    ng = NBLK if has_ctx else LAT_BLKS
    j0 = lambda g: jnp.clip(g - 1, 0, LAT_BLKS - 3)
    blk = lambda f: pl.BlockSpec((1, ROW_BLK, 256), f)
    return pl.pallas_call(
        functools.partial(_na_kernel, has_ctx=has_ctx),
        grid=(ng, nbatch),
        in_specs=[blk(lambda g, b: (b, g, 0)),
                  blk(lambda g, b: (b, LAT_BLKS, 1)),
                  blk(lambda g, b: (b, j0(g), 1)),
                  blk(lambda g, b: (b, j0(g) + 1, 1)),
                  blk(lambda g, b: (b, j0(g) + 2, 1)),
                  blk(lambda g, b: (b, LAT_BLKS, 2)),
                  blk(lambda g, b: (b, j0(g), 2)),
                  blk(lambda g, b: (b, j0(g) + 1, 2)),
                  blk(lambda g, b: (b, j0(g) + 2, 2)),
                  _lspec(bias, l)],
        out_specs=blk(lambda g, b: (b, g, 0)),
        out_shape=jax.ShapeDtypeStruct((nbatch, ng * ROW_BLK, 256), BF16),
        compiler_params=_cp("arbitrary", "arbitrary"),
        name="na_attn",
    )(qkv, qkv, qkv, qkv, qkv, qkv, qkv, qkv, qkv, bias)


def _na_col_structure():
    onehot = np.zeros((31, 64, 64), np.float32)
    colmask = np.zeros((64, 64), bool)
    for c in range(64):
        cs = min(max(c - 8, 0), 48)
        for kc in range(cs, cs + 16):
            onehot[kc - c + 15, c, kc] = 1.0
            colmask[c, kc] = True
    return np.tile(onehot.reshape(31, 4096), (3, 1)), colmask


_NA_COL_ONEHOT, _NA_COLMASK = _na_col_structure()


def _na_tables(rpb):
    nl = rpb.shape[0]
    r = rpb.astype(F32)
    hi = lax.reduce_precision(r, 8, 7)
    mid = lax.reduce_precision(r - hi, 8, 7)
    parts = jnp.stack([hi, mid, r - hi - mid], axis=3).reshape(nl * 60, 93)
    tz = jnp.dot(parts, _NA_COL_ONEHOT, preferred_element_type=F32).reshape(nl, 4, 15, 64, 64)
    tz = jnp.where(_NA_COLMASK, tz, NEG)
    edge = jnp.full((nl, 4, 1, 64, 64), NEG, F32)
    tz17 = jnp.concatenate([edge, tz, edge], axis=2)
    return jnp.concatenate([tz17[:, :, :16], tz17[:, :, 1:]], axis=-1)


def _s5_kernel(uf_ref, ub_ref, bf_ref, bb_ref, af_ref, ab_ref, cf_ref, cb_ref, yf_ref, yb_ref,
               hf, hb, buf_f, buf_b, tm_f, tm_b):
    i = pl.program_id(0)
    half = 1024
    nb = uf_ref.shape[0]

    @pl.when(i == 0)
    def _():
        hf[...] = jnp.zeros_like(hf)
        hb[...] = jnp.zeros_like(hb)

    def expand(u_ref, tm, b_ref, buf):
        for b in range(nb):
            for c in range(2):
                tm[c, pl.ds(b, S5_CHUNK, stride=nb), :] = u_ref[b, :, 128 * c:128 * c + 128]
        buf[...] = _dot(jnp.concatenate([tm[0], tm[1]], axis=1), b_ref[...])

    def scan(buf, a_ref, h, order):
        hr, hi = h[:, :half], h[:, half:]
        ar, ai = a_ref[:, :half], a_ref[:, half:]
        for k in order:
            rows = slice(k * nb, (k + 1) * nb)
            nr = ar * hr - ai * hi + buf[rows, :half]
            ni = ar * hi + ai * hr + buf[rows, half:]
            buf[rows, :half] = nr
            buf[rows, half:] = ni
            hr, hi = nr, ni
        h[:, :half] = hr
        h[:, half:] = hi

    def readout(buf, c_ref, tm, y_ref):
        y = _dot(buf[...], c_ref[...])
        for c in range(2):
            tm[c] = y[:, 128 * c:128 * c + 128]
        for b in range(nb):
            for c in range(2):
                y_ref[b, :, 128 * c:128 * c + 128] = tm[c, pl.ds(b, S5_CHUNK, stride=nb), :]

    expand(uf_ref, tm_f, bf_ref, buf_f)
    expand(ub_ref, tm_b, bb_ref, buf_b)
    scan(buf_f, af_ref, hf, range(S5_CHUNK))
    readout(buf_f, cf_ref, tm_f, yf_ref)
    scan(buf_b, ab_ref, hb, range(S5_CHUNK - 1, -1, -1))
    readout(buf_b, cb_ref, tm_b, yb_ref)


def _s5_scan(u, bmat, avec, cmat, l):
    nbatch = u.shape[0]
    cr = S5_CHUNK * nbatch
    nch = N // S5_CHUNK
    nctx = LC // S5_CHUNK
    fidx = lambda i: jnp.where(i < nctx, nch - nctx + i, i - nctx)
    bidx = lambda i: nch - 1 - i
    ublk = lambda f: pl.BlockSpec((nbatch, S5_CHUNK, 256), lambda i: (0, f(i), 0))
    return pl.pallas_call(
        _s5_kernel,
        grid=(nch,),
        in_specs=[ublk(fidx), ublk(bidx),
                  _lspec(bmat, l, 0), _lspec(bmat, l, 1), _lspec(avec, l, 0), _lspec(avec, l, 1),
                  _lspec(cmat, l, 0), _lspec(cmat, l, 1)],
        out_specs=[ublk(fidx), ublk(bidx)],
        out_shape=[jax.ShapeDtypeStruct((nbatch, N, 256), F32)] * 2,
        scratch_shapes=[pltpu.VMEM((nbatch, 2048), F32), pltpu.VMEM((nbatch, 2048), F32),
                        pltpu.VMEM((cr, 2048), F32), pltpu.VMEM((cr, 2048), F32),
                        pltpu.VMEM((2, cr, 128), F32), pltpu.VMEM((2, cr, 128), F32)],
        compiler_params=_cp("arbitrary"),
        name="s5_scan",
    )(u, u, bmat, bmat, avec, avec, cmat, cmat)


def _s5_glu_kernel(u_ref, yf_ref, yb_ref, d_ref, w_ref, b_ref, o_ref):
    y = d_ref[...] * u_ref[...] + yf_ref[...] + yb_ref[...]
    z = y * (0.5 * (1.0 + jnp.tanh(math.sqrt(2.0 / math.pi) * (y + 0.044715 * (y * y * y)))))
    o_ref[...] = (z * _sigmoid(_dot(z, w_ref[...]) + b_ref[...])).astype(o_ref.dtype)


def _s5_glu(u_tb, yf, yb, d, w, b, l):
    rows = u_tb.shape[0]
    rb = 512
    blk = pl.BlockSpec((rb, 256), lambda i: (i, 0))
    full = lambda a: _lspec(a, l)
    return pl.pallas_call(
        _s5_glu_kernel,
        grid=(rows // rb,),
        in_specs=[blk, blk, blk, full(d), full(w), full(b)],
        out_specs=blk,
        out_shape=jax.ShapeDtypeStruct((rows, 256), BF16),
        compiler_params=_cp("arbitrary"),
        name="s5_glu",
    )(u_tb, yf, yb, d, w, b)


def _s5_params(a_re, a_im, log_step, b_re, b_im, c_re, c_im, nbatch):
    nl = a_re.shape[0]
    eye = jnp.eye(16, dtype=F32)[:, None, :, None]
    a = lax.complex(a_re.astype(F32), a_im.astype(F32))
    abar = jnp.exp(jnp.exp(log_step.astype(F32))[..., None] * a)
    bbar = ((abar - 1.0) / a)[..., None] * lax.complex(b_re.astype(F32), b_im.astype(F32))
    blk_in = lambda m: (eye * jnp.swapaxes(m, -1, -2)[..., :, :, None, :]).reshape(nl, 2, 256, 1024)
    bmat = jnp.concatenate([blk_in(jnp.real(bbar)), blk_in(jnp.imag(bbar))], axis=-1).astype(BF16)
    avec = jnp.concatenate([jnp.real(abar).reshape(nl, 2, 1, 1024), jnp.imag(abar).reshape(nl, 2, 1, 1024)], axis=-1)
    avec = jnp.broadcast_to(avec, (nl, 2, nbatch, 2048))
    blk_out = lambda m: (eye * jnp.swapaxes(m, -1, -2)[..., :, :, None, :]).reshape(nl, 2, 1024, 256)
    cmat = jnp.concatenate([blk_out(c_re.astype(F32)), -blk_out(c_im.astype(F32))], axis=-2).astype(BF16)
    return bmat, avec, cmat


def _m2_prep_kernel(x_ref, dt_ref, w_ref, b_ref, dtb_ref, xo_ref, dtt_ref, bmt_ref):
    c = pl.program_id(1)
    x = x_ref[0]
    t = lax.broadcasted_iota(jnp.int32, x.shape, 0)
    m2 = ((t >= 2) & (t < T)) | (t >= T + 2)
    m1 = ((t >= 1) & (t < T)) | (t >= T + 1)
    p1 = (t <= T - 2) | ((t >= T) & (t <= N - 2))
    w = w_ref[...]
    y = (w[0:1] * jnp.where(m2, pltpu.roll(x, 2, 0), 0.0)
         + w[1:2] * jnp.where(m1, pltpu.roll(x, 1, 0), 0.0)
         + w[2:3] * x
         + w[3:4] * jnp.where(p1, pltpu.roll(x, N - 1, 0), 0.0)) + b_ref[...]
    act = _silu(y)
    xo_ref[0] = act

    @pl.when(c == 0)
    def _():
        v = dt_ref[0] + dtb_ref[...]
        dtt_ref[0] = (jnp.maximum(v, 0.0) + jnp.log1p(jnp.exp(-jnp.abs(v)))).T

    @pl.when(c == 1)
    def _():
        bmt_ref[0] = act.T.astype(BF16)


def _m2_prep(p_m2, conv_w, conv_b, dtb, l):
    nbatch = p_m2.shape[0]
    return pl.pallas_call(
        _m2_prep_kernel,
        grid=(nbatch, 3),
        in_specs=[pl.BlockSpec((1, N, 256), lambda b, c: (b, 0, 1 + c)),
                  pl.BlockSpec((1, N, 128), lambda b, c: (b, 0, 8)),
                  pl.BlockSpec((None, 4, 256), lambda b, c: (l, 0, c)),
                  pl.BlockSpec((None, 1, 256), lambda b, c: (l, 0, c)),
                  _lspec(dtb, l)],
        out_specs=[pl.BlockSpec((1, N, 256), lambda b, c: (b, 0, c)),
                   pl.BlockSpec((1, 128, N), lambda b, c: (b, 0, 0)),
                   pl.BlockSpec((1, 256, N), lambda b, c: (b, 0, 0))],
        out_shape=[jax.ShapeDtypeStruct((nbatch, N, 768), F32),
                   jax.ShapeDtypeStruct((nbatch, 128, N), F32),
                   jax.ShapeDtypeStruct((nbatch, 256, N), BF16)],
        compiler_params=_cp("arbitrary", "arbitrary"),
        name="m2_prep",
    )(p_m2, p_m2, conv_w, conv_b, dtb)


def _ssd_kernel(xf_ref, dttf_ref, bmtf_ref, xb_ref, dttb_ref, bmtb_ref, acol_ref, yf_ref, yb_ref, hs):
    i = pl.program_id(1)
    L = SSD_CHUNK

    @pl.when(i == 0)
    def _():
        hs[...] = jnp.zeros_like(hs)

    li = lax.broadcasted_iota(jnp.int32, (L, L), 0)
    si = lax.broadcasted_iota(jnp.int32, (L, L), 1)
    left = si < 64
    pick = lambda a, b: jnp.where(left, a, b)

    def direction(x_ref, dtt_ref, bmt_ref, y_ref, d, causal, causal_t, last):
        cz = jnp.where(causal, 1.0, 0.0).astype(BF16)
        czt = jnp.where(causal_t, 1.0, 0.0).astype(BF16)
        xbc = x_ref[0]
        dtt = dtt_ref[0]
        parts = jnp.concatenate([p.astype(F32) for p in _split3(dtt * acol_ref[...])], axis=1)
        cumt = jnp.dot(parts.astype(BF16), jnp.concatenate([czt, czt, czt], axis=0),
                       preferred_element_type=F32)
        reps = jnp.concatenate([jnp.broadcast_to(parts[4 * d + h:4 * d + h + 1, :], (L, 3 * L)) for h in range(4)],
                               axis=0).astype(BF16)
        cumb = lax.dot_general(jnp.concatenate([cz, cz, cz], axis=1), reps, (((1,), (1,)), ((), ())),
                               preferred_element_type=F32)
        for g in range(2):
            xp = xbc[:, 128 * g:128 * g + 128].astype(BF16)
            cm = xbc[:, 512 + 128 * g:640 + 128 * g].astype(BF16)
            bmt = bmt_ref[0, 128 * g:128 * g + 128, :]
            hprev = hs[d, g]
            go = jnp.dot(cm, jnp.concatenate([bmt, hprev.astype(BF16)], axis=1), preferred_element_type=F32)
            gmat, yo = go[:, :L], go[:, L:]
            bmf = bmt.astype(F32)
            lhs, cbs, cls = [], [], []
            for hh in range(2):
                h = 2 * g + hh
                cb = cumb[:, L * h:L * h + L]
                crow = cumt[4 * d + h:4 * d + h + 1, :]
                dtrow = dtt[4 * d + h:4 * d + h + 1, :]
                cl = cb[last:last + 1, :]
                lhs.append((gmat * jnp.where(causal, jnp.exp(cb - crow), 0.0) * dtrow).astype(BF16))
                lhs.append((bmf * (jnp.exp(cl - crow) * dtrow)).astype(BF16))
                cbs.append(cb)
                cls.append(cl)
            big = jnp.dot(jnp.concatenate(lhs, axis=0), xp, preferred_element_type=F32)
            y_ref[0, :, 128 * g:128 * g + 128] = pick(big[0:L], big[2 * L:3 * L]) + yo * jnp.exp(pick(cbs[0], cbs[1]))
            hs[d, g] = hprev * jnp.exp(pick(cls[0], cls[1])) + pick(big[L:2 * L], big[3 * L:4 * L])

    direction(xf_ref, dttf_ref, bmtf_ref, yf_ref, 0, si <= li, li <= si, L - 1)
    direction(xb_ref, dttb_ref, bmtb_ref, yb_ref, 1, si >= li, li >= si, 0)


def _ssd(xbc, dtt, bmt, a_col, l):
    nbatch = xbc.shape[0]
    nch = N // SSD_CHUNK
    nctx = LC // SSD_CHUNK
    fidx = lambda i: jnp.where(i < nctx, nch - nctx + i, i - nctx)
    bidx = lambda i: nch - 1 - i
    ins = lambda f: [pl.BlockSpec((1, SSD_CHUNK, 768), lambda b, i: (b, f(i), 0)),
                     pl.BlockSpec((1, 128, SSD_CHUNK), lambda b, i: (b, 0, f(i))),
                     pl.BlockSpec((1, 256, SSD_CHUNK), lambda b, i: (b, 0, f(i)))]
    return pl.pallas_call(
        _ssd_kernel,
        grid=(nbatch, nch),
        in_specs=ins(fidx) + ins(bidx) + [_lspec(a_col, l)],
        out_specs=[pl.BlockSpec((1, SSD_CHUNK, 256), lambda b, i: (b, fidx(i), 0)),
                   pl.BlockSpec((1, SSD_CHUNK, 256), lambda b, i: (b, bidx(i), 0))],
        out_shape=[jax.ShapeDtypeStruct((nbatch, N, 256), F32)] * 2,
        scratch_shapes=[pltpu.VMEM((2, 2, 128, 128), F32)],
        compiler_params=_cp("arbitrary", "arbitrary"),
        name="ssd",
    )(xbc, dtt, bmt, xbc, dtt, bmt, a_col)


def _m2_norm_kernel(x_ref, z_ref, yf_ref, yb_ref, d_ref, g_ref, o_ref):
    y = d_ref[...] * x_ref[...] + yf_ref[...] + yb_ref[...]
    o_ref[...] = _rms(y * _silu(z_ref[...]), g_ref[...]).astype(o_ref.dtype)


def _m2_norm(xbc2d, p_m2_2d, yf2d, yb2d, dvec, gn, l):
    rows = xbc2d.shape[0]
    blk = pl.BlockSpec((ROW_BLK, 256), lambda i: (i, 0))
    full = lambda a: _lspec(a, l)
    return pl.pallas_call(
        _m2_norm_kernel,
        grid=(rows // ROW_BLK,),
        in_specs=[blk, blk, blk, blk, full(dvec), full(gn)],
        out_specs=blk,
        out_shape=jax.ShapeDtypeStruct((rows, 256), BF16),
        compiler_params=_cp("arbitrary"),
        name="m2_norm",
    )(xbc2d, p_m2_2d, yf2d, yb2d, dvec, gn)


def _merge_kernel(xl_ref, xc_ref, oa_ref, on_ref, os_ref, om_ref, gl_ref, wb_ref, wo_ref, gpm_ref, gate1_ref,
                  gpf_ref, sh2_ref, sc2_ref, wr_ref, x1_ref, h2_ref, aff_ref, *, nb):
    outs = (oa_ref, on_ref, os_ref, om_ref)
    y = None
    for j in range(4):
        d = jnp.dot(outs[j][...], wb_ref[j], preferred_element_type=F32)
        t = d * jnp.tanh(gl_ref[:, 1024 * j:1024 * j + 1024]) + d
        y = t if y is None else y + t
    y2 = _dot(y, wo_ref[...])
    x1 = _stream_block(xl_ref, xc_ref, nb) + gate1_ref[0] * _rms(y2, gpm_ref[...])
    x1_ref[...] = x1
    h2 = _rms(x1, gpf_ref[...]) * (1.0 + sc2_ref[0]) + sh2_ref[0]
    h2_ref[...] = h2.astype(BF16)
    logits = _dot_hi_nt(wr_ref[...], h2)
    m = jnp.max(logits, axis=0, keepdims=True)
    e = jnp.exp(logits - m)
    aff_ref[0, 0] = e / jnp.sum(e, axis=0, keepdims=True)


def _merge(stream, o_mla, o_na, o_s5, o_m2, gl, wb, wo, gpm, mod3, gpf, wrt, nbatch, nb, l):
    mrow = _mod_row(nb)
    xargs, xspecs = _stream_specs(stream, nb)
    fb = lambda i: (_flat_blk(i, nb), 0)
    blk = lambda w: pl.BlockSpec((ROW_BLK, w), fb)
    oblk = lambda w: pl.BlockSpec((ROW_BLK, w), lambda i: (i, 0))
    full = lambda a: _lspec(a, l)
    modspec = lambda k: pl.BlockSpec((1, 1, D), lambda i: (mrow(i, nbatch), 0, k))
    orows = nbatch * nb * ROW_BLK
    return pl.pallas_call(
        functools.partial(_merge_kernel, nb=nb),
        grid=(nbatch * nb,),
        in_specs=xspecs + [oblk(256), oblk(256), blk(256), blk(256), blk(GATE_W),
                  full(wb), full(wo), full(gpm), modspec(2), full(gpf), modspec(3), modspec(4), full(wrt)],
        out_specs=[oblk(D), oblk(D),
                   pl.BlockSpec((1, 1, N_EXPERTS, ROW_BLK), lambda i: (i // nb, i % nb, 0, 0))],
        out_shape=[jax.ShapeDtypeStruct((orows, D), F32),
                   jax.ShapeDtypeStruct((orows, D), BF16),
                   jax.ShapeDtypeStruct((nbatch, nb, N_EXPERTS, ROW_BLK), F32)],
        compiler_params=_cp("arbitrary"),
        name="merge",
    )(*xargs, o_mla, o_na, o_s5, o_m2, gl, wb, wo, gpm, mod3, gpf, mod3, mod3, wrt)


def _topk_kernel(aff_ref, slot_ref, *, nk, cap):
    a = aff_ref[0]
    bits = lax.bitcast_convert_type(a, jnp.int32)
    count = lambda m: jnp.sum(jnp.sum(jnp.where(m, 1.0, 0.0), axis=2, keepdims=True), axis=0, keepdims=True)
    thr = jnp.zeros((1, N_EXPERTS, 1), jnp.int32)
    for bit in range(30, -1, -1):
        cand = thr | (1 << bit)
        thr = jnp.where(count(bits >= cand) >= cap, cand, thr)
    gt = bits > thr
    eq = bits == thr
    need = cap - count(gt)

    tri = jnp.where(lax.broadcasted_iota(jnp.int32, (256, 256), 0) <= lax.broadcasted_iota(jnp.int32, (256, 256), 1),
                    1.0, 0.0).astype(BF16)

    def prefix_excl(m):
        incl = jnp.dot(m.reshape(nk * N_EXPERTS, 256).astype(BF16), tri,
                       preferred_element_type=F32).reshape(nk, N_EXPERTS, 256)
        offs = []
        run = jnp.zeros((1, N_EXPERTS, 1), F32)
        for k in range(nk):
            offs.append(run)
            run = run + incl[k:k + 1, :, 255:256]
        off = offs[0] if nk == 1 else jnp.concatenate(offs, axis=0)
        return incl - m + off

    eqf = jnp.where(eq, 1.0, 0.0)
    sel = jnp.where(gt, 1.0, jnp.where(eq & (prefix_excl(eqf) < need), 1.0, 0.0))
    slot = jnp.where(sel > 0.5, prefix_excl(sel), -1.0)
    slot_ref[0] = slot.astype(jnp.int32)


def _topk(aff, blk0, nk, cap):
    nbatch = aff.shape[0]
    return pl.pallas_call(
        functools.partial(_topk_kernel, nk=nk, cap=cap),
        grid=(nbatch,),
        in_specs=[pl.BlockSpec((1, nk, N_EXPERTS, ROW_BLK), lambda b: (b, blk0, 0, 0))],
        out_specs=pl.BlockSpec((1, nk, N_EXPERTS, ROW_BLK), lambda b: (b, 0, 0, 0)),
        out_shape=jax.ShapeDtypeStruct((nbatch, nk, N_EXPERTS, ROW_BLK), jnp.int32),
        compiler_params=_cp("arbitrary"),
        name="topk",
    )(aff)


def _expert_kernel(*refs, has_ctx):
    if has_ctx:
        (w0_ref, fits_ref, hl_ref, sl_ref, al_ref, hc_ref, sc_ref, ac_ref, wg_ref, wu_ref, wd_ref,
         yl_ref, yc_ref, wgb, wub, wdb, xs_acc, gs_acc, xc_all, gc_all) = refs
    else:
        (w0_ref, fits_ref, hl_ref, sl_ref, al_ref, wg_ref, wu_ref, wd_ref, yl_ref,
         wgb, wub, wdb, xs_acc, gs_acc) = refs
    e = pl.program_id(0)
    b = pl.program_id(1)

    @pl.when(b == 0)
    def _():
        wgb[...] = wg_ref[0, 0].astype(BF16)
        wub[...] = wu_ref[0, 0].astype(BF16)
        wdb[...] = wd_ref[0, 0].astype(BF16)

    def gather(h_ref, slot_ref, aff_ref, nk, cap):
        r_iota = lax.broadcasted_iota(jnp.int32, (cap, ROW_BLK), 0)
        xs = jnp.zeros((cap, D), F32)
        gs = jnp.zeros((cap, 1), F32)
        for k in range(nk):
            pm = slot_ref[0, k, pl.ds(e, 1), :] == r_iota
            xs = xs + jnp.dot(jnp.where(pm, 1.0, 0.0).astype(BF16), h_ref[0, ROW_BLK * k:ROW_BLK * k + ROW_BLK, :],
                              preferred_element_type=F32)
            gs = gs + jnp.sum(jnp.where(pm, aff_ref[0, k, pl.ds(e, 1), :], 0.0), axis=1, keepdims=True)
        return xs.astype(BF16), gs

    def ffn(xsb):
        gt = jnp.dot(xsb, wgb[...], preferred_element_type=F32)
        up = jnp.dot(xsb, wub[...], preferred_element_type=F32)
        return jnp.dot((_silu(gt) * up).astype(BF16), wdb[...], preferred_element_type=F32)

    @pl.when(fits_ref[b] == 1)
    def _():
        xs_acc[...] = jnp.zeros_like(xs_acc)
        gs_acc[...] = jnp.zeros_like(gs_acc)
        r_iota = lax.broadcasted_iota(jnp.int32, (MOE_WIN, ROW_BLK), 0)
        for k in range(LAT_BLKS):
            w0 = pl.multiple_of(w0_ref[(b * LAT_BLKS + k) * N_EXPERTS + e], 16)
            pm = (sl_ref[0, k, pl.ds(e, 1), :] - w0) == r_iota
            xs_acc[pl.ds(w0, MOE_WIN), :] += jnp.dot(jnp.where(pm, 1.0, 0.0).astype(BF16),
                                                     hl_ref[0, ROW_BLK * k:ROW_BLK * k + ROW_BLK, :],
                                                     preferred_element_type=F32)
            gs_acc[pl.ds(w0, MOE_WIN), :] += jnp.sum(jnp.where(pm, al_ref[0, k, pl.ds(e, 1), :], 0.0),
                                                     axis=1, keepdims=True)

    @pl.when(fits_ref[b] != 1)
    def _():
        xs, gs = gather(hl_ref, sl_ref, al_ref, LAT_BLKS, CAP_LAT)
        xs_acc[...] = xs.astype(F32)
        gs_acc[...] = gs

    yl_ref[0, 0] = (ffn(xs_acc[...].astype(BF16)) * gs_acc[...]).astype(yl_ref.dtype)
    if has_ctx:
        xs, gs = gather(hc_ref, sc_ref, ac_ref, 1, CAP_CTX)
        row = pl.multiple_of(b * CAP_CTX, CAP_CTX)
        xc_all[pl.ds(row, CAP_CTX), :] = xs
        gc_all[pl.ds(row, CAP_CTX), :] = gs

        @pl.when(b == pl.num_programs(1) - 1)
        def _():
            yc_ref[0] = (ffn(xc_all[...]) * gc_all[...]).astype(yc_ref.dtype)


def _experts(win0, fits, h2, slot_lat, slot_ctx, aff, w_gate, w_up, w_down, l, has_ctx):
    nbatch = h2.shape[0]
    idx4 = lambda e, b: (b, 0, 0, 0)
    smem = pl.BlockSpec(memory_space=pltpu.SMEM)
    in_specs = [smem, smem,
                pl.BlockSpec((1, T, D), lambda e, b: (b, 0, 0)),
                pl.BlockSpec((1, LAT_BLKS, N_EXPERTS, ROW_BLK), idx4),
                pl.BlockSpec((1, LAT_BLKS, N_EXPERTS, ROW_BLK), idx4)]
    args = [win0, fits, h2, slot_lat, aff]
    out_specs = [pl.BlockSpec((1, 1, CAP_LAT, D), lambda e, b: (b, e, 0, 0))]
    out_shape = [jax.ShapeDtypeStruct((nbatch, N_EXPERTS, CAP_LAT, D), BF16)]
    if has_ctx:
        in_specs += [pl.BlockSpec((1, LC, D), lambda e, b: (b, LAT_BLKS, 0)),
                     pl.BlockSpec((1, 1, N_EXPERTS, ROW_BLK), idx4),
                     pl.BlockSpec((1, 1, N_EXPERTS, ROW_BLK), lambda e, b: (b, LAT_BLKS, 0, 0))]
        args += [h2, slot_ctx, aff]
        out_specs.append(pl.BlockSpec((1, nbatch * CAP_CTX, D), lambda e, b: (e, 0, 0)))
        out_shape.append(jax.ShapeDtypeStruct((N_EXPERTS, nbatch * CAP_CTX, D), BF16))
    wspec = pl.BlockSpec((1, 1, D, D), lambda e, b: (l, e, 0, 0))
    in_specs += [wspec, wspec, wspec]
    args += [w_gate, w_up, w_down]
    scratch = [pltpu.VMEM((D, D), BF16)] * 3 + [pltpu.VMEM((CAP_LAT, D), F32), pltpu.VMEM((CAP_LAT, 1), F32)]
    if has_ctx:
        scratch += [pltpu.VMEM((nbatch * CAP_CTX, D), BF16), pltpu.VMEM((nbatch * CAP_CTX, 1), F32)]
    return pl.pallas_call(
        functools.partial(_expert_kernel, has_ctx=has_ctx),
        grid=(N_EXPERTS, nbatch),
        in_specs=in_specs,
        out_specs=out_specs,
        out_shape=out_shape,
        scratch_shapes=scratch,
        compiler_params=_cp("arbitrary", "arbitrary"),
        name="experts",
    )(*args)


def _combine_kernel(*refs, has_ctx):
    if has_ctx:
        w0_ref, fits_ref, x1_ref, yl_ref, scl_ref, yc_ref, scc_ref, gate2_ref, g_ref, o_ref, ycat = refs
    else:
        w0_ref, fits_ref, x1_ref, yl_ref, scl_ref, gate2_ref, g_ref, o_ref, ycat = refs
    b = pl.program_id(0)
    j = pl.program_id(1)

    def finish(acc):
        o_ref[0] = x1_ref[0] + gate2_ref[0] * _rms(acc, g_ref[...])

    def comb(sc_ref, y_of, cap):
        lane = lax.broadcasted_iota(jnp.int32, (ROW_BLK, cap), 1)
        sc = sc_ref[0]
        acc = jnp.zeros((ROW_BLK, D), F32)
        for e in range(N_EXPERTS):
            pt = jnp.where(sc[:, e:e + 1] == lane, 1.0, 0.0).astype(BF16)
            acc = acc + jnp.dot(pt, y_of(e), preferred_element_type=F32)
        finish(acc)

    def comb_windowed():
        lane = lax.broadcasted_iota(jnp.int32, (ROW_BLK, ROW_BLK), 1)
        sc = scl_ref[0]
        per = ROW_BLK // MOE_WIN
        if per * MOE_WIN < ROW_BLK:
            ycat[:, per * MOE_WIN:, :] = jnp.zeros((2, ROW_BLK - per * MOE_WIN, D), BF16)
        acc = jnp.zeros((ROW_BLK, D), F32)
        for gi, e0 in enumerate(range(0, N_EXPERTS, per)):
            hit = None
            for q, e in enumerate(range(e0, min(e0 + per, N_EXPERTS))):
                w0 = pl.multiple_of(w0_ref[(b * LAT_BLKS + j) * N_EXPERTS + e], 16)
                d = sc[:, e:e + 1] - w0
                d = jnp.where(jnp.logical_and(d >= 0, d < MOE_WIN), d + MOE_WIN * q, -1)
                hit = (d == lane) if hit is None else jnp.logical_or(hit, d == lane)
                ycat[gi % 2, MOE_WIN * q:MOE_WIN * q + MOE_WIN, :] = yl_ref[0, e, pl.ds(w0, MOE_WIN), :]
            acc = acc + jnp.dot(jnp.where(hit, 1.0, 0.0).astype(BF16), ycat[gi % 2], preferred_element_type=F32)
        finish(acc)

    lat_y = lambda e: yl_ref[0, e]
    fits = fits_ref[b] == 1

    @pl.when(jnp.logical_and(j < LAT_BLKS, fits))
    def _():
        comb_windowed()

    @pl.when(jnp.logical_and(j < LAT_BLKS, jnp.logical_not(fits)))
    def _():
        comb(scl_ref, lat_y, CAP_LAT)

    if has_ctx:
        @pl.when(j == LAT_BLKS)
        def _():
            comb(scc_ref, lambda e: yc_ref[e], CAP_CTX)


def _combine(win0, fits, x1, y_lat, scol_lat, y_ctx, scol_ctx, mod3, g, has_ctx, l):
    nbatch = x1.shape[0]
    nb = NBLK if has_ctx else LAT_BLKS
    ntok = N if has_ctx else T
    smem = pl.BlockSpec(memory_space=pltpu.SMEM)
    in_specs = [smem, smem,
                pl.BlockSpec((1, ROW_BLK, D), lambda b, j: (b, j, 0)),
                pl.BlockSpec((1, N_EXPERTS, CAP_LAT, D), lambda b, j: (b, 0, 0, 0)),
                pl.BlockSpec((1, ROW_BLK, N_EXPERTS), lambda b, j: (b, jnp.minimum(j, LAT_BLKS - 1), 0))]
    args = [win0, fits, x1, y_lat, scol_lat]
    if has_ctx:
        in_specs += [pl.BlockSpec((N_EXPERTS, CAP_CTX, D), lambda b, j: (0, b, 0)),
                     pl.BlockSpec((1, ROW_BLK, N_EXPERTS), lambda b, j: (b, 0, 0))]
        args += [y_ctx, scol_ctx]
    in_specs += [pl.BlockSpec((1, 1, D), lambda b, j: (jnp.where(j == LAT_BLKS, nbatch, b), 0, 5)),
                 _lspec(g, l)]
    args += [mod3, g]
    return pl.pallas_call(
        functools.partial(_combine_kernel, has_ctx=has_ctx),
        grid=(nbatch, nb),
        in_specs=in_specs,
        out_specs=pl.BlockSpec((1, ROW_BLK, D), lambda b, j: (b, j, 0)),
        out_shape=jax.ShapeDtypeStruct((nbatch, ntok, D), F32),
        scratch_shapes=[pltpu.VMEM((2, ROW_BLK, D), BF16)],
        compiler_params=_cp("arbitrary", "arbitrary"),
        name="combine",
    )(*args)


def _rope_tables():
    f = 1.0 / (ROPE_THETA ** (jnp.arange(0, 16, 2, dtype=F32) / 16))
    pos = jnp.arange(T)
    row, col = pos // GRID_W, pos % GRID_W
    ar = row.astype(F32)[:, None] * f[None, :]
    ac = col.astype(F32)[:, None] * f[None, :]
    cos32 = jnp.concatenate([jnp.cos(ar), jnp.cos(ar), jnp.cos(ac), jnp.cos(ac)], axis=-1)
    sin32 = jnp.concatenate([jnp.sin(ar), jnp.sin(ar), jnp.sin(ac), jnp.sin(ac)], axis=-1)
    cos32 = jnp.concatenate([cos32, jnp.ones((LC, 32), F32)], axis=0)
    sin32 = jnp.concatenate([sin32, jnp.zeros((LC, 32), F32)], axis=0)
    return cos32, sin32


def _rot_cols(w):
    a, b, c, d = w[..., 0:8], w[..., 8:16], w[..., 16:24], w[..., 24:32]
    return jnp.concatenate([-b, a, -d, c], axis=-1)


def _mla_weights(w_uq, w_ukv):
    nl, r, _ = w_uq.shape
    wq3 = w_uq.reshape(nl, r, 4, 96)
    z32 = jnp.zeros((nl, r, 4, 32), F32)
    wq = jnp.concatenate([wq3, z32], axis=-1).reshape(nl, r, 512)
    wqr = jnp.concatenate([jnp.zeros((nl, r, 4, 64), F32), _rot_cols(wq3[..., 64:96]), z32], axis=-1).reshape(nl, r, 512)
    rk = w_ukv.shape[1]
    wkv3 = w_ukv.reshape(nl, rk, 4, 128)
    wk = jnp.concatenate([wkv3[..., :64], jnp.zeros((nl, rk, 4, 64), F32)], axis=-1).reshape(nl, rk, 512)
    z64 = jnp.zeros((nl, rk, 64), F32)
    vh = [wkv3[:, :, h, 64:] for h in range(4)]
    wv = jnp.concatenate([vh[0], z64, z64, vh[1], vh[2], z64, z64, vh[3]], axis=-1)
    vone = np.tile(np.repeat(np.array([0.0, 1.0, 1.0, 0.0], np.float32), 64), 2)[None, :]
    e = np.zeros((32, 512), np.float32)
    for h in range(4):
        e[np.arange(32), 128 * h + 64 + np.arange(32)] = 1.0
    return (wq.astype(BF16), wqr.astype(BF16), wk.astype(BF16), wv.astype(BF16), jnp.asarray(e, BF16),
            jnp.asarray(vone))


def _inproj_weights(w):
    o = np.cumsum([0, 256, 128, 32, 768, 256, 256, 768, 8, 4096])
    w = w.astype(BF16)
    seg = lambda i: w[:, :, o[i]:o[i + 1]]
    nl = w.shape[0]
    kr = seg(2)
    wm = jnp.concatenate([seg(0), seg(1), kr, _rot_cols(kr), jnp.zeros((nl, D, 64), BF16)], axis=-1)
    w2 = jnp.concatenate([seg(5), seg(6), seg(7), jnp.zeros((nl, D, 120), BF16)], axis=-1)
    return [wm, seg(3), seg(4), w2, 0.5 * seg(8)]


def kernel(x, c, ctx, c_ctx, w_ada, b_ada, g_pre_mix, g_post_mix, g_pre_ffn, g_post_ffn, w_in, mla_g_cq, mla_g_ckv, mla_w_uq, mla_w_ukv, na_rpb, s5_a_re, s5_a_im, s5_log_step, s5_b_re, s5_b_im, s5_c_re, s5_c_im, s5_d, s5_w_glu, s5_b_glu, m2_conv_w, m2_conv_b, m2_a_log, m2_dt_bias, m2_d, m2_g_norm, w_branch, w_out, w_router, w_gate, w_up, w_down):
    nbatch = x.shape[0]
    depth = w_ada.shape[0]
    vec = lambda v: v.reshape(depth, 1, -1).astype(F32)

    stream = (x, ctx, 0)
    cvec = jnp.concatenate([c, c_ctx[None, :], jnp.zeros((7, D), F32)], axis=0)
    cvec = cvec[: ((nbatch + 1 + 7) // 8) * 8]
    b_ada3 = b_ada.reshape(depth, 1, 6 * D)

    cos32, sin32 = _rope_tables()
    qscale = (64 + 32) ** -0.5 * math.log2(math.e)
    lane_is_rope = np.tile(np.concatenate([np.zeros(64, bool), np.ones(32, bool), np.zeros(32, bool)]), 4)
    pick = lambda t32, fill: jnp.where(lane_is_rope[None, :], jnp.tile(jnp.pad(t32, ((0, 0), (64, 32))), (1, 4)), fill)
    cos_q = pick(cos32, 1.0) * qscale
    sin_q = pick(sin32, 0.0) * qscale

    g_pre_mix, g_post_mix, g_pre_ffn, g_post_ffn = vec(g_pre_mix), vec(g_post_mix), vec(g_pre_ffn), vec(g_post_ffn)
    w_in_segs = _inproj_weights(w_in)
    wq, wqr, wk, wv, e_mat, vone = _mla_weights(mla_w_uq, mla_w_ukv)
    mla_g_cq, mla_g_ckv = vec(mla_g_cq), vec(mla_g_ckv)
    na_tabs = _na_tables(na_rpb)
    s5_bmat, s5_avec, s5_cmat = _s5_params(s5_a_re, s5_a_im, s5_log_step, s5_b_re, s5_b_im, s5_c_re, s5_c_im, nbatch)
    s5_d, s5_b_glu, s5_w_glu = vec(s5_d), vec(s5_b_glu), s5_w_glu.astype(BF16)
    m2_conv_w, m2_conv_b = m2_conv_w.astype(F32), vec(m2_conv_b)
    dt_bias = jnp.pad(m2_dt_bias.reshape(depth, 1, 8).astype(F32), ((0, 0), (0, 0), (0, 120)))
    a_col = jnp.pad(-jnp.exp(m2_a_log.astype(F32)).reshape(depth, 8, 1), ((0, 0), (0, 120), (0, 0)))
    m2_dvec, m2_g_norm = vec(jnp.repeat(m2_d, 64, axis=-1)), vec(m2_g_norm)
    w_branch, w_out = (0.5 * w_branch).astype(BF16), w_out.astype(BF16)
    w_router_t = jnp.swapaxes(w_router, 1, 2).astype(F32)

    for l in range(depth):
        has_ctx = l < depth - 1
        nb = NBLK if has_ctx else LAT_BLKS
        mod = _ada(cvec, w_ada, b_ada3, l)
        mod3 = mod.reshape(mod.shape[0], 1, 6 * D)

        p_mla, p_na, p_s5, p_m2, p_gate = _inproj(stream, g_pre_mix, mod3, w_in_segs, nbatch, l)

        q, k, v = _mla_prep(p_mla, mla_g_cq, mla_g_ckv, wq, wqr, wk, wv, e_mat, vone, cos_q, sin_q, cos32, sin32, l)
        o_mla = _mla_attn(q.reshape(nbatch, N, 512), k.reshape(nbatch, N, 512), v.reshape(nbatch, N, 512), has_ctx)

        o_na = _na_attn(p_na.reshape(nbatch, N, NA_W), na_tabs, has_ctx, l)

        yf, yb = _s5_scan(p_s5.reshape(nbatch, N, S5_W), s5_bmat, s5_avec, s5_cmat, l)
        o_s5 = _s5_glu(p_s5, yf.reshape(nbatch * N, S5_W), yb.reshape(nbatch * N, S5_W), s5_d, s5_w_glu, s5_b_glu, l)

        xbc, dtt, bmt = _m2_prep(p_m2.reshape(nbatch, N, M2_W), m2_conv_w, m2_conv_b, dt_bias, l)
        ssd_f, ssd_b = _ssd(xbc, dtt, bmt, a_col, l)
        o_m2 = _m2_norm(xbc.reshape(nbatch * N, 768), p_m2, ssd_f.reshape(nbatch * N, 256),
                        ssd_b.reshape(nbatch * N, 256), m2_dvec, m2_g_norm, l)

        x1, h2, aff = _merge(stream, o_mla.reshape(-1, 256), o_na.reshape(-1, 256), o_s5, o_m2, p_gate,
                             w_branch, w_out, g_post_mix, mod3, g_pre_ffn, w_router_t, nbatch, nb, l)

        slot_lat = _topk(aff, 0, LAT_BLKS, CAP_LAT)
        slot_ctx = _topk(aff, LAT_BLKS, 1, CAP_CTX) if has_ctx else None
        cnt = jnp.sum((slot_lat >= 0).astype(jnp.int32), axis=-1)
        first = jnp.cumsum(cnt, axis=1) - cnt
        win0 = jnp.minimum((first // 16) * 16, CAP_LAT - MOE_WIN)
        fits = jnp.all(first + cnt <= win0 + MOE_WIN, axis=(1, 2)).astype(jnp.int32)
        win0 = win0.reshape(-1)
        ys = _experts(win0, fits, h2.reshape(nbatch, nb * ROW_BLK, D), slot_lat, slot_ctx, aff, w_gate, w_up, w_down,
                      l, has_ctx)
        scol_lat = jnp.transpose(slot_lat, (0, 1, 3, 2)).reshape(nbatch, T, N_EXPERTS)
        scol_ctx = jnp.transpose(slot_ctx, (0, 1, 3, 2)).reshape(nbatch, LC, N_EXPERTS) if has_ctx else None
        xs = _combine(win0, fits, x1.reshape(nbatch, nb * ROW_BLK, D), ys[0], scol_lat, ys[1] if has_ctx else None,
                      scol_ctx, mod3, g_post_ffn, has_ctx, l)
        stream = (xs, xs, LAT_BLKS)
    return xs
```

```python
import functools
import math

import numpy as np
import jax
import jax.numpy as jnp
from jax import lax
from jax.experimental import pallas as pl
from jax.experimental.pallas import tpu as pltpu

F32 = jnp.float32
BF16 = jnp.bfloat16

D = 1024
T = 2048
LC = 256
N = T + LC
GRID_W = 64
ROW_BLK = 256
NBLK = N // ROW_BLK
LAT_BLKS = T // ROW_BLK
EPS = 1e-6
N_EXPERTS = 16
CAP_LAT = 2 * T // N_EXPERTS
CAP_CTX = 2 * LC // N_EXPERTS
ROPE_THETA = 10000.0
NEG = -1e30

MLA_W = 512
NA_W = 768
S5_W = 256
M2_W = 1152
GATE_W = 4096

MOE_WIN = 80
S5_CHUNK = 64
SSD_CHUNK = 128

VMEM_LIMIT = 56 * 1024 * 1024


def _cp(*sem):
    return pltpu.CompilerParams(dimension_semantics=sem, vmem_limit_bytes=VMEM_LIMIT)


def _lspec(a, l, *lead):
    nd = a.ndim - 1 - len(lead)
    return pl.BlockSpec((None,) * (1 + len(lead)) + a.shape[1 + len(lead):], lambda *_: (l,) + lead + (0,) * nd)


def _dot(a, b):
    return jnp.dot(a.astype(BF16), b.astype(BF16), preferred_element_type=F32)


def _dot_nt(a, b):
    return lax.dot_general(a.astype(BF16), b.astype(BF16), (((1,), (1,)), ((), ())),
                           preferred_element_type=F32)


def _split3(a):
    hi = a.astype(BF16)
    r = a - hi.astype(F32)
    mid = r.astype(BF16)
    lo = (r - mid.astype(F32)).astype(BF16)
    return hi, mid, lo


def _dot_hi(a, b):
    ah, am, _ = _split3(a)
    bh, bm, _ = _split3(b)
    f = lambda x, y: jnp.dot(x, y, preferred_element_type=F32)
    return f(ah, bh) + (f(ah, bm) + f(am, bh))


def _dot_hi_nt(a, b):
    ah, am, _ = _split3(a)
    bh, bm, _ = _split3(b)
    f = lambda x, y: lax.dot_general(x, y, (((1,), (1,)), ((), ())), preferred_element_type=F32)
    return f(ah, bh) + (f(ah, bm) + f(am, bh))


def _dot_exact_lhs(m_bf16, a):
    h, m, l = _split3(a)
    f = lambda y: jnp.dot(m_bf16, y, preferred_element_type=F32)
    return f(h) + (f(m) + f(l))


def _dot_exact_rhs(a, m_bf16):
    h, m, l = _split3(a)
    f = lambda y: jnp.dot(y, m_bf16, preferred_element_type=F32)
    return f(h) + (f(m) + f(l))


def _sigmoid(x):
    return 0.5 * jnp.tanh(0.5 * x) + 0.5


def _silu(x):
    return x * _sigmoid(x)


def _rms(x, g):
    return x * lax.rsqrt(jnp.mean(x * x, axis=-1, keepdims=True) + EPS) * g


def _mod_row(nb):
    def f(i, nbatch):
        return jnp.where(i % nb == LAT_BLKS, nbatch, i // nb)
    return f


def _flat_blk(i, nb):
    return (i // nb) * NBLK + i % nb


def _stream_specs(stream, nb):
    lat, ctxa, cblk = stream
    return [lat, ctxa], [pl.BlockSpec((1, ROW_BLK, D), lambda i: (i // nb, jnp.minimum(i % nb, LAT_BLKS - 1), 0)),
                         pl.BlockSpec((1, ROW_BLK, D), lambda i: (i // nb, cblk, 0))]


def _stream_block(xl_ref, xc_ref, nb):
    return jnp.where(pl.program_id(0) % nb == LAT_BLKS, xc_ref[0], xl_ref[0])


def _ada_kernel(c_ref, w_ref, b_ref, o_ref):
    c = c_ref[...]
    o_ref[...] = _dot_hi(_silu(c), w_ref[0]) + b_ref[0]


def _ada(cvec, w_ada, b_ada3, l):
    rows = cvec.shape[0]
    return pl.pallas_call(
        _ada_kernel,
        grid=(6,),
        in_specs=[pl.BlockSpec((rows, D), lambda k: (0, 0)),
                  pl.BlockSpec((1, D, D), lambda k: (l, 0, k)),
                  pl.BlockSpec((1, 1, D), lambda k: (l, 0, k))],
        out_specs=pl.BlockSpec((rows, D), lambda k: (0, k)),
        out_shape=jax.ShapeDtypeStruct((rows, 6 * D), F32),
        compiler_params=_cp("arbitrary"),
        name="ada",
    )(cvec, w_ada, b_ada3)


def _inproj_kernel(xl_ref, xc_ref, g_ref, sh_ref, sc_ref, wm, wn, ws, w2, wg, om, on, os_, o2, og, *, ctx_gates):
    x = _stream_block(xl_ref, xc_ref, NBLK)
    h = (_rms(x, g_ref[...]) * (1.0 + sc_ref[0]) + sh_ref[0]).astype(BF16)
    om[...] = jnp.dot(h, wm[...], preferred_element_type=F32)
    on[...] = jnp.dot(h, wn[...], preferred_element_type=F32).astype(BF16)
    os_[...] = jnp.dot(h, ws[...], preferred_element_type=F32)
    o2[...] = jnp.dot(h, w2[...], preferred_element_type=F32)
    if ctx_gates:
        og[...] = jnp.dot(h, wg[...], preferred_element_type=F32)
    else:
        is_ctx = pl.program_id(0) % NBLK == LAT_BLKS

        @pl.when(jnp.logical_not(is_ctx))
        def _():
            og[...] = jnp.dot(h, wg[...], preferred_element_type=F32)

        @pl.when(is_ctx)
        def _():
            og[...] = jnp.zeros_like(og)


def _inproj(stream, g, mod3, ws, nbatch, l, ctx_gates):
    rows = nbatch * N
    nblk = rows // ROW_BLK
    mrow = _mod_row(NBLK)
    xargs, xspecs = _stream_specs(stream, NBLK)
    full = lambda w: _lspec(w, l)
    widths = (MLA_W, NA_W, S5_W, M2_W, GATE_W)
    dts = (F32, BF16, F32, F32, F32)
    return pl.pallas_call(
        functools.partial(_inproj_kernel, ctx_gates=ctx_gates),
        grid=(nblk,),
        in_specs=xspecs + [
                  full(g),
                  pl.BlockSpec((1, 1, D), lambda i: (mrow(i, nbatch), 0, 0)),
                  pl.BlockSpec((1, 1, D), lambda i: (mrow(i, nbatch), 0, 1))]
                 + [full(w) for w in ws],
        out_specs=[pl.BlockSpec((ROW_BLK, w), lambda i: (i, 0)) for w in widths],
        out_shape=[jax.ShapeDtypeStruct((rows, w), dt) for w, dt in zip(widths, dts)],
        compiler_params=_cp("arbitrary"),
        name="inproj",
    )(*xargs, g, mod3, mod3, *ws)


def _mla_prep_kernel(p_ref, gq_ref, gkv_ref, wq_ref, wqr_ref, wk_ref, wv_ref, e_ref, vone_ref,
                     cos_ref, sin_ref, ck_ref, sk_ref, q_out, k_out, v_out):
    p = p_ref[...]
    cqn = _rms(p[:, :256], gq_ref[...]).astype(BF16)
    q = jnp.dot(cqn, wq_ref[...], preferred_element_type=F32)
    qr = jnp.dot(cqn, wqr_ref[...], preferred_element_type=F32)
    q_out[...] = (q * cos_ref[...] + qr * sin_ref[...]).astype(BF16)
    ckvn = _rms(p[:, 256:384], gkv_ref[...]).astype(BF16)
    kro = (p[:, 384:416] * ck_ref[...] + p[:, 416:448] * sk_ref[...]).astype(BF16)
    k = jnp.dot(ckvn, wk_ref[...], preferred_element_type=F32) + jnp.dot(kro, e_ref[...], preferred_element_type=F32)
    k_out[...] = k.astype(BF16)
    v_out[...] = (jnp.dot(ckvn, wv_ref[...], preferred_element_type=F32) + vone_ref[...]).astype(BF16)


def _mla_prep(p_mla, gq, gkv, wq, wqr, wk, wv, e, vone, cos_q, sin_q, cos_k, sin_k, l):
    rows = p_mla.shape[0]
    full = lambda w: _lspec(w, l)
    const = lambda w: pl.BlockSpec(w.shape, lambda j, b: (0, 0))
    row = lambda j, b: (b * NBLK + j, 0)
    tab = lambda w: pl.BlockSpec((ROW_BLK, w), lambda j, b: (j, 0))
    return pl.pallas_call(
        _mla_prep_kernel,
        grid=(NBLK, rows // N),
        in_specs=[pl.BlockSpec((ROW_BLK, MLA_W), row),
                  full(gq), full(gkv), full(wq), full(wqr), full(wk), full(wv), const(e), const(vone),
                  tab(512), tab(512), tab(32), tab(32)],
        out_specs=[pl.BlockSpec((ROW_BLK, 512), row)] * 3,
        out_shape=[jax.ShapeDtypeStruct((rows, 512), BF16)] * 3,
        compiler_params=_cp("arbitrary", "arbitrary"),
        name="mla_prep",
    )(p_mla, gq, gkv, wq, wqr, wk, wv, e, vone, cos_q, sin_q, cos_k, sin_k)


def _mla_attn_kernel(q_ref, k_ref, v_ref, o_ref, *, has_ctx):
    lane = lax.broadcasted_iota(jnp.int32, (q_ref.shape[1], 128), 1)

    def run(k0, nk):
        for hp in range(2):
            pv = []
            for h in (2 * hp, 2 * hp + 1):
                qh = q_ref[0, :, 128 * h:128 * h + 128]
                kh = k_ref[0, k0:k0 + nk, 128 * h:128 * h + 128]
                s = _dot_nt(qh, kh)
                p = jnp.exp2(s - jnp.max(s, axis=-1, keepdims=True)).astype(BF16)
                pv.append(jnp.dot(p, v_ref[0, k0:k0 + nk, 128 * h:128 * h + 128], preferred_element_type=F32))
            oa, ob = pv
            o = jnp.where(lane < 64, oa * (1.0 / oa[:, 64:65]), ob * (1.0 / ob[:, 0:1]))
            o_ref[0, :, 128 * hp:128 * hp + 128] = o.astype(o_ref.dtype)

    if has_ctx:
        j = pl.program_id(1)

        @pl.when(j < LAT_BLKS)
        def _():
            run(0, N)

        @pl.when(j == LAT_BLKS)
        def _():
            run(T, LC)
    else:
        run(0, N)


def _mla_attn(q, k, v, has_ctx):
    nbatch = q.shape[0]
    nb = NBLK if has_ctx else LAT_BLKS
    qb = ROW_BLK if has_ctx else 2 * ROW_BLK
    return pl.pallas_call(
        functools.partial(_mla_attn_kernel, has_ctx=has_ctx),
        grid=(nbatch, nb * ROW_BLK // qb),
        in_specs=[pl.BlockSpec((1, qb, 512), lambda b, j: (b, j, 0)),
                  pl.BlockSpec((1, N, 512), lambda b, j: (b, 0, 0)),
                  pl.BlockSpec((1, N, 512), lambda b, j: (b, 0, 0))],
        out_specs=pl.BlockSpec((1, qb, 256), lambda b, j: (b, j, 0)),
        out_shape=jax.ShapeDtypeStruct((nbatch, nb * ROW_BLK, 256), BF16),
        compiler_params=_cp("arbitrary", "arbitrary"),
        name="mla_attn",
    )(q, k, v)


def _na_kernel(q_ref, kc_ref, k0_ref, k1_ref, k2_ref, vc_ref, v0_ref, v1_ref, v2_ref, b_ref, o_ref, *, has_ctx):
    scale = jnp.asarray(0.125, BF16)

    lane = lax.broadcasted_iota(jnp.int32, (ROW_BLK, 128), 1)
    lane64 = lax.broadcasted_iota(jnp.int32, (GRID_W, 128), 1) < 64
    grp = jnp.minimum(pl.program_id(0), LAT_BLKS - 1)
    krow0 = 4 * jnp.clip(grp - 1, 0, LAT_BLKS - 3)

    def bias_piece(h, i):
        rows = []
        for qr in range(4):
            r = 4 * grp + qr
            rs = jnp.clip(r - 4, 0, 24)
            tiles = []
            for pr in range(2):
                kr = krow0 + (4 * i + 2 * pr)
                t = b_ref[h, jnp.clip(kr - r + 8, 0, 15)]
                ok_a = jnp.logical_and(kr >= rs, kr < rs + 8).astype(jnp.int32)
                ok_b = jnp.logical_and(kr + 1 >= rs, kr + 1 < rs + 8).astype(jnp.int32)
                tiles.append(jnp.where(jnp.where(lane64, ok_a, ok_b) > 0, t, NEG))
            rows.append(jnp.concatenate(tiles, axis=1))
        return jnp.concatenate(rows, axis=0)

    def heads(win):
        kws = (k0_ref, k1_ref, k2_ref)
        vws = (v0_ref, v1_ref, v2_ref)
        for hp in range(2):
            sl = slice(128 * hp, 128 * hp + 128)
            qp = q_ref[0, :, sl] * scale
            outs = []
            for hh in range(2):
                h = 2 * hp + hh
                qh = jnp.where((lane < 64) if hh == 0 else (lane >= 64), qp, jnp.zeros_like(qp))
                s_c = _dot_nt(qh, kc_ref[0, :, sl])
                s_w = []
                smax = s_c
                if win:
                    for i in range(3):
                        s = _dot_nt(qh, kws[i][0, :, sl]) + bias_piece(h, i)
                        s_w.append(s)
                        smax = jnp.maximum(smax, s)
                m = jnp.max(smax, axis=-1, keepdims=True)
                p = jnp.exp(s_c - m)
                psum = p
                o = _dot(p, vc_ref[0, :, sl])
                for i, s in enumerate(s_w):
                    p = jnp.exp(s - m)
                    psum = psum + p
                    o = o + _dot(p, vws[i][0, :, sl])
                outs.append(o * (1.0 / jnp.sum(psum, axis=-1, keepdims=True)))
            o_ref[0, :, sl] = jnp.where(lane < 64, outs[0], outs[1]).astype(o_ref.dtype)

    if has_ctx:
        g = pl.program_id(0)

        @pl.when(g < LAT_BLKS)
        def _():
            heads(True)

        @pl.when(g == LAT_BLKS)
        def _():
            heads(False)
    else:
        heads(True)


def _na_attn(qkv, bias, has_ctx, l):
    nbatch = qkv.shape[0]
    ng = NBLK if has_ctx else LAT_BLKS
    j0 = lambda g: jnp.clip(g - 1, 0, LAT_BLKS - 3)
    blk = lambda f: pl.BlockSpec((1, ROW_BLK, 256), f)
    return pl.pallas_call(
        functools.partial(_na_kernel, has_ctx=has_ctx),
        grid=(ng, nbatch),
        in_specs=[blk(lambda g, b: (b, g, 0)),
                  blk(lambda g, b: (b, LAT_BLKS, 1)),
                  blk(lambda g, b: (b, j0(g), 1)),
                  blk(lambda g, b: (b, j0(g) + 1, 1)),
                  blk(lambda g, b: (b, j0(g) + 2, 1)),
                  blk(lambda g, b: (b, LAT_BLKS, 2)),
                  blk(lambda g, b: (b, j0(g), 2)),
                  blk(lambda g, b: (b, j0(g) + 1, 2)),
                  blk(lambda g, b: (b, j0(g) + 2, 2)),
                  _lspec(bias, l)],
        out_specs=blk(lambda g, b: (b, g, 0)),
        out_shape=jax.ShapeDtypeStruct((nbatch, ng * ROW_BLK, 256), BF16),
        compiler_params=_cp("arbitrary", "arbitrary"),
        name="na_attn",
    )(qkv, qkv, qkv, qkv, qkv, qkv, qkv, qkv, qkv, bias)


def _na_col_structure():
    onehot = np.zeros((31, 64, 64), np.float32)
    colmask = np.zeros((64, 64), bool)
    for c in range(64):
        cs = min(max(c - 8, 0), 48)
        for kc in range(cs, cs + 16):
            onehot[kc - c + 15, c, kc] = 1.0
            colmask[c, kc] = True
    return np.tile(onehot.reshape(31, 4096), (3, 1)), colmask


_NA_COL_ONEHOT, _NA_COLMASK = _na_col_structure()


def _na_tables(rpb):
    nl = rpb.shape[0]
    r = rpb.astype(F32)
    hi = lax.reduce_precision(r, 8, 7)
    mid = lax.reduce_precision(r - hi, 8, 7)
    parts = jnp.stack([hi, mid, r - hi - mid], axis=3).reshape(nl * 60, 93)
    tz = jnp.dot(parts, _NA_COL_ONEHOT, preferred_element_type=F32).reshape(nl, 4, 15, 64, 64)
    tz = jnp.where(_NA_COLMASK, tz, NEG)
    edge = jnp.full((nl, 4, 1, 64, 64), NEG, F32)
    tz17 = jnp.concatenate([edge, tz, edge], axis=2)
    return jnp.concatenate([tz17[:, :, :16], tz17[:, :, 1:]], axis=-1)


def _s5_kernel(uf_ref, ub_ref, bf_ref, bb_ref, af_ref, ab_ref, cf_ref, cb_ref, yf_ref, yb_ref,
               hf, hb, buf_f, buf_b, tm_f, tm_b):
    i = pl.program_id(0)
    half = 1024
    nb = uf_ref.shape[0]

    @pl.when(i == 0)
    def _():
        hf[...] = jnp.zeros_like(hf)
        hb[...] = jnp.zeros_like(hb)

    def expand(u_ref, tm, b_ref, buf):
        for b in range(nb):
            for c in range(2):
                tm[c, pl.ds(b, S5_CHUNK, stride=nb), :] = u_ref[b, :, 128 * c:128 * c + 128]
        buf[...] = _dot(jnp.concatenate([tm[0], tm[1]], axis=1), b_ref[...])

    def scan(buf, a_ref, h, order):
        hr, hi = h[:, :half], h[:, half:]
        ar, ai = a_ref[:, :half], a_ref[:, half:]
        for k in order:
            rows = slice(k * nb, (k + 1) * nb)
            nr = ar * hr - ai * hi + buf[rows, :half]
            ni = ar * hi + ai * hr + buf[rows, half:]
            buf[rows, :half] = nr
            buf[rows, half:] = ni
            hr, hi = nr, ni
        h[:, :half] = hr
        h[:, half:] = hi

    def readout(buf, c_ref, tm, y_ref):
        y = _dot(buf[...], c_ref[...])
        for c in range(2):
            tm[c] = y[:, 128 * c:128 * c + 128]
        for b in range(nb):
            for c in range(2):
                y_ref[b, :, 128 * c:128 * c + 128] = tm[c, pl.ds(b, S5_CHUNK, stride=nb), :]

    expand(uf_ref, tm_f, bf_ref, buf_f)
    expand(ub_ref, tm_b, bb_ref, buf_b)
    scan(buf_f, af_ref, hf, range(S5_CHUNK))
    readout(buf_f, cf_ref, tm_f, yf_ref)
    scan(buf_b, ab_ref, hb, range(S5_CHUNK - 1, -1, -1))
    readout(buf_b, cb_ref, tm_b, yb_ref)


def _s5_scan(u, bmat, avec, cmat, l):
    nbatch = u.shape[0]
    cr = S5_CHUNK * nbatch
    nch = N // S5_CHUNK
    nctx = LC // S5_CHUNK
    fidx = lambda i: jnp.where(i < nctx, nch - nctx + i, i - nctx)
    bidx = lambda i: nch - 1 - i
    ublk = lambda f: pl.BlockSpec((nbatch, S5_CHUNK, 256), lambda i: (0, f(i), 0))
    return pl.pallas_call(
        _s5_kernel,
        grid=(nch,),
        in_specs=[ublk(fidx), ublk(bidx),
                  _lspec(bmat, l, 0), _lspec(bmat, l, 1), _lspec(avec, l, 0), _lspec(avec, l, 1),
                  _lspec(cmat, l, 0), _lspec(cmat, l, 1)],
        out_specs=[ublk(fidx), ublk(bidx)],
        out_shape=[jax.ShapeDtypeStruct((nbatch, N, 256), F32)] * 2,
        scratch_shapes=[pltpu.VMEM((nbatch, 2048), F32), pltpu.VMEM((nbatch, 2048), F32),
                        pltpu.VMEM((cr, 2048), F32), pltpu.VMEM((cr, 2048), F32),
                        pltpu.VMEM((2, cr, 128), F32), pltpu.VMEM((2, cr, 128), F32)],
        compiler_params=_cp("arbitrary"),
        name="s5_scan",
    )(u, u, bmat, bmat, avec, avec, cmat, cmat)


def _s5_glu_kernel(u_ref, yf_ref, yb_ref, d_ref, w_ref, b_ref, o_ref):
    y = d_ref[...] * u_ref[...] + yf_ref[...] + yb_ref[...]
    z = y * (0.5 * (1.0 + jnp.tanh(math.sqrt(2.0 / math.pi) * (y + 0.044715 * (y * y * y)))))
    o_ref[...] = (z * _sigmoid(_dot(z, w_ref[...]) + b_ref[...])).astype(o_ref.dtype)


def _s5_glu(u_tb, yf, yb, d, w, b, l):
    rows = u_tb.shape[0]
    rb = 4 * ROW_BLK
    blk = pl.BlockSpec((rb, 256), lambda i: (i, 0))
    full = lambda a: _lspec(a, l)
    return pl.pallas_call(
        _s5_glu_kernel,
        grid=(rows // rb,),
        in_specs=[blk, blk, blk, full(d), full(w), full(b)],
        out_specs=blk,
        out_shape=jax.ShapeDtypeStruct((rows, 256), BF16),
        compiler_params=_cp("arbitrary"),
        name="s5_glu",
    )(u_tb, yf, yb, d, w, b)


def _s5_params(a_re, a_im, log_step, b_re, b_im, c_re, c_im, nbatch):
    nl = a_re.shape[0]
    same_group = (np.arange(256)[:, None] // 16 == np.arange(1024)[None, :] // 64).astype(np.float32)
    a = lax.complex(a_re.astype(F32), a_im.astype(F32))
    abar = jnp.exp(jnp.exp(log_step.astype(F32))[..., None] * a)
    bbar = ((abar - 1.0) / a)[..., None] * lax.complex(b_re.astype(F32), b_im.astype(F32))
    blk_in = lambda m: jnp.tile(jnp.swapaxes(m, -1, -2).reshape(nl, 2, 256, 64), (1, 1, 1, 16)) * same_group
    bmat = jnp.concatenate([blk_in(jnp.real(bbar)), blk_in(jnp.imag(bbar))], axis=-1).astype(BF16)
    avec = jnp.concatenate([jnp.real(abar).reshape(nl, 2, 1, 1024), jnp.imag(abar).reshape(nl, 2, 1, 1024)], axis=-1)
    avec = jnp.broadcast_to(avec, (nl, 2, nbatch, 2048))
    blk_out = lambda m: jnp.tile(jnp.swapaxes(m, -1, -2).reshape(nl, 2, 1024, 16), (1, 1, 1, 16)) * same_group.T
    cmat = jnp.concatenate([blk_out(c_re.astype(F32)), -blk_out(c_im.astype(F32))], axis=-2).astype(BF16)
    return bmat, avec, cmat


def _m2_prep_kernel(x_ref, dt_ref, w_ref, b_ref, dtb_ref, xo_ref, dtt_ref, bmt_ref):
    c = pl.program_id(1)
    x = x_ref[0]
    t = lax.broadcasted_iota(jnp.int32, x.shape, 0)
    m2 = ((t >= 2) & (t < T)) | (t >= T + 2)
    m1 = ((t >= 1) & (t < T)) | (t >= T + 1)
    p1 = (t <= T - 2) | ((t >= T) & (t <= N - 2))
    w = w_ref[...]
    y = (w[0:1] * jnp.where(m2, pltpu.roll(x, 2, 0), 0.0)
         + w[1:2] * jnp.where(m1, pltpu.roll(x, 1, 0), 0.0)
         + w[2:3] * x
         + w[3:4] * jnp.where(p1, pltpu.roll(x, N - 1, 0), 0.0)) + b_ref[...]
    act = _silu(y)
    xo_ref[0] = act

    @pl.when(c == 0)
    def _():
        v = dt_ref[0] + dtb_ref[...]
        dtt_ref[0] = (jnp.maximum(v, 0.0) + jnp.log1p(jnp.exp(-jnp.abs(v)))).T

    @pl.when(c == 1)
    def _():
        bmt_ref[0] = act.T.astype(BF16)


def _m2_prep(p_m2, conv_w, conv_b, dtb, l):
    nbatch = p_m2.shape[0]
    return pl.pallas_call(
        _m2_prep_kernel,
        grid=(nbatch, 3),
        in_specs=[pl.BlockSpec((1, N, 256), lambda b, c: (b, 0, 1 + c)),
                  pl.BlockSpec((1, N, 128), lambda b, c: (b, 0, 8)),
                  pl.BlockSpec((None, 4, 256), lambda b, c: (l, 0, c)),
                  pl.BlockSpec((None, 1, 256), lambda b, c: (l, 0, c)),
                  _lspec(dtb, l)],
        out_specs=[pl.BlockSpec((1, N, 256), lambda b, c: (b, 0, c)),
                   pl.BlockSpec((1, 128, N), lambda b, c: (b, 0, 0)),
                   pl.BlockSpec((1, 256, N), lambda b, c: (b, 0, 0))],
        out_shape=[jax.ShapeDtypeStruct((nbatch, N, 768), F32),
                   jax.ShapeDtypeStruct((nbatch, 128, N), F32),
                   jax.ShapeDtypeStruct((nbatch, 256, N), BF16)],
        compiler_params=_cp("arbitrary", "arbitrary"),
        name="m2_prep",
    )(p_m2, p_m2, conv_w, conv_b, dtb)


def _ssd_kernel(xf_ref, dttf_ref, bmtf_ref, xb_ref, dttb_ref, bmtb_ref, acol_ref, yf_ref, yb_ref, hs):
    i = pl.program_id(1)
    L = SSD_CHUNK

    @pl.when(i == 0)
    def _():
        hs[...] = jnp.zeros_like(hs)

    li = lax.broadcasted_iota(jnp.int32, (L, L), 0)
    si = lax.broadcasted_iota(jnp.int32, (L, L), 1)
    left = si < 64
    pick = lambda a, b: jnp.where(left, a, b)

    def direction(x_ref, dtt_ref, bmt_ref, y_ref, d, causal, causal_t, last):
        cz = jnp.where(causal, 1.0, 0.0).astype(BF16)
        czt = jnp.where(causal_t, 1.0, 0.0).astype(BF16)
        xbc = x_ref[0]
        dtt = dtt_ref[0]
        parts = jnp.concatenate([p.astype(F32) for p in _split3(dtt * acol_ref[...])], axis=1)
        cumt = jnp.dot(parts.astype(BF16), jnp.concatenate([czt, czt, czt], axis=0),
                       preferred_element_type=F32)
        reps = jnp.concatenate([jnp.broadcast_to(parts[4 * d + h:4 * d + h + 1, :], (L, 3 * L)) for h in range(4)],
                               axis=0).astype(BF16)
        cumb = lax.dot_general(jnp.concatenate([cz, cz, cz], axis=1), reps, (((1,), (1,)), ((), ())),
                               preferred_element_type=F32)
        for g in range(2):
            xp = xbc[:, 128 * g:128 * g + 128].astype(BF16)
            cm = xbc[:, 512 + 128 * g:640 + 128 * g].astype(BF16)
            bmt = bmt_ref[0, 128 * g:128 * g + 128, :]
            hprev = hs[d, g]
            go = jnp.dot(cm, jnp.concatenate([bmt, hprev.astype(BF16)], axis=1), preferred_element_type=F32)
            gmat, yo = go[:, :L], go[:, L:]
            bmf = bmt.astype(F32)
            lhs, cbs, cls = [], [], []
            for hh in range(2):
                h = 2 * g + hh
                cb = cumb[:, L * h:L * h + L]
                crow = cumt[4 * d + h:4 * d + h + 1, :]
                dtrow = dtt[4 * d + h:4 * d + h + 1, :]
                cl = cb[last:last + 1, :]
                lhs.append((gmat * jnp.where(causal, jnp.exp(cb - crow), 0.0) * dtrow).astype(BF16))
                lhs.append((bmf * (jnp.exp(cl - crow) * dtrow)).astype(BF16))
                cbs.append(cb)
                cls.append(cl)
            big = jnp.dot(jnp.concatenate(lhs, axis=0), xp, preferred_element_type=F32)
            y_ref[0, :, 128 * g:128 * g + 128] = pick(big[0:L], big[2 * L:3 * L]) + yo * jnp.exp(pick(cbs[0], cbs[1]))
            hs[d, g] = hprev * jnp.exp(pick(cls[0], cls[1])) + pick(big[L:2 * L], big[3 * L:4 * L])

    direction(xf_ref, dttf_ref, bmtf_ref, yf_ref, 0, si <= li, li <= si, L - 1)
    direction(xb_ref, dttb_ref, bmtb_ref, yb_ref, 1, si >= li, li >= si, 0)


def _ssd(xbc, dtt, bmt, a_col, l):
    nbatch = xbc.shape[0]
    nch = N // SSD_CHUNK
    nctx = LC // SSD_CHUNK
    fidx = lambda i: jnp.where(i < nctx, nch - nctx + i, i - nctx)
    bidx = lambda i: nch - 1 - i
    ins = lambda f: [pl.BlockSpec((1, SSD_CHUNK, 768), lambda b, i: (b, f(i), 0)),
                     pl.BlockSpec((1, 128, SSD_CHUNK), lambda b, i: (b, 0, f(i))),
                     pl.BlockSpec((1, 256, SSD_CHUNK), lambda b, i: (b, 0, f(i)))]
    return pl.pallas_call(
        _ssd_kernel,
        grid=(nbatch, nch),
        in_specs=ins(fidx) + ins(bidx) + [_lspec(a_col, l)],
        out_specs=[pl.BlockSpec((1, SSD_CHUNK, 256), lambda b, i: (b, fidx(i), 0)),
                   pl.BlockSpec((1, SSD_CHUNK, 256), lambda b, i: (b, bidx(i), 0))],
        out_shape=[jax.ShapeDtypeStruct((nbatch, N, 256), F32)] * 2,
        scratch_shapes=[pltpu.VMEM((2, 2, 128, 128), F32)],
        compiler_params=_cp("arbitrary", "arbitrary"),
        name="ssd",
    )(xbc, dtt, bmt, xbc, dtt, bmt, a_col)


def _m2_norm_kernel(x_ref, z_ref, yf_ref, yb_ref, d_ref, g_ref, o_ref):
    y = d_ref[...] * x_ref[...] + yf_ref[...] + yb_ref[...]
    o_ref[...] = _rms(y * _silu(z_ref[...]), g_ref[...]).astype(o_ref.dtype)


def _m2_norm(xbc2d, p_m2_2d, yf2d, yb2d, dvec, gn, l):
    rows = xbc2d.shape[0]
    rb = 4 * ROW_BLK
    blk = pl.BlockSpec((rb, 256), lambda i: (i, 0))
    full = lambda a: _lspec(a, l)
    return pl.pallas_call(
        _m2_norm_kernel,
        grid=(rows // rb,),
        in_specs=[blk, blk, blk, blk, full(dvec), full(gn)],
        out_specs=blk,
        out_shape=jax.ShapeDtypeStruct((rows, 256), BF16),
        compiler_params=_cp("arbitrary"),
        name="m2_norm",
    )(xbc2d, p_m2_2d, yf2d, yb2d, dvec, gn)


def _merge_kernel(xl_ref, xc_ref, oa_ref, on_ref, os_ref, om_ref, gl_ref, wb_ref, wo_ref, gpm_ref, gate1_ref,
                  gpf_ref, sh2_ref, sc2_ref, wr_ref, x1_ref, h2_ref, aff_ref, *, nb):
    outs = (oa_ref, on_ref, os_ref, om_ref)
    y = None
    for j in range(4):
        d = jnp.dot(outs[j][...], wb_ref[j], preferred_element_type=F32)
        t = d * jnp.tanh(gl_ref[:, 1024 * j:1024 * j + 1024]) + d
        y = t if y is None else y + t
    y2 = _dot(y, wo_ref[...])
    x1 = _stream_block(xl_ref, xc_ref, nb) + gate1_ref[0] * _rms(y2, gpm_ref[...])
    x1_ref[...] = x1
    h2 = _rms(x1, gpf_ref[...]) * (1.0 + sc2_ref[0]) + sh2_ref[0]
    h2_ref[...] = h2.astype(BF16)
    logits = _dot_hi_nt(wr_ref[...], h2)
    m = jnp.max(logits, axis=0, keepdims=True)
    e = jnp.exp(logits - m)
    aff_ref[0, 0] = e / jnp.sum(e, axis=0, keepdims=True)


def _merge(stream, o_mla, o_na, o_s5, o_m2, gl, wb, wo, gpm, mod3, gpf, wrt, nbatch, nb, l):
    mrow = _mod_row(nb)
    xargs, xspecs = _stream_specs(stream, nb)
    fb = lambda i: (_flat_blk(i, nb), 0)
    blk = lambda w: pl.BlockSpec((ROW_BLK, w), fb)
    oblk = lambda w: pl.BlockSpec((ROW_BLK, w), lambda i: (i, 0))
    full = lambda a: _lspec(a, l)
    modspec = lambda k: pl.BlockSpec((1, 1, D), lambda i: (mrow(i, nbatch), 0, k))
    orows = nbatch * nb * ROW_BLK
    return pl.pallas_call(
        functools.partial(_merge_kernel, nb=nb),
        grid=(nbatch * nb,),
        in_specs=xspecs + [oblk(256), oblk(256), blk(256), blk(256), blk(GATE_W),
                  full(wb), full(wo), full(gpm), modspec(2), full(gpf), modspec(3), modspec(4), full(wrt)],
        out_specs=[oblk(D), oblk(D),
                   pl.BlockSpec((1, 1, N_EXPERTS, ROW_BLK), lambda i: (i // nb, i % nb, 0, 0))],
        out_shape=[jax.ShapeDtypeStruct((orows, D), F32),
                   jax.ShapeDtypeStruct((orows, D), BF16),
                   jax.ShapeDtypeStruct((nbatch, nb, N_EXPERTS, ROW_BLK), F32)],
        compiler_params=_cp("arbitrary"),
        name="merge",
    )(*xargs, o_mla, o_na, o_s5, o_m2, gl, wb, wo, gpm, mod3, gpf, mod3, mod3, wrt)


def _topk_kernel(aff_ref, slot_ref, *, nk, cap):
    a = aff_ref[0]
    bits = lax.bitcast_convert_type(a, jnp.int32)
    count = lambda m: jnp.sum(jnp.sum(jnp.where(m, 1.0, 0.0), axis=2, keepdims=True), axis=0, keepdims=True)
    thr = jnp.zeros((1, N_EXPERTS, 1), jnp.int32)
    for bit in range(30, -1, -1):
        cand = thr | (1 << bit)
        thr = jnp.where(count(bits >= cand) >= cap, cand, thr)
    gt = bits > thr
    eq = bits == thr
    need = cap - count(gt)

    tri = jnp.where(lax.broadcasted_iota(jnp.int32, (256, 256), 0) <= lax.broadcasted_iota(jnp.int32, (256, 256), 1),
                    1.0, 0.0).astype(BF16)

    def prefix_excl(m):
        incl = jnp.dot(m.reshape(nk * N_EXPERTS, 256).astype(BF16), tri,
                       preferred_element_type=F32).reshape(nk, N_EXPERTS, 256)
        offs = []
        run = jnp.zeros((1, N_EXPERTS, 1), F32)
        for k in range(nk):
            offs.append(run)
            run = run + incl[k:k + 1, :, 255:256]
        off = offs[0] if nk == 1 else jnp.concatenate(offs, axis=0)
        return incl - m + off

    eqf = jnp.where(eq, 1.0, 0.0)
    sel = jnp.where(gt, 1.0, jnp.where(eq & (prefix_excl(eqf) < need), 1.0, 0.0))
    slot = jnp.where(sel > 0.5, prefix_excl(sel), -1.0)
    slot_ref[0] = slot.astype(jnp.int32)


def _topk(aff, blk0, nk, cap):
    nbatch = aff.shape[0]
    return pl.pallas_call(
        functools.partial(_topk_kernel, nk=nk, cap=cap),
        grid=(nbatch,),
        in_specs=[pl.BlockSpec((1, nk, N_EXPERTS, ROW_BLK), lambda b: (b, blk0, 0, 0))],
        out_specs=pl.BlockSpec((1, nk, N_EXPERTS, ROW_BLK), lambda b: (b, 0, 0, 0)),
        out_shape=jax.ShapeDtypeStruct((nbatch, nk, N_EXPERTS, ROW_BLK), jnp.int32),
        compiler_params=_cp("arbitrary"),
        name="topk",
    )(aff)


def _expert_kernel(*refs, has_ctx):
    if has_ctx:
        (w0_ref, fits_ref, hl_ref, sl_ref, al_ref, hc_ref, sc_ref, ac_ref, wg_ref, wu_ref, wd_ref,
         yl_ref, yc_ref, wgb, wub, wdb, xs_acc, gs_acc, xc_all, gc_all) = refs
    else:
        (w0_ref, fits_ref, hl_ref, sl_ref, al_ref, wg_ref, wu_ref, wd_ref, yl_ref,
         wgb, wub, wdb, xs_acc, gs_acc) = refs
    e = pl.program_id(0)
    b = pl.program_id(1)

    @pl.when(b == 0)
    def _():
        wgb[...] = wg_ref[0, 0].astype(BF16)
        wub[...] = wu_ref[0, 0].astype(BF16)
        wdb[...] = wd_ref[0, 0].astype(BF16)

    def gather(h_ref, slot_ref, aff_ref, nk, cap):
        r_iota = lax.broadcasted_iota(jnp.int32, (cap, ROW_BLK), 0)
        xs = jnp.zeros((cap, D), F32)
        gs = jnp.zeros((cap, 1), F32)
        for k in range(nk):
            pm = slot_ref[0, k, pl.ds(e, 1), :] == r_iota
            xs = xs + jnp.dot(jnp.where(pm, 1.0, 0.0).astype(BF16), h_ref[0, ROW_BLK * k:ROW_BLK * k + ROW_BLK, :],
                              preferred_element_type=F32)
            gs = gs + jnp.sum(jnp.where(pm, aff_ref[0, k, pl.ds(e, 1), :], 0.0), axis=1, keepdims=True)
        return xs.astype(BF16), gs

    def ffn(xsb):
        gt = jnp.dot(xsb, wgb[...], preferred_element_type=F32)
        up = jnp.dot(xsb, wub[...], preferred_element_type=F32)
        return jnp.dot((_silu(gt) * up).astype(BF16), wdb[...], preferred_element_type=F32)

    @pl.when(fits_ref[b] == 1)
    def _():
        xs_acc[...] = jnp.zeros_like(xs_acc)
        gs_acc[...] = jnp.zeros_like(gs_acc)
        r_iota = lax.broadcasted_iota(jnp.int32, (MOE_WIN, ROW_BLK), 0)
        for k in range(LAT_BLKS):
            w0 = pl.multiple_of(w0_ref[(b * LAT_BLKS + k) * N_EXPERTS + e], 16)
            pm = (sl_ref[0, k, pl.ds(e, 1), :] - w0) == r_iota
            xs_acc[pl.ds(w0, MOE_WIN), :] += jnp.dot(jnp.where(pm, 1.0, 0.0).astype(BF16),
                                                     hl_ref[0, ROW_BLK * k:ROW_BLK * k + ROW_BLK, :],
                                                     preferred_element_type=F32)
            gs_acc[pl.ds(w0, MOE_WIN), :] += jnp.sum(jnp.where(pm, al_ref[0, k, pl.ds(e, 1), :], 0.0),
                                                     axis=1, keepdims=True)

    @pl.when(fits_ref[b] != 1)
    def _():
        xs, gs = gather(hl_ref, sl_ref, al_ref, LAT_BLKS, CAP_LAT)
        xs_acc[...] = xs.astype(F32)
        gs_acc[...] = gs

    yl_ref[0, 0] = (ffn(xs_acc[...].astype(BF16)) * gs_acc[...]).astype(yl_ref.dtype)
    if has_ctx:
        xs, gs = gather(hc_ref, sc_ref, ac_ref, 1, CAP_CTX)
        row = pl.multiple_of(b * CAP_CTX, CAP_CTX)
        xc_all[pl.ds(row, CAP_CTX), :] = xs
        gc_all[pl.ds(row, CAP_CTX), :] = gs

        @pl.when(b == pl.num_programs(1) - 1)
        def _():
            yc_ref[0] = (ffn(xc_all[...]) * gc_all[...]).astype(yc_ref.dtype)


def _experts(win0, fits, h2, slot_lat, slot_ctx, aff, w_gate, w_up, w_down, l, has_ctx):
    nbatch = h2.shape[0]
    idx4 = lambda e, b: (b, 0, 0, 0)
    smem = pl.BlockSpec(memory_space=pltpu.SMEM)
    in_specs = [smem, smem,
                pl.BlockSpec((1, T, D), lambda e, b: (b, 0, 0)),
                pl.BlockSpec((1, LAT_BLKS, N_EXPERTS, ROW_BLK), idx4),
                pl.BlockSpec((1, LAT_BLKS, N_EXPERTS, ROW_BLK), idx4)]
    args = [win0, fits, h2, slot_lat, aff]
    out_specs = [pl.BlockSpec((1, 1, CAP_LAT, D), lambda e, b: (b, e, 0, 0))]
    out_shape = [jax.ShapeDtypeStruct((nbatch, N_EXPERTS, CAP_LAT, D), BF16)]
    if has_ctx:
        in_specs += [pl.BlockSpec((1, LC, D), lambda e, b: (b, LAT_BLKS, 0)),
                     pl.BlockSpec((1, 1, N_EXPERTS, ROW_BLK), idx4),
                     pl.BlockSpec((1, 1, N_EXPERTS, ROW_BLK), lambda e, b: (b, LAT_BLKS, 0, 0))]
        args += [h2, slot_ctx, aff]
        out_specs.append(pl.BlockSpec((1, nbatch * CAP_CTX, D), lambda e, b: (e, 0, 0)))
        out_shape.append(jax.ShapeDtypeStruct((N_EXPERTS, nbatch * CAP_CTX, D), BF16))
    wspec = pl.BlockSpec((1, 1, D, D), lambda e, b: (l, e, 0, 0))
    in_specs += [wspec, wspec, wspec]
    args += [w_gate, w_up, w_down]
    scratch = [pltpu.VMEM((D, D), BF16)] * 3 + [pltpu.VMEM((CAP_LAT, D), F32), pltpu.VMEM((CAP_LAT, 1), F32)]
    if has_ctx:
        scratch += [pltpu.VMEM((nbatch * CAP_CTX, D), BF16), pltpu.VMEM((nbatch * CAP_CTX, 1), F32)]
    return pl.pallas_call(
        functools.partial(_expert_kernel, has_ctx=has_ctx),
        grid=(N_EXPERTS, nbatch),
        in_specs=in_specs,
        out_specs=out_specs,
        out_shape=out_shape,
        scratch_shapes=scratch,
        compiler_params=_cp("arbitrary", "arbitrary"),
        name="experts",
    )(*args)


def _combine_kernel(*refs, has_ctx):
    if has_ctx:
        w0_ref, fits_ref, x1_ref, yl_ref, scl_ref, yc_ref, scc_ref, gate2_ref, g_ref, o_ref, ycat = refs
    else:
        w0_ref, fits_ref, x1_ref, yl_ref, scl_ref, gate2_ref, g_ref, o_ref, ycat = refs
    b = pl.program_id(0)
    j = pl.program_id(1)

    def finish(acc):
        o_ref[0] = x1_ref[0] + gate2_ref[0] * _rms(acc, g_ref[...])

    def comb(sc_ref, y_of, cap):
        lane = lax.broadcasted_iota(jnp.int32, (ROW_BLK, cap), 1)
        sc = sc_ref[0]
        acc = jnp.zeros((ROW_BLK, D), F32)
        for e in range(N_EXPERTS):
            pt = jnp.where(sc[:, e:e + 1] == lane, 1.0, 0.0).astype(BF16)
            acc = acc + jnp.dot(pt, y_of(e), preferred_element_type=F32)
        finish(acc)

    def comb_windowed():
        lane = lax.broadcasted_iota(jnp.int32, (ROW_BLK, ROW_BLK), 1)
        sc = scl_ref[0]
        per = ROW_BLK // MOE_WIN
        if per * MOE_WIN < ROW_BLK:
            ycat[:, per * MOE_WIN:, :] = jnp.zeros((2, ROW_BLK - per * MOE_WIN, D), BF16)
        acc = jnp.zeros((ROW_BLK, D), F32)
        for gi, e0 in enumerate(range(0, N_EXPERTS, per)):
            hit = None
            for q, e in enumerate(range(e0, min(e0 + per, N_EXPERTS))):
                w0 = pl.multiple_of(w0_ref[(b * LAT_BLKS + j) * N_EXPERTS + e], 16)
                d = sc[:, e:e + 1] - w0
                d = jnp.where(jnp.logical_and(d >= 0, d < MOE_WIN), d + MOE_WIN * q, -1)
                hit = (d == lane) if hit is None else jnp.logical_or(hit, d == lane)
                ycat[gi % 2, MOE_WIN * q:MOE_WIN * q + MOE_WIN, :] = yl_ref[0, e, pl.ds(w0, MOE_WIN), :]
            acc = acc + jnp.dot(jnp.where(hit, 1.0, 0.0).astype(BF16), ycat[gi % 2], preferred_element_type=F32)
        finish(acc)

    lat_y = lambda e: yl_ref[0, e]
    fits = fits_ref[b] == 1

    @pl.when(jnp.logical_and(j < LAT_BLKS, fits))
    def _():
        comb_windowed()

    @pl.when(jnp.logical_and(j < LAT_BLKS, jnp.logical_not(fits)))
    def _():
        comb(scl_ref, lat_y, CAP_LAT)

    if has_ctx:
        @pl.when(j == LAT_BLKS)
        def _():
            comb(scc_ref, lambda e: yc_ref[e], CAP_CTX)


def _combine(win0, fits, x1, y_lat, scol_lat, y_ctx, scol_ctx, mod3, g, has_ctx, l):
    nbatch = x1.shape[0]
    nb = NBLK if has_ctx else LAT_BLKS
    ntok = N if has_ctx else T
    smem = pl.BlockSpec(memory_space=pltpu.SMEM)
    in_specs = [smem, smem,
                pl.BlockSpec((1, ROW_BLK, D), lambda b, j: (b, j, 0)),
                pl.BlockSpec((1, N_EXPERTS, CAP_LAT, D), lambda b, j: (b, 0, 0, 0)),
                pl.BlockSpec((1, ROW_BLK, N_EXPERTS), lambda b, j: (b, jnp.minimum(j, LAT_BLKS - 1), 0))]
    args = [win0, fits, x1, y_lat, scol_lat]
    if has_ctx:
        in_specs += [pl.BlockSpec((N_EXPERTS, CAP_CTX, D), lambda b, j: (0, b, 0)),
                     pl.BlockSpec((1, ROW_BLK, N_EXPERTS), lambda b, j: (b, 0, 0))]
        args += [y_ctx, scol_ctx]
    in_specs += [pl.BlockSpec((1, 1, D), lambda b, j: (jnp.where(j == LAT_BLKS, nbatch, b), 0, 5)),
                 _lspec(g, l)]
    args += [mod3, g]
    return pl.pallas_call(
        functools.partial(_combine_kernel, has_ctx=has_ctx),
        grid=(nbatch, nb),
        in_specs=in_specs,
        out_specs=pl.BlockSpec((1, ROW_BLK, D), lambda b, j: (b, j, 0)),
        out_shape=jax.ShapeDtypeStruct((nbatch, ntok, D), F32),
        scratch_shapes=[pltpu.VMEM((2, ROW_BLK, D), BF16)],
        compiler_params=_cp("arbitrary", "arbitrary"),
        name="combine",
    )(*args)


def _rope_tables():
    f = 1.0 / (ROPE_THETA ** (jnp.arange(0, 16, 2, dtype=F32) / 16))
    pos = jnp.arange(T)
    row, col = pos // GRID_W, pos % GRID_W
    ar = row.astype(F32)[:, None] * f[None, :]
    ac = col.astype(F32)[:, None] * f[None, :]
    cos32 = jnp.concatenate([jnp.cos(ar), jnp.cos(ar), jnp.cos(ac), jnp.cos(ac)], axis=-1)
    sin32 = jnp.concatenate([jnp.sin(ar), jnp.sin(ar), jnp.sin(ac), jnp.sin(ac)], axis=-1)
    cos32 = jnp.concatenate([cos32, jnp.ones((LC, 32), F32)], axis=0)
    sin32 = jnp.concatenate([sin32, jnp.zeros((LC, 32), F32)], axis=0)
    return cos32, sin32


def _rot_cols(w):
    a, b, c, d = w[..., 0:8], w[..., 8:16], w[..., 16:24], w[..., 24:32]
    return jnp.concatenate([-b, a, -d, c], axis=-1)


def _mla_weights(w_uq, w_ukv):
    nl, r, _ = w_uq.shape
    wq3 = w_uq.reshape(nl, r, 4, 96)
    z32 = jnp.zeros((nl, r, 4, 32), F32)
    wq = jnp.concatenate([wq3, z32], axis=-1).reshape(nl, r, 512)
    wqr = jnp.concatenate([jnp.zeros((nl, r, 4, 64), F32), _rot_cols(wq3[..., 64:96]), z32], axis=-1).reshape(nl, r, 512)
    rk = w_ukv.shape[1]
    wkv3 = w_ukv.reshape(nl, rk, 4, 128)
    wk = jnp.concatenate([wkv3[..., :64], jnp.zeros((nl, rk, 4, 64), F32)], axis=-1).reshape(nl, rk, 512)
    z64 = jnp.zeros((nl, rk, 64), F32)
    vh = [wkv3[:, :, h, 64:] for h in range(4)]
    wv = jnp.concatenate([vh[0], z64, z64, vh[1], vh[2], z64, z64, vh[3]], axis=-1)
    vone = np.tile(np.repeat(np.array([0.0, 1.0, 1.0, 0.0], np.float32), 64), 2)[None, :]
    e = np.zeros((32, 512), np.float32)
    for h in range(4):
        e[np.arange(32), 128 * h + 64 + np.arange(32)] = 1.0
    return (wq.astype(BF16), wqr.astype(BF16), wk.astype(BF16), wv.astype(BF16), jnp.asarray(e, BF16),
            jnp.asarray(vone))


def _inproj_weights(w):
    o = np.cumsum([0, 256, 128, 32, 768, 256, 256, 768, 8, 4096])
    w = w.astype(BF16)
    seg = lambda i: w[:, :, o[i]:o[i + 1]]
    nl = w.shape[0]
    kr = seg(2)
    wm = jnp.concatenate([seg(0), seg(1), kr, _rot_cols(kr), jnp.zeros((nl, D, 64), BF16)], axis=-1)
    w2 = jnp.concatenate([seg(5), seg(6), seg(7), jnp.zeros((nl, D, 120), BF16)], axis=-1)
    return [wm, seg(3), seg(4), w2, 0.5 * seg(8)]


def kernel(x, c, ctx, c_ctx, w_ada, b_ada, g_pre_mix, g_post_mix, g_pre_ffn, g_post_ffn, w_in, mla_g_cq, mla_g_ckv, mla_w_uq, mla_w_ukv, na_rpb, s5_a_re, s5_a_im, s5_log_step, s5_b_re, s5_b_im, s5_c_re, s5_c_im, s5_d, s5_w_glu, s5_b_glu, m2_conv_w, m2_conv_b, m2_a_log, m2_dt_bias, m2_d, m2_g_norm, w_branch, w_out, w_router, w_gate, w_up, w_down):
    nbatch = x.shape[0]
    depth = w_ada.shape[0]
    vec = lambda v: v.reshape(depth, 1, -1).astype(F32)

    stream = (x, ctx, 0)
    cvec = jnp.concatenate([c, c_ctx[None, :], jnp.zeros((7, D), F32)], axis=0)
    cvec = cvec[: ((nbatch + 1 + 7) // 8) * 8]
    b_ada3 = b_ada.reshape(depth, 1, 6 * D)

    cos32, sin32 = _rope_tables()
    qscale = (64 + 32) ** -0.5 * math.log2(math.e)
    lane_is_rope = np.tile(np.concatenate([np.zeros(64, bool), np.ones(32, bool), np.zeros(32, bool)]), 4)
    pick = lambda t32, fill: jnp.where(lane_is_rope[None, :], jnp.tile(jnp.pad(t32, ((0, 0), (64, 32))), (1, 4)), fill)
    cos_q = pick(cos32, 1.0) * qscale
    sin_q = pick(sin32, 0.0) * qscale

    g_pre_mix, g_post_mix, g_pre_ffn, g_post_ffn = vec(g_pre_mix), vec(g_post_mix), vec(g_pre_ffn), vec(g_post_ffn)
    w_in_segs = _inproj_weights(w_in)
    wq, wqr, wk, wv, e_mat, vone = _mla_weights(mla_w_uq, mla_w_ukv)
    mla_g_cq, mla_g_ckv = vec(mla_g_cq), vec(mla_g_ckv)
    na_tabs = _na_tables(na_rpb)
    s5_bmat, s5_avec, s5_cmat = _s5_params(s5_a_re, s5_a_im, s5_log_step, s5_b_re, s5_b_im, s5_c_re, s5_c_im, nbatch)
    s5_d, s5_b_glu, s5_w_glu = vec(s5_d), vec(s5_b_glu), s5_w_glu.astype(BF16)
    m2_conv_w, m2_conv_b = m2_conv_w.astype(F32), vec(m2_conv_b)
    dt_bias = jnp.pad(m2_dt_bias.reshape(depth, 1, 8).astype(F32), ((0, 0), (0, 0), (0, 120)))
    a_col = jnp.pad(-jnp.exp(m2_a_log.astype(F32)).reshape(depth, 8, 1), ((0, 0), (0, 120), (0, 0)))
    m2_dvec, m2_g_norm = vec(jnp.repeat(m2_d, 64, axis=-1)), vec(m2_g_norm)
    w_branch, w_out = (0.5 * w_branch).astype(BF16), w_out.astype(BF16)
    w_router_t = jnp.swapaxes(w_router, 1, 2).astype(F32)

    for l in range(depth):
        has_ctx = l < depth - 1
        nb = NBLK if has_ctx else LAT_BLKS
        mod = _ada(cvec, w_ada, b_ada3, l)
        mod3 = mod.reshape(mod.shape[0], 1, 6 * D)

        p_mla, p_na, p_s5, p_m2, p_gate = _inproj(stream, g_pre_mix, mod3, w_in_segs, nbatch, l, has_ctx)

        q, k, v = _mla_prep(p_mla, mla_g_cq, mla_g_ckv, wq, wqr, wk, wv, e_mat, vone, cos_q, sin_q, cos32, sin32, l)
        o_mla = _mla_attn(q.reshape(nbatch, N, 512), k.reshape(nbatch, N, 512), v.reshape(nbatch, N, 512), has_ctx)

        o_na = _na_attn(p_na.reshape(nbatch, N, NA_W), na_tabs, has_ctx, l)

        yf, yb = _s5_scan(p_s5.reshape(nbatch, N, S5_W), s5_bmat, s5_avec, s5_cmat, l)
        o_s5 = _s5_glu(p_s5, yf.reshape(nbatch * N, S5_W), yb.reshape(nbatch * N, S5_W), s5_d, s5_w_glu, s5_b_glu, l)

        xbc, dtt, bmt = _m2_prep(p_m2.reshape(nbatch, N, M2_W), m2_conv_w, m2_conv_b, dt_bias, l)
        ssd_f, ssd_b = _ssd(xbc, dtt, bmt, a_col, l)
        o_m2 = _m2_norm(xbc.reshape(nbatch * N, 768), p_m2, ssd_f.reshape(nbatch * N, 256),
                        ssd_b.reshape(nbatch * N, 256), m2_dvec, m2_g_norm, l)

        x1, h2, aff = _merge(stream, o_mla.reshape(-1, 256), o_na.reshape(-1, 256), o_s5, o_m2, p_gate,
                             w_branch, w_out, g_post_mix, mod3, g_pre_ffn, w_router_t, nbatch, nb, l)

        slot_lat = _topk(aff, 0, LAT_BLKS, CAP_LAT)
        slot_ctx = _topk(aff, LAT_BLKS, 1, CAP_CTX) if has_ctx else None
        cnt = jnp.sum((slot_lat >= 0).astype(jnp.int32), axis=-1)
        first = jnp.cumsum(cnt, axis=1) - cnt
        win0 = jnp.minimum((first // 16) * 16, CAP_LAT - MOE_WIN)
        fits = jnp.all(first + cnt <= win0 + MOE_WIN, axis=(1, 2)).astype(jnp.int32)
        win0 = win0.reshape(-1)
        ys = _experts(win0, fits, h2.reshape(nbatch, nb * ROW_BLK, D), slot_lat, slot_ctx, aff, w_gate, w_up, w_down,
                      l, has_ctx)
        scol_lat = jnp.transpose(slot_lat, (0, 1, 3, 2)).reshape(nbatch, T, N_EXPERTS)
        scol_ctx = jnp.transpose(slot_ctx, (0, 1, 3, 2)).reshape(nbatch, LC, N_EXPERTS) if has_ctx else None
        xs = _combine(win0, fits, x1.reshape(nbatch, nb * ROW_BLK, D), ys[0], scol_lat, ys[1] if has_ctx else None,
                      scol_ctx, mod3, g_post_ffn, has_ctx, l)
        stream = (xs, xs, LAT_BLKS)
    return xs
```

```python
import functools
import math

import numpy as np
import jax
import jax.numpy as jnp
from jax import lax
from jax.experimental import pallas as pl
from jax.experimental.pallas import tpu as pltpu

F32 = jnp.float32
BF16 = jnp.bfloat16

D = 1024
T = 2048
LC = 256
N = T + LC
GRID_W = 64
ROW_BLK = 256
NBLK = N // ROW_BLK
LAT_BLKS = T // ROW_BLK
EPS = 1e-6
N_EXPERTS = 16
CAP_LAT = 2 * T // N_EXPERTS
CAP_CTX = 2 * LC // N_EXPERTS
ROPE_THETA = 10000.0
NEG = -1e30

MLA_W = 512
NA_W = 768
S5_W = 256
M2_W = 1152
GATE_W = 4096

MOE_WIN = 80
EXP_SB = 2
S5_CHUNK = 64
SSD_CHUNK = 128

VMEM_LIMIT = 56 * 1024 * 1024
VMEM_LIMIT_EXPERTS = 60 * 1024 * 1024


def _cp(*sem, vmem=VMEM_LIMIT):
    return pltpu.CompilerParams(dimension_semantics=sem, vmem_limit_bytes=vmem)


def _lspec(a, l, *lead):
    nd = a.ndim - 1 - len(lead)
    return pl.BlockSpec((None,) * (1 + len(lead)) + a.shape[1 + len(lead):], lambda *_: (l,) + lead + (0,) * nd)


def _dot(a, b):
    return jnp.dot(a.astype(BF16), b.astype(BF16), preferred_element_type=F32)


def _dot_nt(a, b):
    return lax.dot_general(a.astype(BF16), b.astype(BF16), (((1,), (1,)), ((), ())),
                           preferred_element_type=F32)


def _split3(a):
    hi = a.astype(BF16)
    r = a - hi.astype(F32)
    mid = r.astype(BF16)
    lo = (r - mid.astype(F32)).astype(BF16)
    return hi, mid, lo


def _dot_hi(a, b):
    ah, am, _ = _split3(a)
    bh, bm, _ = _split3(b)
    f = lambda x, y: jnp.dot(x, y, preferred_element_type=F32)
    return f(ah, bh) + (f(ah, bm) + f(am, bh))


def _dot_hi_nt(a, b):
    ah, am, _ = _split3(a)
    bh, bm, _ = _split3(b)
    f = lambda x, y: lax.dot_general(x, y, (((1,), (1,)), ((), ())), preferred_element_type=F32)
    return f(ah, bh) + (f(ah, bm) + f(am, bh))


def _dot_exact_lhs(m_bf16, a):
    h, m, l = _split3(a)
    f = lambda y: jnp.dot(m_bf16, y, preferred_element_type=F32)
    return f(h) + (f(m) + f(l))


def _dot_exact_rhs(a, m_bf16):
    h, m, l = _split3(a)
    f = lambda y: jnp.dot(y, m_bf16, preferred_element_type=F32)
    return f(h) + (f(m) + f(l))


def _sigmoid(x):
    return 0.5 * jnp.tanh(0.5 * x) + 0.5


def _silu(x):
    return x * _sigmoid(x)


def _rms(x, g):
    return x * lax.rsqrt(jnp.mean(x * x, axis=-1, keepdims=True) + EPS) * g


def _mod_row(nb):
    def f(i, nbatch):
        return jnp.where(i % nb == LAT_BLKS, nbatch, i // nb)
    return f


def _flat_blk(i, nb):
    return (i // nb) * NBLK + i % nb


def _stream_specs(stream, nb):
    lat, ctxa, cblk = stream
    return [lat, ctxa], [pl.BlockSpec((1, ROW_BLK, D), lambda i: (i // nb, jnp.minimum(i % nb, LAT_BLKS - 1), 0)),
                         pl.BlockSpec((1, ROW_BLK, D), lambda i: (i // nb, cblk, 0))]


def _stream_block(xl_ref, xc_ref, nb):
    return jnp.where(pl.program_id(0) % nb == LAT_BLKS, xc_ref[0], xl_ref[0])


def _ada_kernel(c_ref, w_ref, b_ref, o_ref):
    c = c_ref[...]
    o_ref[...] = _dot_hi(_silu(c), w_ref[0]) + b_ref[0]


def _ada(cvec, w_ada, b_ada3, l):
    rows = cvec.shape[0]
    return pl.pallas_call(
        _ada_kernel,
        grid=(6,),
        in_specs=[pl.BlockSpec((rows, D), lambda k: (0, 0)),
                  pl.BlockSpec((1, D, D), lambda k: (l, 0, k)),
                  pl.BlockSpec((1, 1, D), lambda k: (l, 0, k))],
        out_specs=pl.BlockSpec((rows, D), lambda k: (0, k)),
        out_shape=jax.ShapeDtypeStruct((rows, 6 * D), F32),
        compiler_params=_cp("arbitrary"),
        name="ada",
    )(cvec, w_ada, b_ada3)


def _inproj_kernel(xl_ref, xc_ref, g_ref, sh_ref, sc_ref, wm, wn, ws, w2, wg, om, on, os_, o2, og, *, ctx_gates):
    x = _stream_block(xl_ref, xc_ref, NBLK)
    h = (_rms(x, g_ref[...]) * (1.0 + sc_ref[0]) + sh_ref[0]).astype(BF16)
    om[...] = jnp.dot(h, wm[...], preferred_element_type=F32)
    on[...] = jnp.dot(h, wn[...], preferred_element_type=F32).astype(BF16)
    os_[...] = jnp.dot(h, ws[...], preferred_element_type=F32)
    o2[...] = jnp.dot(h, w2[...], preferred_element_type=F32)
    if ctx_gates:
        og[...] = jnp.dot(h, wg[...], preferred_element_type=F32)
    else:
        is_ctx = pl.program_id(0) % NBLK == LAT_BLKS

        @pl.when(jnp.logical_not(is_ctx))
        def _():
            og[...] = jnp.dot(h, wg[...], preferred_element_type=F32)

        @pl.when(is_ctx)
        def _():
            og[...] = jnp.zeros_like(og)


def _inproj(stream, g, mod3, ws, nbatch, l, ctx_gates):
    rows = nbatch * N
    nblk = rows // ROW_BLK
    mrow = _mod_row(NBLK)
    xargs, xspecs = _stream_specs(stream, NBLK)
    full = lambda w: _lspec(w, l)
    widths = (MLA_W, NA_W, S5_W, M2_W, GATE_W)
    dts = (F32, BF16, F32, F32, F32)
    return pl.pallas_call(
        functools.partial(_inproj_kernel, ctx_gates=ctx_gates),
        grid=(nblk,),
        in_specs=xspecs + [
                  full(g),
                  pl.BlockSpec((1, 1, D), lambda i: (mrow(i, nbatch), 0, 0)),
                  pl.BlockSpec((1, 1, D), lambda i: (mrow(i, nbatch), 0, 1))]
                 + [full(w) for w in ws],
        out_specs=[pl.BlockSpec((ROW_BLK, w), lambda i: (i, 0)) for w in widths],
        out_shape=[jax.ShapeDtypeStruct((rows, w), dt) for w, dt in zip(widths, dts)],
        compiler_params=_cp("arbitrary"),
        name="inproj",
    )(*xargs, g, mod3, mod3, *ws)


def _mla_prep_kernel(p_ref, gq_ref, gkv_ref, wq_ref, wqr_ref, wk_ref, wv_ref, e_ref, vone_ref,
                     cos_ref, sin_ref, ck_ref, sk_ref, q_out, k_out, v_out):
    p = p_ref[...]
    cqn = _rms(p[:, :256], gq_ref[...]).astype(BF16)
    q = jnp.dot(cqn, wq_ref[...], preferred_element_type=F32)
    qr = jnp.dot(cqn, wqr_ref[...], preferred_element_type=F32)
    q_out[...] = (q * cos_ref[...] + qr * sin_ref[...]).astype(BF16)
    ckvn = _rms(p[:, 256:384], gkv_ref[...]).astype(BF16)
    kro = (p[:, 384:416] * ck_ref[...] + p[:, 416:448] * sk_ref[...]).astype(BF16)
    k = jnp.dot(ckvn, wk_ref[...], preferred_element_type=F32) + jnp.dot(kro, e_ref[...], preferred_element_type=F32)
    k_out[...] = k.astype(BF16)
    v_out[...] = (jnp.dot(ckvn, wv_ref[...], preferred_element_type=F32) + vone_ref[...]).astype(BF16)


def _mla_prep(p_mla, gq, gkv, wq, wqr, wk, wv, e, vone, cos_q, sin_q, cos_k, sin_k, l):
    rows = p_mla.shape[0]
    full = lambda w: _lspec(w, l)
    const = lambda w: pl.BlockSpec(w.shape, lambda j, b: (0, 0))
    row = lambda j, b: (b * NBLK + j, 0)
    tab = lambda w: pl.BlockSpec((ROW_BLK, w), lambda j, b: (j, 0))
    return pl.pallas_call(
        _mla_prep_kernel,
        grid=(NBLK, rows // N),
        in_specs=[pl.BlockSpec((ROW_BLK, MLA_W), row),
                  full(gq), full(gkv), full(wq), full(wqr), full(wk), full(wv), const(e), const(vone),
                  tab(512), tab(512), tab(32), tab(32)],
        out_specs=[pl.BlockSpec((ROW_BLK, 512), row)] * 3,
        out_shape=[jax.ShapeDtypeStruct((rows, 512), BF16)] * 3,
        compiler_params=_cp("arbitrary", "arbitrary"),
        name="mla_prep",
    )(p_mla, gq, gkv, wq, wqr, wk, wv, e, vone, cos_q, sin_q, cos_k, sin_k)


def _mla_attn_kernel(q_ref, k_ref, v_ref, o_ref, *, has_ctx):
    lane = lax.broadcasted_iota(jnp.int32, (q_ref.shape[1], 128), 1)

    def run(k0, nk):
        for hp in range(2):
            pv = []
            for h in (2 * hp, 2 * hp + 1):
                qh = q_ref[0, :, 128 * h:128 * h + 128]
                kh = k_ref[0, k0:k0 + nk, 128 * h:128 * h + 128]
                s = _dot_nt(qh, kh)
                p = jnp.exp2(s - jnp.max(s, axis=-1, keepdims=True)).astype(BF16)
                pv.append(jnp.dot(p, v_ref[0, k0:k0 + nk, 128 * h:128 * h + 128], preferred_element_type=F32))
            oa, ob = pv
            o = jnp.where(lane < 64, oa * (1.0 / oa[:, 64:65]), ob * (1.0 / ob[:, 0:1]))
            o_ref[0, :, 128 * hp:128 * hp + 128] = o.astype(o_ref.dtype)

    if has_ctx:
        j = pl.program_id(1)

        @pl.when(j < LAT_BLKS)
        def _():
            run(0, N)

        @pl.when(j == LAT_BLKS)
        def _():
            run(T, LC)
    else:
        run(0, N)


def _mla_attn(q, k, v, has_ctx):
    nbatch = q.shape[0]
    nb = NBLK if has_ctx else LAT_BLKS
    qb = ROW_BLK if has_ctx else 2 * ROW_BLK
    return pl.pallas_call(
        functools.partial(_mla_attn_kernel, has_ctx=has_ctx),
        grid=(nbatch, nb * ROW_BLK // qb),
        in_specs=[pl.BlockSpec((1, qb, 512), lambda b, j: (b, j, 0)),
                  pl.BlockSpec((1, N, 512), lambda b, j: (b, 0, 0)),
                  pl.BlockSpec((1, N, 512), lambda b, j: (b, 0, 0))],
        out_specs=pl.BlockSpec((1, qb, 256), lambda b, j: (b, j, 0)),
        out_shape=jax.ShapeDtypeStruct((nbatch, nb * ROW_BLK, 256), BF16),
        compiler_params=_cp("arbitrary", "arbitrary"),
        name="mla_attn",
    )(q, k, v)


def _na_kernel(q_ref, kc_ref, k0_ref, k1_ref, k2_ref, vc_ref, v0_ref, v1_ref, v2_ref, b_ref, o_ref, *, has_ctx):
    scale = jnp.asarray(0.125, BF16)

    lane = lax.broadcasted_iota(jnp.int32, (ROW_BLK, 128), 1)
    lane64 = lax.broadcasted_iota(jnp.int32, (GRID_W, 128), 1) < 64
    grp = jnp.minimum(pl.program_id(0), LAT_BLKS - 1)
    krow0 = 4 * jnp.clip(grp - 1, 0, LAT_BLKS - 3)

    def bias_piece(h, i):
        rows = []
        for qr in range(4):
            r = 4 * grp + qr
            rs = jnp.clip(r - 4, 0, 24)
            tiles = []
            for pr in range(2):
                kr = krow0 + (4 * i + 2 * pr)
                t = b_ref[h, jnp.clip(kr - r + 8, 0, 15)]
                ok_a = jnp.logical_and(kr >= rs, kr < rs + 8).astype(jnp.int32)
                ok_b = jnp.logical_and(kr + 1 >= rs, kr + 1 < rs + 8).astype(jnp.int32)
                tiles.append(jnp.where(jnp.where(lane64, ok_a, ok_b) > 0, t, NEG))
            rows.append(jnp.concatenate(tiles, axis=1))
        return jnp.concatenate(rows, axis=0)

    def heads(win):
        kws = (k0_ref, k1_ref, k2_ref)
        vws = (v0_ref, v1_ref, v2_ref)
        for hp in range(2):
            sl = slice(128 * hp, 128 * hp + 128)
            qp = q_ref[0, :, sl] * scale
            outs = []
            for hh in range(2):
                h = 2 * hp + hh
                qh = jnp.where((lane < 64) if hh == 0 else (lane >= 64), qp, jnp.zeros_like(qp))
                s_c = _dot_nt(qh, kc_ref[0, :, sl])
                s_w = []
                smax = s_c
                if win:
                    for i in range(3):
                        s = _dot_nt(qh, kws[i][0, :, sl]) + bias_piece(h, i)
                        s_w.append(s)
                        smax = jnp.maximum(smax, s)
                m = jnp.max(smax, axis=-1, keepdims=True)
                p = jnp.exp(s_c - m)
                psum = p
                o = _dot(p, vc_ref[0, :, sl])
                for i, s in enumerate(s_w):
                    p = jnp.exp(s - m)
                    psum = psum + p
                    o = o + _dot(p, vws[i][0, :, sl])
                outs.append(o * (1.0 / jnp.sum(psum, axis=-1, keepdims=True)))
            o_ref[0, :, sl] = jnp.where(lane < 64, outs[0], outs[1]).astype(o_ref.dtype)

    if has_ctx:
        g = pl.program_id(0)

        @pl.when(g < LAT_BLKS)
        def _():
            heads(True)

        @pl.when(g == LAT_BLKS)
        def _():
            heads(False)
    else:
        heads(True)


def _na_attn(qkv, bias, has_ctx, l):
    nbatch = qkv.shape[0]
    ng = NBLK if has_ctx else LAT_BLKS
    j0 = lambda g: jnp.clip(g - 1, 0, LAT_BLKS - 3)
    blk = lambda f: pl.BlockSpec((1, ROW_BLK, 256), f)
    return pl.pallas_call(
        functools.partial(_na_kernel, has_ctx=has_ctx),
        grid=(ng, nbatch),
        in_specs=[blk(lambda g, b: (b, g, 0)),
                  blk(lambda g, b: (b, LAT_BLKS, 1)),
                  blk(lambda g, b: (b, j0(g), 1)),
                  blk(lambda g, b: (b, j0(g) + 1, 1)),
                  blk(lambda g, b: (b, j0(g) + 2, 1)),
                  blk(lambda g, b: (b, LAT_BLKS, 2)),
                  blk(lambda g, b: (b, j0(g), 2)),
                  blk(lambda g, b: (b, j0(g) + 1, 2)),
                  blk(lambda g, b: (b, j0(g) + 2, 2)),
                  _lspec(bias, l)],
        out_specs=blk(lambda g, b: (b, g, 0)),
        out_shape=jax.ShapeDtypeStruct((nbatch, ng * ROW_BLK, 256), BF16),
        compiler_params=_cp("arbitrary", "arbitrary"),
        name="na_attn",
    )(qkv, qkv, qkv, qkv, qkv, qkv, qkv, qkv, qkv, bias)


def _na_col_structure():
    onehot = np.zeros((31, 64, 64), np.float32)
    colmask = np.zeros((64, 64), bool)
    for c in range(64):
        cs = min(max(c - 8, 0), 48)
        for kc in range(cs, cs + 16):
            onehot[kc - c + 15, c, kc] = 1.0
            colmask[c, kc] = True
    return np.tile(onehot.reshape(31, 4096), (3, 1)), colmask


_NA_COL_ONEHOT, _NA_COLMASK = _na_col_structure()


def _na_tables(rpb):
    nl = rpb.shape[0]
    r = rpb.astype(F32)
    hi = lax.reduce_precision(r, 8, 7)
    mid = lax.reduce_precision(r - hi, 8, 7)
    parts = jnp.stack([hi, mid, r - hi - mid], axis=3).reshape(nl * 60, 93)
    tz = jnp.dot(parts, _NA_COL_ONEHOT, preferred_element_type=F32).reshape(nl, 4, 15, 64, 64)
    tz = jnp.where(_NA_COLMASK, tz, NEG)
    edge = jnp.full((nl, 4, 1, 64, 64), NEG, F32)
    tz17 = jnp.concatenate([edge, tz, edge], axis=2)
    return jnp.concatenate([tz17[:, :, :16], tz17[:, :, 1:]], axis=-1)


def _s5_kernel(uf_ref, ub_ref, bf_ref, bb_ref, af_ref, ab_ref, cf_ref, cb_ref, yf_ref, yb_ref,
               hf, hb, buf_f, buf_b, tm_f, tm_b):
    i = pl.program_id(0)
    half = 1024
    nb = uf_ref.shape[0]

    @pl.when(i == 0)
    def _():
        hf[...] = jnp.zeros_like(hf)
        hb[...] = jnp.zeros_like(hb)

    def expand(u_ref, tm, b_ref, buf):
        for b in range(nb):
            for c in range(2):
                tm[c, pl.ds(b, S5_CHUNK, stride=nb), :] = u_ref[b, :, 128 * c:128 * c + 128]
        buf[...] = _dot(jnp.concatenate([tm[0], tm[1]], axis=1), b_ref[...])

    def scan(buf, a_ref, h, order):
        hr, hi = h[:, :half], h[:, half:]
        ar, ai = a_ref[:, :half], a_ref[:, half:]
        for k in order:
            rows = slice(k * nb, (k + 1) * nb)
            nr = ar * hr - ai * hi + buf[rows, :half]
            ni = ar * hi + ai * hr + buf[rows, half:]
            buf[rows, :half] = nr
            buf[rows, half:] = ni
            hr, hi = nr, ni
        h[:, :half] = hr
        h[:, half:] = hi

    def readout(buf, c_ref, tm, y_ref):
        y = _dot(buf[...], c_ref[...])
        for c in range(2):
            tm[c] = y[:, 128 * c:128 * c + 128]
        for b in range(nb):
            for c in range(2):
                y_ref[b, :, 128 * c:128 * c + 128] = tm[c, pl.ds(b, S5_CHUNK, stride=nb), :]

    expand(uf_ref, tm_f, bf_ref, buf_f)
    expand(ub_ref, tm_b, bb_ref, buf_b)
    scan(buf_f, af_ref, hf, range(S5_CHUNK))
    readout(buf_f, cf_ref, tm_f, yf_ref)
    scan(buf_b, ab_ref, hb, range(S5_CHUNK - 1, -1, -1))
    readout(buf_b, cb_ref, tm_b, yb_ref)


def _s5_scan(u, bmat, avec, cmat, l):
    nbatch = u.shape[0]
    cr = S5_CHUNK * nbatch
    nch = N // S5_CHUNK
    nctx = LC // S5_CHUNK
    fidx = lambda i: jnp.where(i < nctx, nch - nctx + i, i - nctx)
    bidx = lambda i: nch - 1 - i
    ublk = lambda f: pl.BlockSpec((nbatch, S5_CHUNK, 256), lambda i: (0, f(i), 0))
    return pl.pallas_call(
        _s5_kernel,
        grid=(nch,),
        in_specs=[ublk(fidx), ublk(bidx),
                  _lspec(bmat, l, 0), _lspec(bmat, l, 1), _lspec(avec, l, 0), _lspec(avec, l, 1),
                  _lspec(cmat, l, 0), _lspec(cmat, l, 1)],
        out_specs=[ublk(fidx), ublk(bidx)],
        out_shape=[jax.ShapeDtypeStruct((nbatch, N, 256), F32)] * 2,
        scratch_shapes=[pltpu.VMEM((nbatch, 2048), F32), pltpu.VMEM((nbatch, 2048), F32),
                        pltpu.VMEM((cr, 2048), F32), pltpu.VMEM((cr, 2048), F32),
                        pltpu.VMEM((2, cr, 128), F32), pltpu.VMEM((2, cr, 128), F32)],
        compiler_params=_cp("arbitrary"),
        name="s5_scan",
    )(u, u, bmat, bmat, avec, avec, cmat, cmat)


def _s5_glu_kernel(u_ref, yf_ref, yb_ref, d_ref, w_ref, b_ref, o_ref):
    y = d_ref[...] * u_ref[...] + yf_ref[...] + yb_ref[...]
    z = y * (0.5 * (1.0 + jnp.tanh(math.sqrt(2.0 / math.pi) * (y + 0.044715 * (y * y * y)))))
    o_ref[...] = (z * _sigmoid(_dot(z, w_ref[...]) + b_ref[...])).astype(o_ref.dtype)


def _s5_glu(u_tb, yf, yb, d, w, b, l):
    rows = u_tb.shape[0]
    rb = 4 * ROW_BLK
    blk = pl.BlockSpec((rb, 256), lambda i: (i, 0))
    full = lambda a: _lspec(a, l)
    return pl.pallas_call(
        _s5_glu_kernel,
        grid=(rows // rb,),
        in_specs=[blk, blk, blk, full(d), full(w), full(b)],
        out_specs=blk,
        out_shape=jax.ShapeDtypeStruct((rows, 256), BF16),
        compiler_params=_cp("arbitrary"),
        name="s5_glu",
    )(u_tb, yf, yb, d, w, b)


def _s5_params(a_re, a_im, log_step, b_re, b_im, c_re, c_im, nbatch):
    nl = a_re.shape[0]
    same_group = (np.arange(256)[:, None] // 16 == np.arange(1024)[None, :] // 64).astype(np.float32)
    a = lax.complex(a_re.astype(F32), a_im.astype(F32))
    abar = jnp.exp(jnp.exp(log_step.astype(F32))[..., None] * a)
    bbar = ((abar - 1.0) / a)[..., None] * lax.complex(b_re.astype(F32), b_im.astype(F32))
    blk_in = lambda m: jnp.tile(jnp.swapaxes(m, -1, -2).reshape(nl, 2, 256, 64), (1, 1, 1, 16)) * same_group
    bmat = jnp.concatenate([blk_in(jnp.real(bbar)), blk_in(jnp.imag(bbar))], axis=-1).astype(BF16)
    avec = jnp.concatenate([jnp.real(abar).reshape(nl, 2, 1, 1024), jnp.imag(abar).reshape(nl, 2, 1, 1024)], axis=-1)
    avec = jnp.broadcast_to(avec, (nl, 2, nbatch, 2048))
    blk_out = lambda m: jnp.tile(jnp.swapaxes(m, -1, -2).reshape(nl, 2, 1024, 16), (1, 1, 1, 16)) * same_group.T
    cmat = jnp.concatenate([blk_out(c_re.astype(F32)), -blk_out(c_im.astype(F32))], axis=-2).astype(BF16)
    return bmat, avec, cmat


def _m2_prep_kernel(x_ref, dt_ref, w_ref, b_ref, dtb_ref, xo_ref, dtt_ref, bmt_ref):
    c = pl.program_id(1)
    x = x_ref[0]
    t = lax.broadcasted_iota(jnp.int32, x.shape, 0)
    m2 = ((t >= 2) & (t < T)) | (t >= T + 2)
    m1 = ((t >= 1) & (t < T)) | (t >= T + 1)
    p1 = (t <= T - 2) | ((t >= T) & (t <= N - 2))
    w = w_ref[...]
    y = (w[0:1] * jnp.where(m2, pltpu.roll(x, 2, 0), 0.0)
         + w[1:2] * jnp.where(m1, pltpu.roll(x, 1, 0), 0.0)
         + w[2:3] * x
         + w[3:4] * jnp.where(p1, pltpu.roll(x, N - 1, 0), 0.0)) + b_ref[...]
    act = _silu(y)
    xo_ref[0] = act

    @pl.when(c == 0)
    def _():
        v = dt_ref[0] + dtb_ref[...]
        dtt_ref[0] = (jnp.maximum(v, 0.0) + jnp.log1p(jnp.exp(-jnp.abs(v)))).T

    @pl.when(c == 1)
    def _():
        bmt_ref[0] = act.T.astype(BF16)


def _m2_prep(p_m2, conv_w, conv_b, dtb, l):
    nbatch = p_m2.shape[0]
    return pl.pallas_call(
        _m2_prep_kernel,
        grid=(nbatch, 3),
        in_specs=[pl.BlockSpec((1, N, 256), lambda b, c: (b, 0, 1 + c)),
                  pl.BlockSpec((1, N, 128), lambda b, c: (b, 0, 8)),
                  pl.BlockSpec((None, 4, 256), lambda b, c: (l, 0, c)),
                  pl.BlockSpec((None, 1, 256), lambda b, c: (l, 0, c)),
                  _lspec(dtb, l)],
        out_specs=[pl.BlockSpec((1, N, 256), lambda b, c: (b, 0, c)),
                   pl.BlockSpec((1, 128, N), lambda b, c: (b, 0, 0)),
                   pl.BlockSpec((1, 256, N), lambda b, c: (b, 0, 0))],
        out_shape=[jax.ShapeDtypeStruct((nbatch, N, 768), F32),
                   jax.ShapeDtypeStruct((nbatch, 128, N), F32),
                   jax.ShapeDtypeStruct((nbatch, 256, N), BF16)],
        compiler_params=_cp("arbitrary", "arbitrary"),
        name="m2_prep",
    )(p_m2, p_m2, conv_w, conv_b, dtb)


def _ssd_kernel(xf_ref, dttf_ref, bmtf_ref, xb_ref, dttb_ref, bmtb_ref, acol_ref, yf_ref, yb_ref, hs):
    i = pl.program_id(1)
    L = SSD_CHUNK

    @pl.when(i == 0)
    def _():
        hs[...] = jnp.zeros_like(hs)

    li = lax.broadcasted_iota(jnp.int32, (L, L), 0)
    si = lax.broadcasted_iota(jnp.int32, (L, L), 1)
    left = si < 64
    pick = lambda a, b: jnp.where(left, a, b)

    def direction(x_ref, dtt_ref, bmt_ref, y_ref, d, causal, causal_t, last):
        cz = jnp.where(causal, 1.0, 0.0).astype(BF16)
        czt = jnp.where(causal_t, 1.0, 0.0).astype(BF16)
        xbc = x_ref[0]
        dtt = dtt_ref[0]
        parts = jnp.concatenate([p.astype(F32) for p in _split3(dtt * acol_ref[...])], axis=1)
        cumt = jnp.dot(parts.astype(BF16), jnp.concatenate([czt, czt, czt], axis=0),
                       preferred_element_type=F32)
        reps = jnp.concatenate([jnp.broadcast_to(parts[4 * d + h:4 * d + h + 1, :], (L, 3 * L)) for h in range(4)],
                               axis=0).astype(BF16)
        cumb = lax.dot_general(jnp.concatenate([cz, cz, cz], axis=1), reps, (((1,), (1,)), ((), ())),
                               preferred_element_type=F32)
        for g in range(2):
            xp = xbc[:, 128 * g:128 * g + 128].astype(BF16)
            cm = xbc[:, 512 + 128 * g:640 + 128 * g].astype(BF16)
            bmt = bmt_ref[0, 128 * g:128 * g + 128, :]
            hprev = hs[d, g]
            go = jnp.dot(cm, jnp.concatenate([bmt, hprev.astype(BF16)], axis=1), preferred_element_type=F32)
            gmat, yo = go[:, :L], go[:, L:]
            bmf = bmt.astype(F32)
            lhs, cbs, cls = [], [], []
            for hh in range(2):
                h = 2 * g + hh
                cb = cumb[:, L * h:L * h + L]
                crow = cumt[4 * d + h:4 * d + h + 1, :]
                dtrow = dtt[4 * d + h:4 * d + h + 1, :]
                cl = cb[last:last + 1, :]
                lhs.append((gmat * jnp.where(causal, jnp.exp(cb - crow), 0.0) * dtrow).astype(BF16))
                lhs.append((bmf * (jnp.exp(cl - crow) * dtrow)).astype(BF16))
                cbs.append(cb)
                cls.append(cl)
            big = jnp.dot(jnp.concatenate(lhs, axis=0), xp, preferred_element_type=F32)
            y_ref[0, :, 128 * g:128 * g + 128] = pick(big[0:L], big[2 * L:3 * L]) + yo * jnp.exp(pick(cbs[0], cbs[1]))
            hs[d, g] = hprev * jnp.exp(pick(cls[0], cls[1])) + pick(big[L:2 * L], big[3 * L:4 * L])

    direction(xf_ref, dttf_ref, bmtf_ref, yf_ref, 0, si <= li, li <= si, L - 1)
    direction(xb_ref, dttb_ref, bmtb_ref, yb_ref, 1, si >= li, li >= si, 0)


def _ssd(xbc, dtt, bmt, a_col, l):
    nbatch = xbc.shape[0]
    nch = N // SSD_CHUNK
    nctx = LC // SSD_CHUNK
    fidx = lambda i: jnp.where(i < nctx, nch - nctx + i, i - nctx)
    bidx = lambda i: nch - 1 - i
    ins = lambda f: [pl.BlockSpec((1, SSD_CHUNK, 768), lambda b, i: (b, f(i), 0)),
                     pl.BlockSpec((1, 128, SSD_CHUNK), lambda b, i: (b, 0, f(i))),
                     pl.BlockSpec((1, 256, SSD_CHUNK), lambda b, i: (b, 0, f(i)))]
    return pl.pallas_call(
        _ssd_kernel,
        grid=(nbatch, nch),
        in_specs=ins(fidx) + ins(bidx) + [_lspec(a_col, l)],
        out_specs=[pl.BlockSpec((1, SSD_CHUNK, 256), lambda b, i: (b, fidx(i), 0)),
                   pl.BlockSpec((1, SSD_CHUNK, 256), lambda b, i: (b, bidx(i), 0))],
        out_shape=[jax.ShapeDtypeStruct((nbatch, N, 256), F32)] * 2,
        scratch_shapes=[pltpu.VMEM((2, 2, 128, 128), F32)],
        compiler_params=_cp("arbitrary", "arbitrary"),
        name="ssd",
    )(xbc, dtt, bmt, xbc, dtt, bmt, a_col)


def _m2_norm_kernel(x_ref, z_ref, yf_ref, yb_ref, d_ref, g_ref, o_ref):
    y = d_ref[...] * x_ref[...] + yf_ref[...] + yb_ref[...]
    o_ref[...] = _rms(y * _silu(z_ref[...]), g_ref[...]).astype(o_ref.dtype)


def _m2_norm(xbc2d, p_m2_2d, yf2d, yb2d, dvec, gn, l):
    rows = xbc2d.shape[0]
    rb = 4 * ROW_BLK
    blk = pl.BlockSpec((rb, 256), lambda i: (i, 0))
    full = lambda a: _lspec(a, l)
    return pl.pallas_call(
        _m2_norm_kernel,
        grid=(rows // rb,),
        in_specs=[blk, blk, blk, blk, full(dvec), full(gn)],
        out_specs=blk,
        out_shape=jax.ShapeDtypeStruct((rows, 256), BF16),
        compiler_params=_cp("arbitrary"),
        name="m2_norm",
    )(xbc2d, p_m2_2d, yf2d, yb2d, dvec, gn)


def _merge_kernel(xl_ref, xc_ref, oa_ref, on_ref, os_ref, om_ref, gl_ref, wb_ref, wo_ref, gpm_ref, gate1_ref,
                  gpf_ref, sh2_ref, sc2_ref, wr_ref, x1_ref, h2_ref, aff_ref, *, nb):
    outs = (oa_ref, on_ref, os_ref, om_ref)
    y = None
    for j in range(4):
        d = jnp.dot(outs[j][...], wb_ref[j], preferred_element_type=F32)
        t = d * jnp.tanh(gl_ref[:, 1024 * j:1024 * j + 1024]) + d
        y = t if y is None else y + t
    y2 = _dot(y, wo_ref[...])
    x1 = _stream_block(xl_ref, xc_ref, nb) + gate1_ref[0] * _rms(y2, gpm_ref[...])
    x1_ref[...] = x1
    h2 = _rms(x1, gpf_ref[...]) * (1.0 + sc2_ref[0]) + sh2_ref[0]
    h2_ref[...] = h2.astype(BF16)
    logits = _dot_hi_nt(wr_ref[...], h2)
    m = jnp.max(logits, axis=0, keepdims=True)
    e = jnp.exp(logits - m)
    aff_ref[0, 0] = e / jnp.sum(e, axis=0, keepdims=True)


def _merge(stream, o_mla, o_na, o_s5, o_m2, gl, wb, wo, gpm, mod3, gpf, wrt, nbatch, nb, l):
    mrow = _mod_row(nb)
    xargs, xspecs = _stream_specs(stream, nb)
    fb = lambda i: (_flat_blk(i, nb), 0)
    blk = lambda w: pl.BlockSpec((ROW_BLK, w), fb)
    oblk = lambda w: pl.BlockSpec((ROW_BLK, w), lambda i: (i, 0))
    full = lambda a: _lspec(a, l)
    modspec = lambda k: pl.BlockSpec((1, 1, D), lambda i: (mrow(i, nbatch), 0, k))
    orows = nbatch * nb * ROW_BLK
    return pl.pallas_call(
        functools.partial(_merge_kernel, nb=nb),
        grid=(nbatch * nb,),
        in_specs=xspecs + [oblk(256), oblk(256), blk(256), blk(256), blk(GATE_W),
                  full(wb), full(wo), full(gpm), modspec(2), full(gpf), modspec(3), modspec(4), full(wrt)],
        out_specs=[oblk(D), oblk(D),
                   pl.BlockSpec((1, 1, N_EXPERTS, ROW_BLK), lambda i: (i // nb, i % nb, 0, 0))],
        out_shape=[jax.ShapeDtypeStruct((orows, D), F32),
                   jax.ShapeDtypeStruct((orows, D), BF16),
                   jax.ShapeDtypeStruct((nbatch, nb, N_EXPERTS, ROW_BLK), F32)],
        compiler_params=_cp("arbitrary"),
        name="merge",
    )(*xargs, o_mla, o_na, o_s5, o_m2, gl, wb, wo, gpm, mod3, gpf, mod3, mod3, wrt)


def _topk_kernel(aff_ref, slot_ref, *, nk, cap):
    a = aff_ref[0]
    bits = lax.bitcast_convert_type(a, jnp.int32)
    count = lambda m: jnp.sum(jnp.sum(jnp.where(m, 1.0, 0.0), axis=2, keepdims=True), axis=0, keepdims=True)
    thr = jnp.where(count(bits >= (1 << 30)) >= cap, jnp.int32(1 << 30), jnp.zeros((1, N_EXPERTS, 1), jnp.int32))
    for bit in range(29, 0, -2):
        hi, lo = 1 << bit, 1 << (bit - 1)
        ok = [count(bits >= (thr | c)) >= cap for c in (hi | lo, hi, lo)]
        thr = thr | jnp.where(ok[0], hi | lo, jnp.where(ok[1], hi, jnp.where(ok[2], lo, 0)))
    gt = bits > thr
    eq = bits == thr
    need = cap - count(gt)

    tri = jnp.where(lax.broadcasted_iota(jnp.int32, (256, 256), 0) <= lax.broadcasted_iota(jnp.int32, (256, 256), 1),
                    1.0, 0.0).astype(BF16)

    def prefix_excl(m):
        incl = jnp.dot(m.reshape(nk * N_EXPERTS, 256).astype(BF16), tri,
                       preferred_element_type=F32).reshape(nk, N_EXPERTS, 256)
        offs = []
        run = jnp.zeros((1, N_EXPERTS, 1), F32)
        for k in range(nk):
            offs.append(run)
            run = run + incl[k:k + 1, :, 255:256]
        off = offs[0] if nk == 1 else jnp.concatenate(offs, axis=0)
        return incl - m + off

    eqf = jnp.where(eq, 1.0, 0.0)
    sel = jnp.where(gt, 1.0, jnp.where(eq & (prefix_excl(eqf) < need), 1.0, 0.0))
    slot = jnp.where(sel > 0.5, prefix_excl(sel), -1.0)
    slot_ref[0] = slot.astype(jnp.int32)


def _topk(aff, blk0, nk, cap):
    nbatch = aff.shape[0]
    return pl.pallas_call(
        functools.partial(_topk_kernel, nk=nk, cap=cap),
        grid=(nbatch,),
        in_specs=[pl.BlockSpec((1, nk, N_EXPERTS, ROW_BLK), lambda b: (b, blk0, 0, 0))],
        out_specs=pl.BlockSpec((1, nk, N_EXPERTS, ROW_BLK), lambda b: (b, 0, 0, 0)),
        out_shape=jax.ShapeDtypeStruct((nbatch, nk, N_EXPERTS, ROW_BLK), jnp.int32),
        compiler_params=_cp("arbitrary"),
        name="topk",
    )(aff)


def _expert_kernel(*refs, has_ctx):
    if has_ctx:
        (w0_ref, fits_ref, hl_ref, sl_ref, al_ref, hc_ref, sc_ref, ac_ref, wg_ref, wu_ref, wd_ref,
         yl_ref, yc_ref, wgb, wub, wdb, xs_acc, gs_acc, xc_all, gc_all) = refs
    else:
        (w0_ref, fits_ref, hl_ref, sl_ref, al_ref, wg_ref, wu_ref, wd_ref, yl_ref,
         wgb, wub, wdb, xs_acc, gs_acc) = refs
    e = pl.program_id(0)
    bp = pl.program_id(1)

    @pl.when(bp == 0)
    def _():
        wgb[...] = wg_ref[0, 0].astype(BF16)
        wub[...] = wu_ref[0, 0].astype(BF16)
        wdb[...] = wd_ref[0, 0].astype(BF16)

    def gather(h_ref, slot_ref, aff_ref, s, nk, cap):
        r_iota = lax.broadcasted_iota(jnp.int32, (cap, ROW_BLK), 0)
        xs = jnp.zeros((cap, D), F32)
        gs = jnp.zeros((cap, 1), F32)
        for k in range(nk):
            pm = slot_ref[s, k, pl.ds(e, 1), :] == r_iota
            xs = xs + jnp.dot(jnp.where(pm, 1.0, 0.0).astype(BF16), h_ref[s, ROW_BLK * k:ROW_BLK * k + ROW_BLK, :],
                              preferred_element_type=F32)
            gs = gs + jnp.sum(jnp.where(pm, aff_ref[s, k, pl.ds(e, 1), :], 0.0), axis=1, keepdims=True)
        return xs.astype(BF16), gs

    def ffn(xsb):
        gt = jnp.dot(xsb, wgb[...], preferred_element_type=F32)
        up = jnp.dot(xsb, wub[...], preferred_element_type=F32)
        return jnp.dot((_silu(gt) * up).astype(BF16), wdb[...], preferred_element_type=F32)

    for s in range(EXP_SB):
        b = bp * EXP_SB + s
        base = CAP_LAT * s

        @pl.when(fits_ref[b] == 1)
        def _():
            xs_acc[base:base + CAP_LAT, :] = jnp.zeros((CAP_LAT, D), F32)
            gs_acc[base:base + CAP_LAT, :] = jnp.zeros((CAP_LAT, 1), F32)
            r_iota = lax.broadcasted_iota(jnp.int32, (MOE_WIN, ROW_BLK), 0)
            for k in range(LAT_BLKS):
                w0 = pl.multiple_of(w0_ref[(b * LAT_BLKS + k) * N_EXPERTS + e], 16)
                pm = (sl_ref[s, k, pl.ds(e, 1), :] - w0) == r_iota
                xs_acc[pl.ds(base + w0, MOE_WIN), :] += jnp.dot(jnp.where(pm, 1.0, 0.0).astype(BF16),
                                                                hl_ref[s, ROW_BLK * k:ROW_BLK * k + ROW_BLK, :],
                                                                preferred_element_type=F32)
                gs_acc[pl.ds(base + w0, MOE_WIN), :] += jnp.sum(jnp.where(pm, al_ref[s, k, pl.ds(e, 1), :], 0.0),
                                                                axis=1, keepdims=True)

        @pl.when(fits_ref[b] != 1)
        def _():
            xs, gs = gather(hl_ref, sl_ref, al_ref, s, LAT_BLKS, CAP_LAT)
            xs_acc[base:base + CAP_LAT, :] = xs.astype(F32)
            gs_acc[base:base + CAP_LAT, :] = gs

    y = ffn(xs_acc[...].astype(BF16)) * gs_acc[...]
    for s in range(EXP_SB):
        yl_ref[s, 0] = y[CAP_LAT * s:CAP_LAT * s + CAP_LAT].astype(yl_ref.dtype)
    if has_ctx:
        for s in range(EXP_SB):
            xs, gs = gather(hc_ref, sc_ref, ac_ref, s, 1, CAP_CTX)
            row = pl.multiple_of((bp * EXP_SB + s) * CAP_CTX, CAP_CTX)
            xc_all[pl.ds(row, CAP_CTX), :] = xs
            gc_all[pl.ds(row, CAP_CTX), :] = gs

        @pl.when(bp == pl.num_programs(1) - 1)
        def _():
            yc_ref[0] = (ffn(xc_all[...]) * gc_all[...]).astype(yc_ref.dtype)


def _experts(win0, fits, h2, slot_lat, slot_ctx, aff, w_gate, w_up, w_down, l, has_ctx):
    nbatch = h2.shape[0]
    assert nbatch % EXP_SB == 0
    idx4 = lambda e, b: (b, 0, 0, 0)
    smem = pl.BlockSpec(memory_space=pltpu.SMEM)
    in_specs = [smem, smem,
                pl.BlockSpec((EXP_SB, T, D), lambda e, b: (b, 0, 0)),
                pl.BlockSpec((EXP_SB, LAT_BLKS, N_EXPERTS, ROW_BLK), idx4),
                pl.BlockSpec((EXP_SB, LAT_BLKS, N_EXPERTS, ROW_BLK), idx4)]
    args = [win0, fits, h2, slot_lat, aff]
    out_specs = [pl.BlockSpec((EXP_SB, 1, CAP_LAT, D), lambda e, b: (b, e, 0, 0))]
    out_shape = [jax.ShapeDtypeStruct((nbatch, N_EXPERTS, CAP_LAT, D), BF16)]
    if has_ctx:
        in_specs += [pl.BlockSpec((EXP_SB, LC, D), lambda e, b: (b, LAT_BLKS, 0)),
                     pl.BlockSpec((EXP_SB, 1, N_EXPERTS, ROW_BLK), idx4),
                     pl.BlockSpec((EXP_SB, 1, N_EXPERTS, ROW_BLK), lambda e, b: (b, LAT_BLKS, 0, 0))]
        args += [h2, slot_ctx, aff]
        out_specs.append(pl.BlockSpec((1, nbatch * CAP_CTX, D), lambda e, b: (e, 0, 0)))
        out_shape.append(jax.ShapeDtypeStruct((N_EXPERTS, nbatch * CAP_CTX, D), BF16))
    wspec = pl.BlockSpec((1, 1, D, D), lambda e, b: (l, e, 0, 0))
    in_specs += [wspec, wspec, wspec]
    args += [w_gate, w_up, w_down]
    scratch = [pltpu.VMEM((D, D), BF16)] * 3 + [pltpu.VMEM((EXP_SB * CAP_LAT, D), F32),
                                                pltpu.VMEM((EXP_SB * CAP_LAT, 1), F32)]
    if has_ctx:
        scratch += [pltpu.VMEM((nbatch * CAP_CTX, D), BF16), pltpu.VMEM((nbatch * CAP_CTX, 1), F32)]
    return pl.pallas_call(
        functools.partial(_expert_kernel, has_ctx=has_ctx),
        grid=(N_EXPERTS, nbatch // EXP_SB),
        in_specs=in_specs,
        out_specs=out_specs,
        out_shape=out_shape,
        scratch_shapes=scratch,
        compiler_params=_cp("arbitrary", "arbitrary", vmem=VMEM_LIMIT_EXPERTS),
        name="experts",
    )(*args)


def _combine_kernel(*refs, has_ctx):
    if has_ctx:
        w0_ref, fits_ref, x1_ref, yl_ref, scl_ref, yc_ref, scc_ref, gate2_ref, g_ref, o_ref, ycat = refs
    else:
        w0_ref, fits_ref, x1_ref, yl_ref, scl_ref, gate2_ref, g_ref, o_ref, ycat = refs
    b = pl.program_id(0)
    j = pl.program_id(1)

    def finish(acc):
        o_ref[0] = x1_ref[0] + gate2_ref[0] * _rms(acc, g_ref[...])

    def comb(sc_ref, y_of, cap):
        lane = lax.broadcasted_iota(jnp.int32, (ROW_BLK, cap), 1)
        sc = sc_ref[0]
        acc = jnp.zeros((ROW_BLK, D), F32)
        for e in range(N_EXPERTS):
            pt = jnp.where(sc[:, e:e + 1] == lane, 1.0, 0.0).astype(BF16)
            acc = acc + jnp.dot(pt, y_of(e), preferred_element_type=F32)
        finish(acc)

    def comb_windowed():
        lane = lax.broadcasted_iota(jnp.int32, (ROW_BLK, ROW_BLK), 1)
        sc = scl_ref[0]
        per = ROW_BLK // MOE_WIN
        if per * MOE_WIN < ROW_BLK:
            ycat[:, per * MOE_WIN:, :] = jnp.zeros((2, ROW_BLK - per * MOE_WIN, D), BF16)
        acc = jnp.zeros((ROW_BLK, D), F32)
        for gi, e0 in enumerate(range(0, N_EXPERTS, per)):
            hit = None
            for q, e in enumerate(range(e0, min(e0 + per, N_EXPERTS))):
                w0 = pl.multiple_of(w0_ref[(b * LAT_BLKS + j) * N_EXPERTS + e], 16)
                d = sc[:, e:e + 1] - w0
                d = jnp.where(jnp.logical_and(d >= 0, d < MOE_WIN), d + MOE_WIN * q, -1)
                hit = (d == lane) if hit is None else jnp.logical_or(hit, d == lane)
                ycat[gi % 2, MOE_WIN * q:MOE_WIN * q + MOE_WIN, :] = yl_ref[0, e, pl.ds(w0, MOE_WIN), :]
            acc = acc + jnp.dot(jnp.where(hit, 1.0, 0.0).astype(BF16), ycat[gi % 2], preferred_element_type=F32)
        finish(acc)

    lat_y = lambda e: yl_ref[0, e]
    fits = fits_ref[b] == 1

    @pl.when(jnp.logical_and(j < LAT_BLKS, fits))
    def _():
        comb_windowed()

    @pl.when(jnp.logical_and(j < LAT_BLKS, jnp.logical_not(fits)))
    def _():
        comb(scl_ref, lat_y, CAP_LAT)

    if has_ctx:
        @pl.when(j == LAT_BLKS)
        def _():
            comb(scc_ref, lambda e: yc_ref[e], CAP_CTX)


def _combine(win0, fits, x1, y_lat, scol_lat, y_ctx, scol_ctx, mod3, g, has_ctx, l):
    nbatch = x1.shape[0]
    nb = NBLK if has_ctx else LAT_BLKS
    ntok = N if has_ctx else T
    smem = pl.BlockSpec(memory_space=pltpu.SMEM)
    in_specs = [smem, smem,
                pl.BlockSpec((1, ROW_BLK, D), lambda b, j: (b, j, 0)),
                pl.BlockSpec((1, N_EXPERTS, CAP_LAT, D), lambda b, j: (b, 0, 0, 0)),
                pl.BlockSpec((1, ROW_BLK, N_EXPERTS), lambda b, j: (b, jnp.minimum(j, LAT_BLKS - 1), 0))]
    args = [win0, fits, x1, y_lat, scol_lat]
    if has_ctx:
        in_specs += [pl.BlockSpec((N_EXPERTS, CAP_CTX, D), lambda b, j: (0, b, 0)),
                     pl.BlockSpec((1, ROW_BLK, N_EXPERTS), lambda b, j: (b, 0, 0))]
        args += [y_ctx, scol_ctx]
    in_specs += [pl.BlockSpec((1, 1, D), lambda b, j: (jnp.where(j == LAT_BLKS, nbatch, b), 0, 5)),
                 _lspec(g, l)]
    args += [mod3, g]
    return pl.pallas_call(
        functools.partial(_combine_kernel, has_ctx=has_ctx),
        grid=(nbatch, nb),
        in_specs=in_specs,
        out_specs=pl.BlockSpec((1, ROW_BLK, D), lambda b, j: (b, j, 0)),
        out_shape=jax.ShapeDtypeStruct((nbatch, ntok, D), F32),
        scratch_shapes=[pltpu.VMEM((2, ROW_BLK, D), BF16)],
        compiler_params=_cp("arbitrary", "arbitrary"),
        name="combine",
    )(*args)


def _rope_tables():
    f = 1.0 / (ROPE_THETA ** (jnp.arange(0, 16, 2, dtype=F32) / 16))
    pos = jnp.arange(T)
    row, col = pos // GRID_W, pos % GRID_W
    ar = row.astype(F32)[:, None] * f[None, :]
    ac = col.astype(F32)[:, None] * f[None, :]
    cos32 = jnp.concatenate([jnp.cos(ar), jnp.cos(ar), jnp.cos(ac), jnp.cos(ac)], axis=-1)
    sin32 = jnp.concatenate([jnp.sin(ar), jnp.sin(ar), jnp.sin(ac), jnp.sin(ac)], axis=-1)
    cos32 = jnp.concatenate([cos32, jnp.ones((LC, 32), F32)], axis=0)
    sin32 = jnp.concatenate([sin32, jnp.zeros((LC, 32), F32)], axis=0)
    return cos32, sin32


def _rot_cols(w):
    a, b, c, d = w[..., 0:8], w[..., 8:16], w[..., 16:24], w[..., 24:32]
    return jnp.concatenate([-b, a, -d, c], axis=-1)


def _mla_weights(w_uq, w_ukv):
    nl, r, _ = w_uq.shape
    wq3 = w_uq.reshape(nl, r, 4, 96)
    z32 = jnp.zeros((nl, r, 4, 32), F32)
    wq = jnp.concatenate([wq3, z32], axis=-1).reshape(nl, r, 512)
    wqr = jnp.concatenate([jnp.zeros((nl, r, 4, 64), F32), _rot_cols(wq3[..., 64:96]), z32], axis=-1).reshape(nl, r, 512)
    rk = w_ukv.shape[1]
    wkv3 = w_ukv.reshape(nl, rk, 4, 128)
    wk = jnp.concatenate([wkv3[..., :64], jnp.zeros((nl, rk, 4, 64), F32)], axis=-1).reshape(nl, rk, 512)
    z64 = jnp.zeros((nl, rk, 64), F32)
    vh = [wkv3[:, :, h, 64:] for h in range(4)]
    wv = jnp.concatenate([vh[0], z64, z64, vh[1], vh[2], z64, z64, vh[3]], axis=-1)
    vone = np.tile(np.repeat(np.array([0.0, 1.0, 1.0, 0.0], np.float32), 64), 2)[None, :]
    e = np.zeros((32, 512), np.float32)
    for h in range(4):
        e[np.arange(32), 128 * h + 64 + np.arange(32)] = 1.0
    return (wq.astype(BF16), wqr.astype(BF16), wk.astype(BF16), wv.astype(BF16), jnp.asarray(e, BF16),
            jnp.asarray(vone))


def _inproj_weights(w):
    o = np.cumsum([0, 256, 128, 32, 768, 256, 256, 768, 8, 4096])
    w = w.astype(BF16)
    seg = lambda i: w[:, :, o[i]:o[i + 1]]
    nl = w.shape[0]
    kr = seg(2)
    wm = jnp.concatenate([seg(0), seg(1), kr, _rot_cols(kr), jnp.zeros((nl, D, 64), BF16)], axis=-1)
    w2 = jnp.concatenate([seg(5), seg(6), seg(7), jnp.zeros((nl, D, 120), BF16)], axis=-1)
    return [wm, seg(3), seg(4), w2, 0.5 * seg(8)]


def kernel(x, c, ctx, c_ctx, w_ada, b_ada, g_pre_mix, g_post_mix, g_pre_ffn, g_post_ffn, w_in, mla_g_cq, mla_g_ckv, mla_w_uq, mla_w_ukv, na_rpb, s5_a_re, s5_a_im, s5_log_step, s5_b_re, s5_b_im, s5_c_re, s5_c_im, s5_d, s5_w_glu, s5_b_glu, m2_conv_w, m2_conv_b, m2_a_log, m2_dt_bias, m2_d, m2_g_norm, w_branch, w_out, w_router, w_gate, w_up, w_down):
    nbatch = x.shape[0]
    depth = w_ada.shape[0]
    vec = lambda v: v.reshape(depth, 1, -1).astype(F32)

    stream = (x, ctx, 0)
    cvec = jnp.concatenate([c, c_ctx[None, :], jnp.zeros((7, D), F32)], axis=0)
    cvec = cvec[: ((nbatch + 1 + 7) // 8) * 8]
    b_ada3 = b_ada.reshape(depth, 1, 6 * D)

    cos32, sin32 = _rope_tables()
    qscale = (64 + 32) ** -0.5 * math.log2(math.e)
    lane_is_rope = np.tile(np.concatenate([np.zeros(64, bool), np.ones(32, bool), np.zeros(32, bool)]), 4)
    pick = lambda t32, fill: jnp.where(lane_is_rope[None, :], jnp.tile(jnp.pad(t32, ((0, 0), (64, 32))), (1, 4)), fill)
    cos_q = pick(cos32, 1.0) * qscale
    sin_q = pick(sin32, 0.0) * qscale

    g_pre_mix, g_post_mix, g_pre_ffn, g_post_ffn = vec(g_pre_mix), vec(g_post_mix), vec(g_pre_ffn), vec(g_post_ffn)
    w_in_segs = _inproj_weights(w_in)
    wq, wqr, wk, wv, e_mat, vone = _mla_weights(mla_w_uq, mla_w_ukv)
    mla_g_cq, mla_g_ckv = vec(mla_g_cq), vec(mla_g_ckv)
    na_tabs = _na_tables(na_rpb)
    s5_bmat, s5_avec, s5_cmat = _s5_params(s5_a_re, s5_a_im, s5_log_step, s5_b_re, s5_b_im, s5_c_re, s5_c_im, nbatch)
    s5_d, s5_b_glu, s5_w_glu = vec(s5_d), vec(s5_b_glu), s5_w_glu.astype(BF16)
    m2_conv_w, m2_conv_b = m2_conv_w.astype(F32), vec(m2_conv_b)
    dt_bias = jnp.pad(m2_dt_bias.reshape(depth, 1, 8).astype(F32), ((0, 0), (0, 0), (0, 120)))
    a_col = jnp.pad(-jnp.exp(m2_a_log.astype(F32)).reshape(depth, 8, 1), ((0, 0), (0, 120), (0, 0)))
    m2_dvec, m2_g_norm = vec(jnp.repeat(m2_d, 64, axis=-1)), vec(m2_g_norm)
    w_branch, w_out = (0.5 * w_branch).astype(BF16), w_out.astype(BF16)
    w_router_t = jnp.swapaxes(w_router, 1, 2).astype(F32)

    for l in range(depth):
        has_ctx = l < depth - 1
        nb = NBLK if has_ctx else LAT_BLKS
        mod = _ada(cvec, w_ada, b_ada3, l)
        mod3 = mod.reshape(mod.shape[0], 1, 6 * D)

        p_mla, p_na, p_s5, p_m2, p_gate = _inproj(stream, g_pre_mix, mod3, w_in_segs, nbatch, l, has_ctx)

        q, k, v = _mla_prep(p_mla, mla_g_cq, mla_g_ckv, wq, wqr, wk, wv, e_mat, vone, cos_q, sin_q, cos32, sin32, l)
        o_mla = _mla_attn(q.reshape(nbatch, N, 512), k.reshape(nbatch, N, 512), v.reshape(nbatch, N, 512), has_ctx)

        o_na = _na_attn(p_na.reshape(nbatch, N, NA_W), na_tabs, has_ctx, l)

        yf, yb = _s5_scan(p_s5.reshape(nbatch, N, S5_W), s5_bmat, s5_avec, s5_cmat, l)
        o_s5 = _s5_glu(p_s5, yf.reshape(nbatch * N, S5_W), yb.reshape(nbatch * N, S5_W), s5_d, s5_w_glu, s5_b_glu, l)

        xbc, dtt, bmt = _m2_prep(p_m2.reshape(nbatch, N, M2_W), m2_conv_w, m2_conv_b, dt_bias, l)
        ssd_f, ssd_b = _ssd(xbc, dtt, bmt, a_col, l)
        o_m2 = _m2_norm(xbc.reshape(nbatch * N, 768), p_m2, ssd_f.reshape(nbatch * N, 256),
                        ssd_b.reshape(nbatch * N, 256), m2_dvec, m2_g_norm, l)

        x1, h2, aff = _merge(stream, o_mla.reshape(-1, 256), o_na.reshape(-1, 256), o_s5, o_m2, p_gate,
                             w_branch, w_out, g_post_mix, mod3, g_pre_ffn, w_router_t, nbatch, nb, l)

        slot_lat = _topk(aff, 0, LAT_BLKS, CAP_LAT)
        slot_ctx = _topk(aff, LAT_BLKS, 1, CAP_CTX) if has_ctx else None
        cnt = jnp.sum((slot_lat >= 0).astype(jnp.int32), axis=-1)
        first = jnp.cumsum(cnt, axis=1) - cnt
        win0 = jnp.minimum((first // 16) * 16, CAP_LAT - MOE_WIN)
        fits = jnp.all(first + cnt <= win0 + MOE_WIN, axis=(1, 2)).astype(jnp.int32)
        win0 = win0.reshape(-1)
        ys = _experts(win0, fits, h2.reshape(nbatch, nb * ROW_BLK, D), slot_lat, slot_ctx, aff, w_gate, w_up, w_down,
                      l, has_ctx)
        scol_lat = jnp.transpose(slot_lat, (0, 1, 3, 2)).reshape(nbatch, T, N_EXPERTS)
        scol_ctx = jnp.transpose(slot_ctx, (0, 1, 3, 2)).reshape(nbatch, LC, N_EXPERTS) if has_ctx else None
        xs = _combine(win0, fits, x1.reshape(nbatch, nb * ROW_BLK, D), ys[0], scol_lat, ys[1] if has_ctx else None,
                      scol_ctx, mod3, g_post_ffn, has_ctx, l)
        stream = (xs, xs, LAT_BLKS)
    return xs
```

```python
import functools
import math

import numpy as np
import jax
import jax.numpy as jnp
from jax import lax
from jax.experimental import pallas as pl
from jax.experimental.pallas import tpu as pltpu

F32 = jnp.float32
BF16 = jnp.bfloat16

D = 1024
T = 2048
LC = 256
N = T + LC
GRID_W = 64
ROW_BLK = 256
NBLK = N // ROW_BLK
LAT_BLKS = T // ROW_BLK
EPS = 1e-6
N_EXPERTS = 16
CAP_LAT = 2 * T // N_EXPERTS
CAP_CTX = 2 * LC // N_EXPERTS
ROPE_THETA = 10000.0
NEG = -1e30

MLA_W = 512
NA_W = 768
S5_W = 256
M2_W = 1152
GATE_W = 4096

MOE_WIN = 80
EXP_SB = 2
S5_CHUNK = 64
SSD_CHUNK = 128
SSD_SB = 8

VMEM_LIMIT = 56 * 1024 * 1024
VMEM_LIMIT_EXPERTS = 60 * 1024 * 1024


def _cp(*sem, vmem=VMEM_LIMIT):
    return pltpu.CompilerParams(dimension_semantics=sem, vmem_limit_bytes=vmem)


def _lspec(a, l, *lead):
    nd = a.ndim - 1 - len(lead)
    return pl.BlockSpec((None,) * (1 + len(lead)) + a.shape[1 + len(lead):], lambda *_: (l,) + lead + (0,) * nd)


def _dot(a, b):
    return jnp.dot(a.astype(BF16), b.astype(BF16), preferred_element_type=F32)


def _dot_nt(a, b):
    return lax.dot_general(a.astype(BF16), b.astype(BF16), (((1,), (1,)), ((), ())),
                           preferred_element_type=F32)


def _split3(a):
    hi = a.astype(BF16)
    r = a - hi.astype(F32)
    mid = r.astype(BF16)
    lo = (r - mid.astype(F32)).astype(BF16)
    return hi, mid, lo


def _dot_hi(a, b):
    ah, am, _ = _split3(a)
    bh, bm, _ = _split3(b)
    f = lambda x, y: jnp.dot(x, y, preferred_element_type=F32)
    return f(ah, bh) + (f(ah, bm) + f(am, bh))


def _dot_hi_nt(a, b):
    ah, am, _ = _split3(a)
    bh, bm, _ = _split3(b)
    f = lambda x, y: lax.dot_general(x, y, (((1,), (1,)), ((), ())), preferred_element_type=F32)
    return f(ah, bh) + (f(ah, bm) + f(am, bh))


def _dot_exact_lhs(m_bf16, a):
    h, m, l = _split3(a)
    f = lambda y: jnp.dot(m_bf16, y, preferred_element_type=F32)
    return f(h) + (f(m) + f(l))


def _dot_exact_rhs(a, m_bf16):
    h, m, l = _split3(a)
    f = lambda y: jnp.dot(y, m_bf16, preferred_element_type=F32)
    return f(h) + (f(m) + f(l))


def _sigmoid(x):
    return 0.5 * jnp.tanh(0.5 * x) + 0.5


def _silu(x):
    return x * _sigmoid(x)


def _rms(x, g):
    return x * lax.rsqrt(jnp.mean(x * x, axis=-1, keepdims=True) + EPS) * g


def _mod_row(nb):
    def f(i, nbatch):
        return jnp.where(i % nb == LAT_BLKS, nbatch, i // nb)
    return f


def _flat_blk(i, nb):
    return (i // nb) * NBLK + i % nb


def _stream_specs(stream, nb):
    lat, ctxa, cblk = stream
    return [lat, ctxa], [pl.BlockSpec((1, ROW_BLK, D), lambda i: (i // nb, jnp.minimum(i % nb, LAT_BLKS - 1), 0)),
                         pl.BlockSpec((1, ROW_BLK, D), lambda i: (i // nb, cblk, 0))]


def _stream_block(xl_ref, xc_ref, nb):
    return jnp.where(pl.program_id(0) % nb == LAT_BLKS, xc_ref[0], xl_ref[0])


def _ada_kernel(c_ref, w_ref, b_ref, o_ref):
    c = c_ref[...]
    o_ref[...] = _dot_hi(_silu(c), w_ref[0]) + b_ref[0]


def _ada(cvec, w_ada, b_ada3, l):
    rows = cvec.shape[0]
    return pl.pallas_call(
        _ada_kernel,
        grid=(6,),
        in_specs=[pl.BlockSpec((rows, D), lambda k: (0, 0)),
                  pl.BlockSpec((1, D, D), lambda k: (l, 0, k)),
                  pl.BlockSpec((1, 1, D), lambda k: (l, 0, k))],
        out_specs=pl.BlockSpec((rows, D), lambda k: (0, k)),
        out_shape=jax.ShapeDtypeStruct((rows, 6 * D), F32),
        compiler_params=_cp("arbitrary"),
        name="ada",
    )(cvec, w_ada, b_ada3)


def _inproj_kernel(xl_ref, xc_ref, g_ref, sh_ref, sc_ref, wm, wn, ws, w2, wg, om, on, os_, o2, og, *, ctx_gates):
    x = _stream_block(xl_ref, xc_ref, NBLK)
    h = (_rms(x, g_ref[...]) * (1.0 + sc_ref[0]) + sh_ref[0]).astype(BF16)
    om[...] = jnp.dot(h, wm[...], preferred_element_type=F32)
    on[...] = jnp.dot(h, wn[...], preferred_element_type=F32).astype(BF16)
    os_[...] = jnp.dot(h, ws[...], preferred_element_type=F32)
    o2[...] = jnp.dot(h, w2[...], preferred_element_type=F32)
    if ctx_gates:
        og[...] = jnp.dot(h, wg[...], preferred_element_type=F32)
    else:
        is_ctx = pl.program_id(0) % NBLK == LAT_BLKS

        @pl.when(jnp.logical_not(is_ctx))
        def _():
            og[...] = jnp.dot(h, wg[...], preferred_element_type=F32)

        @pl.when(is_ctx)
        def _():
            og[...] = jnp.zeros_like(og)


def _inproj(stream, g, mod3, ws, nbatch, l, ctx_gates):
    rows = nbatch * N
    nblk = rows // ROW_BLK
    mrow = _mod_row(NBLK)
    xargs, xspecs = _stream_specs(stream, NBLK)
    full = lambda w: _lspec(w, l)
    widths = (MLA_W, NA_W, S5_W, M2_W, GATE_W)
    dts = (F32, BF16, F32, F32, F32)
    return pl.pallas_call(
        functools.partial(_inproj_kernel, ctx_gates=ctx_gates),
        grid=(nblk,),
        in_specs=xspecs + [
                  full(g),
                  pl.BlockSpec((1, 1, D), lambda i: (mrow(i, nbatch), 0, 0)),
                  pl.BlockSpec((1, 1, D), lambda i: (mrow(i, nbatch), 0, 1))]
                 + [full(w) for w in ws],
        out_specs=[pl.BlockSpec((ROW_BLK, w), lambda i: (i, 0)) for w in widths],
        out_shape=[jax.ShapeDtypeStruct((rows, w), dt) for w, dt in zip(widths, dts)],
        compiler_params=_cp("arbitrary"),
        name="inproj",
    )(*xargs, g, mod3, mod3, *ws)


def _mla_prep_kernel(p_ref, gq_ref, gkv_ref, wq_ref, wqr_ref, wk_ref, wv_ref, e_ref, vone_ref,
                     cos_ref, sin_ref, ck_ref, sk_ref, q_out, k_out, v_out):
    p = p_ref[...]
    cqn = _rms(p[:, :256], gq_ref[...]).astype(BF16)
    q = jnp.dot(cqn, wq_ref[...], preferred_element_type=F32)
    qr = jnp.dot(cqn, wqr_ref[...], preferred_element_type=F32)
    q_out[...] = (q * cos_ref[...] + qr * sin_ref[...]).astype(BF16)
    ckvn = _rms(p[:, 256:384], gkv_ref[...]).astype(BF16)
    kro = (p[:, 384:416] * ck_ref[...] + p[:, 416:448] * sk_ref[...]).astype(BF16)
    k = jnp.dot(ckvn, wk_ref[...], preferred_element_type=F32) + jnp.dot(kro, e_ref[...], preferred_element_type=F32)
    k_out[...] = k.astype(BF16)
    v_out[...] = (jnp.dot(ckvn, wv_ref[...], preferred_element_type=F32) + vone_ref[...]).astype(BF16)


def _mla_prep(p_mla, gq, gkv, wq, wqr, wk, wv, e, vone, cos_q, sin_q, cos_k, sin_k, l):
    rows = p_mla.shape[0]
    full = lambda w: _lspec(w, l)
    const = lambda w: pl.BlockSpec(w.shape, lambda j, b: (0, 0))
    row = lambda j, b: (b * NBLK + j, 0)
    tab = lambda w: pl.BlockSpec((ROW_BLK, w), lambda j, b: (j, 0))
    return pl.pallas_call(
        _mla_prep_kernel,
        grid=(NBLK, rows // N),
        in_specs=[pl.BlockSpec((ROW_BLK, MLA_W), row),
                  full(gq), full(gkv), full(wq), full(wqr), full(wk), full(wv), const(e), const(vone),
                  tab(512), tab(512), tab(32), tab(32)],
        out_specs=[pl.BlockSpec((ROW_BLK, 512), row)] * 3,
        out_shape=[jax.ShapeDtypeStruct((rows, 512), BF16)] * 3,
        compiler_params=_cp("arbitrary", "arbitrary"),
        name="mla_prep",
    )(p_mla, gq, gkv, wq, wqr, wk, wv, e, vone, cos_q, sin_q, cos_k, sin_k)


def _mla_attn_kernel(q_ref, k_ref, v_ref, o_ref, *, has_ctx):
    lane = lax.broadcasted_iota(jnp.int32, (q_ref.shape[1], 128), 1)

    def run(k0, nk):
        for hp in range(2):
            pv = []
            for h in (2 * hp, 2 * hp + 1):
                qh = q_ref[0, :, 128 * h:128 * h + 128]
                kh = k_ref[0, k0:k0 + nk, 128 * h:128 * h + 128]
                s = _dot_nt(qh, kh)
                p = jnp.exp2(s - jnp.max(s, axis=-1, keepdims=True)).astype(BF16)
                pv.append(jnp.dot(p, v_ref[0, k0:k0 + nk, 128 * h:128 * h + 128], preferred_element_type=F32))
            oa, ob = pv
            o = jnp.where(lane < 64, oa * (1.0 / oa[:, 64:65]), ob * (1.0 / ob[:, 0:1]))
            o_ref[0, :, 128 * hp:128 * hp + 128] = o.astype(o_ref.dtype)

    if has_ctx:
        j = pl.program_id(1)

        @pl.when(j < LAT_BLKS)
        def _():
            run(0, N)

        @pl.when(j == LAT_BLKS)
        def _():
            run(T, LC)
    else:
        run(0, N)


def _mla_attn(q, k, v, has_ctx):
    nbatch = q.shape[0]
    nb = NBLK if has_ctx else LAT_BLKS
    qb = ROW_BLK if has_ctx else 2 * ROW_BLK
    return pl.pallas_call(
        functools.partial(_mla_attn_kernel, has_ctx=has_ctx),
        grid=(nbatch, nb * ROW_BLK // qb),
        in_specs=[pl.BlockSpec((1, qb, 512), lambda b, j: (b, j, 0)),
                  pl.BlockSpec((1, N, 512), lambda b, j: (b, 0, 0)),
                  pl.BlockSpec((1, N, 512), lambda b, j: (b, 0, 0))],
        out_specs=pl.BlockSpec((1, qb, 256), lambda b, j: (b, j, 0)),
        out_shape=jax.ShapeDtypeStruct((nbatch, nb * ROW_BLK, 256), BF16),
        compiler_params=_cp("arbitrary", "arbitrary"),
        name="mla_attn",
    )(q, k, v)


def _na_kernel(q_ref, kc_ref, k0_ref, k1_ref, k2_ref, vc_ref, v0_ref, v1_ref, v2_ref, b_ref, o_ref, *, has_ctx):
    scale = jnp.asarray(0.125, BF16)

    lane = lax.broadcasted_iota(jnp.int32, (ROW_BLK, 128), 1)
    lane64 = lax.broadcasted_iota(jnp.int32, (GRID_W, 128), 1) < 64
    grp = jnp.minimum(pl.program_id(0), LAT_BLKS - 1)
    krow0 = 4 * jnp.clip(grp - 1, 0, LAT_BLKS - 3)

    def bias_piece(h, i):
        rows = []
        for qr in range(4):
            r = 4 * grp + qr
            rs = jnp.clip(r - 4, 0, 24)
            tiles = []
            for pr in range(2):
                kr = krow0 + (4 * i + 2 * pr)
                t = b_ref[h, jnp.clip(kr - r + 8, 0, 15)]
                ok_a = jnp.logical_and(kr >= rs, kr < rs + 8).astype(jnp.int32)
                ok_b = jnp.logical_and(kr + 1 >= rs, kr + 1 < rs + 8).astype(jnp.int32)
                tiles.append(jnp.where(jnp.where(lane64, ok_a, ok_b) > 0, t, NEG))
            rows.append(jnp.concatenate(tiles, axis=1))
        return jnp.concatenate(rows, axis=0)

    def heads(win):
        kws = (k0_ref, k1_ref, k2_ref)
        vws = (v0_ref, v1_ref, v2_ref)
        for hp in range(2):
            sl = slice(128 * hp, 128 * hp + 128)
            qp = q_ref[0, :, sl] * scale
            outs = []
            for hh in range(2):
                h = 2 * hp + hh
                qh = jnp.where((lane < 64) if hh == 0 else (lane >= 64), qp, jnp.zeros_like(qp))
                s_c = _dot_nt(qh, kc_ref[0, :, sl])
                s_w = []
                smax = s_c
                if win:
                    for i in range(3):
                        s = _dot_nt(qh, kws[i][0, :, sl]) + bias_piece(h, i)
                        s_w.append(s)
                        smax = jnp.maximum(smax, s)
                m = jnp.max(smax, axis=-1, keepdims=True)
                p = jnp.exp(s_c - m)
                psum = p
                o = _dot(p, vc_ref[0, :, sl])
                for i, s in enumerate(s_w):
                    p = jnp.exp(s - m)
                    psum = psum + p
                    o = o + _dot(p, vws[i][0, :, sl])
                outs.append(o * (1.0 / jnp.sum(psum, axis=-1, keepdims=True)))
            o_ref[0, :, sl] = jnp.where(lane < 64, outs[0], outs[1]).astype(o_ref.dtype)

    if has_ctx:
        g = pl.program_id(0)

        @pl.when(g < LAT_BLKS)
        def _():
            heads(True)

        @pl.when(g == LAT_BLKS)
        def _():
            heads(False)
    else:
        heads(True)


def _na_attn(qkv, bias, has_ctx, l):
    nbatch = qkv.shape[0]
    ng = NBLK if has_ctx else LAT_BLKS
    j0 = lambda g: jnp.clip(g - 1, 0, LAT_BLKS - 3)
    blk = lambda f: pl.BlockSpec((1, ROW_BLK, 256), f)
    return pl.pallas_call(
        functools.partial(_na_kernel, has_ctx=has_ctx),
        grid=(ng, nbatch),
        in_specs=[blk(lambda g, b: (b, g, 0)),
                  blk(lambda g, b: (b, LAT_BLKS, 1)),
                  blk(lambda g, b: (b, j0(g), 1)),
                  blk(lambda g, b: (b, j0(g) + 1, 1)),
                  blk(lambda g, b: (b, j0(g) + 2, 1)),
                  blk(lambda g, b: (b, LAT_BLKS, 2)),
                  blk(lambda g, b: (b, j0(g), 2)),
                  blk(lambda g, b: (b, j0(g) + 1, 2)),
                  blk(lambda g, b: (b, j0(g) + 2, 2)),
                  _lspec(bias, l)],
        out_specs=blk(lambda g, b: (b, g, 0)),
        out_shape=jax.ShapeDtypeStruct((nbatch, ng * ROW_BLK, 256), BF16),
        compiler_params=_cp("arbitrary", "arbitrary"),
        name="na_attn",
    )(qkv, qkv, qkv, qkv, qkv, qkv, qkv, qkv, qkv, bias)


def _na_col_structure():
    onehot = np.zeros((31, 64, 64), np.float32)
    colmask = np.zeros((64, 64), bool)
    for c in range(64):
        cs = min(max(c - 8, 0), 48)
        for kc in range(cs, cs + 16):
            onehot[kc - c + 15, c, kc] = 1.0
            colmask[c, kc] = True
    return np.tile(onehot.reshape(31, 4096), (3, 1)), colmask


_NA_COL_ONEHOT, _NA_COLMASK = _na_col_structure()


def _na_tables(rpb):
    nl = rpb.shape[0]
    r = rpb.astype(F32)
    hi = lax.reduce_precision(r, 8, 7)
    mid = lax.reduce_precision(r - hi, 8, 7)
    parts = jnp.stack([hi, mid, r - hi - mid], axis=3).reshape(nl * 60, 93)
    tz = jnp.dot(parts, _NA_COL_ONEHOT, preferred_element_type=F32).reshape(nl, 4, 15, 64, 64)
    tz = jnp.where(_NA_COLMASK, tz, NEG)
    edge = jnp.full((nl, 4, 1, 64, 64), NEG, F32)
    tz17 = jnp.concatenate([edge, tz, edge], axis=2)
    return jnp.concatenate([tz17[:, :, :16], tz17[:, :, 1:]], axis=-1)


def _s5_kernel(uf_ref, ub_ref, bf_ref, bb_ref, af_ref, ab_ref, cf_ref, cb_ref, yf_ref, yb_ref,
               hf, hb, buf_f, buf_b, tm_f, tm_b):
    i = pl.program_id(0)
    half = 1024
    nb = uf_ref.shape[0]

    @pl.when(i == 0)
    def _():
        hf[...] = jnp.zeros_like(hf)
        hb[...] = jnp.zeros_like(hb)

    def expand(u_ref, tm, b_ref, buf):
        for b in range(nb):
            for c in range(2):
                tm[c, pl.ds(b, S5_CHUNK, stride=nb), :] = u_ref[b, :, 128 * c:128 * c + 128]
        buf[...] = _dot(jnp.concatenate([tm[0], tm[1]], axis=1), b_ref[...])

    def scan(buf, a_ref, h, order):
        hr, hi = h[:, :half], h[:, half:]
        ar, ai = a_ref[:, :half], a_ref[:, half:]
        for k in order:
            rows = slice(k * nb, (k + 1) * nb)
            nr = ar * hr - ai * hi + buf[rows, :half]
            ni = ar * hi + ai * hr + buf[rows, half:]
            buf[rows, :half] = nr
            buf[rows, half:] = ni
            hr, hi = nr, ni
        h[:, :half] = hr
        h[:, half:] = hi

    def readout(buf, c_ref, tm, y_ref):
        y = _dot(buf[...], c_ref[...])
        for c in range(2):
            tm[c] = y[:, 128 * c:128 * c + 128]
        for b in range(nb):
            for c in range(2):
                y_ref[b, :, 128 * c:128 * c + 128] = tm[c, pl.ds(b, S5_CHUNK, stride=nb), :]

    expand(uf_ref, tm_f, bf_ref, buf_f)
    expand(ub_ref, tm_b, bb_ref, buf_b)
    scan(buf_f, af_ref, hf, range(S5_CHUNK))
    readout(buf_f, cf_ref, tm_f, yf_ref)
    scan(buf_b, ab_ref, hb, range(S5_CHUNK - 1, -1, -1))
    readout(buf_b, cb_ref, tm_b, yb_ref)


def _s5_scan(u, bmat, avec, cmat, l):
    nbatch = u.shape[0]
    cr = S5_CHUNK * nbatch
    nch = N // S5_CHUNK
    nctx = LC // S5_CHUNK
    fidx = lambda i: jnp.where(i < nctx, nch - nctx + i, i - nctx)
    bidx = lambda i: nch - 1 - i
    ublk = lambda f: pl.BlockSpec((nbatch, S5_CHUNK, 256), lambda i: (0, f(i), 0))
    return pl.pallas_call(
        _s5_kernel,
        grid=(nch,),
        in_specs=[ublk(fidx), ublk(bidx),
                  _lspec(bmat, l, 0), _lspec(bmat, l, 1), _lspec(avec, l, 0), _lspec(avec, l, 1),
                  _lspec(cmat, l, 0), _lspec(cmat, l, 1)],
        out_specs=[ublk(fidx), ublk(bidx)],
        out_shape=[jax.ShapeDtypeStruct((nbatch, N, 256), F32)] * 2,
        scratch_shapes=[pltpu.VMEM((nbatch, 2048), F32), pltpu.VMEM((nbatch, 2048), F32),
                        pltpu.VMEM((cr, 2048), F32), pltpu.VMEM((cr, 2048), F32),
                        pltpu.VMEM((2, cr, 128), F32), pltpu.VMEM((2, cr, 128), F32)],
        compiler_params=_cp("arbitrary"),
        name="s5_scan",
    )(u, u, bmat, bmat, avec, avec, cmat, cmat)


def _s5_glu_kernel(u_ref, yf_ref, yb_ref, d_ref, w_ref, b_ref, o_ref):
    y = d_ref[...] * u_ref[...] + yf_ref[...] + yb_ref[...]
    z = y * (0.5 * (1.0 + jnp.tanh(math.sqrt(2.0 / math.pi) * (y + 0.044715 * (y * y * y)))))
    o_ref[...] = (z * _sigmoid(_dot(z, w_ref[...]) + b_ref[...])).astype(o_ref.dtype)


def _s5_glu(u_tb, yf, yb, d, w, b, l):
    rows = u_tb.shape[0]
    rb = 4 * ROW_BLK
    blk = pl.BlockSpec((rb, 256), lambda i: (i, 0))
    full = lambda a: _lspec(a, l)
    return pl.pallas_call(
        _s5_glu_kernel,
        grid=(rows // rb,),
        in_specs=[blk, blk, blk, full(d), full(w), full(b)],
        out_specs=blk,
        out_shape=jax.ShapeDtypeStruct((rows, 256), BF16),
        compiler_params=_cp("arbitrary"),
        name="s5_glu",
    )(u_tb, yf, yb, d, w, b)


def _s5_params(a_re, a_im, log_step, b_re, b_im, c_re, c_im, nbatch):
    nl = a_re.shape[0]
    same_group = (np.arange(256)[:, None] // 16 == np.arange(1024)[None, :] // 64).astype(np.float32)
    a = lax.complex(a_re.astype(F32), a_im.astype(F32))
    abar = jnp.exp(jnp.exp(log_step.astype(F32))[..., None] * a)
    bbar = ((abar - 1.0) / a)[..., None] * lax.complex(b_re.astype(F32), b_im.astype(F32))
    blk_in = lambda m: jnp.tile(jnp.swapaxes(m, -1, -2).reshape(nl, 2, 256, 64), (1, 1, 1, 16)) * same_group
    bmat = jnp.concatenate([blk_in(jnp.real(bbar)), blk_in(jnp.imag(bbar))], axis=-1).astype(BF16)
    avec = jnp.concatenate([jnp.real(abar).reshape(nl, 2, 1, 1024), jnp.imag(abar).reshape(nl, 2, 1, 1024)], axis=-1)
    avec = jnp.broadcast_to(avec, (nl, 2, nbatch, 2048))
    blk_out = lambda m: jnp.tile(jnp.swapaxes(m, -1, -2).reshape(nl, 2, 1024, 16), (1, 1, 1, 16)) * same_group.T
    cmat = jnp.concatenate([blk_out(c_re.astype(F32)), -blk_out(c_im.astype(F32))], axis=-2).astype(BF16)
    return bmat, avec, cmat


def _m2_prep_kernel(x_ref, dt_ref, w_ref, b_ref, dtb_ref, xo_ref, dtt_ref, bmt_ref):
    c = pl.program_id(1)
    x = x_ref[0]
    t = lax.broadcasted_iota(jnp.int32, x.shape, 0)
    m2 = ((t >= 2) & (t < T)) | (t >= T + 2)
    m1 = ((t >= 1) & (t < T)) | (t >= T + 1)
    p1 = (t <= T - 2) | ((t >= T) & (t <= N - 2))
    w = w_ref[...]
    y = (w[0:1] * jnp.where(m2, pltpu.roll(x, 2, 0), 0.0)
         + w[1:2] * jnp.where(m1, pltpu.roll(x, 1, 0), 0.0)
         + w[2:3] * x
         + w[3:4] * jnp.where(p1, pltpu.roll(x, N - 1, 0), 0.0)) + b_ref[...]
    act = _silu(y)
    xo_ref[0] = act

    @pl.when(c == 0)
    def _():
        v = dt_ref[0] + dtb_ref[...]
        dtt_ref[0] = (jnp.maximum(v, 0.0) + jnp.log1p(jnp.exp(-jnp.abs(v)))).T

    @pl.when(c == 1)
    def _():
        bmt_ref[0] = act.T.astype(BF16)


def _m2_prep(p_m2, conv_w, conv_b, dtb, l):
    nbatch = p_m2.shape[0]
    return pl.pallas_call(
        _m2_prep_kernel,
        grid=(nbatch, 3),
        in_specs=[pl.BlockSpec((1, N, 256), lambda b, c: (b, 0, 1 + c)),
                  pl.BlockSpec((1, N, 128), lambda b, c: (b, 0, 8)),
                  pl.BlockSpec((None, 4, 256), lambda b, c: (l, 0, c)),
                  pl.BlockSpec((None, 1, 256), lambda b, c: (l, 0, c)),
                  _lspec(dtb, l)],
        out_specs=[pl.BlockSpec((1, N, 256), lambda b, c: (b, 0, c)),
                   pl.BlockSpec((1, 128, N), lambda b, c: (b, 0, 0)),
                   pl.BlockSpec((1, 256, N), lambda b, c: (b, 0, 0))],
        out_shape=[jax.ShapeDtypeStruct((nbatch, N, 768), F32),
                   jax.ShapeDtypeStruct((nbatch, 128, N), F32),
                   jax.ShapeDtypeStruct((nbatch, 256, N), BF16)],
        compiler_params=_cp("arbitrary", "arbitrary"),
        name="m2_prep",
    )(p_m2, p_m2, conv_w, conv_b, dtb)


def _ssd_kernel(xf_ref, dttf_ref, bmtf_ref, xb_ref, dttb_ref, bmtb_ref, acol_ref, yf_ref, yb_ref, hs):
    i = pl.program_id(1)
    L = SSD_CHUNK

    @pl.when(i == 0)
    def _():
        hs[...] = jnp.zeros_like(hs)

    li = lax.broadcasted_iota(jnp.int32, (L, L), 0)
    si = lax.broadcasted_iota(jnp.int32, (L, L), 1)
    left = si < 64
    pick = lambda a, b: jnp.where(left, a, b)

    def direction(sm, x_ref, dtt_ref, bmt_ref, y_ref, d, causal, causal_t, last):
        cz = jnp.where(causal, 1.0, 0.0).astype(BF16)
        czt = jnp.where(causal_t, 1.0, 0.0).astype(BF16)
        xbc = x_ref[sm]
        dtt = dtt_ref[sm]
        parts = jnp.concatenate([p.astype(F32) for p in _split3(dtt * acol_ref[...])], axis=1)
        cumt = jnp.dot(parts.astype(BF16), jnp.concatenate([czt, czt, czt], axis=0),
                       preferred_element_type=F32)
        reps = jnp.concatenate([jnp.broadcast_to(parts[4 * d + h:4 * d + h + 1, :], (L, 3 * L)) for h in range(4)],
                               axis=0).astype(BF16)
        cumb = lax.dot_general(jnp.concatenate([cz, cz, cz], axis=1), reps, (((1,), (1,)), ((), ())),
                               preferred_element_type=F32)
        for g in range(2):
            xp = xbc[:, 128 * g:128 * g + 128].astype(BF16)
            cm = xbc[:, 512 + 128 * g:640 + 128 * g].astype(BF16)
            bmt = bmt_ref[sm, 128 * g:128 * g + 128, :]
            hprev = hs[sm, d, g]
            go = jnp.dot(cm, jnp.concatenate([bmt, hprev.astype(BF16)], axis=1), preferred_element_type=F32)
            gmat, yo = go[:, :L], go[:, L:]
            bmf = bmt.astype(F32)
            lhs, cbs, cls = [], [], []
            for hh in range(2):
                h = 2 * g + hh
                cb = cumb[:, L * h:L * h + L]
                crow = cumt[4 * d + h:4 * d + h + 1, :]
                dtrow = dtt[4 * d + h:4 * d + h + 1, :]
                cl = cb[last:last + 1, :]
                lhs.append((gmat * jnp.where(causal, jnp.exp(cb - crow), 0.0) * dtrow).astype(BF16))
                lhs.append((bmf * (jnp.exp(cl - crow) * dtrow)).astype(BF16))
                cbs.append(cb)
                cls.append(cl)
            big = jnp.dot(jnp.concatenate(lhs, axis=0), xp, preferred_element_type=F32)
            y_ref[sm, :, 128 * g:128 * g + 128] = pick(big[0:L], big[2 * L:3 * L]) + yo * jnp.exp(pick(cbs[0], cbs[1]))
            hs[sm, d, g] = hprev * jnp.exp(pick(cls[0], cls[1])) + pick(big[L:2 * L], big[3 * L:4 * L])

    for sm in range(SSD_SB):
        direction(sm, xf_ref, dttf_ref, bmtf_ref, yf_ref, 0, si <= li, li <= si, L - 1)
        direction(sm, xb_ref, dttb_ref, bmtb_ref, yb_ref, 1, si >= li, li >= si, 0)


def _ssd(xbc, dtt, bmt, a_col, l):
    nbatch = xbc.shape[0]
    nch = N // SSD_CHUNK
    nctx = LC // SSD_CHUNK
    fidx = lambda i: jnp.where(i < nctx, nch - nctx + i, i - nctx)
    bidx = lambda i: nch - 1 - i
    ins = lambda f: [pl.BlockSpec((SSD_SB, SSD_CHUNK, 768), lambda b, i: (b, f(i), 0)),
                     pl.BlockSpec((SSD_SB, 128, SSD_CHUNK), lambda b, i: (b, 0, f(i))),
                     pl.BlockSpec((SSD_SB, 256, SSD_CHUNK), lambda b, i: (b, 0, f(i)))]
    return pl.pallas_call(
        _ssd_kernel,
        grid=(nbatch // SSD_SB, nch),
        in_specs=ins(fidx) + ins(bidx) + [_lspec(a_col, l)],
        out_specs=[pl.BlockSpec((SSD_SB, SSD_CHUNK, 256), lambda b, i: (b, fidx(i), 0)),
                   pl.BlockSpec((SSD_SB, SSD_CHUNK, 256), lambda b, i: (b, bidx(i), 0))],
        out_shape=[jax.ShapeDtypeStruct((nbatch, N, 256), F32)] * 2,
        scratch_shapes=[pltpu.VMEM((SSD_SB, 2, 2, 128, 128), F32)],
        compiler_params=_cp("arbitrary", "arbitrary"),
        name="ssd",
    )(xbc, dtt, bmt, xbc, dtt, bmt, a_col)


def _m2_norm_kernel(x_ref, z_ref, yf_ref, yb_ref, d_ref, g_ref, o_ref):
    y = d_ref[...] * x_ref[...] + yf_ref[...] + yb_ref[...]
    o_ref[...] = _rms(y * _silu(z_ref[...]), g_ref[...]).astype(o_ref.dtype)


def _m2_norm(xbc2d, p_m2_2d, yf2d, yb2d, dvec, gn, l):
    rows = xbc2d.shape[0]
    rb = 4 * ROW_BLK
    blk = pl.BlockSpec((rb, 256), lambda i: (i, 0))
    full = lambda a: _lspec(a, l)
    return pl.pallas_call(
        _m2_norm_kernel,
        grid=(rows // rb,),
        in_specs=[blk, blk, blk, blk, full(dvec), full(gn)],
        out_specs=blk,
        out_shape=jax.ShapeDtypeStruct((rows, 256), BF16),
        compiler_params=_cp("arbitrary"),
        name="m2_norm",
    )(xbc2d, p_m2_2d, yf2d, yb2d, dvec, gn)


def _merge_kernel(xl_ref, xc_ref, oa_ref, on_ref, os_ref, om_ref, gl_ref, wb_ref, wo_ref, gpm_ref, gate1_ref,
                  gpf_ref, sh2_ref, sc2_ref, wr_ref, x1_ref, h2_ref, aff_ref, *, nb):
    outs = (oa_ref, on_ref, os_ref, om_ref)
    y = None
    for j in range(4):
        d = jnp.dot(outs[j][...], wb_ref[j], preferred_element_type=F32)
        t = d * jnp.tanh(gl_ref[:, 1024 * j:1024 * j + 1024]) + d
        y = t if y is None else y + t
    y2 = _dot(y, wo_ref[...])
    x1 = _stream_block(xl_ref, xc_ref, nb) + gate1_ref[0] * _rms(y2, gpm_ref[...])
    x1_ref[...] = x1
    h2 = _rms(x1, gpf_ref[...]) * (1.0 + sc2_ref[0]) + sh2_ref[0]
    h2_ref[...] = h2.astype(BF16)
    logits = _dot_hi_nt(wr_ref[...], h2)
    m = jnp.max(logits, axis=0, keepdims=True)
    e = jnp.exp(logits - m)
    aff_ref[0, 0] = e / jnp.sum(e, axis=0, keepdims=True)


def _merge(stream, o_mla, o_na, o_s5, o_m2, gl, wb, wo, gpm, mod3, gpf, wrt, nbatch, nb, l):
    mrow = _mod_row(nb)
    xargs, xspecs = _stream_specs(stream, nb)
    fb = lambda i: (_flat_blk(i, nb), 0)
    blk = lambda w: pl.BlockSpec((ROW_BLK, w), fb)
    oblk = lambda w: pl.BlockSpec((ROW_BLK, w), lambda i: (i, 0))
    full = lambda a: _lspec(a, l)
    modspec = lambda k: pl.BlockSpec((1, 1, D), lambda i: (mrow(i, nbatch), 0, k))
    orows = nbatch * nb * ROW_BLK
    return pl.pallas_call(
        functools.partial(_merge_kernel, nb=nb),
        grid=(nbatch * nb,),
        in_specs=xspecs + [oblk(256), oblk(256), blk(256), blk(256), blk(GATE_W),
                  full(wb), full(wo), full(gpm), modspec(2), full(gpf), modspec(3), modspec(4), full(wrt)],
        out_specs=[oblk(D), oblk(D),
                   pl.BlockSpec((1, 1, N_EXPERTS, ROW_BLK), lambda i: (i // nb, i % nb, 0, 0))],
        out_shape=[jax.ShapeDtypeStruct((orows, D), F32),
                   jax.ShapeDtypeStruct((orows, D), BF16),
                   jax.ShapeDtypeStruct((nbatch, nb, N_EXPERTS, ROW_BLK), F32)],
        compiler_params=_cp("arbitrary"),
        name="merge",
    )(*xargs, o_mla, o_na, o_s5, o_m2, gl, wb, wo, gpm, mod3, gpf, mod3, mod3, wrt)


def _topk_kernel(aff_ref, slot_ref, *, nk, cap):
    a = aff_ref[0]
    bits = lax.bitcast_convert_type(a, jnp.int32)
    count = lambda m: jnp.sum(jnp.sum(jnp.where(m, 1.0, 0.0), axis=2, keepdims=True), axis=0, keepdims=True)
    thr = jnp.where(count(bits >= (1 << 30)) >= cap, jnp.int32(1 << 30), jnp.zeros((1, N_EXPERTS, 1), jnp.int32))
    for bit in range(29, 0, -2):
        hi, lo = 1 << bit, 1 << (bit - 1)
        ok = [count(bits >= (thr | c)) >= cap for c in (hi | lo, hi, lo)]
        thr = thr | jnp.where(ok[0], hi | lo, jnp.where(ok[1], hi, jnp.where(ok[2], lo, 0)))
    gt = bits > thr
    eq = bits == thr
    need = cap - count(gt)

    tri = jnp.where(lax.broadcasted_iota(jnp.int32, (256, 256), 0) <= lax.broadcasted_iota(jnp.int32, (256, 256), 1),
                    1.0, 0.0).astype(BF16)

    def prefix_excl(m):
        incl = jnp.dot(m.reshape(nk * N_EXPERTS, 256).astype(BF16), tri,
                       preferred_element_type=F32).reshape(nk, N_EXPERTS, 256)
        offs = []
        run = jnp.zeros((1, N_EXPERTS, 1), F32)
        for k in range(nk):
            offs.append(run)
            run = run + incl[k:k + 1, :, 255:256]
        off = offs[0] if nk == 1 else jnp.concatenate(offs, axis=0)
        return incl - m + off

    eqf = jnp.where(eq, 1.0, 0.0)
    sel = jnp.where(gt, 1.0, jnp.where(eq & (prefix_excl(eqf) < need), 1.0, 0.0))
    slot = jnp.where(sel > 0.5, prefix_excl(sel), -1.0)
    slot_ref[0] = slot.astype(jnp.int32)


def _topk(aff, blk0, nk, cap):
    nbatch = aff.shape[0]
    return pl.pallas_call(
        functools.partial(_topk_kernel, nk=nk, cap=cap),
        grid=(nbatch,),
        in_specs=[pl.BlockSpec((1, nk, N_EXPERTS, ROW_BLK), lambda b: (b, blk0, 0, 0))],
        out_specs=pl.BlockSpec((1, nk, N_EXPERTS, ROW_BLK), lambda b: (b, 0, 0, 0)),
        out_shape=jax.ShapeDtypeStruct((nbatch, nk, N_EXPERTS, ROW_BLK), jnp.int32),
        compiler_params=_cp("arbitrary"),
        name="topk",
    )(aff)


def _expert_kernel(*refs, has_ctx):
    if has_ctx:
        (w0_ref, fits_ref, hl_ref, sl_ref, al_ref, hc_ref, sc_ref, ac_ref, wg_ref, wu_ref, wd_ref,
         yl_ref, yc_ref, wgb, wub, wdb, xs_acc, gs_acc, xc_all, gc_all) = refs
    else:
        (w0_ref, fits_ref, hl_ref, sl_ref, al_ref, wg_ref, wu_ref, wd_ref, yl_ref,
         wgb, wub, wdb, xs_acc, gs_acc) = refs
    e = pl.program_id(0)
    bp = pl.program_id(1)

    @pl.when(bp == 0)
    def _():
        wgb[...] = wg_ref[0, 0].astype(BF16)
        wub[...] = wu_ref[0, 0].astype(BF16)
        wdb[...] = wd_ref[0, 0].astype(BF16)

    def gather(h_ref, slot_ref, aff_ref, s, nk, cap):
        r_iota = lax.broadcasted_iota(jnp.int32, (cap, ROW_BLK), 0)
        xs = jnp.zeros((cap, D), F32)
        gs = jnp.zeros((cap, 1), F32)
        for k in range(nk):
            pm = slot_ref[s, k, pl.ds(e, 1), :] == r_iota
            xs = xs + jnp.dot(jnp.where(pm, 1.0, 0.0).astype(BF16), h_ref[s, ROW_BLK * k:ROW_BLK * k + ROW_BLK, :],
                              preferred_element_type=F32)
            gs = gs + jnp.sum(jnp.where(pm, aff_ref[s, k, pl.ds(e, 1), :], 0.0), axis=1, keepdims=True)
        return xs.astype(BF16), gs

    def ffn(xsb):
        gt = jnp.dot(xsb, wgb[...], preferred_element_type=F32)
        up = jnp.dot(xsb, wub[...], preferred_element_type=F32)
        return jnp.dot((_silu(gt) * up).astype(BF16), wdb[...], preferred_element_type=F32)

    for s in range(EXP_SB):
        b = bp * EXP_SB + s
        base = CAP_LAT * s

        @pl.when(fits_ref[b] == 1)
        def _():
            xs_acc[base:base + CAP_LAT, :] = jnp.zeros((CAP_LAT, D), F32)
            gs_acc[base:base + CAP_LAT, :] = jnp.zeros((CAP_LAT, 1), F32)
            r_iota = lax.broadcasted_iota(jnp.int32, (MOE_WIN, ROW_BLK), 0)
            for k in range(LAT_BLKS):
                w0 = pl.multiple_of(w0_ref[(b * LAT_BLKS + k) * N_EXPERTS + e], 16)
                pm = (sl_ref[s, k, pl.ds(e, 1), :] - w0) == r_iota
                xs_acc[pl.ds(base + w0, MOE_WIN), :] += jnp.dot(jnp.where(pm, 1.0, 0.0).astype(BF16),
                                                                hl_ref[s, ROW_BLK * k:ROW_BLK * k + ROW_BLK, :],
                                                                preferred_element_type=F32)
                gs_acc[pl.ds(base + w0, MOE_WIN), :] += jnp.sum(jnp.where(pm, al_ref[s, k, pl.ds(e, 1), :], 0.0),
                                                                axis=1, keepdims=True)

        @pl.when(fits_ref[b] != 1)
        def _():
            xs, gs = gather(hl_ref, sl_ref, al_ref, s, LAT_BLKS, CAP_LAT)
            xs_acc[base:base + CAP_LAT, :] = xs.astype(F32)
            gs_acc[base:base + CAP_LAT, :] = gs

    y = ffn(xs_acc[...].astype(BF16)) * gs_acc[...]
    for s in range(EXP_SB):
        yl_ref[s, 0] = y[CAP_LAT * s:CAP_LAT * s + CAP_LAT].astype(yl_ref.dtype)
    if has_ctx:
        for s in range(EXP_SB):
            xs, gs = gather(hc_ref, sc_ref, ac_ref, s, 1, CAP_CTX)
            row = pl.multiple_of((bp * EXP_SB + s) * CAP_CTX, CAP_CTX)
            xc_all[pl.ds(row, CAP_CTX), :] = xs
            gc_all[pl.ds(row, CAP_CTX), :] = gs

        @pl.when(bp == pl.num_programs(1) - 1)
        def _():
            yc_ref[0] = (ffn(xc_all[...]) * gc_all[...]).astype(yc_ref.dtype)


def _experts(win0, fits, h2, slot_lat, slot_ctx, aff, w_gate, w_up, w_down, l, has_ctx):
    nbatch = h2.shape[0]
    assert nbatch % EXP_SB == 0
    idx4 = lambda e, b: (b, 0, 0, 0)
    smem = pl.BlockSpec(memory_space=pltpu.SMEM)
    in_specs = [smem, smem,
                pl.BlockSpec((EXP_SB, T, D), lambda e, b: (b, 0, 0)),
                pl.BlockSpec((EXP_SB, LAT_BLKS, N_EXPERTS, ROW_BLK), idx4),
                pl.BlockSpec((EXP_SB, LAT_BLKS, N_EXPERTS, ROW_BLK), idx4)]
    args = [win0, fits, h2, slot_lat, aff]
    out_specs = [pl.BlockSpec((EXP_SB, 1, CAP_LAT, D), lambda e, b: (b, e, 0, 0))]
    out_shape = [jax.ShapeDtypeStruct((nbatch, N_EXPERTS, CAP_LAT, D), BF16)]
    if has_ctx:
        in_specs += [pl.BlockSpec((EXP_SB, LC, D), lambda e, b: (b, LAT_BLKS, 0)),
                     pl.BlockSpec((EXP_SB, 1, N_EXPERTS, ROW_BLK), idx4),
                     pl.BlockSpec((EXP_SB, 1, N_EXPERTS, ROW_BLK), lambda e, b: (b, LAT_BLKS, 0, 0))]
        args += [h2, slot_ctx, aff]
        out_specs.append(pl.BlockSpec((1, nbatch * CAP_CTX, D), lambda e, b: (e, 0, 0)))
        out_shape.append(jax.ShapeDtypeStruct((N_EXPERTS, nbatch * CAP_CTX, D), BF16))
    wspec = pl.BlockSpec((1, 1, D, D), lambda e, b: (l, e, 0, 0))
    in_specs += [wspec, wspec, wspec]
    args += [w_gate, w_up, w_down]
    scratch = [pltpu.VMEM((D, D), BF16)] * 3 + [pltpu.VMEM((EXP_SB * CAP_LAT, D), F32),
                                                pltpu.VMEM((EXP_SB * CAP_LAT, 1), F32)]
    if has_ctx:
        scratch += [pltpu.VMEM((nbatch * CAP_CTX, D), BF16), pltpu.VMEM((nbatch * CAP_CTX, 1), F32)]
    return pl.pallas_call(
        functools.partial(_expert_kernel, has_ctx=has_ctx),
        grid=(N_EXPERTS, nbatch // EXP_SB),
        in_specs=in_specs,
        out_specs=out_specs,
        out_shape=out_shape,
        scratch_shapes=scratch,
        compiler_params=_cp("arbitrary", "arbitrary", vmem=VMEM_LIMIT_EXPERTS),
        name="experts",
    )(*args)


def _combine_kernel(*refs, has_ctx):
    if has_ctx:
        w0_ref, fits_ref, x1_ref, yl_ref, scl_ref, yc_ref, scc_ref, gate2_ref, g_ref, o_ref, ycat = refs
    else:
        w0_ref, fits_ref, x1_ref, yl_ref, scl_ref, gate2_ref, g_ref, o_ref, ycat = refs
    b = pl.program_id(0)
    j = pl.program_id(1)

    def finish(acc):
        o_ref[0] = x1_ref[0] + gate2_ref[0] * _rms(acc, g_ref[...])

    def comb(sc_ref, y_of, cap):
        lane = lax.broadcasted_iota(jnp.int32, (ROW_BLK, cap), 1)
        sc = sc_ref[0]
        acc = jnp.zeros((ROW_BLK, D), F32)
        for e in range(N_EXPERTS):
            pt = jnp.where(sc[:, e:e + 1] == lane, 1.0, 0.0).astype(BF16)
            acc = acc + jnp.dot(pt, y_of(e), preferred_element_type=F32)
        finish(acc)

    def comb_windowed():
        lane = lax.broadcasted_iota(jnp.int32, (ROW_BLK, ROW_BLK), 1)
        sc = scl_ref[0]
        per = ROW_BLK // MOE_WIN
        if per * MOE_WIN < ROW_BLK:
            ycat[:, per * MOE_WIN:, :] = jnp.zeros((2, ROW_BLK - per * MOE_WIN, D), BF16)
        acc = jnp.zeros((ROW_BLK, D), F32)
        for gi, e0 in enumerate(range(0, N_EXPERTS, per)):
            hit = None
            for q, e in enumerate(range(e0, min(e0 + per, N_EXPERTS))):
                w0 = pl.multiple_of(w0_ref[(b * LAT_BLKS + j) * N_EXPERTS + e], 16)
                d = sc[:, e:e + 1] - w0
                d = jnp.where(jnp.logical_and(d >= 0, d < MOE_WIN), d + MOE_WIN * q, -1)
                hit = (d == lane) if hit is None else jnp.logical_or(hit, d == lane)
                ycat[gi % 2, MOE_WIN * q:MOE_WIN * q + MOE_WIN, :] = yl_ref[0, e, pl.ds(w0, MOE_WIN), :]
            acc = acc + jnp.dot(jnp.where(hit, 1.0, 0.0).astype(BF16), ycat[gi % 2], preferred_element_type=F32)
        finish(acc)

    lat_y = lambda e: yl_ref[0, e]
    fits = fits_ref[b] == 1

    @pl.when(jnp.logical_and(j < LAT_BLKS, fits))
    def _():
        comb_windowed()

    @pl.when(jnp.logical_and(j < LAT_BLKS, jnp.logical_not(fits)))
    def _():
        comb(scl_ref, lat_y, CAP_LAT)

    if has_ctx:
        @pl.when(j == LAT_BLKS)
        def _():
            comb(scc_ref, lambda e: yc_ref[e], CAP_CTX)


def _combine(win0, fits, x1, y_lat, scol_lat, y_ctx, scol_ctx, mod3, g, has_ctx, l):
    nbatch = x1.shape[0]
    nb = NBLK if has_ctx else LAT_BLKS
    ntok = N if has_ctx else T
    smem = pl.BlockSpec(memory_space=pltpu.SMEM)
    in_specs = [smem, smem,
                pl.BlockSpec((1, ROW_BLK, D), lambda b, j: (b, j, 0)),
                pl.BlockSpec((1, N_EXPERTS, CAP_LAT, D), lambda b, j: (b, 0, 0, 0)),
                pl.BlockSpec((1, ROW_BLK, N_EXPERTS), lambda b, j: (b, jnp.minimum(j, LAT_BLKS - 1), 0))]
    args = [win0, fits, x1, y_lat, scol_lat]
    if has_ctx:
        in_specs += [pl.BlockSpec((N_EXPERTS, CAP_CTX, D), lambda b, j: (0, b, 0)),
                     pl.BlockSpec((1, ROW_BLK, N_EXPERTS), lambda b, j: (b, 0, 0))]
        args += [y_ctx, scol_ctx]
    in_specs += [pl.BlockSpec((1, 1, D), lambda b, j: (jnp.where(j == LAT_BLKS, nbatch, b), 0, 5)),
                 _lspec(g, l)]
    args += [mod3, g]
    return pl.pallas_call(
        functools.partial(_combine_kernel, has_ctx=has_ctx),
        grid=(nbatch, nb),
        in_specs=in_specs,
        out_specs=pl.BlockSpec((1, ROW_BLK, D), lambda b, j: (b, j, 0)),
        out_shape=jax.ShapeDtypeStruct((nbatch, ntok, D), F32),
        scratch_shapes=[pltpu.VMEM((2, ROW_BLK, D), BF16)],
        compiler_params=_cp("arbitrary", "arbitrary"),
        name="combine",
    )(*args)


def _rope_tables():
    f = 1.0 / (ROPE_THETA ** (jnp.arange(0, 16, 2, dtype=F32) / 16))
    pos = jnp.arange(T)
    row, col = pos // GRID_W, pos % GRID_W
    ar = row.astype(F32)[:, None] * f[None, :]
    ac = col.astype(F32)[:, None] * f[None, :]
    cos32 = jnp.concatenate([jnp.cos(ar), jnp.cos(ar), jnp.cos(ac), jnp.cos(ac)], axis=-1)
    sin32 = jnp.concatenate([jnp.sin(ar), jnp.sin(ar), jnp.sin(ac), jnp.sin(ac)], axis=-1)
    cos32 = jnp.concatenate([cos32, jnp.ones((LC, 32), F32)], axis=0)
    sin32 = jnp.concatenate([sin32, jnp.zeros((LC, 32), F32)], axis=0)
    return cos32, sin32


def _rot_cols(w):
    a, b, c, d = w[..., 0:8], w[..., 8:16], w[..., 16:24], w[..., 24:32]
    return jnp.concatenate([-b, a, -d, c], axis=-1)


def _mla_weights(w_uq, w_ukv):
    nl, r, _ = w_uq.shape
    wq3 = w_uq.reshape(nl, r, 4, 96)
    z32 = jnp.zeros((nl, r, 4, 32), F32)
    wq = jnp.concatenate([wq3, z32], axis=-1).reshape(nl, r, 512)
    wqr = jnp.concatenate([jnp.zeros((nl, r, 4, 64), F32), _rot_cols(wq3[..., 64:96]), z32], axis=-1).reshape(nl, r, 512)
    rk = w_ukv.shape[1]
    wkv3 = w_ukv.reshape(nl, rk, 4, 128)
    wk = jnp.concatenate([wkv3[..., :64], jnp.zeros((nl, rk, 4, 64), F32)], axis=-1).reshape(nl, rk, 512)
    z64 = jnp.zeros((nl, rk, 64), F32)
    vh = [wkv3[:, :, h, 64:] for h in range(4)]
    wv = jnp.concatenate([vh[0], z64, z64, vh[1], vh[2], z64, z64, vh[3]], axis=-1)
    vone = np.tile(np.repeat(np.array([0.0, 1.0, 1.0, 0.0], np.float32), 64), 2)[None, :]
    e = np.zeros((32, 512), np.float32)
    for h in range(4):
        e[np.arange(32), 128 * h + 64 + np.arange(32)] = 1.0
    return (wq.astype(BF16), wqr.astype(BF16), wk.astype(BF16), wv.astype(BF16), jnp.asarray(e, BF16),
            jnp.asarray(vone))


def _inproj_weights(w):
    o = np.cumsum([0, 256, 128, 32, 768, 256, 256, 768, 8, 4096])
    w = w.astype(BF16)
    seg = lambda i: w[:, :, o[i]:o[i + 1]]
    nl = w.shape[0]
    kr = seg(2)
    wm = jnp.concatenate([seg(0), seg(1), kr, _rot_cols(kr), jnp.zeros((nl, D, 64), BF16)], axis=-1)
    w2 = jnp.concatenate([seg(5), seg(6), seg(7), jnp.zeros((nl, D, 120), BF16)], axis=-1)
    return [wm, seg(3), seg(4), w2, 0.5 * seg(8)]


def kernel(x, c, ctx, c_ctx, w_ada, b_ada, g_pre_mix, g_post_mix, g_pre_ffn, g_post_ffn, w_in, mla_g_cq, mla_g_ckv, mla_w_uq, mla_w_ukv, na_rpb, s5_a_re, s5_a_im, s5_log_step, s5_b_re, s5_b_im, s5_c_re, s5_c_im, s5_d, s5_w_glu, s5_b_glu, m2_conv_w, m2_conv_b, m2_a_log, m2_dt_bias, m2_d, m2_g_norm, w_branch, w_out, w_router, w_gate, w_up, w_down):
    nbatch = x.shape[0]
    depth = w_ada.shape[0]
    vec = lambda v: v.reshape(depth, 1, -1).astype(F32)

    stream = (x, ctx, 0)
    cvec = jnp.concatenate([c, c_ctx[None, :], jnp.zeros((7, D), F32)], axis=0)
    cvec = cvec[: ((nbatch + 1 + 7) // 8) * 8]
    b_ada3 = b_ada.reshape(depth, 1, 6 * D)

    cos32, sin32 = _rope_tables()
    qscale = (64 + 32) ** -0.5 * math.log2(math.e)
    lane_is_rope = np.tile(np.concatenate([np.zeros(64, bool), np.ones(32, bool), np.zeros(32, bool)]), 4)
    pick = lambda t32, fill: jnp.where(lane_is_rope[None, :], jnp.tile(jnp.pad(t32, ((0, 0), (64, 32))), (1, 4)), fill)
    cos_q = pick(cos32, 1.0) * qscale
    sin_q = pick(sin32, 0.0) * qscale

    g_pre_mix, g_post_mix, g_pre_ffn, g_post_ffn = vec(g_pre_mix), vec(g_post_mix), vec(g_pre_ffn), vec(g_post_ffn)
    w_in_segs = _inproj_weights(w_in)
    wq, wqr, wk, wv, e_mat, vone = _mla_weights(mla_w_uq, mla_w_ukv)
    mla_g_cq, mla_g_ckv = vec(mla_g_cq), vec(mla_g_ckv)
    na_tabs = _na_tables(na_rpb)
    s5_bmat, s5_avec, s5_cmat = _s5_params(s5_a_re, s5_a_im, s5_log_step, s5_b_re, s5_b_im, s5_c_re, s5_c_im, nbatch)
    s5_d, s5_b_glu, s5_w_glu = vec(s5_d), vec(s5_b_glu), s5_w_glu.astype(BF16)
    m2_conv_w, m2_conv_b = m2_conv_w.astype(F32), vec(m2_conv_b)
    dt_bias = jnp.pad(m2_dt_bias.reshape(depth, 1, 8).astype(F32), ((0, 0), (0, 0), (0, 120)))
    a_col = jnp.pad(-jnp.exp(m2_a_log.astype(F32)).reshape(depth, 8, 1), ((0, 0), (0, 120), (0, 0)))
    m2_dvec, m2_g_norm = vec(jnp.repeat(m2_d, 64, axis=-1)), vec(m2_g_norm)
    w_branch, w_out = (0.5 * w_branch).astype(BF16), w_out.astype(BF16)
    w_router_t = jnp.swapaxes(w_router, 1, 2).astype(F32)

    for l in range(depth):
        has_ctx = l < depth - 1
        nb = NBLK if has_ctx else LAT_BLKS
        mod = _ada(cvec, w_ada, b_ada3, l)
        mod3 = mod.reshape(mod.shape[0], 1, 6 * D)

        p_mla, p_na, p_s5, p_m2, p_gate = _inproj(stream, g_pre_mix, mod3, w_in_segs, nbatch, l, has_ctx)

        q, k, v = _mla_prep(p_mla, mla_g_cq, mla_g_ckv, wq, wqr, wk, wv, e_mat, vone, cos_q, sin_q, cos32, sin32, l)
        o_mla = _mla_attn(q.reshape(nbatch, N, 512), k.reshape(nbatch, N, 512), v.reshape(nbatch, N, 512), has_ctx)

        o_na = _na_attn(p_na.reshape(nbatch, N, NA_W), na_tabs, has_ctx, l)

        yf, yb = _s5_scan(p_s5.reshape(nbatch, N, S5_W), s5_bmat, s5_avec, s5_cmat, l)
        o_s5 = _s5_glu(p_s5, yf.reshape(nbatch * N, S5_W), yb.reshape(nbatch * N, S5_W), s5_d, s5_w_glu, s5_b_glu, l)

        xbc, dtt, bmt = _m2_prep(p_m2.reshape(nbatch, N, M2_W), m2_conv_w, m2_conv_b, dt_bias, l)
        ssd_f, ssd_b = _ssd(xbc, dtt, bmt, a_col, l)
        o_m2 = _m2_norm(xbc.reshape(nbatch * N, 768), p_m2, ssd_f.reshape(nbatch * N, 256),
                        ssd_b.reshape(nbatch * N, 256), m2_dvec, m2_g_norm, l)

        x1, h2, aff = _merge(stream, o_mla.reshape(-1, 256), o_na.reshape(-1, 256), o_s5, o_m2, p_gate,
                             w_branch, w_out, g_post_mix, mod3, g_pre_ffn, w_router_t, nbatch, nb, l)

        slot_lat = _topk(aff, 0, LAT_BLKS, CAP_LAT)
        slot_ctx = _topk(aff, LAT_BLKS, 1, CAP_CTX) if has_ctx else None
        cnt = jnp.sum((slot_lat >= 0).astype(jnp.int32), axis=-1)
        first = jnp.cumsum(cnt, axis=1) - cnt
        win0 = jnp.minimum((first // 16) * 16, CAP_LAT - MOE_WIN)
        fits = jnp.all(first + cnt <= win0 + MOE_WIN, axis=(1, 2)).astype(jnp.int32)
        win0 = win0.reshape(-1)
        ys = _experts(win0, fits, h2.reshape(nbatch, nb * ROW_BLK, D), slot_lat, slot_ctx, aff, w_gate, w_up, w_down,
                      l, has_ctx)
        scol_lat = jnp.transpose(slot_lat, (0, 1, 3, 2)).reshape(nbatch, T, N_EXPERTS)
        scol_ctx = jnp.transpose(slot_ctx, (0, 1, 3, 2)).reshape(nbatch, LC, N_EXPERTS) if has_ctx else None
        xs = _combine(win0, fits, x1.reshape(nbatch, nb * ROW_BLK, D), ys[0], scol_lat, ys[1] if has_ctx else None,
                      scol_ctx, mod3, g_post_ffn, has_ctx, l)
        stream = (xs, xs, LAT_BLKS)
    return xs
```

```python
import functools
import math

import numpy as np
import jax
import jax.numpy as jnp
from jax import lax
from jax.experimental import pallas as pl
from jax.experimental.pallas import tpu as pltpu

F32 = jnp.float32
BF16 = jnp.bfloat16

D = 1024
T = 2048
LC = 256
N = T + LC
GRID_W = 64
ROW_BLK = 256
NBLK = N // ROW_BLK
LAT_BLKS = T // ROW_BLK
EPS = 1e-6
N_EXPERTS = 16
CAP_LAT = 2 * T // N_EXPERTS
CAP_CTX = 2 * LC // N_EXPERTS
ROPE_THETA = 10000.0
NEG = -1e30

MLA_W = 512
NA_W = 768
S5_W = 256
M2_W = 1152
GATE_W = 4096

MOE_WIN = 80
EXP_SB = 2
S5_CHUNK = 64
SSD_CHUNK = 128
SSD_SB = 8
NA_SB = 2

VMEM_LIMIT = 56 * 1024 * 1024
VMEM_LIMIT_EXPERTS = 60 * 1024 * 1024


def _cp(*sem, vmem=VMEM_LIMIT):
    return pltpu.CompilerParams(dimension_semantics=sem, vmem_limit_bytes=vmem)


def _lspec(a, l, *lead):
    nd = a.ndim - 1 - len(lead)
    return pl.BlockSpec((None,) * (1 + len(lead)) + a.shape[1 + len(lead):], lambda *_: (l,) + lead + (0,) * nd)


def _dot(a, b):
    return jnp.dot(a.astype(BF16), b.astype(BF16), preferred_element_type=F32)


def _dot_nt(a, b):
    return lax.dot_general(a.astype(BF16), b.astype(BF16), (((1,), (1,)), ((), ())),
                           preferred_element_type=F32)


def _split3(a):
    hi = a.astype(BF16)
    r = a - hi.astype(F32)
    mid = r.astype(BF16)
    lo = (r - mid.astype(F32)).astype(BF16)
    return hi, mid, lo


def _dot_hi(a, b):
    ah, am, _ = _split3(a)
    bh, bm, _ = _split3(b)
    f = lambda x, y: jnp.dot(x, y, preferred_element_type=F32)
    return f(ah, bh) + (f(ah, bm) + f(am, bh))


def _dot_hi_nt(a, b):
    ah, am, _ = _split3(a)
    bh, bm, _ = _split3(b)
    f = lambda x, y: lax.dot_general(x, y, (((1,), (1,)), ((), ())), preferred_element_type=F32)
    return f(ah, bh) + (f(ah, bm) + f(am, bh))


def _dot_exact_lhs(m_bf16, a):
    h, m, l = _split3(a)
    f = lambda y: jnp.dot(m_bf16, y, preferred_element_type=F32)
    return f(h) + (f(m) + f(l))


def _dot_exact_rhs(a, m_bf16):
    h, m, l = _split3(a)
    f = lambda y: jnp.dot(y, m_bf16, preferred_element_type=F32)
    return f(h) + (f(m) + f(l))


def _sigmoid(x):
    return 0.5 * jnp.tanh(0.5 * x) + 0.5


def _silu(x):
    return x * _sigmoid(x)


def _rms(x, g):
    return x * lax.rsqrt(jnp.mean(x * x, axis=-1, keepdims=True) + EPS) * g


def _mod_row(nb):
    def f(i, nbatch):
        return jnp.where(i % nb == LAT_BLKS, nbatch, i // nb)
    return f


def _flat_blk(i, nb):
    return (i // nb) * NBLK + i % nb


def _stream_specs(stream, nb):
    lat, ctxa, cblk = stream
    w = lat.shape[-1]
    return [lat, ctxa], [pl.BlockSpec((1, ROW_BLK, w), lambda i: (i // nb, jnp.minimum(i % nb, LAT_BLKS - 1), 0)),
                         pl.BlockSpec((1, ROW_BLK, w), lambda i: (i // nb, cblk, 0))]


def _stream_block(xl_ref, xc_ref, nb):
    return jnp.where(pl.program_id(0) % nb == LAT_BLKS, xc_ref[0], xl_ref[0])


def _ada_kernel(c_ref, w_ref, b_ref, o_ref):
    c = c_ref[...]
    o_ref[...] = _dot_hi(_silu(c), w_ref[0]) + b_ref[0]


def _ada(cvec, w_ada, b_ada3, l):
    rows = cvec.shape[0]
    return pl.pallas_call(
        _ada_kernel,
        grid=(6,),
        in_specs=[pl.BlockSpec((rows, D), lambda k: (0, 0)),
                  pl.BlockSpec((1, D, D), lambda k: (l, 0, k)),
                  pl.BlockSpec((1, 1, D), lambda k: (l, 0, k))],
        out_specs=pl.BlockSpec((rows, D), lambda k: (0, k)),
        out_shape=jax.ShapeDtypeStruct((rows, 6 * D), F32),
        compiler_params=_cp("arbitrary"),
        name="ada",
    )(cvec, w_ada, b_ada3)


def _inproj_kernel(xl_ref, xc_ref, g_ref, sh_ref, sc_ref, wm, wn, ws, w2, wg, om, on, os_, o2, og, *, ctx_gates):
    x = _stream_block(xl_ref, xc_ref, NBLK)
    h = (_rms(x, g_ref[...]) * (1.0 + sc_ref[0]) + sh_ref[0]).astype(BF16)
    om[...] = jnp.dot(h, wm[...], preferred_element_type=F32)
    on[...] = jnp.dot(h, wn[...], preferred_element_type=F32).astype(BF16)
    os_[...] = jnp.dot(h, ws[...], preferred_element_type=F32)
    o2[...] = jnp.dot(h, w2[...], preferred_element_type=F32)
    if ctx_gates:
        og[...] = jnp.dot(h, wg[...], preferred_element_type=F32)
    else:
        is_ctx = pl.program_id(0) % NBLK == LAT_BLKS

        @pl.when(jnp.logical_not(is_ctx))
        def _():
            og[...] = jnp.dot(h, wg[...], preferred_element_type=F32)

        @pl.when(is_ctx)
        def _():
            og[...] = jnp.zeros_like(og)


def _inproj(stream, g, mod3, ws, nbatch, l, ctx_gates):
    rows = nbatch * N
    nblk = rows // ROW_BLK
    mrow = _mod_row(NBLK)
    xargs, xspecs = _stream_specs(stream, NBLK)
    full = lambda w: _lspec(w, l)
    widths = (MLA_W, NA_W, S5_W, M2_W, GATE_W)
    dts = (F32, BF16, F32, F32, F32)
    return pl.pallas_call(
        functools.partial(_inproj_kernel, ctx_gates=ctx_gates),
        grid=(nblk,),
        in_specs=xspecs + [
                  full(g),
                  pl.BlockSpec((1, 1, D), lambda i: (mrow(i, nbatch), 0, 0)),
                  pl.BlockSpec((1, 1, D), lambda i: (mrow(i, nbatch), 0, 1))]
                 + [full(w) for w in ws],
        out_specs=[pl.BlockSpec((ROW_BLK, w), lambda i: (i, 0)) for w in widths],
        out_shape=[jax.ShapeDtypeStruct((rows, w), dt) for w, dt in zip(widths, dts)],
        compiler_params=_cp("arbitrary"),
        name="inproj",
    )(*xargs, g, mod3, mod3, *ws)


def _mla_prep_kernel(p_ref, gq_ref, gkv_ref, wq_ref, wqr_ref, wk_ref, wv_ref, e_ref, vone_ref,
                     cos_ref, sin_ref, ck_ref, sk_ref, q_out, k_out, v_out):
    p = p_ref[...]
    cqn = _rms(p[:, :256], gq_ref[...]).astype(BF16)
    q = jnp.dot(cqn, wq_ref[...], preferred_element_type=F32)
    qr = jnp.dot(cqn, wqr_ref[...], preferred_element_type=F32)
    q_out[...] = (q * cos_ref[...] + qr * sin_ref[...]).astype(BF16)
    ckvn = _rms(p[:, 256:384], gkv_ref[...]).astype(BF16)
    kro = (p[:, 384:416] * ck_ref[...] + p[:, 416:448] * sk_ref[...]).astype(BF16)
    k = jnp.dot(ckvn, wk_ref[...], preferred_element_type=F32) + jnp.dot(kro, e_ref[...], preferred_element_type=F32)
    k_out[...] = k.astype(BF16)
    v_out[...] = (jnp.dot(ckvn, wv_ref[...], preferred_element_type=F32) + vone_ref[...]).astype(BF16)


def _mla_prep(p_mla, gq, gkv, wq, wqr, wk, wv, e, vone, cos_q, sin_q, cos_k, sin_k, l):
    rows = p_mla.shape[0]
    full = lambda w: _lspec(w, l)
    const = lambda w: pl.BlockSpec(w.shape, lambda j, b: (0, 0))
    rb = 3 * ROW_BLK
    row = lambda j, b: (b * (N // rb) + j, 0)
    tab = lambda w: pl.BlockSpec((rb, w), lambda j, b: (j, 0))
    return pl.pallas_call(
        _mla_prep_kernel,
        grid=(N // rb, rows // N),
        in_specs=[pl.BlockSpec((rb, MLA_W), row),
                  full(gq), full(gkv), full(wq), full(wqr), full(wk), full(wv), const(e), const(vone),
                  tab(512), tab(512), tab(32), tab(32)],
        out_specs=[pl.BlockSpec((rb, 512), row)] * 3,
        out_shape=[jax.ShapeDtypeStruct((rows, 512), BF16)] * 3,
        compiler_params=_cp("arbitrary", "arbitrary"),
        name="mla_prep",
    )(p_mla, gq, gkv, wq, wqr, wk, wv, e, vone, cos_q, sin_q, cos_k, sin_k)


def _mla_attn_kernel(q_ref, k_ref, v_ref, o_ref):
    lane = lax.broadcasted_iota(jnp.int32, (q_ref.shape[1], 128), 1)
    for hp in range(2):
        pv = []
        for h in (2 * hp, 2 * hp + 1):
            s = _dot_nt(q_ref[0, :, 128 * h:128 * h + 128], k_ref[0, :, 128 * h:128 * h + 128])
            p = jnp.exp2(s - jnp.max(s, axis=-1, keepdims=True)).astype(BF16)
            pv.append(jnp.dot(p, v_ref[0, :, 128 * h:128 * h + 128], preferred_element_type=F32))
        oa, ob = pv
        o = jnp.where(lane < 64, oa * (1.0 / oa[:, 64:65]), ob * (1.0 / ob[:, 0:1]))
        o_ref[0, :, 128 * hp:128 * hp + 128] = o.astype(o_ref.dtype)


def _mla_attn(q, k, v, has_ctx):
    nbatch = q.shape[0]
    qb = 2 * ROW_BLK

    def call(grid, qspec, kvspec, ospec, ntok):
        return pl.pallas_call(
            _mla_attn_kernel,
            grid=grid,
            in_specs=[qspec, kvspec, kvspec],
            out_specs=ospec,
            out_shape=jax.ShapeDtypeStruct((nbatch, ntok, 256), BF16),
            compiler_params=_cp("arbitrary", "arbitrary"),
            name="mla_attn",
        )(q, k, v)

    o_lat = call((nbatch, T // qb), pl.BlockSpec((1, qb, 512), lambda b, j: (b, j, 0)),
                 pl.BlockSpec((1, N, 512), lambda b, j: (b, 0, 0)),
                 pl.BlockSpec((1, qb, 256), lambda b, j: (b, j, 0)), T)
    if not has_ctx:
        return o_lat, o_lat
    ctx_blk = pl.BlockSpec((1, LC, 512), lambda b, j: (b, LAT_BLKS, 0))
    o_ctx = call((nbatch, 1), ctx_blk, ctx_blk, pl.BlockSpec((1, LC, 256), lambda b, j: (b, 0, 0)), LC)
    return o_lat, o_ctx


def _na_kernel(q_ref, kc_ref, k0_ref, k1_ref, k2_ref, vc_ref, v0_ref, v1_ref, v2_ref, b_ref, o_ref, *, has_ctx):
    scale = jnp.asarray(0.125, BF16)

    lane = lax.broadcasted_iota(jnp.int32, (ROW_BLK, 128), 1)
    lane64 = lax.broadcasted_iota(jnp.int32, (GRID_W, 128), 1) < 64
    grp = jnp.minimum(pl.program_id(0), LAT_BLKS - 1)
    krow0 = 4 * jnp.clip(grp - 1, 0, LAT_BLKS - 3)

    def bias_piece(h, i):
        rows = []
        for qr in range(4):
            r = 4 * grp + qr
            rs = jnp.clip(r - 4, 0, 24)
            tiles = []
            for pr in range(2):
                kr = krow0 + (4 * i + 2 * pr)
                t = b_ref[h, jnp.clip(kr - r + 8, 0, 15)]
                ok_a = jnp.logical_and(kr >= rs, kr < rs + 8).astype(jnp.int32)
                ok_b = jnp.logical_and(kr + 1 >= rs, kr + 1 < rs + 8).astype(jnp.int32)
                tiles.append(jnp.where(jnp.where(lane64, ok_a, ok_b) > 0, t, NEG))
            rows.append(jnp.concatenate(tiles, axis=1))
        return jnp.concatenate(rows, axis=0)

    def heads(win, sm):
        kws = (k0_ref, k1_ref, k2_ref)
        vws = (v0_ref, v1_ref, v2_ref)
        for hp in range(2):
            sl = slice(128 * hp, 128 * hp + 128)
            qp = q_ref[sm, :, sl] * scale
            outs = []
            for hh in range(2):
                h = 2 * hp + hh
                qh = jnp.where((lane < 64) if hh == 0 else (lane >= 64), qp, jnp.zeros_like(qp))
                s_c = _dot_nt(qh, kc_ref[sm, :, sl])
                s_w = []
                smax = s_c
                if win:
                    for i in range(3):
                        s = _dot_nt(qh, kws[i][sm, :, sl]) + bias[h][i]
                        s_w.append(s)
                        smax = jnp.maximum(smax, s)
                m = jnp.max(smax, axis=-1, keepdims=True)
                p = jnp.exp(s_c - m)
                psum = p
                o = _dot(p, vc_ref[sm, :, sl])
                for i, s in enumerate(s_w):
                    p = jnp.exp(s - m)
                    psum = psum + p
                    o = o + _dot(p, vws[i][sm, :, sl])
                outs.append(o * (1.0 / jnp.sum(psum, axis=-1, keepdims=True)))
            o_ref[sm, :, sl] = jnp.where(lane < 64, outs[0], outs[1]).astype(o_ref.dtype)

    def windowed():
        nonlocal bias
        bias = [[bias_piece(h, i) for i in range(3)] for h in range(4)]
        for sm in range(NA_SB):
            heads(True, sm)

    bias = None
    if has_ctx:
        g = pl.program_id(0)

        @pl.when(g < LAT_BLKS)
        def _():
            windowed()

        @pl.when(g == LAT_BLKS)
        def _():
            for sm in range(NA_SB):
                heads(False, sm)
    else:
        windowed()


def _na_attn(qkv, bias, has_ctx, l):
    nbatch = qkv.shape[0]
    ng = NBLK if has_ctx else LAT_BLKS
    j0 = lambda g: jnp.clip(g - 1, 0, LAT_BLKS - 3)
    blk = lambda f: pl.BlockSpec((NA_SB, ROW_BLK, 256), f)
    return pl.pallas_call(
        functools.partial(_na_kernel, has_ctx=has_ctx),
        grid=(ng, nbatch // NA_SB),
        in_specs=[blk(lambda g, b: (b, g, 0)),
                  blk(lambda g, b: (b, LAT_BLKS, 1)),
                  blk(lambda g, b: (b, j0(g), 1)),
                  blk(lambda g, b: (b, j0(g) + 1, 1)),
                  blk(lambda g, b: (b, j0(g) + 2, 1)),
                  blk(lambda g, b: (b, LAT_BLKS, 2)),
                  blk(lambda g, b: (b, j0(g), 2)),
                  blk(lambda g, b: (b, j0(g) + 1, 2)),
                  blk(lambda g, b: (b, j0(g) + 2, 2)),
                  _lspec(bias, l)],
        out_specs=blk(lambda g, b: (b, g, 0)),
        out_shape=jax.ShapeDtypeStruct((nbatch, ng * ROW_BLK, 256), BF16),
        compiler_params=_cp("arbitrary", "arbitrary"),
        name="na_attn",
    )(qkv, qkv, qkv, qkv, qkv, qkv, qkv, qkv, qkv, bias)


def _na_col_structure():
    onehot = np.zeros((31, 64, 64), np.float32)
    colmask = np.zeros((64, 64), bool)
    for c in range(64):
        cs = min(max(c - 8, 0), 48)
        for kc in range(cs, cs + 16):
            onehot[kc - c + 15, c, kc] = 1.0
            colmask[c, kc] = True
    return np.tile(onehot.reshape(31, 4096), (3, 1)), colmask


_NA_COL_ONEHOT, _NA_COLMASK = _na_col_structure()


def _na_tables(rpb):
    nl = rpb.shape[0]
    r = rpb.astype(F32)
    hi = lax.reduce_precision(r, 8, 7)
    mid = lax.reduce_precision(r - hi, 8, 7)
    parts = jnp.stack([hi, mid, r - hi - mid], axis=3).reshape(nl * 60, 93)
    tz = jnp.dot(parts, _NA_COL_ONEHOT, preferred_element_type=F32).reshape(nl, 4, 15, 64, 64)
    tz = jnp.where(_NA_COLMASK, tz, NEG)
    edge = jnp.full((nl, 4, 1, 64, 64), NEG, F32)
    tz17 = jnp.concatenate([edge, tz, edge], axis=2)
    return jnp.concatenate([tz17[:, :, :16], tz17[:, :, 1:]], axis=-1)


def _s5_kernel(uf_ref, ub_ref, bf_ref, bb_ref, af_ref, ab_ref, cf_ref, cb_ref, yf_ref, yb_ref,
               hf, hb, buf_f, buf_b, tm_f, tm_b):
    i = pl.program_id(0)
    half = 1024
    nb = uf_ref.shape[0]

    @pl.when(i == 0)
    def _():
        hf[...] = jnp.zeros_like(hf)
        hb[...] = jnp.zeros_like(hb)

    def expand(u_ref, tm, b_ref, buf):
        for b in range(nb):
            for c in range(2):
                tm[c, pl.ds(b, S5_CHUNK, stride=nb), :] = u_ref[b, :, 128 * c:128 * c + 128]
        buf[...] = _dot(jnp.concatenate([tm[0], tm[1]], axis=1), b_ref[...])

    def scan(buf, a_ref, h, order):
        hr, hi = h[:, :half], h[:, half:]
        ar, ai = a_ref[:, :half], a_ref[:, half:]
        for k in order:
            rows = slice(k * nb, (k + 1) * nb)
            nr = ar * hr - ai * hi + buf[rows, :half]
            ni = ar * hi + ai * hr + buf[rows, half:]
            buf[rows, :half] = nr
            buf[rows, half:] = ni
            hr, hi = nr, ni
        h[:, :half] = hr
        h[:, half:] = hi

    def readout(buf, c_ref, tm, y_ref):
        y = _dot(buf[...], c_ref[...])
        for c in range(2):
            tm[c] = y[:, 128 * c:128 * c + 128]
        for b in range(nb):
            for c in range(2):
                y_ref[b, :, 128 * c:128 * c + 128] = tm[c, pl.ds(b, S5_CHUNK, stride=nb), :]

    expand(uf_ref, tm_f, bf_ref, buf_f)
    expand(ub_ref, tm_b, bb_ref, buf_b)
    scan(buf_f, af_ref, hf, range(S5_CHUNK))
    readout(buf_f, cf_ref, tm_f, yf_ref)
    scan(buf_b, ab_ref, hb, range(S5_CHUNK - 1, -1, -1))
    readout(buf_b, cb_ref, tm_b, yb_ref)


def _s5_scan(u, bmat, avec, cmat, l):
    nbatch = u.shape[0]
    cr = S5_CHUNK * nbatch
    nch = N // S5_CHUNK
    nctx = LC // S5_CHUNK
    fidx = lambda i: jnp.where(i < nctx, nch - nctx + i, i - nctx)
    bidx = lambda i: nch - 1 - i
    ublk = lambda f: pl.BlockSpec((nbatch, S5_CHUNK, 256), lambda i: (0, f(i), 0))
    return pl.pallas_call(
        _s5_kernel,
        grid=(nch,),
        in_specs=[ublk(fidx), ublk(bidx),
                  _lspec(bmat, l, 0), _lspec(bmat, l, 1), _lspec(avec, l, 0), _lspec(avec, l, 1),
                  _lspec(cmat, l, 0), _lspec(cmat, l, 1)],
        out_specs=[ublk(fidx), ublk(bidx)],
        out_shape=[jax.ShapeDtypeStruct((nbatch, N, 256), F32)] * 2,
        scratch_shapes=[pltpu.VMEM((nbatch, 2048), F32), pltpu.VMEM((nbatch, 2048), F32),
                        pltpu.VMEM((cr, 2048), F32), pltpu.VMEM((cr, 2048), F32),
                        pltpu.VMEM((2, cr, 128), F32), pltpu.VMEM((2, cr, 128), F32)],
        compiler_params=_cp("arbitrary"),
        name="s5_scan",
    )(u, u, bmat, bmat, avec, avec, cmat, cmat)


def _s5_glu_kernel(u_ref, yf_ref, yb_ref, d_ref, w_ref, b_ref, o_ref):
    y = d_ref[...] * u_ref[...] + yf_ref[...] + yb_ref[...]
    z = y * (0.5 * (1.0 + jnp.tanh(math.sqrt(2.0 / math.pi) * (y + 0.044715 * (y * y * y)))))
    o_ref[...] = (z * _sigmoid(_dot(z, w_ref[...]) + b_ref[...])).astype(o_ref.dtype)


def _s5_glu(u_tb, yf, yb, d, w, b, l):
    rows = u_tb.shape[0]
    rb = 4 * ROW_BLK
    blk = pl.BlockSpec((rb, 256), lambda i: (i, 0))
    full = lambda a: _lspec(a, l)
    return pl.pallas_call(
        _s5_glu_kernel,
        grid=(rows // rb,),
        in_specs=[blk, blk, blk, full(d), full(w), full(b)],
        out_specs=blk,
        out_shape=jax.ShapeDtypeStruct((rows, 256), BF16),
        compiler_params=_cp("arbitrary"),
        name="s5_glu",
    )(u_tb, yf, yb, d, w, b)


def _s5_params(a_re, a_im, log_step, b_re, b_im, c_re, c_im, nbatch):
    nl = a_re.shape[0]
    same_group = (np.arange(256)[:, None] // 16 == np.arange(1024)[None, :] // 64).astype(np.float32)
    a = lax.complex(a_re.astype(F32), a_im.astype(F32))
    abar = jnp.exp(jnp.exp(log_step.astype(F32))[..., None] * a)
    bbar = ((abar - 1.0) / a)[..., None] * lax.complex(b_re.astype(F32), b_im.astype(F32))
    blk_in = lambda m: jnp.tile(jnp.swapaxes(m, -1, -2).reshape(nl, 2, 256, 64), (1, 1, 1, 16)) * same_group
    bmat = jnp.concatenate([blk_in(jnp.real(bbar)), blk_in(jnp.imag(bbar))], axis=-1).astype(BF16)
    avec = jnp.concatenate([jnp.real(abar).reshape(nl, 2, 1, 1024), jnp.imag(abar).reshape(nl, 2, 1, 1024)], axis=-1)
    avec = jnp.broadcast_to(avec, (nl, 2, nbatch, 2048))
    blk_out = lambda m: jnp.tile(jnp.swapaxes(m, -1, -2).reshape(nl, 2, 1024, 16), (1, 1, 1, 16)) * same_group.T
    cmat = jnp.concatenate([blk_out(c_re.astype(F32)), -blk_out(c_im.astype(F32))], axis=-2).astype(BF16)
    return bmat, avec, cmat


def _m2_prep_kernel(x_ref, dt_ref, w_ref, b_ref, dtb_ref, xo_ref, dtt_ref, bmt_ref):
    c = pl.program_id(1)
    x = x_ref[0]
    t = lax.broadcasted_iota(jnp.int32, x.shape, 0)
    m2 = ((t >= 2) & (t < T)) | (t >= T + 2)
    m1 = ((t >= 1) & (t < T)) | (t >= T + 1)
    p1 = (t <= T - 2) | ((t >= T) & (t <= N - 2))
    w = w_ref[...]
    y = (w[0:1] * jnp.where(m2, pltpu.roll(x, 2, 0), 0.0)
         + w[1:2] * jnp.where(m1, pltpu.roll(x, 1, 0), 0.0)
         + w[2:3] * x
         + w[3:4] * jnp.where(p1, pltpu.roll(x, N - 1, 0), 0.0)) + b_ref[...]
    act = _silu(y)
    xo_ref[0] = act

    @pl.when(c == 0)
    def _():
        v = dt_ref[0] + dtb_ref[...]
        dtt_ref[0] = (jnp.maximum(v, 0.0) + jnp.log1p(jnp.exp(-jnp.abs(v)))).T

    @pl.when(c == 1)
    def _():
        bmt_ref[0] = act.T.astype(BF16)


def _m2_prep(p_m2, conv_w, conv_b, dtb, l):
    nbatch = p_m2.shape[0]
    return pl.pallas_call(
        _m2_prep_kernel,
        grid=(nbatch, 3),
        in_specs=[pl.BlockSpec((1, N, 256), lambda b, c: (b, 0, 1 + c)),
                  pl.BlockSpec((1, N, 128), lambda b, c: (b, 0, 8)),
                  pl.BlockSpec((None, 4, 256), lambda b, c: (l, 0, c)),
                  pl.BlockSpec((None, 1, 256), lambda b, c: (l, 0, c)),
                  _lspec(dtb, l)],
        out_specs=[pl.BlockSpec((1, N, 256), lambda b, c: (b, 0, c)),
                   pl.BlockSpec((1, 128, N), lambda b, c: (b, 0, 0)),
                   pl.BlockSpec((1, 256, N), lambda b, c: (b, 0, 0))],
        out_shape=[jax.ShapeDtypeStruct((nbatch, N, 768), F32),
                   jax.ShapeDtypeStruct((nbatch, 128, N), F32),
                   jax.ShapeDtypeStruct((nbatch, 256, N), BF16)],
        compiler_params=_cp("arbitrary", "arbitrary"),
        name="m2_prep",
    )(p_m2, p_m2, conv_w, conv_b, dtb)


def _ssd_kernel(xf_ref, dttf_ref, bmtf_ref, xb_ref, dttb_ref, bmtb_ref, acol_ref, yf_ref, yb_ref, hs):
    i = pl.program_id(1)
    L = SSD_CHUNK

    @pl.when(i == 0)
    def _():
        hs[...] = jnp.zeros_like(hs)

    li = lax.broadcasted_iota(jnp.int32, (L, L), 0)
    si = lax.broadcasted_iota(jnp.int32, (L, L), 1)
    left = si < 64
    pick = lambda a, b: jnp.where(left, a, b)

    def direction(sm, x_ref, dtt_ref, bmt_ref, y_ref, d, causal, causal_t, last):
        cz = jnp.where(causal, 1.0, 0.0).astype(BF16)
        czt = jnp.where(causal_t, 1.0, 0.0).astype(BF16)
        xbc = x_ref[sm]
        dtt = dtt_ref[sm]
        parts = jnp.concatenate([p.astype(F32) for p in _split3(dtt * acol_ref[...])], axis=1)
        cumt = jnp.dot(parts.astype(BF16), jnp.concatenate([czt, czt, czt], axis=0),
                       preferred_element_type=F32)
        reps = jnp.concatenate([jnp.broadcast_to(parts[4 * d + h:4 * d + h + 1, :], (L, 3 * L)) for h in range(4)],
                               axis=0).astype(BF16)
        cumb = lax.dot_general(jnp.concatenate([cz, cz, cz], axis=1), reps, (((1,), (1,)), ((), ())),
                               preferred_element_type=F32)
        for g in range(2):
            xp = xbc[:, 128 * g:128 * g + 128].astype(BF16)
            cm = xbc[:, 512 + 128 * g:640 + 128 * g].astype(BF16)
            bmt = bmt_ref[sm, 128 * g:128 * g + 128, :]
            hprev = hs[sm, d, g]
            go = jnp.dot(cm, jnp.concatenate([bmt, hprev.astype(BF16)], axis=1), preferred_element_type=F32)
            gmat, yo = go[:, :L], go[:, L:]
            bmf = bmt.astype(F32)
            lhs, cbs, cls = [], [], []
            for hh in range(2):
                h = 2 * g + hh
                cb = cumb[:, L * h:L * h + L]
                crow = cumt[4 * d + h:4 * d + h + 1, :]
                dtrow = dtt[4 * d + h:4 * d + h + 1, :]
                cl = cb[last:last + 1, :]
                lhs.append((gmat * jnp.where(causal, jnp.exp(cb - crow), 0.0) * dtrow).astype(BF16))
                lhs.append((bmf * (jnp.exp(cl - crow) * dtrow)).astype(BF16))
                cbs.append(cb)
                cls.append(cl)
            big = jnp.dot(jnp.concatenate(lhs, axis=0), xp, preferred_element_type=F32)
            y_ref[sm, :, 128 * g:128 * g + 128] = pick(big[0:L], big[2 * L:3 * L]) + yo * jnp.exp(pick(cbs[0], cbs[1]))
            hs[sm, d, g] = hprev * jnp.exp(pick(cls[0], cls[1])) + pick(big[L:2 * L], big[3 * L:4 * L])

    for sm in range(SSD_SB):
        direction(sm, xf_ref, dttf_ref, bmtf_ref, yf_ref, 0, si <= li, li <= si, L - 1)
        direction(sm, xb_ref, dttb_ref, bmtb_ref, yb_ref, 1, si >= li, li >= si, 0)


def _ssd(xbc, dtt, bmt, a_col, l):
    nbatch = xbc.shape[0]
    nch = N // SSD_CHUNK
    nctx = LC // SSD_CHUNK
    fidx = lambda i: jnp.where(i < nctx, nch - nctx + i, i - nctx)
    bidx = lambda i: nch - 1 - i
    ins = lambda f: [pl.BlockSpec((SSD_SB, SSD_CHUNK, 768), lambda b, i: (b, f(i), 0)),
                     pl.BlockSpec((SSD_SB, 128, SSD_CHUNK), lambda b, i: (b, 0, f(i))),
                     pl.BlockSpec((SSD_SB, 256, SSD_CHUNK), lambda b, i: (b, 0, f(i)))]
    return pl.pallas_call(
        _ssd_kernel,
        grid=(nbatch // SSD_SB, nch),
        in_specs=ins(fidx) + ins(bidx) + [_lspec(a_col, l)],
        out_specs=[pl.BlockSpec((SSD_SB, SSD_CHUNK, 256), lambda b, i: (b, fidx(i), 0)),
                   pl.BlockSpec((SSD_SB, SSD_CHUNK, 256), lambda b, i: (b, bidx(i), 0))],
        out_shape=[jax.ShapeDtypeStruct((nbatch, N, 256), F32)] * 2,
        scratch_shapes=[pltpu.VMEM((SSD_SB, 2, 2, 128, 128), F32)],
        compiler_params=_cp("arbitrary", "arbitrary"),
        name="ssd",
    )(xbc, dtt, bmt, xbc, dtt, bmt, a_col)


def _m2_norm_kernel(x_ref, z_ref, yf_ref, yb_ref, d_ref, g_ref, o_ref):
    y = d_ref[...] * x_ref[...] + yf_ref[...] + yb_ref[...]
    o_ref[...] = _rms(y * _silu(z_ref[...]), g_ref[...]).astype(o_ref.dtype)


def _m2_norm(xbc2d, p_m2_2d, yf2d, yb2d, dvec, gn, l):
    rows = xbc2d.shape[0]
    rb = 4 * ROW_BLK
    blk = pl.BlockSpec((rb, 256), lambda i: (i, 0))
    full = lambda a: _lspec(a, l)
    return pl.pallas_call(
        _m2_norm_kernel,
        grid=(rows // rb,),
        in_specs=[blk, blk, blk, blk, full(dvec), full(gn)],
        out_specs=blk,
        out_shape=jax.ShapeDtypeStruct((rows, 256), BF16),
        compiler_params=_cp("arbitrary"),
        name="m2_norm",
    )(xbc2d, p_m2_2d, yf2d, yb2d, dvec, gn)


def _merge_kernel(xl_ref, xc_ref, oal_ref, oac_ref, on_ref, os_ref, om_ref, gl_ref, wb_ref, wo_ref, gpm_ref, gate1_ref,
                  gpf_ref, sh2_ref, sc2_ref, wr_ref, x1_ref, h2_ref, aff_ref, *, nb):
    outs = (_stream_block(oal_ref, oac_ref, nb), on_ref[...], os_ref[...], om_ref[...])
    y = None
    for j in range(4):
        d = jnp.dot(outs[j], wb_ref[j], preferred_element_type=F32)
        t = d * jnp.tanh(gl_ref[:, 1024 * j:1024 * j + 1024]) + d
        y = t if y is None else y + t
    y2 = _dot(y, wo_ref[...])
    x1 = _stream_block(xl_ref, xc_ref, nb) + gate1_ref[0] * _rms(y2, gpm_ref[...])
    x1_ref[...] = x1
    h2 = _rms(x1, gpf_ref[...]) * (1.0 + sc2_ref[0]) + sh2_ref[0]
    h2_ref[...] = h2.astype(BF16)
    logits = _dot_hi_nt(wr_ref[...], h2)
    m = jnp.max(logits, axis=0, keepdims=True)
    e = jnp.exp(logits - m)
    aff_ref[0, 0] = e / jnp.sum(e, axis=0, keepdims=True)


def _merge(stream, o_mla, o_na, o_s5, o_m2, gl, wb, wo, gpm, mod3, gpf, wrt, nbatch, nb, l):
    mrow = _mod_row(nb)
    xargs, xspecs = _stream_specs(stream, nb)
    aargs, aspecs = _stream_specs((o_mla[0], o_mla[1], 0), nb)
    fb = lambda i: (_flat_blk(i, nb), 0)
    blk = lambda w: pl.BlockSpec((ROW_BLK, w), fb)
    oblk = lambda w: pl.BlockSpec((ROW_BLK, w), lambda i: (i, 0))
    full = lambda a: _lspec(a, l)
    modspec = lambda k: pl.BlockSpec((1, 1, D), lambda i: (mrow(i, nbatch), 0, k))
    orows = nbatch * nb * ROW_BLK
    return pl.pallas_call(
        functools.partial(_merge_kernel, nb=nb),
        grid=(nbatch * nb,),
        in_specs=xspecs + aspecs + [oblk(256), blk(256), blk(256), blk(GATE_W),
                  full(wb), full(wo), full(gpm), modspec(2), full(gpf), modspec(3), modspec(4), full(wrt)],
        out_specs=[oblk(D), oblk(D),
                   pl.BlockSpec((1, 1, N_EXPERTS, ROW_BLK), lambda i: (i // nb, i % nb, 0, 0))],
        out_shape=[jax.ShapeDtypeStruct((orows, D), F32),
                   jax.ShapeDtypeStruct((orows, D), BF16),
                   jax.ShapeDtypeStruct((nbatch, nb, N_EXPERTS, ROW_BLK), F32)],
        compiler_params=_cp("arbitrary"),
        name="merge",
    )(*xargs, *aargs, o_na, o_s5, o_m2, gl, wb, wo, gpm, mod3, gpf, mod3, mod3, wrt)


def _topk_kernel(aff_ref, slot_ref, *, nk, cap):
    a = aff_ref[0]
    bits = lax.bitcast_convert_type(a, jnp.int32)
    count = lambda m: jnp.sum(jnp.sum(jnp.where(m, 1.0, 0.0), axis=2, keepdims=True), axis=0, keepdims=True)
    thr = jnp.where(count(bits >= (1 << 30)) >= cap, jnp.int32(1 << 30), jnp.zeros((1, N_EXPERTS, 1), jnp.int32))
    for bit in range(29, 0, -2):
        hi, lo = 1 << bit, 1 << (bit - 1)
        ok = [count(bits >= (thr | c)) >= cap for c in (hi | lo, hi, lo)]
        thr = thr | jnp.where(ok[0], hi | lo, jnp.where(ok[1], hi, jnp.where(ok[2], lo, 0)))
    gt = bits > thr
    eq = bits == thr
    need = cap - count(gt)

    tri = jnp.where(lax.broadcasted_iota(jnp.int32, (256, 256), 0) <= lax.broadcasted_iota(jnp.int32, (256, 256), 1),
                    1.0, 0.0).astype(BF16)

    def prefix_excl(m):
        incl = jnp.dot(m.reshape(nk * N_EXPERTS, 256).astype(BF16), tri,
                       preferred_element_type=F32).reshape(nk, N_EXPERTS, 256)
        offs = []
        run = jnp.zeros((1, N_EXPERTS, 1), F32)
        for k in range(nk):
            offs.append(run)
            run = run + incl[k:k + 1, :, 255:256]
        off = offs[0] if nk == 1 else jnp.concatenate(offs, axis=0)
        return incl - m + off

    eqf = jnp.where(eq, 1.0, 0.0)
    sel = jnp.where(gt, 1.0, jnp.where(eq & (prefix_excl(eqf) < need), 1.0, 0.0))
    slot = jnp.where(sel > 0.5, prefix_excl(sel), -1.0)
    slot_ref[0] = slot.astype(jnp.int32)


def _topk(aff, blk0, nk, cap):
    nbatch = aff.shape[0]
    return pl.pallas_call(
        functools.partial(_topk_kernel, nk=nk, cap=cap),
        grid=(nbatch,),
        in_specs=[pl.BlockSpec((1, nk, N_EXPERTS, ROW_BLK), lambda b: (b, blk0, 0, 0))],
        out_specs=pl.BlockSpec((1, nk, N_EXPERTS, ROW_BLK), lambda b: (b, 0, 0, 0)),
        out_shape=jax.ShapeDtypeStruct((nbatch, nk, N_EXPERTS, ROW_BLK), jnp.int32),
        compiler_params=_cp("arbitrary"),
        name="topk",
    )(aff)


def _expert_kernel(*refs, has_ctx):
    if has_ctx:
        (w0_ref, fits_ref, hl_ref, sl_ref, al_ref, hc_ref, sc_ref, ac_ref, wg_ref, wu_ref, wd_ref,
         yl_ref, yc_ref, wgb, wub, wdb, xs_acc, gs_acc, xc_all, gc_all) = refs
    else:
        (w0_ref, fits_ref, hl_ref, sl_ref, al_ref, wg_ref, wu_ref, wd_ref, yl_ref,
         wgb, wub, wdb, xs_acc, gs_acc) = refs
    e = pl.program_id(0)
    bp = pl.program_id(1)

    @pl.when(bp == 0)
    def _():
        wgb[...] = wg_ref[0, 0].astype(BF16)
        wub[...] = wu_ref[0, 0].astype(BF16)
        wdb[...] = wd_ref[0, 0].astype(BF16)

    def gather(h_ref, slot_ref, aff_ref, s, nk, cap):
        r_iota = lax.broadcasted_iota(jnp.int32, (cap, ROW_BLK), 0)
        xs = jnp.zeros((cap, D), F32)
        gs = jnp.zeros((cap, 1), F32)
        for k in range(nk):
            pm = slot_ref[s, k, pl.ds(e, 1), :] == r_iota
            xs = xs + jnp.dot(jnp.where(pm, 1.0, 0.0).astype(BF16), h_ref[s, ROW_BLK * k:ROW_BLK * k + ROW_BLK, :],
                              preferred_element_type=F32)
            gs = gs + jnp.sum(jnp.where(pm, aff_ref[s, k, pl.ds(e, 1), :], 0.0), axis=1, keepdims=True)
        return xs.astype(BF16), gs

    def ffn(xsb):
        gt = jnp.dot(xsb, wgb[...], preferred_element_type=F32)
        up = jnp.dot(xsb, wub[...], preferred_element_type=F32)
        return jnp.dot((_silu(gt) * up).astype(BF16), wdb[...], preferred_element_type=F32)

    for s in range(EXP_SB):
        b = bp * EXP_SB + s
        base = CAP_LAT * s

        @pl.when(fits_ref[b] == 1)
        def _():
            xs_acc[base:base + CAP_LAT, :] = jnp.zeros((CAP_LAT, D), F32)
            gs_acc[base:base + CAP_LAT, :] = jnp.zeros((CAP_LAT, 1), F32)
            r_iota = lax.broadcasted_iota(jnp.int32, (MOE_WIN, ROW_BLK), 0)
            for k in range(LAT_BLKS):
                w0 = pl.multiple_of(w0_ref[(b * LAT_BLKS + k) * N_EXPERTS + e], 16)
                pm = (sl_ref[s, k, pl.ds(e, 1), :] - w0) == r_iota
                xs_acc[pl.ds(base + w0, MOE_WIN), :] += jnp.dot(jnp.where(pm, 1.0, 0.0).astype(BF16),
                                                                hl_ref[s, ROW_BLK * k:ROW_BLK * k + ROW_BLK, :],
                                                                preferred_element_type=F32)
                gs_acc[pl.ds(base + w0, MOE_WIN), :] += jnp.sum(jnp.where(pm, al_ref[s, k, pl.ds(e, 1), :], 0.0),
                                                                axis=1, keepdims=True)

        @pl.when(fits_ref[b] != 1)
        def _():
            xs, gs = gather(hl_ref, sl_ref, al_ref, s, LAT_BLKS, CAP_LAT)
            xs_acc[base:base + CAP_LAT, :] = xs.astype(F32)
            gs_acc[base:base + CAP_LAT, :] = gs

    y = ffn(xs_acc[...].astype(BF16)) * gs_acc[...]
    for s in range(EXP_SB):
        yl_ref[s, 0] = y[CAP_LAT * s:CAP_LAT * s + CAP_LAT].astype(yl_ref.dtype)
    if has_ctx:
        for s in range(EXP_SB):
            xs, gs = gather(hc_ref, sc_ref, ac_ref, s, 1, CAP_CTX)
            row = pl.multiple_of((bp * EXP_SB + s) * CAP_CTX, CAP_CTX)
            xc_all[pl.ds(row, CAP_CTX), :] = xs
            gc_all[pl.ds(row, CAP_CTX), :] = gs

        @pl.when(bp == pl.num_programs(1) - 1)
        def _():
            yc_ref[0] = (ffn(xc_all[...]) * gc_all[...]).astype(yc_ref.dtype)


def _experts(win0, fits, h2, slot_lat, slot_ctx, aff, w_gate, w_up, w_down, l, has_ctx):
    nbatch = h2.shape[0]
    assert nbatch % EXP_SB == 0
    idx4 = lambda e, b: (b, 0, 0, 0)
    smem = pl.BlockSpec(memory_space=pltpu.SMEM)
    in_specs = [smem, smem,
                pl.BlockSpec((EXP_SB, T, D), lambda e, b: (b, 0, 0)),
                pl.BlockSpec((EXP_SB, LAT_BLKS, N_EXPERTS, ROW_BLK), idx4),
                pl.BlockSpec((EXP_SB, LAT_BLKS, N_EXPERTS, ROW_BLK), idx4)]
    args = [win0, fits, h2, slot_lat, aff]
    out_specs = [pl.BlockSpec((EXP_SB, 1, CAP_LAT, D), lambda e, b: (b, e, 0, 0))]
    out_shape = [jax.ShapeDtypeStruct((nbatch, N_EXPERTS, CAP_LAT, D), BF16)]
    if has_ctx:
        in_specs += [pl.BlockSpec((EXP_SB, LC, D), lambda e, b: (b, LAT_BLKS, 0)),
                     pl.BlockSpec((EXP_SB, 1, N_EXPERTS, ROW_BLK), idx4),
                     pl.BlockSpec((EXP_SB, 1, N_EXPERTS, ROW_BLK), lambda e, b: (b, LAT_BLKS, 0, 0))]
        args += [h2, slot_ctx, aff]
        out_specs.append(pl.BlockSpec((1, nbatch * CAP_CTX, D), lambda e, b: (e, 0, 0)))
        out_shape.append(jax.ShapeDtypeStruct((N_EXPERTS, nbatch * CAP_CTX, D), BF16))
    wspec = pl.BlockSpec((1, 1, D, D), lambda e, b: (l, e, 0, 0))
    in_specs += [wspec, wspec, wspec]
    args += [w_gate, w_up, w_down]
    scratch = [pltpu.VMEM((D, D), BF16)] * 3 + [pltpu.VMEM((EXP_SB * CAP_LAT, D), F32),
                                                pltpu.VMEM((EXP_SB * CAP_LAT, 1), F32)]
    if has_ctx:
        scratch += [pltpu.VMEM((nbatch * CAP_CTX, D), BF16), pltpu.VMEM((nbatch * CAP_CTX, 1), F32)]
    return pl.pallas_call(
        functools.partial(_expert_kernel, has_ctx=has_ctx),
        grid=(N_EXPERTS, nbatch // EXP_SB),
        in_specs=in_specs,
        out_specs=out_specs,
        out_shape=out_shape,
        scratch_shapes=scratch,
        compiler_params=_cp("arbitrary", "arbitrary", vmem=VMEM_LIMIT_EXPERTS),
        name="experts",
    )(*args)


def _combine_kernel(*refs, has_ctx):
    if has_ctx:
        w0_ref, fits_ref, x1_ref, yl_ref, scl_ref, yc_ref, scc_ref, gate2_ref, g_ref, o_ref, ycat = refs
    else:
        w0_ref, fits_ref, x1_ref, yl_ref, scl_ref, gate2_ref, g_ref, o_ref, ycat = refs
    b = pl.program_id(0)
    j = pl.program_id(1)

    def finish(acc):
        o_ref[0] = x1_ref[0] + gate2_ref[0] * _rms(acc, g_ref[...])

    def comb(sc_ref, y_of, cap):
        lane = lax.broadcasted_iota(jnp.int32, (ROW_BLK, cap), 1)
        sc = sc_ref[0]
        acc = jnp.zeros((ROW_BLK, D), F32)
        for e in range(N_EXPERTS):
            pt = jnp.where(sc[:, e:e + 1] == lane, 1.0, 0.0).astype(BF16)
            acc = acc + jnp.dot(pt, y_of(e), preferred_element_type=F32)
        finish(acc)

    def comb_windowed():
        lane = lax.broadcasted_iota(jnp.int32, (ROW_BLK, ROW_BLK), 1)
        sc = scl_ref[0]
        per = ROW_BLK // MOE_WIN
        if per * MOE_WIN < ROW_BLK:
            ycat[:, per * MOE_WIN:, :] = jnp.zeros((2, ROW_BLK - per * MOE_WIN, D), BF16)
        acc = jnp.zeros((ROW_BLK, D), F32)
        for gi, e0 in enumerate(range(0, N_EXPERTS, per)):
            hit = None
            for q, e in enumerate(range(e0, min(e0 + per, N_EXPERTS))):
                w0 = pl.multiple_of(w0_ref[(b * LAT_BLKS + j) * N_EXPERTS + e], 16)
                d = sc[:, e:e + 1] - w0
                d = jnp.where(jnp.logical_and(d >= 0, d < MOE_WIN), d + MOE_WIN * q, -1)
                hit = (d == lane) if hit is None else jnp.logical_or(hit, d == lane)
                ycat[gi % 2, MOE_WIN * q:MOE_WIN * q + MOE_WIN, :] = yl_ref[0, e, pl.ds(w0, MOE_WIN), :]
            acc = acc + jnp.dot(jnp.where(hit, 1.0, 0.0).astype(BF16), ycat[gi % 2], preferred_element_type=F32)
        finish(acc)

    lat_y = lambda e: yl_ref[0, e]
    fits = fits_ref[b] == 1

    @pl.when(jnp.logical_and(j < LAT_BLKS, fits))
    def _():
        comb_windowed()

    @pl.when(jnp.logical_and(j < LAT_BLKS, jnp.logical_not(fits)))
    def _():
        comb(scl_ref, lat_y, CAP_LAT)

    if has_ctx:
        @pl.when(j == LAT_BLKS)
        def _():
            comb(scc_ref, lambda e: yc_ref[e], CAP_CTX)


def _combine(win0, fits, x1, y_lat, scol_lat, y_ctx, scol_ctx, mod3, g, has_ctx, l):
    nbatch = x1.shape[0]
    nb = NBLK if has_ctx else LAT_BLKS
    ntok = N if has_ctx else T
    smem = pl.BlockSpec(memory_space=pltpu.SMEM)
    in_specs = [smem, smem,
                pl.BlockSpec((1, ROW_BLK, D), lambda b, j: (b, j, 0)),
                pl.BlockSpec((1, N_EXPERTS, CAP_LAT, D), lambda b, j: (b, 0, 0, 0)),
                pl.BlockSpec((1, ROW_BLK, N_EXPERTS), lambda b, j: (b, jnp.minimum(j, LAT_BLKS - 1), 0))]
    args = [win0, fits, x1, y_lat, scol_lat]
    if has_ctx:
        in_specs += [pl.BlockSpec((N_EXPERTS, CAP_CTX, D), lambda b, j: (0, b, 0)),
                     pl.BlockSpec((1, ROW_BLK, N_EXPERTS), lambda b, j: (b, 0, 0))]
        args += [y_ctx, scol_ctx]
    in_specs += [pl.BlockSpec((1, 1, D), lambda b, j: (jnp.where(j == LAT_BLKS, nbatch, b), 0, 5)),
                 _lspec(g, l)]
    args += [mod3, g]
    return pl.pallas_call(
        functools.partial(_combine_kernel, has_ctx=has_ctx),
        grid=(nbatch, nb),
        in_specs=in_specs,
        out_specs=pl.BlockSpec((1, ROW_BLK, D), lambda b, j: (b, j, 0)),
        out_shape=jax.ShapeDtypeStruct((nbatch, ntok, D), F32),
        scratch_shapes=[pltpu.VMEM((2, ROW_BLK, D), BF16)],
        compiler_params=_cp("arbitrary", "arbitrary"),
        name="combine",
    )(*args)


def _rope_tables():
    f = 1.0 / (ROPE_THETA ** (jnp.arange(0, 16, 2, dtype=F32) / 16))
    pos = jnp.arange(T)
    row, col = pos // GRID_W, pos % GRID_W
    ar = row.astype(F32)[:, None] * f[None, :]
    ac = col.astype(F32)[:, None] * f[None, :]
    cos32 = jnp.concatenate([jnp.cos(ar), jnp.cos(ar), jnp.cos(ac), jnp.cos(ac)], axis=-1)
    sin32 = jnp.concatenate([jnp.sin(ar), jnp.sin(ar), jnp.sin(ac), jnp.sin(ac)], axis=-1)
    cos32 = jnp.concatenate([cos32, jnp.ones((LC, 32), F32)], axis=0)
    sin32 = jnp.concatenate([sin32, jnp.zeros((LC, 32), F32)], axis=0)
    return cos32, sin32


def _rot_cols(w):
    a, b, c, d = w[..., 0:8], w[..., 8:16], w[..., 16:24], w[..., 24:32]
    return jnp.concatenate([-b, a, -d, c], axis=-1)


def _mla_weights(w_uq, w_ukv):
    nl, r, _ = w_uq.shape
    wq3 = w_uq.reshape(nl, r, 4, 96)
    z32 = jnp.zeros((nl, r, 4, 32), F32)
    wq = jnp.concatenate([wq3, z32], axis=-1).reshape(nl, r, 512)
    wqr = jnp.concatenate([jnp.zeros((nl, r, 4, 64), F32), _rot_cols(wq3[..., 64:96]), z32], axis=-1).reshape(nl, r, 512)
    rk = w_ukv.shape[1]
    wkv3 = w_ukv.reshape(nl, rk, 4, 128)
    wk = jnp.concatenate([wkv3[..., :64], jnp.zeros((nl, rk, 4, 64), F32)], axis=-1).reshape(nl, rk, 512)
    z64 = jnp.zeros((nl, rk, 64), F32)
    vh = [wkv3[:, :, h, 64:] for h in range(4)]
    wv = jnp.concatenate([vh[0], z64, z64, vh[1], vh[2], z64, z64, vh[3]], axis=-1)
    vone = np.tile(np.repeat(np.array([0.0, 1.0, 1.0, 0.0], np.float32), 64), 2)[None, :]
    e = np.zeros((32, 512), np.float32)
    for h in range(4):
        e[np.arange(32), 128 * h + 64 + np.arange(32)] = 1.0
    return (wq.astype(BF16), wqr.astype(BF16), wk.astype(BF16), wv.astype(BF16), jnp.asarray(e, BF16),
            jnp.asarray(vone))


def _inproj_weights(w):
    o = np.cumsum([0, 256, 128, 32, 768, 256, 256, 768, 8, 4096])
    w = w.astype(BF16)
    seg = lambda i: w[:, :, o[i]:o[i + 1]]
    nl = w.shape[0]
    kr = seg(2)
    wm = jnp.concatenate([seg(0), seg(1), kr, _rot_cols(kr), jnp.zeros((nl, D, 64), BF16)], axis=-1)
    w2 = jnp.concatenate([seg(5), seg(6), seg(7), jnp.zeros((nl, D, 120), BF16)], axis=-1)
    return [wm, seg(3), seg(4), w2, 0.5 * seg(8)]


def kernel(x, c, ctx, c_ctx, w_ada, b_ada, g_pre_mix, g_post_mix, g_pre_ffn, g_post_ffn, w_in, mla_g_cq, mla_g_ckv, mla_w_uq, mla_w_ukv, na_rpb, s5_a_re, s5_a_im, s5_log_step, s5_b_re, s5_b_im, s5_c_re, s5_c_im, s5_d, s5_w_glu, s5_b_glu, m2_conv_w, m2_conv_b, m2_a_log, m2_dt_bias, m2_d, m2_g_norm, w_branch, w_out, w_router, w_gate, w_up, w_down):
    nbatch = x.shape[0]
    depth = w_ada.shape[0]
    vec = lambda v: v.reshape(depth, 1, -1).astype(F32)

    stream = (x, ctx, 0)
    cvec = jnp.concatenate([c, c_ctx[None, :], jnp.zeros((7, D), F32)], axis=0)
    cvec = cvec[: ((nbatch + 1 + 7) // 8) * 8]
    b_ada3 = b_ada.reshape(depth, 1, 6 * D)

    cos32, sin32 = _rope_tables()
    qscale = (64 + 32) ** -0.5 * math.log2(math.e)
    lane_is_rope = np.tile(np.concatenate([np.zeros(64, bool), np.ones(32, bool), np.zeros(32, bool)]), 4)
    pick = lambda t32, fill: jnp.where(lane_is_rope[None, :], jnp.tile(jnp.pad(t32, ((0, 0), (64, 32))), (1, 4)), fill)
    cos_q = pick(cos32, 1.0) * qscale
    sin_q = pick(sin32, 0.0) * qscale

    g_pre_mix, g_post_mix, g_pre_ffn, g_post_ffn = vec(g_pre_mix), vec(g_post_mix), vec(g_pre_ffn), vec(g_post_ffn)
    w_in_segs = _inproj_weights(w_in)
    wq, wqr, wk, wv, e_mat, vone = _mla_weights(mla_w_uq, mla_w_ukv)
    mla_g_cq, mla_g_ckv = vec(mla_g_cq), vec(mla_g_ckv)
    na_tabs = _na_tables(na_rpb)
    s5_bmat, s5_avec, s5_cmat = _s5_params(s5_a_re, s5_a_im, s5_log_step, s5_b_re, s5_b_im, s5_c_re, s5_c_im, nbatch)
    s5_d, s5_b_glu, s5_w_glu = vec(s5_d), vec(s5_b_glu), s5_w_glu.astype(BF16)
    m2_conv_w, m2_conv_b = m2_conv_w.astype(F32), vec(m2_conv_b)
    dt_bias = jnp.pad(m2_dt_bias.reshape(depth, 1, 8).astype(F32), ((0, 0), (0, 0), (0, 120)))
    a_col = jnp.pad(-jnp.exp(m2_a_log.astype(F32)).reshape(depth, 8, 1), ((0, 0), (0, 120), (0, 0)))
    m2_dvec, m2_g_norm = vec(jnp.repeat(m2_d, 64, axis=-1)), vec(m2_g_norm)
    w_branch, w_out = (0.5 * w_branch).astype(BF16), w_out.astype(BF16)
    w_router_t = jnp.swapaxes(w_router, 1, 2).astype(F32)

    for l in range(depth):
        has_ctx = l < depth - 1
        nb = NBLK if has_ctx else LAT_BLKS
        mod = _ada(cvec, w_ada, b_ada3, l)
        mod3 = mod.reshape(mod.shape[0], 1, 6 * D)

        p_mla, p_na, p_s5, p_m2, p_gate = _inproj(stream, g_pre_mix, mod3, w_in_segs, nbatch, l, has_ctx)

        q, k, v = _mla_prep(p_mla, mla_g_cq, mla_g_ckv, wq, wqr, wk, wv, e_mat, vone, cos_q, sin_q, cos32, sin32, l)
        o_mla = _mla_attn(q.reshape(nbatch, N, 512), k.reshape(nbatch, N, 512), v.reshape(nbatch, N, 512), has_ctx)

        o_na = _na_attn(p_na.reshape(nbatch, N, NA_W), na_tabs, has_ctx, l)

        yf, yb = _s5_scan(p_s5.reshape(nbatch, N, S5_W), s5_bmat, s5_avec, s5_cmat, l)
        o_s5 = _s5_glu(p_s5, yf.reshape(nbatch * N, S5_W), yb.reshape(nbatch * N, S5_W), s5_d, s5_w_glu, s5_b_glu, l)

        xbc, dtt, bmt = _m2_prep(p_m2.reshape(nbatch, N, M2_W), m2_conv_w, m2_conv_b, dt_bias, l)
        ssd_f, ssd_b = _ssd(xbc, dtt, bmt, a_col, l)
        o_m2 = _m2_norm(xbc.reshape(nbatch * N, 768), p_m2, ssd_f.reshape(nbatch * N, 256),
                        ssd_b.reshape(nbatch * N, 256), m2_dvec, m2_g_norm, l)

        x1, h2, aff = _merge(stream, o_mla, o_na.reshape(-1, 256), o_s5, o_m2, p_gate,
                             w_branch, w_out, g_post_mix, mod3, g_pre_ffn, w_router_t, nbatch, nb, l)

        slot_lat = _topk(aff, 0, LAT_BLKS, CAP_LAT)
        slot_ctx = _topk(aff, LAT_BLKS, 1, CAP_CTX) if has_ctx else None
        cnt = jnp.sum((slot_lat >= 0).astype(jnp.int32), axis=-1)
        first = jnp.cumsum(cnt, axis=1) - cnt
        win0 = jnp.minimum((first // 16) * 16, CAP_LAT - MOE_WIN)
        fits = jnp.all(first + cnt <= win0 + MOE_WIN, axis=(1, 2)).astype(jnp.int32)
        win0 = win0.reshape(-1)
        ys = _experts(win0, fits, h2.reshape(nbatch, nb * ROW_BLK, D), slot_lat, slot_ctx, aff, w_gate, w_up, w_down,
                      l, has_ctx)
        scol_lat = jnp.transpose(slot_lat, (0, 1, 3, 2)).reshape(nbatch, T, N_EXPERTS)
        scol_ctx = jnp.transpose(slot_ctx, (0, 1, 3, 2)).reshape(nbatch, LC, N_EXPERTS) if has_ctx else None
        xs = _combine(win0, fits, x1.reshape(nbatch, nb * ROW_BLK, D), ys[0], scol_lat, ys[1] if has_ctx else None,
                      scol_ctx, mod3, g_post_ffn, has_ctx, l)
        stream = (xs, xs, LAT_BLKS)
    return xs
```

```python
import functools
import math

import numpy as np
import jax
import jax.numpy as jnp
from jax import lax
from jax.experimental import pallas as pl
from jax.experimental.pallas import tpu as pltpu

F32 = jnp.float32
BF16 = jnp.bfloat16

D = 1024
T = 2048
LC = 256
N = T + LC
GRID_W = 64
ROW_BLK = 256
NBLK = N // ROW_BLK
LAT_BLKS = T // ROW_BLK
EPS = 1e-6
N_EXPERTS = 16
CAP_LAT = 2 * T // N_EXPERTS
CAP_CTX = 2 * LC // N_EXPERTS
ROPE_THETA = 10000.0
NEG = -1e30

MLA_W = 512
NA_W = 768
S5_W = 256
M2_W = 1152
GATE_W = 4096

MOE_WIN = 80
EXP_SB = 2
S5_CHUNK = 64
SSD_CHUNK = 128
SSD_SB = 8
NA_SB = 2

VMEM_LIMIT = 56 * 1024 * 1024


def _cp(*sem, vmem=VMEM_LIMIT):
    return pltpu.CompilerParams(dimension_semantics=sem, vmem_limit_bytes=vmem)


def _lspec(a, l, *lead):
    nd = a.ndim - 1 - len(lead)
    return pl.BlockSpec((None,) * (1 + len(lead)) + a.shape[1 + len(lead):], lambda *_: (l,) + lead + (0,) * nd)


def _dot(a, b):
    return jnp.dot(a.astype(BF16), b.astype(BF16), preferred_element_type=F32)


def _dot_nt(a, b):
    return lax.dot_general(a.astype(BF16), b.astype(BF16), (((1,), (1,)), ((), ())),
                           preferred_element_type=F32)


def _split3(a):
    hi = a.astype(BF16)
    r = a - hi.astype(F32)
    mid = r.astype(BF16)
    lo = (r - mid.astype(F32)).astype(BF16)
    return hi, mid, lo


def _dot_hi(a, b):
    ah, am, _ = _split3(a)
    bh, bm, _ = _split3(b)
    f = lambda x, y: jnp.dot(x, y, preferred_element_type=F32)
    return f(ah, bh) + (f(ah, bm) + f(am, bh))


def _dot_hi_nt(a, b):
    ah, am, _ = _split3(a)
    bh, bm, _ = _split3(b)
    f = lambda x, y: lax.dot_general(x, y, (((1,), (1,)), ((), ())), preferred_element_type=F32)
    return f(ah, bh) + (f(ah, bm) + f(am, bh))


def _dot_exact_lhs(m_bf16, a):
    h, m, l = _split3(a)
    f = lambda y: jnp.dot(m_bf16, y, preferred_element_type=F32)
    return f(h) + (f(m) + f(l))


def _dot_exact_rhs(a, m_bf16):
    h, m, l = _split3(a)
    f = lambda y: jnp.dot(y, m_bf16, preferred_element_type=F32)
    return f(h) + (f(m) + f(l))


def _sigmoid(x):
    return 0.5 * jnp.tanh(0.5 * x) + 0.5


def _silu(x):
    return x * _sigmoid(x)


def _rms(x, g):
    return x * lax.rsqrt(jnp.mean(x * x, axis=-1, keepdims=True) + EPS) * g


def _mod_row(nb):
    def f(i, nbatch):
        return jnp.where(i % nb == LAT_BLKS, nbatch, i // nb)
    return f


def _flat_blk(i, nb):
    return (i // nb) * NBLK + i % nb


def _stream_specs(stream, nb):
    lat, ctxa, cblk = stream
    w = lat.shape[-1]
    return [lat, ctxa], [pl.BlockSpec((1, ROW_BLK, w), lambda i: (i // nb, jnp.minimum(i % nb, LAT_BLKS - 1), 0)),
                         pl.BlockSpec((1, ROW_BLK, w), lambda i: (i // nb, cblk, 0))]


def _stream_block(xl_ref, xc_ref, nb):
    return jnp.where(pl.program_id(0) % nb == LAT_BLKS, xc_ref[0], xl_ref[0])


def _ada_kernel(c_ref, w_ref, b_ref, o_ref):
    c = c_ref[...]
    o_ref[...] = _dot_hi(_silu(c), w_ref[0]) + b_ref[0]


def _ada(cvec, w_ada, b_ada3, l):
    rows = cvec.shape[0]
    return pl.pallas_call(
        _ada_kernel,
        grid=(6,),
        in_specs=[pl.BlockSpec((rows, D), lambda k: (0, 0)),
                  pl.BlockSpec((1, D, D), lambda k: (l, 0, k)),
                  pl.BlockSpec((1, 1, D), lambda k: (l, 0, k))],
        out_specs=pl.BlockSpec((rows, D), lambda k: (0, k)),
        out_shape=jax.ShapeDtypeStruct((rows, 6 * D), F32),
        compiler_params=_cp("arbitrary"),
        name="ada",
    )(cvec, w_ada, b_ada3)


def _inproj_kernel(xl_ref, xc_ref, g_ref, sh_ref, sc_ref, wm, wn, ws, w2, wg, om, on, os_, o2, og, *, ctx_gates):
    x = _stream_block(xl_ref, xc_ref, NBLK)
    h = (_rms(x, g_ref[...]) * (1.0 + sc_ref[0]) + sh_ref[0]).astype(BF16)
    om[...] = jnp.dot(h, wm[...], preferred_element_type=F32)
    on[...] = jnp.dot(h, wn[...], preferred_element_type=F32).astype(BF16)
    os_[...] = jnp.dot(h, ws[...], preferred_element_type=F32)
    o2[...] = jnp.dot(h, w2[...], preferred_element_type=F32)
    if ctx_gates:
        og[...] = jnp.dot(h, wg[...], preferred_element_type=F32)
    else:
        is_ctx = pl.program_id(0) % NBLK == LAT_BLKS

        @pl.when(jnp.logical_not(is_ctx))
        def _():
            og[...] = jnp.dot(h, wg[...], preferred_element_type=F32)

        @pl.when(is_ctx)
        def _():
            og[...] = jnp.zeros_like(og)


def _inproj(stream, g, mod3, ws, nbatch, l, ctx_gates):
    rows = nbatch * N
    nblk = rows // ROW_BLK
    mrow = _mod_row(NBLK)
    xargs, xspecs = _stream_specs(stream, NBLK)
    full = lambda w: _lspec(w, l)
    widths = (MLA_W, NA_W, S5_W, M2_W, GATE_W)
    dts = (F32, BF16, F32, F32, F32)
    return pl.pallas_call(
        functools.partial(_inproj_kernel, ctx_gates=ctx_gates),
        grid=(nblk,),
        in_specs=xspecs + [
                  full(g),
                  pl.BlockSpec((1, 1, D), lambda i: (mrow(i, nbatch), 0, 0)),
                  pl.BlockSpec((1, 1, D), lambda i: (mrow(i, nbatch), 0, 1))]
                 + [full(w) for w in ws],
        out_specs=[pl.BlockSpec((ROW_BLK, w), lambda i: (i, 0)) for w in widths],
        out_shape=[jax.ShapeDtypeStruct((rows, w), dt) for w, dt in zip(widths, dts)],
        compiler_params=_cp("arbitrary"),
        name="inproj",
    )(*xargs, g, mod3, mod3, *ws)


def _mla_prep_kernel(p_ref, gq_ref, gkv_ref, wq_ref, wqr_ref, wk_ref, wv_ref, e_ref, vone_ref,
                     cos_ref, sin_ref, ck_ref, sk_ref, q_out, k_out, v_out):
    p = p_ref[...]
    cqn = _rms(p[:, :256], gq_ref[...]).astype(BF16)
    q = jnp.dot(cqn, wq_ref[...], preferred_element_type=F32)
    qr = jnp.dot(cqn, wqr_ref[...], preferred_element_type=F32)
    q_out[...] = (q * cos_ref[...] + qr * sin_ref[...]).astype(BF16)
    ckvn = _rms(p[:, 256:384], gkv_ref[...]).astype(BF16)
    kro = (p[:, 384:416] * ck_ref[...] + p[:, 416:448] * sk_ref[...]).astype(BF16)
    k = jnp.dot(ckvn, wk_ref[...], preferred_element_type=F32) + jnp.dot(kro, e_ref[...], preferred_element_type=F32)
    k_out[...] = k.astype(BF16)
    v_out[...] = (jnp.dot(ckvn, wv_ref[...], preferred_element_type=F32) + vone_ref[...]).astype(BF16)


def _mla_prep(p_mla, gq, gkv, wq, wqr, wk, wv, e, vone, cos_q, sin_q, cos_k, sin_k, l):
    rows = p_mla.shape[0]
    full = lambda w: _lspec(w, l)
    const = lambda w: pl.BlockSpec(w.shape, lambda j, b: (0, 0))
    rb = 3 * ROW_BLK
    row = lambda j, b: (b * (N // rb) + j, 0)
    tab = lambda w: pl.BlockSpec((rb, w), lambda j, b: (j, 0))
    return pl.pallas_call(
        _mla_prep_kernel,
        grid=(N // rb, rows // N),
        in_specs=[pl.BlockSpec((rb, MLA_W), row),
                  full(gq), full(gkv), full(wq), full(wqr), full(wk), full(wv), const(e), const(vone),
                  tab(512), tab(512), tab(32), tab(32)],
        out_specs=[pl.BlockSpec((rb, 512), row)] * 3,
        out_shape=[jax.ShapeDtypeStruct((rows, 512), BF16)] * 3,
        compiler_params=_cp("arbitrary", "arbitrary"),
        name="mla_prep",
    )(p_mla, gq, gkv, wq, wqr, wk, wv, e, vone, cos_q, sin_q, cos_k, sin_k)


def _mla_attn_kernel(q_ref, k_ref, v_ref, o_ref):
    lane = lax.broadcasted_iota(jnp.int32, (q_ref.shape[1], 128), 1)
    for hp in range(2):
        pv = []
        for h in (2 * hp, 2 * hp + 1):
            s = _dot_nt(q_ref[0, :, 128 * h:128 * h + 128], k_ref[0, :, 128 * h:128 * h + 128])
            p = jnp.exp2(s - jnp.max(s, axis=-1, keepdims=True)).astype(BF16)
            pv.append(jnp.dot(p, v_ref[0, :, 128 * h:128 * h + 128], preferred_element_type=F32))
        oa, ob = pv
        o = jnp.where(lane < 64, oa * (1.0 / oa[:, 64:65]), ob * (1.0 / ob[:, 0:1]))
        o_ref[0, :, 128 * hp:128 * hp + 128] = o.astype(o_ref.dtype)


def _mla_attn(q, k, v, has_ctx):
    nbatch = q.shape[0]
    qb = 2 * ROW_BLK

    def call(grid, qspec, kvspec, ospec, ntok):
        return pl.pallas_call(
            _mla_attn_kernel,
            grid=grid,
            in_specs=[qspec, kvspec, kvspec],
            out_specs=ospec,
            out_shape=jax.ShapeDtypeStruct((nbatch, ntok, 256), BF16),
            compiler_params=_cp("arbitrary", "arbitrary"),
            name="mla_attn",
        )(q, k, v)

    o_lat = call((nbatch, T // qb), pl.BlockSpec((1, qb, 512), lambda b, j: (b, j, 0)),
                 pl.BlockSpec((1, N, 512), lambda b, j: (b, 0, 0)),
                 pl.BlockSpec((1, qb, 256), lambda b, j: (b, j, 0)), T)
    if not has_ctx:
        return o_lat, o_lat
    ctx_blk = pl.BlockSpec((1, LC, 512), lambda b, j: (b, LAT_BLKS, 0))
    o_ctx = call((nbatch, 1), ctx_blk, ctx_blk, pl.BlockSpec((1, LC, 256), lambda b, j: (b, 0, 0)), LC)
    return o_lat, o_ctx


def _na_kernel(q_ref, kc_ref, k0_ref, k1_ref, k2_ref, vc_ref, v0_ref, v1_ref, v2_ref, b_ref, o_ref, *, has_ctx):
    scale = jnp.asarray(0.125, BF16)

    lane = lax.broadcasted_iota(jnp.int32, (ROW_BLK, 128), 1)
    lane64 = lax.broadcasted_iota(jnp.int32, (GRID_W, 128), 1) < 64
    grp = jnp.minimum(pl.program_id(0), LAT_BLKS - 1)
    krow0 = 4 * jnp.clip(grp - 1, 0, LAT_BLKS - 3)

    def bias_piece(h, i):
        rows = []
        for qr in range(4):
            r = 4 * grp + qr
            rs = jnp.clip(r - 4, 0, 24)
            tiles = []
            for pr in range(2):
                kr = krow0 + (4 * i + 2 * pr)
                t = b_ref[h, jnp.clip(kr - r + 8, 0, 15)]
                ok_a = jnp.logical_and(kr >= rs, kr < rs + 8).astype(jnp.int32)
                ok_b = jnp.logical_and(kr + 1 >= rs, kr + 1 < rs + 8).astype(jnp.int32)
                tiles.append(jnp.where(jnp.where(lane64, ok_a, ok_b) > 0, t, NEG))
            rows.append(jnp.concatenate(tiles, axis=1))
        return jnp.concatenate(rows, axis=0)

    def heads(win, sm):
        kws = (k0_ref, k1_ref, k2_ref)
        vws = (v0_ref, v1_ref, v2_ref)
        for hp in range(2):
            sl = slice(128 * hp, 128 * hp + 128)
            qp = q_ref[sm, :, sl] * scale
            outs = []
            for hh in range(2):
                h = 2 * hp + hh
                qh = jnp.where((lane < 64) if hh == 0 else (lane >= 64), qp, jnp.zeros_like(qp))
                s_c = _dot_nt(qh, kc_ref[sm, :, sl])
                s_w = []
                smax = s_c
                if win:
                    for i in range(3):
                        s = _dot_nt(qh, kws[i][sm, :, sl]) + bias[h][i]
                        s_w.append(s)
                        smax = jnp.maximum(smax, s)
                m = jnp.max(smax, axis=-1, keepdims=True)
                p = jnp.exp(s_c - m)
                psum = p
                o = _dot(p, vc_ref[sm, :, sl])
                for i, s in enumerate(s_w):
                    p = jnp.exp(s - m)
                    psum = psum + p
                    o = o + _dot(p, vws[i][sm, :, sl])
                outs.append(o * (1.0 / jnp.sum(psum, axis=-1, keepdims=True)))
            o_ref[sm, :, sl] = jnp.where(lane < 64, outs[0], outs[1]).astype(o_ref.dtype)

    def windowed():
        nonlocal bias
        bias = [[bias_piece(h, i) for i in range(3)] for h in range(4)]
        for sm in range(NA_SB):
            heads(True, sm)

    bias = None
    if has_ctx:
        g = pl.program_id(0)

        @pl.when(g < LAT_BLKS)
        def _():
            windowed()

        @pl.when(g == LAT_BLKS)
        def _():
            for sm in range(NA_SB):
                heads(False, sm)
    else:
        windowed()


def _na_attn(qkv, bias, has_ctx, l):
    nbatch = qkv.shape[0]
    ng = NBLK if has_ctx else LAT_BLKS
    j0 = lambda g: jnp.clip(g - 1, 0, LAT_BLKS - 3)
    blk = lambda f: pl.BlockSpec((NA_SB, ROW_BLK, 256), f)
    return pl.pallas_call(
        functools.partial(_na_kernel, has_ctx=has_ctx),
        grid=(ng, nbatch // NA_SB),
        in_specs=[blk(lambda g, b: (b, g, 0)),
                  blk(lambda g, b: (b, LAT_BLKS, 1)),
                  blk(lambda g, b: (b, j0(g), 1)),
                  blk(lambda g, b: (b, j0(g) + 1, 1)),
                  blk(lambda g, b: (b, j0(g) + 2, 1)),
                  blk(lambda g, b: (b, LAT_BLKS, 2)),
                  blk(lambda g, b: (b, j0(g), 2)),
                  blk(lambda g, b: (b, j0(g) + 1, 2)),
                  blk(lambda g, b: (b, j0(g) + 2, 2)),
                  _lspec(bias, l)],
        out_specs=blk(lambda g, b: (b, g, 0)),
        out_shape=jax.ShapeDtypeStruct((nbatch, ng * ROW_BLK, 256), BF16),
        compiler_params=_cp("arbitrary", "arbitrary"),
        name="na_attn",
    )(qkv, qkv, qkv, qkv, qkv, qkv, qkv, qkv, qkv, bias)


def _na_col_structure():
    onehot = np.zeros((31, 64, 64), np.float32)
    colmask = np.zeros((64, 64), bool)
    for c in range(64):
        cs = min(max(c - 8, 0), 48)
        for kc in range(cs, cs + 16):
            onehot[kc - c + 15, c, kc] = 1.0
            colmask[c, kc] = True
    return np.tile(onehot.reshape(31, 4096), (3, 1)), colmask


_NA_COL_ONEHOT, _NA_COLMASK = _na_col_structure()


def _na_tables(rpb):
    nl = rpb.shape[0]
    r = rpb.astype(F32)
    hi = lax.reduce_precision(r, 8, 7)
    mid = lax.reduce_precision(r - hi, 8, 7)
    parts = jnp.stack([hi, mid, r - hi - mid], axis=3).reshape(nl * 60, 93)
    tz = jnp.dot(parts, _NA_COL_ONEHOT, preferred_element_type=F32).reshape(nl, 4, 15, 64, 64)
    tz = jnp.where(_NA_COLMASK, tz, NEG)
    edge = jnp.full((nl, 4, 1, 64, 64), NEG, F32)
    tz17 = jnp.concatenate([edge, tz, edge], axis=2)
    return jnp.concatenate([tz17[:, :, :16], tz17[:, :, 1:]], axis=-1)


def _s5_kernel(uf_ref, ub_ref, bf_ref, bb_ref, af_ref, ab_ref, cf_ref, cb_ref, yf_ref, yb_ref,
               hf, hb, buf_f, buf_b, tm_f, tm_b):
    i = pl.program_id(0)
    half = 1024
    nb = uf_ref.shape[0]

    @pl.when(i == 0)
    def _():
        hf[...] = jnp.zeros_like(hf)
        hb[...] = jnp.zeros_like(hb)

    def expand(u_ref, tm, b_ref, buf):
        for b in range(nb):
            for c in range(2):
                tm[c, pl.ds(b, S5_CHUNK, stride=nb), :] = u_ref[b, :, 128 * c:128 * c + 128]
        buf[...] = _dot(jnp.concatenate([tm[0], tm[1]], axis=1), b_ref[...])

    def scan(buf, a_ref, h, order):
        hr, hi = h[:, :half], h[:, half:]
        ar, ai = a_ref[:, :half], a_ref[:, half:]
        for k in order:
            rows = slice(k * nb, (k + 1) * nb)
            nr = ar * hr - ai * hi + buf[rows, :half]
            ni = ar * hi + ai * hr + buf[rows, half:]
            buf[rows, :half] = nr
            buf[rows, half:] = ni
            hr, hi = nr, ni
        h[:, :half] = hr
        h[:, half:] = hi

    def readout(buf, c_ref, tm, y_ref):
        y = _dot(buf[...], c_ref[...])
        for c in range(2):
            tm[c] = y[:, 128 * c:128 * c + 128]
        for b in range(nb):
            for c in range(2):
                y_ref[b, :, 128 * c:128 * c + 128] = tm[c, pl.ds(b, S5_CHUNK, stride=nb), :]

    expand(uf_ref, tm_f, bf_ref, buf_f)
    expand(ub_ref, tm_b, bb_ref, buf_b)
    scan(buf_f, af_ref, hf, range(S5_CHUNK))
    readout(buf_f, cf_ref, tm_f, yf_ref)
    scan(buf_b, ab_ref, hb, range(S5_CHUNK - 1, -1, -1))
    readout(buf_b, cb_ref, tm_b, yb_ref)


def _s5_scan(u, bmat, avec, cmat, l):
    nbatch = u.shape[0]
    cr = S5_CHUNK * nbatch
    nch = N // S5_CHUNK
    nctx = LC // S5_CHUNK
    fidx = lambda i: jnp.where(i < nctx, nch - nctx + i, i - nctx)
    bidx = lambda i: nch - 1 - i
    ublk = lambda f: pl.BlockSpec((nbatch, S5_CHUNK, 256), lambda i: (0, f(i), 0))
    return pl.pallas_call(
        _s5_kernel,
        grid=(nch,),
        in_specs=[ublk(fidx), ublk(bidx),
                  _lspec(bmat, l, 0), _lspec(bmat, l, 1), _lspec(avec, l, 0), _lspec(avec, l, 1),
                  _lspec(cmat, l, 0), _lspec(cmat, l, 1)],
        out_specs=[ublk(fidx), ublk(bidx)],
        out_shape=[jax.ShapeDtypeStruct((nbatch, N, 256), F32)] * 2,
        scratch_shapes=[pltpu.VMEM((nbatch, 2048), F32), pltpu.VMEM((nbatch, 2048), F32),
                        pltpu.VMEM((cr, 2048), F32), pltpu.VMEM((cr, 2048), F32),
                        pltpu.VMEM((2, cr, 128), F32), pltpu.VMEM((2, cr, 128), F32)],
        compiler_params=_cp("arbitrary"),
        name="s5_scan",
    )(u, u, bmat, bmat, avec, avec, cmat, cmat)


def _s5_glu_kernel(u_ref, yf_ref, yb_ref, d_ref, w_ref, b_ref, o_ref):
    y = d_ref[...] * u_ref[...] + yf_ref[...] + yb_ref[...]
    z = y * (0.5 * (1.0 + jnp.tanh(math.sqrt(2.0 / math.pi) * (y + 0.044715 * (y * y * y)))))
    o_ref[...] = (z * _sigmoid(_dot(z, w_ref[...]) + b_ref[...])).astype(o_ref.dtype)


def _s5_glu(u_tb, yf, yb, d, w, b, l):
    rows = u_tb.shape[0]
    rb = 4 * ROW_BLK
    blk = pl.BlockSpec((rb, 256), lambda i: (i, 0))
    full = lambda a: _lspec(a, l)
    return pl.pallas_call(
        _s5_glu_kernel,
        grid=(rows // rb,),
        in_specs=[blk, blk, blk, full(d), full(w), full(b)],
        out_specs=blk,
        out_shape=jax.ShapeDtypeStruct((rows, 256), BF16),
        compiler_params=_cp("arbitrary"),
        name="s5_glu",
    )(u_tb, yf, yb, d, w, b)


def _s5_params(a_re, a_im, log_step, b_re, b_im, c_re, c_im, nbatch):
    nl = a_re.shape[0]
    same_group = (np.arange(256)[:, None] // 16 == np.arange(1024)[None, :] // 64).astype(np.float32)
    a = lax.complex(a_re.astype(F32), a_im.astype(F32))
    abar = jnp.exp(jnp.exp(log_step.astype(F32))[..., None] * a)
    bbar = ((abar - 1.0) / a)[..., None] * lax.complex(b_re.astype(F32), b_im.astype(F32))
    blk_in = lambda m: jnp.tile(jnp.swapaxes(m, -1, -2).reshape(nl, 2, 256, 64), (1, 1, 1, 16)) * same_group
    bmat = jnp.concatenate([blk_in(jnp.real(bbar)), blk_in(jnp.imag(bbar))], axis=-1).astype(BF16)
    avec = jnp.concatenate([jnp.real(abar).reshape(nl, 2, 1, 1024), jnp.imag(abar).reshape(nl, 2, 1, 1024)], axis=-1)
    avec = jnp.broadcast_to(avec, (nl, 2, nbatch, 2048))
    blk_out = lambda m: jnp.tile(jnp.swapaxes(m, -1, -2).reshape(nl, 2, 1024, 16), (1, 1, 1, 16)) * same_group.T
    cmat = jnp.concatenate([blk_out(c_re.astype(F32)), -blk_out(c_im.astype(F32))], axis=-2).astype(BF16)
    return bmat, avec, cmat


def _m2_prep_kernel(x_ref, dt_ref, w_ref, b_ref, dtb_ref, xo_ref, dtt_ref, bmt_ref):
    c = pl.program_id(1)
    x = x_ref[0]
    t = lax.broadcasted_iota(jnp.int32, x.shape, 0)
    m2 = ((t >= 2) & (t < T)) | (t >= T + 2)
    m1 = ((t >= 1) & (t < T)) | (t >= T + 1)
    p1 = (t <= T - 2) | ((t >= T) & (t <= N - 2))
    w = w_ref[...]
    y = (w[0:1] * jnp.where(m2, pltpu.roll(x, 2, 0), 0.0)
         + w[1:2] * jnp.where(m1, pltpu.roll(x, 1, 0), 0.0)
         + w[2:3] * x
         + w[3:4] * jnp.where(p1, pltpu.roll(x, N - 1, 0), 0.0)) + b_ref[...]
    act = _silu(y)
    xo_ref[0] = act

    @pl.when(c == 0)
    def _():
        v = dt_ref[0] + dtb_ref[...]
        dtt_ref[0] = (jnp.maximum(v, 0.0) + jnp.log1p(jnp.exp(-jnp.abs(v)))).T

    @pl.when(c == 1)
    def _():
        bmt_ref[0] = act.T.astype(BF16)


def _m2_prep(p_m2, conv_w, conv_b, dtb, l):
    nbatch = p_m2.shape[0]
    return pl.pallas_call(
        _m2_prep_kernel,
        grid=(nbatch, 3),
        in_specs=[pl.BlockSpec((1, N, 256), lambda b, c: (b, 0, 1 + c)),
                  pl.BlockSpec((1, N, 128), lambda b, c: (b, 0, 8)),
                  pl.BlockSpec((None, 4, 256), lambda b, c: (l, 0, c)),
                  pl.BlockSpec((None, 1, 256), lambda b, c: (l, 0, c)),
                  _lspec(dtb, l)],
        out_specs=[pl.BlockSpec((1, N, 256), lambda b, c: (b, 0, c)),
                   pl.BlockSpec((1, 128, N), lambda b, c: (b, 0, 0)),
                   pl.BlockSpec((1, 256, N), lambda b, c: (b, 0, 0))],
        out_shape=[jax.ShapeDtypeStruct((nbatch, N, 768), F32),
                   jax.ShapeDtypeStruct((nbatch, 128, N), F32),
                   jax.ShapeDtypeStruct((nbatch, 256, N), BF16)],
        compiler_params=_cp("arbitrary", "arbitrary"),
        name="m2_prep",
    )(p_m2, p_m2, conv_w, conv_b, dtb)


def _ssd_kernel(xf_ref, dttf_ref, bmtf_ref, xb_ref, dttb_ref, bmtb_ref, acol_ref, yf_ref, yb_ref, hs):
    i = pl.program_id(1)
    L = SSD_CHUNK

    @pl.when(i == 0)
    def _():
        hs[...] = jnp.zeros_like(hs)

    li = lax.broadcasted_iota(jnp.int32, (L, L), 0)
    si = lax.broadcasted_iota(jnp.int32, (L, L), 1)
    left = si < 64
    pick = lambda a, b: jnp.where(left, a, b)

    def direction(sm, x_ref, dtt_ref, bmt_ref, y_ref, d, causal, causal_t, last):
        cz = jnp.where(causal, 1.0, 0.0).astype(BF16)
        czt = jnp.where(causal_t, 1.0, 0.0).astype(BF16)
        xbc = x_ref[sm]
        dtt = dtt_ref[sm]
        parts = jnp.concatenate([p.astype(F32) for p in _split3(dtt * acol_ref[...])], axis=1)
        cumt = jnp.dot(parts.astype(BF16), jnp.concatenate([czt, czt, czt], axis=0),
                       preferred_element_type=F32)
        reps = jnp.concatenate([jnp.broadcast_to(parts[4 * d + h:4 * d + h + 1, :], (L, 3 * L)) for h in range(4)],
                               axis=0).astype(BF16)
        cumb = lax.dot_general(jnp.concatenate([cz, cz, cz], axis=1), reps, (((1,), (1,)), ((), ())),
                               preferred_element_type=F32)
        for g in range(2):
            xp = xbc[:, 128 * g:128 * g + 128].astype(BF16)
            cm = xbc[:, 512 + 128 * g:640 + 128 * g].astype(BF16)
            bmt = bmt_ref[sm, 128 * g:128 * g + 128, :]
            hprev = hs[sm, d, g]
            go = jnp.dot(cm, jnp.concatenate([bmt, hprev.astype(BF16)], axis=1), preferred_element_type=F32)
            gmat, yo = go[:, :L], go[:, L:]
            bmf = bmt.astype(F32)
            lhs, cbs, cls = [], [], []
            for hh in range(2):
                h = 2 * g + hh
                cb = cumb[:, L * h:L * h + L]
                crow = cumt[4 * d + h:4 * d + h + 1, :]
                dtrow = dtt[4 * d + h:4 * d + h + 1, :]
                cl = cb[last:last + 1, :]
                lhs.append((gmat * jnp.where(causal, jnp.exp(cb - crow), 0.0) * dtrow).astype(BF16))
                lhs.append((bmf * (jnp.exp(cl - crow) * dtrow)).astype(BF16))
                cbs.append(cb)
                cls.append(cl)
            big = jnp.dot(jnp.concatenate(lhs, axis=0), xp, preferred_element_type=F32)
            y_ref[sm, :, 128 * g:128 * g + 128] = pick(big[0:L], big[2 * L:3 * L]) + yo * jnp.exp(pick(cbs[0], cbs[1]))
            hs[sm, d, g] = hprev * jnp.exp(pick(cls[0], cls[1])) + pick(big[L:2 * L], big[3 * L:4 * L])

    for sm in range(SSD_SB):
        direction(sm, xf_ref, dttf_ref, bmtf_ref, yf_ref, 0, si <= li, li <= si, L - 1)
        direction(sm, xb_ref, dttb_ref, bmtb_ref, yb_ref, 1, si >= li, li >= si, 0)


def _ssd(xbc, dtt, bmt, a_col, l):
    nbatch = xbc.shape[0]
    nch = N // SSD_CHUNK
    nctx = LC // SSD_CHUNK
    fidx = lambda i: jnp.where(i < nctx, nch - nctx + i, i - nctx)
    bidx = lambda i: nch - 1 - i
    ins = lambda f: [pl.BlockSpec((SSD_SB, SSD_CHUNK, 768), lambda b, i: (b, f(i), 0)),
                     pl.BlockSpec((SSD_SB, 128, SSD_CHUNK), lambda b, i: (b, 0, f(i))),
                     pl.BlockSpec((SSD_SB, 256, SSD_CHUNK), lambda b, i: (b, 0, f(i)))]
    return pl.pallas_call(
        _ssd_kernel,
        grid=(nbatch // SSD_SB, nch),
        in_specs=ins(fidx) + ins(bidx) + [_lspec(a_col, l)],
        out_specs=[pl.BlockSpec((SSD_SB, SSD_CHUNK, 256), lambda b, i: (b, fidx(i), 0)),
                   pl.BlockSpec((SSD_SB, SSD_CHUNK, 256), lambda b, i: (b, bidx(i), 0))],
        out_shape=[jax.ShapeDtypeStruct((nbatch, N, 256), F32)] * 2,
        scratch_shapes=[pltpu.VMEM((SSD_SB, 2, 2, 128, 128), F32)],
        compiler_params=_cp("arbitrary", "arbitrary"),
        name="ssd",
    )(xbc, dtt, bmt, xbc, dtt, bmt, a_col)


def _m2_norm_kernel(x_ref, z_ref, yf_ref, yb_ref, d_ref, g_ref, o_ref):
    y = d_ref[...] * x_ref[...] + yf_ref[...] + yb_ref[...]
    o_ref[...] = _rms(y * _silu(z_ref[...]), g_ref[...]).astype(o_ref.dtype)


def _m2_norm(xbc2d, p_m2_2d, yf2d, yb2d, dvec, gn, l):
    rows = xbc2d.shape[0]
    rb = 4 * ROW_BLK
    blk = pl.BlockSpec((rb, 256), lambda i: (i, 0))
    full = lambda a: _lspec(a, l)
    return pl.pallas_call(
        _m2_norm_kernel,
        grid=(rows // rb,),
        in_specs=[blk, blk, blk, blk, full(dvec), full(gn)],
        out_specs=blk,
        out_shape=jax.ShapeDtypeStruct((rows, 256), BF16),
        compiler_params=_cp("arbitrary"),
        name="m2_norm",
    )(xbc2d, p_m2_2d, yf2d, yb2d, dvec, gn)


def _merge_kernel(xl_ref, xc_ref, oal_ref, oac_ref, on_ref, os_ref, om_ref, gl_ref, wb_ref, wo_ref, gpm_ref, gate1_ref,
                  gpf_ref, sh2_ref, sc2_ref, wr_ref, x1_ref, h2_ref, aff_ref, *, nb):
    outs = (_stream_block(oal_ref, oac_ref, nb), on_ref[...], os_ref[...], om_ref[...])
    y = None
    for j in range(4):
        d = jnp.dot(outs[j], wb_ref[j], preferred_element_type=F32)
        t = d * jnp.tanh(gl_ref[:, 1024 * j:1024 * j + 1024]) + d
        y = t if y is None else y + t
    y2 = _dot(y, wo_ref[...])
    x1 = _stream_block(xl_ref, xc_ref, nb) + gate1_ref[0] * _rms(y2, gpm_ref[...])
    x1_ref[...] = x1
    h2 = _rms(x1, gpf_ref[...]) * (1.0 + sc2_ref[0]) + sh2_ref[0]
    h2_ref[...] = h2.astype(BF16)
    logits = _dot_hi_nt(wr_ref[...], h2)
    m = jnp.max(logits, axis=0, keepdims=True)
    e = jnp.exp(logits - m)
    aff_ref[0, 0] = e / jnp.sum(e, axis=0, keepdims=True)


def _merge(stream, o_mla, o_na, o_s5, o_m2, gl, wb, wo, gpm, mod3, gpf, wrt, nbatch, nb, l):
    mrow = _mod_row(nb)
    xargs, xspecs = _stream_specs(stream, nb)
    aargs, aspecs = _stream_specs((o_mla[0], o_mla[1], 0), nb)
    fb = lambda i: (_flat_blk(i, nb), 0)
    blk = lambda w: pl.BlockSpec((ROW_BLK, w), fb)
    oblk = lambda w: pl.BlockSpec((ROW_BLK, w), lambda i: (i, 0))
    full = lambda a: _lspec(a, l)
    modspec = lambda k: pl.BlockSpec((1, 1, D), lambda i: (mrow(i, nbatch), 0, k))
    orows = nbatch * nb * ROW_BLK
    return pl.pallas_call(
        functools.partial(_merge_kernel, nb=nb),
        grid=(nbatch * nb,),
        in_specs=xspecs + aspecs + [oblk(256), blk(256), blk(256), blk(GATE_W),
                  full(wb), full(wo), full(gpm), modspec(2), full(gpf), modspec(3), modspec(4), full(wrt)],
        out_specs=[oblk(D), oblk(D),
                   pl.BlockSpec((1, 1, N_EXPERTS, ROW_BLK), lambda i: (i // nb, i % nb, 0, 0))],
        out_shape=[jax.ShapeDtypeStruct((orows, D), F32),
                   jax.ShapeDtypeStruct((orows, D), BF16),
                   jax.ShapeDtypeStruct((nbatch, nb, N_EXPERTS, ROW_BLK), F32)],
        compiler_params=_cp("arbitrary"),
        name="merge",
    )(*xargs, *aargs, o_na, o_s5, o_m2, gl, wb, wo, gpm, mod3, gpf, mod3, mod3, wrt)


def _topk_kernel(aff_ref, slot_ref, *, nk, cap):
    a = aff_ref[0]
    bits = lax.bitcast_convert_type(a, jnp.int32)
    count = lambda m: jnp.sum(jnp.sum(jnp.where(m, 1.0, 0.0), axis=2, keepdims=True), axis=0, keepdims=True)
    thr = jnp.where(count(bits >= (1 << 30)) >= cap, jnp.int32(1 << 30), jnp.zeros((1, N_EXPERTS, 1), jnp.int32))
    for bit in range(29, 0, -2):
        hi, lo = 1 << bit, 1 << (bit - 1)
        ok = [count(bits >= (thr | c)) >= cap for c in (hi | lo, hi, lo)]
        thr = thr | jnp.where(ok[0], hi | lo, jnp.where(ok[1], hi, jnp.where(ok[2], lo, 0)))
    gt = bits > thr
    eq = bits == thr
    need = cap - count(gt)

    tri = jnp.where(lax.broadcasted_iota(jnp.int32, (256, 256), 0) <= lax.broadcasted_iota(jnp.int32, (256, 256), 1),
                    1.0, 0.0).astype(BF16)

    def prefix_excl(m):
        incl = jnp.dot(m.reshape(nk * N_EXPERTS, 256).astype(BF16), tri,
                       preferred_element_type=F32).reshape(nk, N_EXPERTS, 256)
        offs = []
        run = jnp.zeros((1, N_EXPERTS, 1), F32)
        for k in range(nk):
            offs.append(run)
            run = run + incl[k:k + 1, :, 255:256]
        off = offs[0] if nk == 1 else jnp.concatenate(offs, axis=0)
        return incl - m + off

    eqf = jnp.where(eq, 1.0, 0.0)
    sel = jnp.where(gt, 1.0, jnp.where(eq & (prefix_excl(eqf) < need), 1.0, 0.0))
    slot = jnp.where(sel > 0.5, prefix_excl(sel), -1.0)
    slot_ref[0] = slot.astype(jnp.int32)


def _topk(aff, blk0, nk, cap):
    nbatch = aff.shape[0]
    return pl.pallas_call(
        functools.partial(_topk_kernel, nk=nk, cap=cap),
        grid=(nbatch,),
        in_specs=[pl.BlockSpec((1, nk, N_EXPERTS, ROW_BLK), lambda b: (b, blk0, 0, 0))],
        out_specs=pl.BlockSpec((1, nk, N_EXPERTS, ROW_BLK), lambda b: (b, 0, 0, 0)),
        out_shape=jax.ShapeDtypeStruct((nbatch, nk, N_EXPERTS, ROW_BLK), jnp.int32),
        compiler_params=_cp("arbitrary"),
        name="topk",
    )(aff)


def _gather_kernel(w0_ref, fits_ref, h_ref, sl_ref, xs_ref, acc):
    b = pl.program_id(0)
    k = pl.program_id(1)

    @pl.when(k == 0)
    def _():
        acc[...] = jnp.zeros_like(acc)

    h = h_ref[0]

    @pl.when(fits_ref[b] == 1)
    def _():
        r_iota = lax.broadcasted_iota(jnp.int32, (MOE_WIN, ROW_BLK), 0)
        w0s = [pl.multiple_of(w0_ref[(b * LAT_BLKS + k) * N_EXPERTS + e], 16) for e in range(N_EXPERTS)]
        onehot = jnp.concatenate([jnp.where((sl_ref[0, 0, e:e + 1, :] - w0s[e]) == r_iota, 1.0, 0.0).astype(BF16)
                                  for e in range(N_EXPERTS)], axis=0)
        rows = jnp.dot(onehot, h, preferred_element_type=F32)
        for e in range(N_EXPERTS):
            acc[e, pl.ds(w0s[e], MOE_WIN), :] += rows[MOE_WIN * e:MOE_WIN * e + MOE_WIN]

    @pl.when(fits_ref[b] != 1)
    def _():
        r_iota = lax.broadcasted_iota(jnp.int32, (CAP_LAT, ROW_BLK), 0)
        for e in range(N_EXPERTS):
            onehot = jnp.where(sl_ref[0, 0, e:e + 1, :] == r_iota, 1.0, 0.0).astype(BF16)
            acc[e] += jnp.dot(onehot, h, preferred_element_type=F32)

    @pl.when(k == pl.num_programs(1) - 1)
    def _():
        xs_ref[0] = acc[...].astype(xs_ref.dtype)


def _gather(win0, fits, h2, slot_lat):
    nbatch = h2.shape[0]
    smem = pl.BlockSpec(memory_space=pltpu.SMEM)
    return pl.pallas_call(
        _gather_kernel,
        grid=(nbatch, LAT_BLKS),
        in_specs=[smem, smem,
                  pl.BlockSpec((1, ROW_BLK, D), lambda b, k: (b, k, 0)),
                  pl.BlockSpec((1, 1, N_EXPERTS, ROW_BLK), lambda b, k: (b, k, 0, 0))],
        out_specs=pl.BlockSpec((1, N_EXPERTS, CAP_LAT, D), lambda b, k: (b, 0, 0, 0)),
        out_shape=jax.ShapeDtypeStruct((nbatch, N_EXPERTS, CAP_LAT, D), BF16),
        scratch_shapes=[pltpu.VMEM((N_EXPERTS, CAP_LAT, D), F32)],
        compiler_params=_cp("arbitrary", "arbitrary"),
        name="gather",
    )(win0, fits, h2, slot_lat)


def _expert_kernel(*refs, has_ctx):
    if has_ctx:
        (w0_ref, fits_ref, xl_ref, sl_ref, al_ref, hc_ref, sc_ref, ac_ref, wg_ref, wu_ref, wd_ref,
         yl_ref, yc_ref, wgb, wub, wdb, gs_acc, xc_all, gc_all) = refs
    else:
        (w0_ref, fits_ref, xl_ref, sl_ref, al_ref, wg_ref, wu_ref, wd_ref, yl_ref,
         wgb, wub, wdb, gs_acc) = refs
    e = pl.program_id(0)
    bp = pl.program_id(1)

    @pl.when(bp == 0)
    def _():
        wgb[...] = wg_ref[0, 0].astype(BF16)
        wub[...] = wu_ref[0, 0].astype(BF16)
        wdb[...] = wd_ref[0, 0].astype(BF16)

    def gather(h_ref, slot_ref, aff_ref, s, nk, cap):
        r_iota = lax.broadcasted_iota(jnp.int32, (cap, ROW_BLK), 0)
        xs = jnp.zeros((cap, D), F32)
        gs = jnp.zeros((cap, 1), F32)
        for k in range(nk):
            pm = slot_ref[s, k, pl.ds(e, 1), :] == r_iota
            xs = xs + jnp.dot(jnp.where(pm, 1.0, 0.0).astype(BF16), h_ref[s, ROW_BLK * k:ROW_BLK * k + ROW_BLK, :],
                              preferred_element_type=F32)
            gs = gs + jnp.sum(jnp.where(pm, aff_ref[s, k, pl.ds(e, 1), :], 0.0), axis=1, keepdims=True)
        return xs.astype(BF16), gs

    def ffn(xsb):
        gt = jnp.dot(xsb, wgb[...], preferred_element_type=F32)
        up = jnp.dot(xsb, wub[...], preferred_element_type=F32)
        return jnp.dot((_silu(gt) * up).astype(BF16), wdb[...], preferred_element_type=F32)

    for s in range(EXP_SB):
        b = bp * EXP_SB + s
        base = CAP_LAT * s

        @pl.when(fits_ref[b] == 1)
        def _():
            gs_acc[base:base + CAP_LAT, :] = jnp.zeros((CAP_LAT, 1), F32)
            r_iota = lax.broadcasted_iota(jnp.int32, (MOE_WIN, ROW_BLK), 0)
            for k in range(LAT_BLKS):
                w0 = pl.multiple_of(w0_ref[(b * LAT_BLKS + k) * N_EXPERTS + e], 16)
                pm = (sl_ref[s, k, pl.ds(e, 1), :] - w0) == r_iota
                gs_acc[pl.ds(base + w0, MOE_WIN), :] += jnp.sum(jnp.where(pm, al_ref[s, k, pl.ds(e, 1), :], 0.0),
                                                                axis=1, keepdims=True)

        @pl.when(fits_ref[b] != 1)
        def _():
            r_iota = lax.broadcasted_iota(jnp.int32, (CAP_LAT, ROW_BLK), 0)
            gs = jnp.zeros((CAP_LAT, 1), F32)
            for k in range(LAT_BLKS):
                pm = sl_ref[s, k, pl.ds(e, 1), :] == r_iota
                gs = gs + jnp.sum(jnp.where(pm, al_ref[s, k, pl.ds(e, 1), :], 0.0), axis=1, keepdims=True)
            gs_acc[base:base + CAP_LAT, :] = gs

    xs = xl_ref[...].reshape(EXP_SB * CAP_LAT, D)
    y = ffn(xs) * gs_acc[...]
    for s in range(EXP_SB):
        yl_ref[s, 0] = y[CAP_LAT * s:CAP_LAT * s + CAP_LAT].astype(yl_ref.dtype)
    if has_ctx:
        for s in range(EXP_SB):
            xs, gs = gather(hc_ref, sc_ref, ac_ref, s, 1, CAP_CTX)
            row = pl.multiple_of((bp * EXP_SB + s) * CAP_CTX, CAP_CTX)
            xc_all[pl.ds(row, CAP_CTX), :] = xs
            gc_all[pl.ds(row, CAP_CTX), :] = gs

        @pl.when(bp == pl.num_programs(1) - 1)
        def _():
            yc_ref[0] = (ffn(xc_all[...]) * gc_all[...]).astype(yc_ref.dtype)


def _experts(win0, fits, xs_lat, h2, slot_lat, slot_ctx, aff, w_gate, w_up, w_down, l, has_ctx):
    nbatch = h2.shape[0]
    assert nbatch % EXP_SB == 0
    idx4 = lambda e, b: (b, 0, 0, 0)
    smem = pl.BlockSpec(memory_space=pltpu.SMEM)
    in_specs = [smem, smem,
                pl.BlockSpec((EXP_SB, 1, CAP_LAT, D), lambda e, b: (b, e, 0, 0)),
                pl.BlockSpec((EXP_SB, LAT_BLKS, N_EXPERTS, ROW_BLK), idx4),
                pl.BlockSpec((EXP_SB, LAT_BLKS, N_EXPERTS, ROW_BLK), idx4)]
    args = [win0, fits, xs_lat, slot_lat, aff]
    out_specs = [pl.BlockSpec((EXP_SB, 1, CAP_LAT, D), lambda e, b: (b, e, 0, 0))]
    out_shape = [jax.ShapeDtypeStruct((nbatch, N_EXPERTS, CAP_LAT, D), BF16)]
    if has_ctx:
        in_specs += [pl.BlockSpec((EXP_SB, LC, D), lambda e, b: (b, LAT_BLKS, 0)),
                     pl.BlockSpec((EXP_SB, 1, N_EXPERTS, ROW_BLK), idx4),
                     pl.BlockSpec((EXP_SB, 1, N_EXPERTS, ROW_BLK), lambda e, b: (b, LAT_BLKS, 0, 0))]
        args += [h2, slot_ctx, aff]
        out_specs.append(pl.BlockSpec((1, nbatch * CAP_CTX, D), lambda e, b: (e, 0, 0)))
        out_shape.append(jax.ShapeDtypeStruct((N_EXPERTS, nbatch * CAP_CTX, D), BF16))
    wspec = pl.BlockSpec((1, 1, D, D), lambda e, b: (l, e, 0, 0))
    in_specs += [wspec, wspec, wspec]
    args += [w_gate, w_up, w_down]
    scratch = [pltpu.VMEM((D, D), BF16)] * 3 + [pltpu.VMEM((EXP_SB * CAP_LAT, 1), F32)]
    if has_ctx:
        scratch += [pltpu.VMEM((nbatch * CAP_CTX, D), BF16), pltpu.VMEM((nbatch * CAP_CTX, 1), F32)]
    return pl.pallas_call(
        functools.partial(_expert_kernel, has_ctx=has_ctx),
        grid=(N_EXPERTS, nbatch // EXP_SB),
        in_specs=in_specs,
        out_specs=out_specs,
        out_shape=out_shape,
        scratch_shapes=scratch,
        compiler_params=_cp("arbitrary", "arbitrary"),
        name="experts",
    )(*args)


def _combine_kernel(*refs, has_ctx):
    if has_ctx:
        w0_ref, fits_ref, x1_ref, yl_ref, scl_ref, yc_ref, scc_ref, gate2_ref, g_ref, o_ref, ycat = refs
    else:
        w0_ref, fits_ref, x1_ref, yl_ref, scl_ref, gate2_ref, g_ref, o_ref, ycat = refs
    b = pl.program_id(0)
    j = pl.program_id(1)

    def finish(acc):
        o_ref[0] = x1_ref[0] + gate2_ref[0] * _rms(acc, g_ref[...])

    def comb(sc_ref, y_of, cap):
        lane = lax.broadcasted_iota(jnp.int32, (ROW_BLK, cap), 1)
        sc = sc_ref[0]
        acc = jnp.zeros((ROW_BLK, D), F32)
        for e in range(N_EXPERTS):
            pt = jnp.where(sc[:, e:e + 1] == lane, 1.0, 0.0).astype(BF16)
            acc = acc + jnp.dot(pt, y_of(e), preferred_element_type=F32)
        finish(acc)

    def comb_windowed():
        lane = lax.broadcasted_iota(jnp.int32, (ROW_BLK, ROW_BLK), 1)
        sc = scl_ref[0]
        per = ROW_BLK // MOE_WIN
        if per * MOE_WIN < ROW_BLK:
            ycat[:, per * MOE_WIN:, :] = jnp.zeros((2, ROW_BLK - per * MOE_WIN, D), BF16)
        acc = jnp.zeros((ROW_BLK, D), F32)
        for gi, e0 in enumerate(range(0, N_EXPERTS, per)):
            hit = None
            for q, e in enumerate(range(e0, min(e0 + per, N_EXPERTS))):
                w0 = pl.multiple_of(w0_ref[(b * LAT_BLKS + j) * N_EXPERTS + e], 16)
                d = sc[:, e:e + 1] - w0
                d = jnp.where(jnp.logical_and(d >= 0, d < MOE_WIN), d + MOE_WIN * q, -1)
                hit = (d == lane) if hit is None else jnp.logical_or(hit, d == lane)
                ycat[gi % 2, MOE_WIN * q:MOE_WIN * q + MOE_WIN, :] = yl_ref[0, e, pl.ds(w0, MOE_WIN), :]
            acc = acc + jnp.dot(jnp.where(hit, 1.0, 0.0).astype(BF16), ycat[gi % 2], preferred_element_type=F32)
        finish(acc)

    lat_y = lambda e: yl_ref[0, e]
    fits = fits_ref[b] == 1

    @pl.when(jnp.logical_and(j < LAT_BLKS, fits))
    def _():
        comb_windowed()

    @pl.when(jnp.logical_and(j < LAT_BLKS, jnp.logical_not(fits)))
    def _():
        comb(scl_ref, lat_y, CAP_LAT)

    if has_ctx:
        @pl.when(j == LAT_BLKS)
        def _():
            comb(scc_ref, lambda e: yc_ref[e], CAP_CTX)


def _combine(win0, fits, x1, y_lat, scol_lat, y_ctx, scol_ctx, mod3, g, has_ctx, l):
    nbatch = x1.shape[0]
    nb = NBLK if has_ctx else LAT_BLKS
    ntok = N if has_ctx else T
    smem = pl.BlockSpec(memory_space=pltpu.SMEM)
    in_specs = [smem, smem,
                pl.BlockSpec((1, ROW_BLK, D), lambda b, j: (b, j, 0)),
                pl.BlockSpec((1, N_EXPERTS, CAP_LAT, D), lambda b, j: (b, 0, 0, 0)),
                pl.BlockSpec((1, ROW_BLK, N_EXPERTS), lambda b, j: (b, jnp.minimum(j, LAT_BLKS - 1), 0))]
    args = [win0, fits, x1, y_lat, scol_lat]
    if has_ctx:
        in_specs += [pl.BlockSpec((N_EXPERTS, CAP_CTX, D), lambda b, j: (0, b, 0)),
                     pl.BlockSpec((1, ROW_BLK, N_EXPERTS), lambda b, j: (b, 0, 0))]
        args += [y_ctx, scol_ctx]
    in_specs += [pl.BlockSpec((1, 1, D), lambda b, j: (jnp.where(j == LAT_BLKS, nbatch, b), 0, 5)),
                 _lspec(g, l)]
    args += [mod3, g]
    return pl.pallas_call(
        functools.partial(_combine_kernel, has_ctx=has_ctx),
        grid=(nbatch, nb),
        in_specs=in_specs,
        out_specs=pl.BlockSpec((1, ROW_BLK, D), lambda b, j: (b, j, 0)),
        out_shape=jax.ShapeDtypeStruct((nbatch, ntok, D), F32),
        scratch_shapes=[pltpu.VMEM((2, ROW_BLK, D), BF16)],
        compiler_params=_cp("arbitrary", "arbitrary"),
        name="combine",
    )(*args)


def _rope_tables():
    f = 1.0 / (ROPE_THETA ** (jnp.arange(0, 16, 2, dtype=F32) / 16))
    pos = jnp.arange(T)
    row, col = pos // GRID_W, pos % GRID_W
    ar = row.astype(F32)[:, None] * f[None, :]
    ac = col.astype(F32)[:, None] * f[None, :]
    cos32 = jnp.concatenate([jnp.cos(ar), jnp.cos(ar), jnp.cos(ac), jnp.cos(ac)], axis=-1)
    sin32 = jnp.concatenate([jnp.sin(ar), jnp.sin(ar), jnp.sin(ac), jnp.sin(ac)], axis=-1)
    cos32 = jnp.concatenate([cos32, jnp.ones((LC, 32), F32)], axis=0)
    sin32 = jnp.concatenate([sin32, jnp.zeros((LC, 32), F32)], axis=0)
    return cos32, sin32


def _rot_cols(w):
    a, b, c, d = w[..., 0:8], w[..., 8:16], w[..., 16:24], w[..., 24:32]
    return jnp.concatenate([-b, a, -d, c], axis=-1)


def _mla_weights(w_uq, w_ukv):
    nl, r, _ = w_uq.shape
    wq3 = w_uq.reshape(nl, r, 4, 96)
    z32 = jnp.zeros((nl, r, 4, 32), F32)
    wq = jnp.concatenate([wq3, z32], axis=-1).reshape(nl, r, 512)
    wqr = jnp.concatenate([jnp.zeros((nl, r, 4, 64), F32), _rot_cols(wq3[..., 64:96]), z32], axis=-1).reshape(nl, r, 512)
    rk = w_ukv.shape[1]
    wkv3 = w_ukv.reshape(nl, rk, 4, 128)
    wk = jnp.concatenate([wkv3[..., :64], jnp.zeros((nl, rk, 4, 64), F32)], axis=-1).reshape(nl, rk, 512)
    z64 = jnp.zeros((nl, rk, 64), F32)
    vh = [wkv3[:, :, h, 64:] for h in range(4)]
    wv = jnp.concatenate([vh[0], z64, z64, vh[1], vh[2], z64, z64, vh[3]], axis=-1)
    vone = np.tile(np.repeat(np.array([0.0, 1.0, 1.0, 0.0], np.float32), 64), 2)[None, :]
    e = np.zeros((32, 512), np.float32)
    for h in range(4):
        e[np.arange(32), 128 * h + 64 + np.arange(32)] = 1.0
    return (wq.astype(BF16), wqr.astype(BF16), wk.astype(BF16), wv.astype(BF16), jnp.asarray(e, BF16),
            jnp.asarray(vone))


def _inproj_weights(w):
    o = np.cumsum([0, 256, 128, 32, 768, 256, 256, 768, 8, 4096])
    w = w.astype(BF16)
    seg = lambda i: w[:, :, o[i]:o[i + 1]]
    nl = w.shape[0]
    kr = seg(2)
    wm = jnp.concatenate([seg(0), seg(1), kr, _rot_cols(kr), jnp.zeros((nl, D, 64), BF16)], axis=-1)
    w2 = jnp.concatenate([seg(5), seg(6), seg(7), jnp.zeros((nl, D, 120), BF16)], axis=-1)
    return [wm, seg(3), seg(4), w2, 0.5 * seg(8)]


def kernel(x, c, ctx, c_ctx, w_ada, b_ada, g_pre_mix, g_post_mix, g_pre_ffn, g_post_ffn, w_in, mla_g_cq, mla_g_ckv, mla_w_uq, mla_w_ukv, na_rpb, s5_a_re, s5_a_im, s5_log_step, s5_b_re, s5_b_im, s5_c_re, s5_c_im, s5_d, s5_w_glu, s5_b_glu, m2_conv_w, m2_conv_b, m2_a_log, m2_dt_bias, m2_d, m2_g_norm, w_branch, w_out, w_router, w_gate, w_up, w_down):
    nbatch = x.shape[0]
    depth = w_ada.shape[0]
    vec = lambda v: v.reshape(depth, 1, -1).astype(F32)

    stream = (x, ctx, 0)
    cvec = jnp.concatenate([c, c_ctx[None, :], jnp.zeros((7, D), F32)], axis=0)
    cvec = cvec[: ((nbatch + 1 + 7) // 8) * 8]
    b_ada3 = b_ada.reshape(depth, 1, 6 * D)

    cos32, sin32 = _rope_tables()
    qscale = (64 + 32) ** -0.5 * math.log2(math.e)
    lane_is_rope = np.tile(np.concatenate([np.zeros(64, bool), np.ones(32, bool), np.zeros(32, bool)]), 4)
    pick = lambda t32, fill: jnp.where(lane_is_rope[None, :], jnp.tile(jnp.pad(t32, ((0, 0), (64, 32))), (1, 4)), fill)
    cos_q = pick(cos32, 1.0) * qscale
    sin_q = pick(sin32, 0.0) * qscale

    g_pre_mix, g_post_mix, g_pre_ffn, g_post_ffn = vec(g_pre_mix), vec(g_post_mix), vec(g_pre_ffn), vec(g_post_ffn)
    w_in_segs = _inproj_weights(w_in)
    wq, wqr, wk, wv, e_mat, vone = _mla_weights(mla_w_uq, mla_w_ukv)
    mla_g_cq, mla_g_ckv = vec(mla_g_cq), vec(mla_g_ckv)
    na_tabs = _na_tables(na_rpb)
    s5_bmat, s5_avec, s5_cmat = _s5_params(s5_a_re, s5_a_im, s5_log_step, s5_b_re, s5_b_im, s5_c_re, s5_c_im, nbatch)
    s5_d, s5_b_glu, s5_w_glu = vec(s5_d), vec(s5_b_glu), s5_w_glu.astype(BF16)
    m2_conv_w, m2_conv_b = m2_conv_w.astype(F32), vec(m2_conv_b)
    dt_bias = jnp.pad(m2_dt_bias.reshape(depth, 1, 8).astype(F32), ((0, 0), (0, 0), (0, 120)))
    a_col = jnp.pad(-jnp.exp(m2_a_log.astype(F32)).reshape(depth, 8, 1), ((0, 0), (0, 120), (0, 0)))
    m2_dvec, m2_g_norm = vec(jnp.repeat(m2_d, 64, axis=-1)), vec(m2_g_norm)
    w_branch, w_out = (0.5 * w_branch).astype(BF16), w_out.astype(BF16)
    w_router_t = jnp.swapaxes(w_router, 1, 2).astype(F32)

    for l in range(depth):
        has_ctx = l < depth - 1
        nb = NBLK if has_ctx else LAT_BLKS
        mod = _ada(cvec, w_ada, b_ada3, l)
        mod3 = mod.reshape(mod.shape[0], 1, 6 * D)

        p_mla, p_na, p_s5, p_m2, p_gate = _inproj(stream, g_pre_mix, mod3, w_in_segs, nbatch, l, has_ctx)

        q, k, v = _mla_prep(p_mla, mla_g_cq, mla_g_ckv, wq, wqr, wk, wv, e_mat, vone, cos_q, sin_q, cos32, sin32, l)
        o_mla = _mla_attn(q.reshape(nbatch, N, 512), k.reshape(nbatch, N, 512), v.reshape(nbatch, N, 512), has_ctx)

        o_na = _na_attn(p_na.reshape(nbatch, N, NA_W), na_tabs, has_ctx, l)

        yf, yb = _s5_scan(p_s5.reshape(nbatch, N, S5_W), s5_bmat, s5_avec, s5_cmat, l)
        o_s5 = _s5_glu(p_s5, yf.reshape(nbatch * N, S5_W), yb.reshape(nbatch * N, S5_W), s5_d, s5_w_glu, s5_b_glu, l)

        xbc, dtt, bmt = _m2_prep(p_m2.reshape(nbatch, N, M2_W), m2_conv_w, m2_conv_b, dt_bias, l)
        ssd_f, ssd_b = _ssd(xbc, dtt, bmt, a_col, l)
        o_m2 = _m2_norm(xbc.reshape(nbatch * N, 768), p_m2, ssd_f.reshape(nbatch * N, 256),
                        ssd_b.reshape(nbatch * N, 256), m2_dvec, m2_g_norm, l)

        x1, h2, aff = _merge(stream, o_mla, o_na.reshape(-1, 256), o_s5, o_m2, p_gate,
                             w_branch, w_out, g_post_mix, mod3, g_pre_ffn, w_router_t, nbatch, nb, l)

        slot_lat = _topk(aff, 0, LAT_BLKS, CAP_LAT)
        slot_ctx = _topk(aff, LAT_BLKS, 1, CAP_CTX) if has_ctx else None
        cnt = jnp.sum((slot_lat >= 0).astype(jnp.int32), axis=-1)
        first = jnp.cumsum(cnt, axis=1) - cnt
        win0 = jnp.minimum((first // 16) * 16, CAP_LAT - MOE_WIN)
        fits = jnp.all(first + cnt <= win0 + MOE_WIN, axis=(1, 2)).astype(jnp.int32)
        win0 = win0.reshape(-1)
        h2 = h2.reshape(nbatch, nb * ROW_BLK, D)
        ys = _experts(win0, fits, _gather(win0, fits, h2, slot_lat), h2, slot_lat, slot_ctx, aff, w_gate, w_up, w_down,
                      l, has_ctx)
        scol_lat = jnp.transpose(slot_lat, (0, 1, 3, 2)).reshape(nbatch, T, N_EXPERTS)
        scol_ctx = jnp.transpose(slot_ctx, (0, 1, 3, 2)).reshape(nbatch, LC, N_EXPERTS) if has_ctx else None
        xs = _combine(win0, fits, x1.reshape(nbatch, nb * ROW_BLK, D), ys[0], scol_lat, ys[1] if has_ctx else None,
                      scol_ctx, mod3, g_post_ffn, has_ctx, l)
        stream = (xs, xs, LAT_BLKS)
    return xs
```

```python
import functools
import math

import numpy as np
import jax
import jax.numpy as jnp
from jax import lax
from jax.experimental import pallas as pl
from jax.experimental.pallas import tpu as pltpu

F32 = jnp.float32
BF16 = jnp.bfloat16

D = 1024
T = 2048
LC = 256
N = T + LC
GRID_W = 64
ROW_BLK = 256
NBLK = N // ROW_BLK
LAT_BLKS = T // ROW_BLK
EPS = 1e-6
N_EXPERTS = 16
CAP_LAT = 2 * T // N_EXPERTS
CAP_CTX = 2 * LC // N_EXPERTS
ROPE_THETA = 10000.0
NEG = -1e30

MLA_W = 512
NA_W = 768
S5_W = 256
M2_W = 1152
GATE_W = 4096

MOE_WIN = 80
EXP_SB = 4
S5_CHUNK = 64
SSD_CHUNK = 128
SSD_SB = 8
NA_SB = 2

VMEM_LIMIT = 56 * 1024 * 1024


def _cp(*sem, vmem=VMEM_LIMIT):
    return pltpu.CompilerParams(dimension_semantics=sem, vmem_limit_bytes=vmem)


def _lspec(a, l, *lead):
    nd = a.ndim - 1 - len(lead)
    return pl.BlockSpec((None,) * (1 + len(lead)) + a.shape[1 + len(lead):], lambda *_: (l,) + lead + (0,) * nd)


def _dot(a, b):
    return jnp.dot(a.astype(BF16), b.astype(BF16), preferred_element_type=F32)


def _dot_nt(a, b):
    return lax.dot_general(a.astype(BF16), b.astype(BF16), (((1,), (1,)), ((), ())),
                           preferred_element_type=F32)


def _split3(a):
    hi = a.astype(BF16)
    r = a - hi.astype(F32)
    mid = r.astype(BF16)
    lo = (r - mid.astype(F32)).astype(BF16)
    return hi, mid, lo


def _dot_hi(a, b):
    ah, am, _ = _split3(a)
    bh, bm, _ = _split3(b)
    f = lambda x, y: jnp.dot(x, y, preferred_element_type=F32)
    return f(ah, bh) + (f(ah, bm) + f(am, bh))


def _dot_hi_nt(a, b):
    ah, am, _ = _split3(a)
    bh, bm, _ = _split3(b)
    f = lambda x, y: lax.dot_general(x, y, (((1,), (1,)), ((), ())), preferred_element_type=F32)
    return f(ah, bh) + (f(ah, bm) + f(am, bh))


def _dot_exact_lhs(m_bf16, a):
    h, m, l = _split3(a)
    f = lambda y: jnp.dot(m_bf16, y, preferred_element_type=F32)
    return f(h) + (f(m) + f(l))


def _dot_exact_rhs(a, m_bf16):
    h, m, l = _split3(a)
    f = lambda y: jnp.dot(y, m_bf16, preferred_element_type=F32)
    return f(h) + (f(m) + f(l))


def _sigmoid(x):
    return 0.5 * jnp.tanh(0.5 * x) + 0.5


def _silu(x):
    return x * _sigmoid(x)


def _rms(x, g):
    return x * lax.rsqrt(jnp.mean(x * x, axis=-1, keepdims=True) + EPS) * g


def _mod_row(nb):
    def f(i, nbatch):
        return jnp.where(i % nb == LAT_BLKS, nbatch, i // nb)
    return f


def _flat_blk(i, nb):
    return (i // nb) * NBLK + i % nb


def _stream_specs(stream, nb):
    lat, ctxa, cblk = stream
    w = lat.shape[-1]
    return [lat, ctxa], [pl.BlockSpec((1, ROW_BLK, w), lambda i: (i // nb, jnp.minimum(i % nb, LAT_BLKS - 1), 0)),
                         pl.BlockSpec((1, ROW_BLK, w), lambda i: (i // nb, cblk, 0))]


def _stream_block(xl_ref, xc_ref, nb):
    return jnp.where(pl.program_id(0) % nb == LAT_BLKS, xc_ref[0], xl_ref[0])


def _ada_kernel(c_ref, w_ref, b_ref, o_ref):
    c = c_ref[...]
    o_ref[...] = _dot_hi(_silu(c), w_ref[0]) + b_ref[0]


def _ada(cvec, w_ada, b_ada3, l):
    rows = cvec.shape[0]
    return pl.pallas_call(
        _ada_kernel,
        grid=(6,),
        in_specs=[pl.BlockSpec((rows, D), lambda k: (0, 0)),
                  pl.BlockSpec((1, D, D), lambda k: (l, 0, k)),
                  pl.BlockSpec((1, 1, D), lambda k: (l, 0, k))],
        out_specs=pl.BlockSpec((rows, D), lambda k: (0, k)),
        out_shape=jax.ShapeDtypeStruct((rows, 6 * D), F32),
        compiler_params=_cp("arbitrary"),
        name="ada",
    )(cvec, w_ada, b_ada3)


def _inproj_kernel(xl_ref, xc_ref, g_ref, sh_ref, sc_ref, wm, wn, ws, w2, wg, om, on, os_, o2, og, *, ctx_gates):
    x = _stream_block(xl_ref, xc_ref, NBLK)
    h = (_rms(x, g_ref[...]) * (1.0 + sc_ref[0]) + sh_ref[0]).astype(BF16)
    om[...] = jnp.dot(h, wm[...], preferred_element_type=F32)
    on[...] = jnp.dot(h, wn[...], preferred_element_type=F32).astype(BF16)
    os_[...] = jnp.dot(h, ws[...], preferred_element_type=F32)
    o2[...] = jnp.dot(h, w2[...], preferred_element_type=F32)
    if ctx_gates:
        og[...] = jnp.dot(h, wg[...], preferred_element_type=F32)
    else:
        is_ctx = pl.program_id(0) % NBLK == LAT_BLKS

        @pl.when(jnp.logical_not(is_ctx))
        def _():
            og[...] = jnp.dot(h, wg[...], preferred_element_type=F32)

        @pl.when(is_ctx)
        def _():
            og[...] = jnp.zeros_like(og)


def _inproj(stream, g, mod3, ws, nbatch, l, ctx_gates):
    rows = nbatch * N
    nblk = rows // ROW_BLK
    mrow = _mod_row(NBLK)
    xargs, xspecs = _stream_specs(stream, NBLK)
    full = lambda w: _lspec(w, l)
    widths = (MLA_W, NA_W, S5_W, M2_W, GATE_W)
    dts = (F32, BF16, F32, F32, F32)
    return pl.pallas_call(
        functools.partial(_inproj_kernel, ctx_gates=ctx_gates),
        grid=(nblk,),
        in_specs=xspecs + [
                  full(g),
                  pl.BlockSpec((1, 1, D), lambda i: (mrow(i, nbatch), 0, 0)),
                  pl.BlockSpec((1, 1, D), lambda i: (mrow(i, nbatch), 0, 1))]
                 + [full(w) for w in ws],
        out_specs=[pl.BlockSpec((ROW_BLK, w), lambda i: (i, 0)) for w in widths],
        out_shape=[jax.ShapeDtypeStruct((rows, w), dt) for w, dt in zip(widths, dts)],
        compiler_params=_cp("arbitrary"),
        name="inproj",
    )(*xargs, g, mod3, mod3, *ws)


def _mla_prep_kernel(p_ref, gq_ref, gkv_ref, wq_ref, wqr_ref, wk_ref, wv_ref, e_ref, vone_ref,
                     cos_ref, sin_ref, ck_ref, sk_ref, q_out, k_out, v_out):
    p = p_ref[...]
    cqn = _rms(p[:, :256], gq_ref[...]).astype(BF16)
    q = jnp.dot(cqn, wq_ref[...], preferred_element_type=F32)
    qr = jnp.dot(cqn, wqr_ref[...], preferred_element_type=F32)
    q_out[...] = (q * cos_ref[...] + qr * sin_ref[...]).astype(BF16)
    ckvn = _rms(p[:, 256:384], gkv_ref[...]).astype(BF16)
    kro = (p[:, 384:416] * ck_ref[...] + p[:, 416:448] * sk_ref[...]).astype(BF16)
    k = jnp.dot(ckvn, wk_ref[...], preferred_element_type=F32) + jnp.dot(kro, e_ref[...], preferred_element_type=F32)
    k_out[...] = k.astype(BF16)
    v_out[...] = (jnp.dot(ckvn, wv_ref[...], preferred_element_type=F32) + vone_ref[...]).astype(BF16)


def _mla_prep(p_mla, gq, gkv, wq, wqr, wk, wv, e, vone, cos_q, sin_q, cos_k, sin_k, l):
    rows = p_mla.shape[0]
    full = lambda w: _lspec(w, l)
    const = lambda w: pl.BlockSpec(w.shape, lambda j, b: (0, 0))
    rb = 3 * ROW_BLK
    row = lambda j, b: (b * (N // rb) + j, 0)
    tab = lambda w: pl.BlockSpec((rb, w), lambda j, b: (j, 0))
    return pl.pallas_call(
        _mla_prep_kernel,
        grid=(N // rb, rows // N),
        in_specs=[pl.BlockSpec((rb, MLA_W), row),
                  full(gq), full(gkv), full(wq), full(wqr), full(wk), full(wv), const(e), const(vone),
                  tab(512), tab(512), tab(32), tab(32)],
        out_specs=[pl.BlockSpec((rb, 512), row)] * 3,
        out_shape=[jax.ShapeDtypeStruct((rows, 512), BF16)] * 3,
        compiler_params=_cp("arbitrary", "arbitrary"),
        name="mla_prep",
    )(p_mla, gq, gkv, wq, wqr, wk, wv, e, vone, cos_q, sin_q, cos_k, sin_k)


def _mla_attn_kernel(q_ref, k_ref, v_ref, o_ref):
    lane = lax.broadcasted_iota(jnp.int32, (q_ref.shape[1], 128), 1)
    for hp in range(2):
        pv = []
        for h in (2 * hp, 2 * hp + 1):
            s = _dot_nt(q_ref[0, :, 128 * h:128 * h + 128], k_ref[0, :, 128 * h:128 * h + 128])
            p = jnp.exp2(s - jnp.max(s, axis=-1, keepdims=True)).astype(BF16)
            pv.append(jnp.dot(p, v_ref[0, :, 128 * h:128 * h + 128], preferred_element_type=F32))
        oa, ob = pv
        o = jnp.where(lane < 64, oa * (1.0 / oa[:, 64:65]), ob * (1.0 / ob[:, 0:1]))
        o_ref[0, :, 128 * hp:128 * hp + 128] = o.astype(o_ref.dtype)


def _mla_attn(q, k, v, has_ctx):
    nbatch = q.shape[0]
    qb = 2 * ROW_BLK

    def call(grid, qspec, kvspec, ospec, ntok):
        return pl.pallas_call(
            _mla_attn_kernel,
            grid=grid,
            in_specs=[qspec, kvspec, kvspec],
            out_specs=ospec,
            out_shape=jax.ShapeDtypeStruct((nbatch, ntok, 256), BF16),
            compiler_params=_cp("arbitrary", "arbitrary"),
            name="mla_attn",
        )(q, k, v)

    o_lat = call((nbatch, T // qb), pl.BlockSpec((1, qb, 512), lambda b, j: (b, j, 0)),
                 pl.BlockSpec((1, N, 512), lambda b, j: (b, 0, 0)),
                 pl.BlockSpec((1, qb, 256), lambda b, j: (b, j, 0)), T)
    if not has_ctx:
        return o_lat, o_lat
    ctx_blk = pl.BlockSpec((1, LC, 512), lambda b, j: (b, LAT_BLKS, 0))
    o_ctx = call((nbatch, 1), ctx_blk, ctx_blk, pl.BlockSpec((1, LC, 256), lambda b, j: (b, 0, 0)), LC)
    return o_lat, o_ctx


def _na_kernel(q_ref, kc_ref, k0_ref, k1_ref, k2_ref, vc_ref, v0_ref, v1_ref, v2_ref, b_ref, o_ref, *, has_ctx):
    scale = jnp.asarray(0.125, BF16)

    lane = lax.broadcasted_iota(jnp.int32, (ROW_BLK, 128), 1)
    lane64 = lax.broadcasted_iota(jnp.int32, (GRID_W, 128), 1) < 64
    grp = jnp.minimum(pl.program_id(0), LAT_BLKS - 1)
    krow0 = 4 * jnp.clip(grp - 1, 0, LAT_BLKS - 3)

    def bias_piece(h, i):
        rows = []
        for qr in range(4):
            r = 4 * grp + qr
            rs = jnp.clip(r - 4, 0, 24)
            tiles = []
            for pr in range(2):
                kr = krow0 + (4 * i + 2 * pr)
                t = b_ref[h, jnp.clip(kr - r + 8, 0, 15)]
                ok_a = jnp.logical_and(kr >= rs, kr < rs + 8).astype(jnp.int32)
                ok_b = jnp.logical_and(kr + 1 >= rs, kr + 1 < rs + 8).astype(jnp.int32)
                tiles.append(jnp.where(jnp.where(lane64, ok_a, ok_b) > 0, t, NEG))
            rows.append(jnp.concatenate(tiles, axis=1))
        return jnp.concatenate(rows, axis=0)

    def heads(win, sm):
        kws = (k0_ref, k1_ref, k2_ref)
        vws = (v0_ref, v1_ref, v2_ref)
        for hp in range(2):
            sl = slice(128 * hp, 128 * hp + 128)
            qp = q_ref[sm, :, sl] * scale
            outs = []
            for hh in range(2):
                h = 2 * hp + hh
                qh = jnp.where((lane < 64) if hh == 0 else (lane >= 64), qp, jnp.zeros_like(qp))
                s_c = _dot_nt(qh, kc_ref[sm, :, sl])
                s_w = []
                smax = s_c
                if win:
                    for i in range(3):
                        s = _dot_nt(qh, kws[i][sm, :, sl]) + bias[h][i]
                        s_w.append(s)
                        smax = jnp.maximum(smax, s)
                m = jnp.max(smax, axis=-1, keepdims=True)
                p = jnp.exp(s_c - m)
                psum = p
                o = _dot(p, vc_ref[sm, :, sl])
                for i, s in enumerate(s_w):
                    p = jnp.exp(s - m)
                    psum = psum + p
                    o = o + _dot(p, vws[i][sm, :, sl])
                outs.append(o * (1.0 / jnp.sum(psum, axis=-1, keepdims=True)))
            o_ref[sm, :, sl] = jnp.where(lane < 64, outs[0], outs[1]).astype(o_ref.dtype)

    def windowed():
        nonlocal bias
        bias = [[bias_piece(h, i) for i in range(3)] for h in range(4)]
        for sm in range(NA_SB):
            heads(True, sm)

    bias = None
    if has_ctx:
        g = pl.program_id(0)

        @pl.when(g < LAT_BLKS)
        def _():
            windowed()

        @pl.when(g == LAT_BLKS)
        def _():
            for sm in range(NA_SB):
                heads(False, sm)
    else:
        windowed()


def _na_attn(qkv, bias, has_ctx, l):
    nbatch = qkv.shape[0]
    ng = NBLK if has_ctx else LAT_BLKS
    j0 = lambda g: jnp.clip(g - 1, 0, LAT_BLKS - 3)
    blk = lambda f: pl.BlockSpec((NA_SB, ROW_BLK, 256), f)
    return pl.pallas_call(
        functools.partial(_na_kernel, has_ctx=has_ctx),
        grid=(ng, nbatch // NA_SB),
        in_specs=[blk(lambda g, b: (b, g, 0)),
                  blk(lambda g, b: (b, LAT_BLKS, 1)),
                  blk(lambda g, b: (b, j0(g), 1)),
                  blk(lambda g, b: (b, j0(g) + 1, 1)),
                  blk(lambda g, b: (b, j0(g) + 2, 1)),
                  blk(lambda g, b: (b, LAT_BLKS, 2)),
                  blk(lambda g, b: (b, j0(g), 2)),
                  blk(lambda g, b: (b, j0(g) + 1, 2)),
                  blk(lambda g, b: (b, j0(g) + 2, 2)),
                  _lspec(bias, l)],
        out_specs=blk(lambda g, b: (b, g, 0)),
        out_shape=jax.ShapeDtypeStruct((nbatch, ng * ROW_BLK, 256), BF16),
        compiler_params=_cp("arbitrary", "arbitrary"),
        name="na_attn",
    )(qkv, qkv, qkv, qkv, qkv, qkv, qkv, qkv, qkv, bias)


def _na_col_structure():
    onehot = np.zeros((31, 64, 64), np.float32)
    colmask = np.zeros((64, 64), bool)
    for c in range(64):
        cs = min(max(c - 8, 0), 48)
        for kc in range(cs, cs + 16):
            onehot[kc - c + 15, c, kc] = 1.0
            colmask[c, kc] = True
    return np.tile(onehot.reshape(31, 4096), (3, 1)), colmask


_NA_COL_ONEHOT, _NA_COLMASK = _na_col_structure()


def _na_tables(rpb):
    nl = rpb.shape[0]
    r = rpb.astype(F32)
    hi = lax.reduce_precision(r, 8, 7)
    mid = lax.reduce_precision(r - hi, 8, 7)
    parts = jnp.stack([hi, mid, r - hi - mid], axis=3).reshape(nl * 60, 93)
    tz = jnp.dot(parts, _NA_COL_ONEHOT, preferred_element_type=F32).reshape(nl, 4, 15, 64, 64)
    tz = jnp.where(_NA_COLMASK, tz, NEG)
    edge = jnp.full((nl, 4, 1, 64, 64), NEG, F32)
    tz17 = jnp.concatenate([edge, tz, edge], axis=2)
    return jnp.concatenate([tz17[:, :, :16], tz17[:, :, 1:]], axis=-1)


def _s5_kernel(uf_ref, ub_ref, bf_ref, bb_ref, af_ref, ab_ref, cf_ref, cb_ref, yf_ref, yb_ref,
               hf, hb, buf_f, buf_b, tm_f, tm_b):
    i = pl.program_id(0)
    half = 1024
    nb = uf_ref.shape[0]

    @pl.when(i == 0)
    def _():
        hf[...] = jnp.zeros_like(hf)
        hb[...] = jnp.zeros_like(hb)

    def expand(u_ref, tm, b_ref, buf):
        for b in range(nb):
            for c in range(2):
                tm[c, pl.ds(b, S5_CHUNK, stride=nb), :] = u_ref[b, :, 128 * c:128 * c + 128]
        buf[...] = _dot(jnp.concatenate([tm[0], tm[1]], axis=1), b_ref[...])

    def scan(buf, a_ref, h, order):
        hr, hi = h[:, :half], h[:, half:]
        ar, ai = a_ref[:, :half], a_ref[:, half:]
        for k in order:
            rows = slice(k * nb, (k + 1) * nb)
            nr = ar * hr - ai * hi + buf[rows, :half]
            ni = ar * hi + ai * hr + buf[rows, half:]
            buf[rows, :half] = nr
            buf[rows, half:] = ni
            hr, hi = nr, ni
        h[:, :half] = hr
        h[:, half:] = hi

    def readout(buf, c_ref, tm, y_ref):
        y = _dot(buf[...], c_ref[...])
        for c in range(2):
            tm[c] = y[:, 128 * c:128 * c + 128]
        for b in range(nb):
            for c in range(2):
                y_ref[b, :, 128 * c:128 * c + 128] = tm[c, pl.ds(b, S5_CHUNK, stride=nb), :]

    expand(uf_ref, tm_f, bf_ref, buf_f)
    expand(ub_ref, tm_b, bb_ref, buf_b)
    scan(buf_f, af_ref, hf, range(S5_CHUNK))
    readout(buf_f, cf_ref, tm_f, yf_ref)
    scan(buf_b, ab_ref, hb, range(S5_CHUNK - 1, -1, -1))
    readout(buf_b, cb_ref, tm_b, yb_ref)


def _s5_scan(u, bmat, avec, cmat, l):
    nbatch = u.shape[0]
    cr = S5_CHUNK * nbatch
    nch = N // S5_CHUNK
    nctx = LC // S5_CHUNK
    fidx = lambda i: jnp.where(i < nctx, nch - nctx + i, i - nctx)
    bidx = lambda i: nch - 1 - i
    ublk = lambda f: pl.BlockSpec((nbatch, S5_CHUNK, 256), lambda i: (0, f(i), 0))
    return pl.pallas_call(
        _s5_kernel,
        grid=(nch,),
        in_specs=[ublk(fidx), ublk(bidx),
                  _lspec(bmat, l, 0), _lspec(bmat, l, 1), _lspec(avec, l, 0), _lspec(avec, l, 1),
                  _lspec(cmat, l, 0), _lspec(cmat, l, 1)],
        out_specs=[ublk(fidx), ublk(bidx)],
        out_shape=[jax.ShapeDtypeStruct((nbatch, N, 256), F32)] * 2,
        scratch_shapes=[pltpu.VMEM((nbatch, 2048), F32), pltpu.VMEM((nbatch, 2048), F32),
                        pltpu.VMEM((cr, 2048), F32), pltpu.VMEM((cr, 2048), F32),
                        pltpu.VMEM((2, cr, 128), F32), pltpu.VMEM((2, cr, 128), F32)],
        compiler_params=_cp("arbitrary"),
        name="s5_scan",
    )(u, u, bmat, bmat, avec, avec, cmat, cmat)


def _s5_glu_kernel(u_ref, yf_ref, yb_ref, d_ref, w_ref, b_ref, o_ref):
    y = d_ref[...] * u_ref[...] + yf_ref[...] + yb_ref[...]
    z = y * (0.5 * (1.0 + jnp.tanh(math.sqrt(2.0 / math.pi) * (y + 0.044715 * (y * y * y)))))
    o_ref[...] = (z * _sigmoid(_dot(z, w_ref[...]) + b_ref[...])).astype(o_ref.dtype)


def _s5_glu(u_tb, yf, yb, d, w, b, l):
    rows = u_tb.shape[0]
    rb = 4 * ROW_BLK
    blk = pl.BlockSpec((rb, 256), lambda i: (i, 0))
    full = lambda a: _lspec(a, l)
    return pl.pallas_call(
        _s5_glu_kernel,
        grid=(rows // rb,),
        in_specs=[blk, blk, blk, full(d), full(w), full(b)],
        out_specs=blk,
        out_shape=jax.ShapeDtypeStruct((rows, 256), BF16),
        compiler_params=_cp("arbitrary"),
        name="s5_glu",
    )(u_tb, yf, yb, d, w, b)


def _s5_params(a_re, a_im, log_step, b_re, b_im, c_re, c_im, nbatch):
    nl = a_re.shape[0]
    same_group = (np.arange(256)[:, None] // 16 == np.arange(1024)[None, :] // 64).astype(np.float32)
    a = lax.complex(a_re.astype(F32), a_im.astype(F32))
    abar = jnp.exp(jnp.exp(log_step.astype(F32))[..., None] * a)
    bbar = ((abar - 1.0) / a)[..., None] * lax.complex(b_re.astype(F32), b_im.astype(F32))
    blk_in = lambda m: jnp.tile(jnp.swapaxes(m, -1, -2).reshape(nl, 2, 256, 64), (1, 1, 1, 16)) * same_group
    bmat = jnp.concatenate([blk_in(jnp.real(bbar)), blk_in(jnp.imag(bbar))], axis=-1).astype(BF16)
    avec = jnp.concatenate([jnp.real(abar).reshape(nl, 2, 1, 1024), jnp.imag(abar).reshape(nl, 2, 1, 1024)], axis=-1)
    avec = jnp.broadcast_to(avec, (nl, 2, nbatch, 2048))
    blk_out = lambda m: jnp.tile(jnp.swapaxes(m, -1, -2).reshape(nl, 2, 1024, 16), (1, 1, 1, 16)) * same_group.T
    cmat = jnp.concatenate([blk_out(c_re.astype(F32)), -blk_out(c_im.astype(F32))], axis=-2).astype(BF16)
    return bmat, avec, cmat


def _m2_prep_kernel(x_ref, dt_ref, w_ref, b_ref, dtb_ref, xo_ref, dtt_ref, bmt_ref):
    c = pl.program_id(1)
    x = x_ref[0]
    t = lax.broadcasted_iota(jnp.int32, x.shape, 0)
    m2 = ((t >= 2) & (t < T)) | (t >= T + 2)
    m1 = ((t >= 1) & (t < T)) | (t >= T + 1)
    p1 = (t <= T - 2) | ((t >= T) & (t <= N - 2))
    w = w_ref[...]
    y = (w[0:1] * jnp.where(m2, pltpu.roll(x, 2, 0), 0.0)
         + w[1:2] * jnp.where(m1, pltpu.roll(x, 1, 0), 0.0)
         + w[2:3] * x
         + w[3:4] * jnp.where(p1, pltpu.roll(x, N - 1, 0), 0.0)) + b_ref[...]
    act = _silu(y)
    xo_ref[0] = act

    @pl.when(c == 0)
    def _():
        v = dt_ref[0] + dtb_ref[...]
        dtt_ref[0] = (jnp.maximum(v, 0.0) + jnp.log1p(jnp.exp(-jnp.abs(v)))).T

    @pl.when(c == 1)
    def _():
        bmt_ref[0] = act.T.astype(BF16)


def _m2_prep(p_m2, conv_w, conv_b, dtb, l):
    nbatch = p_m2.shape[0]
    return pl.pallas_call(
        _m2_prep_kernel,
        grid=(nbatch, 3),
        in_specs=[pl.BlockSpec((1, N, 256), lambda b, c: (b, 0, 1 + c)),
                  pl.BlockSpec((1, N, 128), lambda b, c: (b, 0, 8)),
                  pl.BlockSpec((None, 4, 256), lambda b, c: (l, 0, c)),
                  pl.BlockSpec((None, 1, 256), lambda b, c: (l, 0, c)),
                  _lspec(dtb, l)],
        out_specs=[pl.BlockSpec((1, N, 256), lambda b, c: (b, 0, c)),
                   pl.BlockSpec((1, 128, N), lambda b, c: (b, 0, 0)),
                   pl.BlockSpec((1, 256, N), lambda b, c: (b, 0, 0))],
        out_shape=[jax.ShapeDtypeStruct((nbatch, N, 768), F32),
                   jax.ShapeDtypeStruct((nbatch, 128, N), F32),
                   jax.ShapeDtypeStruct((nbatch, 256, N), BF16)],
        compiler_params=_cp("arbitrary", "arbitrary"),
        name="m2_prep",
    )(p_m2, p_m2, conv_w, conv_b, dtb)


def _ssd_kernel(xf_ref, dttf_ref, bmtf_ref, xb_ref, dttb_ref, bmtb_ref, acol_ref, yf_ref, yb_ref, hs):
    i = pl.program_id(1)
    L = SSD_CHUNK

    @pl.when(i == 0)
    def _():
        hs[...] = jnp.zeros_like(hs)

    li = lax.broadcasted_iota(jnp.int32, (L, L), 0)
    si = lax.broadcasted_iota(jnp.int32, (L, L), 1)
    left = si < 64
    pick = lambda a, b: jnp.where(left, a, b)

    def direction(sm, x_ref, dtt_ref, bmt_ref, y_ref, d, causal, causal_t, last):
        cz = jnp.where(causal, 1.0, 0.0).astype(BF16)
        czt = jnp.where(causal_t, 1.0, 0.0).astype(BF16)
        xbc = x_ref[sm]
        dtt = dtt_ref[sm]
        parts = jnp.concatenate([p.astype(F32) for p in _split3(dtt * acol_ref[...])], axis=1)
        cumt = jnp.dot(parts.astype(BF16), jnp.concatenate([czt, czt, czt], axis=0),
                       preferred_element_type=F32)
        reps = jnp.concatenate([jnp.broadcast_to(parts[4 * d + h:4 * d + h + 1, :], (L, 3 * L)) for h in range(4)],
                               axis=0).astype(BF16)
        cumb = lax.dot_general(jnp.concatenate([cz, cz, cz], axis=1), reps, (((1,), (1,)), ((), ())),
                               preferred_element_type=F32)
        for g in range(2):
            xp = xbc[:, 128 * g:128 * g + 128].astype(BF16)
            cm = xbc[:, 512 + 128 * g:640 + 128 * g].astype(BF16)
            bmt = bmt_ref[sm, 128 * g:128 * g + 128, :]
            hprev = hs[sm, d, g]
            go = jnp.dot(cm, jnp.concatenate([bmt, hprev.astype(BF16)], axis=1), preferred_element_type=F32)
            gmat, yo = go[:, :L], go[:, L:]
            bmf = bmt.astype(F32)
            lhs, cbs, cls = [], [], []
            for hh in range(2):
                h = 2 * g + hh
                cb = cumb[:, L * h:L * h + L]
                crow = cumt[4 * d + h:4 * d + h + 1, :]
                dtrow = dtt[4 * d + h:4 * d + h + 1, :]
                cl = cb[last:last + 1, :]
                lhs.append((gmat * jnp.where(causal, jnp.exp(cb - crow), 0.0) * dtrow).astype(BF16))
                lhs.append((bmf * (jnp.exp(cl - crow) * dtrow)).astype(BF16))
                cbs.append(cb)
                cls.append(cl)
            big = jnp.dot(jnp.concatenate(lhs, axis=0), xp, preferred_element_type=F32)
            y_ref[sm, :, 128 * g:128 * g + 128] = pick(big[0:L], big[2 * L:3 * L]) + yo * jnp.exp(pick(cbs[0], cbs[1]))
            hs[sm, d, g] = hprev * jnp.exp(pick(cls[0], cls[1])) + pick(big[L:2 * L], big[3 * L:4 * L])

    for sm in range(SSD_SB):
        direction(sm, xf_ref, dttf_ref, bmtf_ref, yf_ref, 0, si <= li, li <= si, L - 1)
        direction(sm, xb_ref, dttb_ref, bmtb_ref, yb_ref, 1, si >= li, li >= si, 0)


def _ssd(xbc, dtt, bmt, a_col, l):
    nbatch = xbc.shape[0]
    nch = N // SSD_CHUNK
    nctx = LC // SSD_CHUNK
    fidx = lambda i: jnp.where(i < nctx, nch - nctx + i, i - nctx)
    bidx = lambda i: nch - 1 - i
    ins = lambda f: [pl.BlockSpec((SSD_SB, SSD_CHUNK, 768), lambda b, i: (b, f(i), 0)),
                     pl.BlockSpec((SSD_SB, 128, SSD_CHUNK), lambda b, i: (b, 0, f(i))),
                     pl.BlockSpec((SSD_SB, 256, SSD_CHUNK), lambda b, i: (b, 0, f(i)))]
    return pl.pallas_call(
        _ssd_kernel,
        grid=(nbatch // SSD_SB, nch),
        in_specs=ins(fidx) + ins(bidx) + [_lspec(a_col, l)],
        out_specs=[pl.BlockSpec((SSD_SB, SSD_CHUNK, 256), lambda b, i: (b, fidx(i), 0)),
                   pl.BlockSpec((SSD_SB, SSD_CHUNK, 256), lambda b, i: (b, bidx(i), 0))],
        out_shape=[jax.ShapeDtypeStruct((nbatch, N, 256), F32)] * 2,
        scratch_shapes=[pltpu.VMEM((SSD_SB, 2, 2, 128, 128), F32)],
        compiler_params=_cp("arbitrary", "arbitrary"),
        name="ssd",
    )(xbc, dtt, bmt, xbc, dtt, bmt, a_col)


def _m2_norm_kernel(x_ref, z_ref, yf_ref, yb_ref, d_ref, g_ref, o_ref):
    y = d_ref[...] * x_ref[...] + yf_ref[...] + yb_ref[...]
    o_ref[...] = _rms(y * _silu(z_ref[...]), g_ref[...]).astype(o_ref.dtype)


def _m2_norm(xbc2d, p_m2_2d, yf2d, yb2d, dvec, gn, l):
    rows = xbc2d.shape[0]
    rb = 4 * ROW_BLK
    blk = pl.BlockSpec((rb, 256), lambda i: (i, 0))
    full = lambda a: _lspec(a, l)
    return pl.pallas_call(
        _m2_norm_kernel,
        grid=(rows // rb,),
        in_specs=[blk, blk, blk, blk, full(dvec), full(gn)],
        out_specs=blk,
        out_shape=jax.ShapeDtypeStruct((rows, 256), BF16),
        compiler_params=_cp("arbitrary"),
        name="m2_norm",
    )(xbc2d, p_m2_2d, yf2d, yb2d, dvec, gn)


def _merge_kernel(xl_ref, xc_ref, oal_ref, oac_ref, on_ref, os_ref, om_ref, gl_ref, wb_ref, wo_ref, gpm_ref, gate1_ref,
                  gpf_ref, sh2_ref, sc2_ref, wr_ref, x1_ref, h2_ref, aff_ref, *, nb):
    outs = (_stream_block(oal_ref, oac_ref, nb), on_ref[...], os_ref[...], om_ref[...])
    y = None
    for j in range(4):
        d = jnp.dot(outs[j], wb_ref[j], preferred_element_type=F32)
        t = d * jnp.tanh(gl_ref[:, 1024 * j:1024 * j + 1024]) + d
        y = t if y is None else y + t
    y2 = _dot(y, wo_ref[...])
    x1 = _stream_block(xl_ref, xc_ref, nb) + gate1_ref[0] * _rms(y2, gpm_ref[...])
    x1_ref[...] = x1
    h2 = _rms(x1, gpf_ref[...]) * (1.0 + sc2_ref[0]) + sh2_ref[0]
    h2_ref[...] = h2.astype(BF16)
    logits = _dot_hi_nt(wr_ref[...], h2)
    m = jnp.max(logits, axis=0, keepdims=True)
    e = jnp.exp(logits - m)
    aff_ref[0, 0] = e / jnp.sum(e, axis=0, keepdims=True)


def _merge(stream, o_mla, o_na, o_s5, o_m2, gl, wb, wo, gpm, mod3, gpf, wrt, nbatch, nb, l):
    mrow = _mod_row(nb)
    xargs, xspecs = _stream_specs(stream, nb)
    aargs, aspecs = _stream_specs((o_mla[0], o_mla[1], 0), nb)
    fb = lambda i: (_flat_blk(i, nb), 0)
    blk = lambda w: pl.BlockSpec((ROW_BLK, w), fb)
    oblk = lambda w: pl.BlockSpec((ROW_BLK, w), lambda i: (i, 0))
    full = lambda a: _lspec(a, l)
    modspec = lambda k: pl.BlockSpec((1, 1, D), lambda i: (mrow(i, nbatch), 0, k))
    orows = nbatch * nb * ROW_BLK
    return pl.pallas_call(
        functools.partial(_merge_kernel, nb=nb),
        grid=(nbatch * nb,),
        in_specs=xspecs + aspecs + [oblk(256), blk(256), blk(256), blk(GATE_W),
                  full(wb), full(wo), full(gpm), modspec(2), full(gpf), modspec(3), modspec(4), full(wrt)],
        out_specs=[oblk(D), oblk(D),
                   pl.BlockSpec((1, 1, N_EXPERTS, ROW_BLK), lambda i: (i // nb, i % nb, 0, 0))],
        out_shape=[jax.ShapeDtypeStruct((orows, D), F32),
                   jax.ShapeDtypeStruct((orows, D), BF16),
                   jax.ShapeDtypeStruct((nbatch, nb, N_EXPERTS, ROW_BLK), F32)],
        compiler_params=_cp("arbitrary"),
        name="merge",
    )(*xargs, *aargs, o_na, o_s5, o_m2, gl, wb, wo, gpm, mod3, gpf, mod3, mod3, wrt)


def _topk_kernel(aff_ref, slot_ref, *, nk, cap):
    a = aff_ref[0]
    bits = lax.bitcast_convert_type(a, jnp.int32)
    count = lambda m: jnp.sum(jnp.sum(jnp.where(m, 1.0, 0.0), axis=2, keepdims=True), axis=0, keepdims=True)
    thr = jnp.where(count(bits >= (1 << 30)) >= cap, jnp.int32(1 << 30), jnp.zeros((1, N_EXPERTS, 1), jnp.int32))
    for bit in range(29, 0, -2):
        hi, lo = 1 << bit, 1 << (bit - 1)
        ok = [count(bits >= (thr | c)) >= cap for c in (hi | lo, hi, lo)]
        thr = thr | jnp.where(ok[0], hi | lo, jnp.where(ok[1], hi, jnp.where(ok[2], lo, 0)))
    gt = bits > thr
    eq = bits == thr
    need = cap - count(gt)

    tri = jnp.where(lax.broadcasted_iota(jnp.int32, (256, 256), 0) <= lax.broadcasted_iota(jnp.int32, (256, 256), 1),
                    1.0, 0.0).astype(BF16)

    def prefix_excl(m):
        incl = jnp.dot(m.reshape(nk * N_EXPERTS, 256).astype(BF16), tri,
                       preferred_element_type=F32).reshape(nk, N_EXPERTS, 256)
        offs = []
        run = jnp.zeros((1, N_EXPERTS, 1), F32)
        for k in range(nk):
            offs.append(run)
            run = run + incl[k:k + 1, :, 255:256]
        off = offs[0] if nk == 1 else jnp.concatenate(offs, axis=0)
        return incl - m + off

    eqf = jnp.where(eq, 1.0, 0.0)
    sel = jnp.where(gt, 1.0, jnp.where(eq & (prefix_excl(eqf) < need), 1.0, 0.0))
    slot = jnp.where(sel > 0.5, prefix_excl(sel), -1.0)
    slot_ref[0] = slot.astype(jnp.int32)


def _topk(aff, blk0, nk, cap):
    nbatch = aff.shape[0]
    return pl.pallas_call(
        functools.partial(_topk_kernel, nk=nk, cap=cap),
        grid=(nbatch,),
        in_specs=[pl.BlockSpec((1, nk, N_EXPERTS, ROW_BLK), lambda b: (b, blk0, 0, 0))],
        out_specs=pl.BlockSpec((1, nk, N_EXPERTS, ROW_BLK), lambda b: (b, 0, 0, 0)),
        out_shape=jax.ShapeDtypeStruct((nbatch, nk, N_EXPERTS, ROW_BLK), jnp.int32),
        compiler_params=_cp("arbitrary"),
        name="topk",
    )(aff)


def _gather_kernel(w0_ref, fits_ref, h_ref, sl_ref, xs_ref, acc):
    b = pl.program_id(0)
    k = pl.program_id(1)

    @pl.when(k == 0)
    def _():
        acc[...] = jnp.zeros_like(acc)

    h = h_ref[0]

    @pl.when(fits_ref[b] == 1)
    def _():
        r_iota = lax.broadcasted_iota(jnp.int32, (MOE_WIN, ROW_BLK), 0)
        w0s = [pl.multiple_of(w0_ref[(b * LAT_BLKS + k) * N_EXPERTS + e], 16) for e in range(N_EXPERTS)]
        onehot = jnp.concatenate([jnp.where((sl_ref[0, 0, e:e + 1, :] - w0s[e]) == r_iota, 1.0, 0.0).astype(BF16)
                                  for e in range(N_EXPERTS)], axis=0)
        rows = jnp.dot(onehot, h, preferred_element_type=F32)
        for e in range(N_EXPERTS):
            acc[e, pl.ds(w0s[e], MOE_WIN), :] += rows[MOE_WIN * e:MOE_WIN * e + MOE_WIN]

    @pl.when(fits_ref[b] != 1)
    def _():
        r_iota = lax.broadcasted_iota(jnp.int32, (CAP_LAT, ROW_BLK), 0)
        for e in range(N_EXPERTS):
            onehot = jnp.where(sl_ref[0, 0, e:e + 1, :] == r_iota, 1.0, 0.0).astype(BF16)
            acc[e] += jnp.dot(onehot, h, preferred_element_type=F32)

    @pl.when(k == pl.num_programs(1) - 1)
    def _():
        xs_ref[0] = acc[...].astype(xs_ref.dtype)


def _gather(win0, fits, h2, slot_lat):
    nbatch = h2.shape[0]
    smem = pl.BlockSpec(memory_space=pltpu.SMEM)
    return pl.pallas_call(
        _gather_kernel,
        grid=(nbatch, LAT_BLKS),
        in_specs=[smem, smem,
                  pl.BlockSpec((1, ROW_BLK, D), lambda b, k: (b, k, 0)),
                  pl.BlockSpec((1, 1, N_EXPERTS, ROW_BLK), lambda b, k: (b, k, 0, 0))],
        out_specs=pl.BlockSpec((1, N_EXPERTS, CAP_LAT, D), lambda b, k: (b, 0, 0, 0)),
        out_shape=jax.ShapeDtypeStruct((nbatch, N_EXPERTS, CAP_LAT, D), BF16),
        scratch_shapes=[pltpu.VMEM((N_EXPERTS, CAP_LAT, D), F32)],
        compiler_params=_cp("arbitrary", "arbitrary"),
        name="gather",
    )(win0, fits, h2, slot_lat)


def _expert_kernel(*refs, has_ctx):
    if has_ctx:
        (w0_ref, fits_ref, xl_ref, sl_ref, al_ref, hc_ref, sc_ref, ac_ref, wg_ref, wu_ref, wd_ref,
         yl_ref, yc_ref, wgb, wub, wdb, gs_acc, xc_all, gc_all) = refs
    else:
        (w0_ref, fits_ref, xl_ref, sl_ref, al_ref, wg_ref, wu_ref, wd_ref, yl_ref,
         wgb, wub, wdb, gs_acc) = refs
    e = pl.program_id(0)
    bp = pl.program_id(1)

    @pl.when(bp == 0)
    def _():
        wgb[...] = wg_ref[0, 0].astype(BF16)
        wub[...] = wu_ref[0, 0].astype(BF16)
        wdb[...] = wd_ref[0, 0].astype(BF16)

    def gather(h_ref, slot_ref, aff_ref, s, nk, cap):
        r_iota = lax.broadcasted_iota(jnp.int32, (cap, ROW_BLK), 0)
        xs = jnp.zeros((cap, D), F32)
        gs = jnp.zeros((cap, 1), F32)
        for k in range(nk):
            pm = slot_ref[s, k, pl.ds(e, 1), :] == r_iota
            xs = xs + jnp.dot(jnp.where(pm, 1.0, 0.0).astype(BF16), h_ref[s, ROW_BLK * k:ROW_BLK * k + ROW_BLK, :],
                              preferred_element_type=F32)
            gs = gs + jnp.sum(jnp.where(pm, aff_ref[s, k, pl.ds(e, 1), :], 0.0), axis=1, keepdims=True)
        return xs.astype(BF16), gs

    def ffn(xsb):
        gt = jnp.dot(xsb, wgb[...], preferred_element_type=F32)
        up = jnp.dot(xsb, wub[...], preferred_element_type=F32)
        return jnp.dot((_silu(gt) * up).astype(BF16), wdb[...], preferred_element_type=F32)

    for s in range(EXP_SB):
        b = bp * EXP_SB + s
        base = CAP_LAT * s

        @pl.when(fits_ref[b] == 1)
        def _():
            gs_acc[base:base + CAP_LAT, :] = jnp.zeros((CAP_LAT, 1), F32)
            r_iota = lax.broadcasted_iota(jnp.int32, (MOE_WIN, ROW_BLK), 0)
            for k in range(LAT_BLKS):
                w0 = pl.multiple_of(w0_ref[(b * LAT_BLKS + k) * N_EXPERTS + e], 16)
                pm = (sl_ref[s, k, pl.ds(e, 1), :] - w0) == r_iota
                gs_acc[pl.ds(base + w0, MOE_WIN), :] += jnp.sum(jnp.where(pm, al_ref[s, k, pl.ds(e, 1), :], 0.0),
                                                                axis=1, keepdims=True)

        @pl.when(fits_ref[b] != 1)
        def _():
            r_iota = lax.broadcasted_iota(jnp.int32, (CAP_LAT, ROW_BLK), 0)
            gs = jnp.zeros((CAP_LAT, 1), F32)
            for k in range(LAT_BLKS):
                pm = sl_ref[s, k, pl.ds(e, 1), :] == r_iota
                gs = gs + jnp.sum(jnp.where(pm, al_ref[s, k, pl.ds(e, 1), :], 0.0), axis=1, keepdims=True)
            gs_acc[base:base + CAP_LAT, :] = gs

    xs = xl_ref[...].reshape(EXP_SB * CAP_LAT, D)
    y = ffn(xs) * gs_acc[...]
    for s in range(EXP_SB):
        yl_ref[s, 0] = y[CAP_LAT * s:CAP_LAT * s + CAP_LAT].astype(yl_ref.dtype)
    if has_ctx:
        for s in range(EXP_SB):
            xs, gs = gather(hc_ref, sc_ref, ac_ref, s, 1, CAP_CTX)
            row = pl.multiple_of((bp * EXP_SB + s) * CAP_CTX, CAP_CTX)
            xc_all[pl.ds(row, CAP_CTX), :] = xs
            gc_all[pl.ds(row, CAP_CTX), :] = gs

        @pl.when(bp == pl.num_programs(1) - 1)
        def _():
            yc_ref[0] = (ffn(xc_all[...]) * gc_all[...]).astype(yc_ref.dtype)


def _experts(win0, fits, xs_lat, h2, slot_lat, slot_ctx, aff, w_gate, w_up, w_down, l, has_ctx):
    nbatch = h2.shape[0]
    assert nbatch % EXP_SB == 0
    idx4 = lambda e, b: (b, 0, 0, 0)
    smem = pl.BlockSpec(memory_space=pltpu.SMEM)
    in_specs = [smem, smem,
                pl.BlockSpec((EXP_SB, 1, CAP_LAT, D), lambda e, b: (b, e, 0, 0)),
                pl.BlockSpec((EXP_SB, LAT_BLKS, N_EXPERTS, ROW_BLK), idx4),
                pl.BlockSpec((EXP_SB, LAT_BLKS, N_EXPERTS, ROW_BLK), idx4)]
    args = [win0, fits, xs_lat, slot_lat, aff]
    out_specs = [pl.BlockSpec((EXP_SB, 1, CAP_LAT, D), lambda e, b: (b, e, 0, 0))]
    out_shape = [jax.ShapeDtypeStruct((nbatch, N_EXPERTS, CAP_LAT, D), BF16)]
    if has_ctx:
        in_specs += [pl.BlockSpec((EXP_SB, LC, D), lambda e, b: (b, LAT_BLKS, 0)),
                     pl.BlockSpec((EXP_SB, 1, N_EXPERTS, ROW_BLK), idx4),
                     pl.BlockSpec((EXP_SB, 1, N_EXPERTS, ROW_BLK), lambda e, b: (b, LAT_BLKS, 0, 0))]
        args += [h2, slot_ctx, aff]
        out_specs.append(pl.BlockSpec((1, nbatch * CAP_CTX, D), lambda e, b: (e, 0, 0)))
        out_shape.append(jax.ShapeDtypeStruct((N_EXPERTS, nbatch * CAP_CTX, D), BF16))
    wspec = pl.BlockSpec((1, 1, D, D), lambda e, b: (l, e, 0, 0))
    in_specs += [wspec, wspec, wspec]
    args += [w_gate, w_up, w_down]
    scratch = [pltpu.VMEM((D, D), BF16)] * 3 + [pltpu.VMEM((EXP_SB * CAP_LAT, 1), F32)]
    if has_ctx:
        scratch += [pltpu.VMEM((nbatch * CAP_CTX, D), BF16), pltpu.VMEM((nbatch * CAP_CTX, 1), F32)]
    return pl.pallas_call(
        functools.partial(_expert_kernel, has_ctx=has_ctx),
        grid=(N_EXPERTS, nbatch // EXP_SB),
        in_specs=in_specs,
        out_specs=out_specs,
        out_shape=out_shape,
        scratch_shapes=scratch,
        compiler_params=_cp("arbitrary", "arbitrary"),
        name="experts",
    )(*args)


def _combine_kernel(*refs, has_ctx):
    if has_ctx:
        w0_ref, fits_ref, x1_ref, yl_ref, scl_ref, yc_ref, scc_ref, gate2_ref, g_ref, o_ref, ycat = refs
    else:
        w0_ref, fits_ref, x1_ref, yl_ref, scl_ref, gate2_ref, g_ref, o_ref, ycat = refs
    b = pl.program_id(0)
    j = pl.program_id(1)

    def finish(acc):
        o_ref[0] = x1_ref[0] + gate2_ref[0] * _rms(acc, g_ref[...])

    def comb(sc_ref, y_of, cap):
        lane = lax.broadcasted_iota(jnp.int32, (ROW_BLK, cap), 1)
        sc = sc_ref[0]
        acc = jnp.zeros((ROW_BLK, D), F32)
        for e in range(N_EXPERTS):
            pt = jnp.where(sc[:, e:e + 1] == lane, 1.0, 0.0).astype(BF16)
            acc = acc + jnp.dot(pt, y_of(e), preferred_element_type=F32)
        finish(acc)

    def comb_windowed():
        lane = lax.broadcasted_iota(jnp.int32, (ROW_BLK, ROW_BLK), 1)
        sc = scl_ref[0]
        per = ROW_BLK // MOE_WIN
        if per * MOE_WIN < ROW_BLK:
            ycat[:, per * MOE_WIN:, :] = jnp.zeros((2, ROW_BLK - per * MOE_WIN, D), BF16)
        acc = jnp.zeros((ROW_BLK, D), F32)
        for gi, e0 in enumerate(range(0, N_EXPERTS, per)):
            hit = None
            for q, e in enumerate(range(e0, min(e0 + per, N_EXPERTS))):
                w0 = pl.multiple_of(w0_ref[(b * LAT_BLKS + j) * N_EXPERTS + e], 16)
                d = sc[:, e:e + 1] - w0
                d = jnp.where(jnp.logical_and(d >= 0, d < MOE_WIN), d + MOE_WIN * q, -1)
                hit = (d == lane) if hit is None else jnp.logical_or(hit, d == lane)
                ycat[gi % 2, MOE_WIN * q:MOE_WIN * q + MOE_WIN, :] = yl_ref[0, e, pl.ds(w0, MOE_WIN), :]
            acc = acc + jnp.dot(jnp.where(hit, 1.0, 0.0).astype(BF16), ycat[gi % 2], preferred_element_type=F32)
        finish(acc)

    lat_y = lambda e: yl_ref[0, e]
    fits = fits_ref[b] == 1

    @pl.when(jnp.logical_and(j < LAT_BLKS, fits))
    def _():
        comb_windowed()

    @pl.when(jnp.logical_and(j < LAT_BLKS, jnp.logical_not(fits)))
    def _():
        comb(scl_ref, lat_y, CAP_LAT)

    if has_ctx:
        @pl.when(j == LAT_BLKS)
        def _():
            comb(scc_ref, lambda e: yc_ref[e], CAP_CTX)


def _combine(win0, fits, x1, y_lat, scol_lat, y_ctx, scol_ctx, mod3, g, has_ctx, l):
    nbatch = x1.shape[0]
    nb = NBLK if has_ctx else LAT_BLKS
    ntok = N if has_ctx else T
    smem = pl.BlockSpec(memory_space=pltpu.SMEM)
    in_specs = [smem, smem,
                pl.BlockSpec((1, ROW_BLK, D), lambda b, j: (b, j, 0)),
                pl.BlockSpec((1, N_EXPERTS, CAP_LAT, D), lambda b, j: (b, 0, 0, 0)),
                pl.BlockSpec((1, ROW_BLK, N_EXPERTS), lambda b, j: (b, jnp.minimum(j, LAT_BLKS - 1), 0))]
    args = [win0, fits, x1, y_lat, scol_lat]
    if has_ctx:
        in_specs += [pl.BlockSpec((N_EXPERTS, CAP_CTX, D), lambda b, j: (0, b, 0)),
                     pl.BlockSpec((1, ROW_BLK, N_EXPERTS), lambda b, j: (b, 0, 0))]
        args += [y_ctx, scol_ctx]
    in_specs += [pl.BlockSpec((1, 1, D), lambda b, j: (jnp.where(j == LAT_BLKS, nbatch, b), 0, 5)),
                 _lspec(g, l)]
    args += [mod3, g]
    return pl.pallas_call(
        functools.partial(_combine_kernel, has_ctx=has_ctx),
        grid=(nbatch, nb),
        in_specs=in_specs,
        out_specs=pl.BlockSpec((1, ROW_BLK, D), lambda b, j: (b, j, 0)),
        out_shape=jax.ShapeDtypeStruct((nbatch, ntok, D), F32),
        scratch_shapes=[pltpu.VMEM((2, ROW_BLK, D), BF16)],
        compiler_params=_cp("arbitrary", "arbitrary"),
        name="combine",
    )(*args)


def _rope_tables():
    f = 1.0 / (ROPE_THETA ** (jnp.arange(0, 16, 2, dtype=F32) / 16))
    pos = jnp.arange(T)
    row, col = pos // GRID_W, pos % GRID_W
    ar = row.astype(F32)[:, None] * f[None, :]
    ac = col.astype(F32)[:, None] * f[None, :]
    cos32 = jnp.concatenate([jnp.cos(ar), jnp.cos(ar), jnp.cos(ac), jnp.cos(ac)], axis=-1)
    sin32 = jnp.concatenate([jnp.sin(ar), jnp.sin(ar), jnp.sin(ac), jnp.sin(ac)], axis=-1)
    cos32 = jnp.concatenate([cos32, jnp.ones((LC, 32), F32)], axis=0)
    sin32 = jnp.concatenate([sin32, jnp.zeros((LC, 32), F32)], axis=0)
    return cos32, sin32


def _rot_cols(w):
    a, b, c, d = w[..., 0:8], w[..., 8:16], w[..., 16:24], w[..., 24:32]
    return jnp.concatenate([-b, a, -d, c], axis=-1)


def _mla_weights(w_uq, w_ukv):
    nl, r, _ = w_uq.shape
    wq3 = w_uq.reshape(nl, r, 4, 96)
    z32 = jnp.zeros((nl, r, 4, 32), F32)
    wq = jnp.concatenate([wq3, z32], axis=-1).reshape(nl, r, 512)
    wqr = jnp.concatenate([jnp.zeros((nl, r, 4, 64), F32), _rot_cols(wq3[..., 64:96]), z32], axis=-1).reshape(nl, r, 512)
    rk = w_ukv.shape[1]
    wkv3 = w_ukv.reshape(nl, rk, 4, 128)
    wk = jnp.concatenate([wkv3[..., :64], jnp.zeros((nl, rk, 4, 64), F32)], axis=-1).reshape(nl, rk, 512)
    z64 = jnp.zeros((nl, rk, 64), F32)
    vh = [wkv3[:, :, h, 64:] for h in range(4)]
    wv = jnp.concatenate([vh[0], z64, z64, vh[1], vh[2], z64, z64, vh[3]], axis=-1)
    vone = np.tile(np.repeat(np.array([0.0, 1.0, 1.0, 0.0], np.float32), 64), 2)[None, :]
    e = np.zeros((32, 512), np.float32)
    for h in range(4):
        e[np.arange(32), 128 * h + 64 + np.arange(32)] = 1.0
    return (wq.astype(BF16), wqr.astype(BF16), wk.astype(BF16), wv.astype(BF16), jnp.asarray(e, BF16),
            jnp.asarray(vone))


def _inproj_weights(w):
    o = np.cumsum([0, 256, 128, 32, 768, 256, 256, 768, 8, 4096])
    w = w.astype(BF16)
    seg = lambda i: w[:, :, o[i]:o[i + 1]]
    nl = w.shape[0]
    kr = seg(2)
    wm = jnp.concatenate([seg(0), seg(1), kr, _rot_cols(kr), jnp.zeros((nl, D, 64), BF16)], axis=-1)
    w2 = jnp.concatenate([seg(5), seg(6), seg(7), jnp.zeros((nl, D, 120), BF16)], axis=-1)
    return [wm, seg(3), seg(4), w2, 0.5 * seg(8)]


def kernel(x, c, ctx, c_ctx, w_ada, b_ada, g_pre_mix, g_post_mix, g_pre_ffn, g_post_ffn, w_in, mla_g_cq, mla_g_ckv, mla_w_uq, mla_w_ukv, na_rpb, s5_a_re, s5_a_im, s5_log_step, s5_b_re, s5_b_im, s5_c_re, s5_c_im, s5_d, s5_w_glu, s5_b_glu, m2_conv_w, m2_conv_b, m2_a_log, m2_dt_bias, m2_d, m2_g_norm, w_branch, w_out, w_router, w_gate, w_up, w_down):
    nbatch = x.shape[0]
    depth = w_ada.shape[0]
    vec = lambda v: v.reshape(depth, 1, -1).astype(F32)

    stream = (x, ctx, 0)
    cvec = jnp.concatenate([c, c_ctx[None, :], jnp.zeros((7, D), F32)], axis=0)
    cvec = cvec[: ((nbatch + 1 + 7) // 8) * 8]
    b_ada3 = b_ada.reshape(depth, 1, 6 * D)

    cos32, sin32 = _rope_tables()
    qscale = (64 + 32) ** -0.5 * math.log2(math.e)
    lane_is_rope = np.tile(np.concatenate([np.zeros(64, bool), np.ones(32, bool), np.zeros(32, bool)]), 4)
    pick = lambda t32, fill: jnp.where(lane_is_rope[None, :], jnp.tile(jnp.pad(t32, ((0, 0), (64, 32))), (1, 4)), fill)
    cos_q = pick(cos32, 1.0) * qscale
    sin_q = pick(sin32, 0.0) * qscale

    g_pre_mix, g_post_mix, g_pre_ffn, g_post_ffn = vec(g_pre_mix), vec(g_post_mix), vec(g_pre_ffn), vec(g_post_ffn)
    w_in_segs = _inproj_weights(w_in)
    wq, wqr, wk, wv, e_mat, vone = _mla_weights(mla_w_uq, mla_w_ukv)
    mla_g_cq, mla_g_ckv = vec(mla_g_cq), vec(mla_g_ckv)
    na_tabs = _na_tables(na_rpb)
    s5_bmat, s5_avec, s5_cmat = _s5_params(s5_a_re, s5_a_im, s5_log_step, s5_b_re, s5_b_im, s5_c_re, s5_c_im, nbatch)
    s5_d, s5_b_glu, s5_w_glu = vec(s5_d), vec(s5_b_glu), s5_w_glu.astype(BF16)
    m2_conv_w, m2_conv_b = m2_conv_w.astype(F32), vec(m2_conv_b)
    dt_bias = jnp.pad(m2_dt_bias.reshape(depth, 1, 8).astype(F32), ((0, 0), (0, 0), (0, 120)))
    a_col = jnp.pad(-jnp.exp(m2_a_log.astype(F32)).reshape(depth, 8, 1), ((0, 0), (0, 120), (0, 0)))
    m2_dvec, m2_g_norm = vec(jnp.repeat(m2_d, 64, axis=-1)), vec(m2_g_norm)
    w_branch, w_out = (0.5 * w_branch).astype(BF16), w_out.astype(BF16)
    w_router_t = jnp.swapaxes(w_router, 1, 2).astype(F32)

    for l in range(depth):
        has_ctx = l < depth - 1
        nb = NBLK if has_ctx else LAT_BLKS
        mod = _ada(cvec, w_ada, b_ada3, l)
        mod3 = mod.reshape(mod.shape[0], 1, 6 * D)

        p_mla, p_na, p_s5, p_m2, p_gate = _inproj(stream, g_pre_mix, mod3, w_in_segs, nbatch, l, has_ctx)

        q, k, v = _mla_prep(p_mla, mla_g_cq, mla_g_ckv, wq, wqr, wk, wv, e_mat, vone, cos_q, sin_q, cos32, sin32, l)
        o_mla = _mla_attn(q.reshape(nbatch, N, 512), k.reshape(nbatch, N, 512), v.reshape(nbatch, N, 512), has_ctx)

        o_na = _na_attn(p_na.reshape(nbatch, N, NA_W), na_tabs, has_ctx, l)

        yf, yb = _s5_scan(p_s5.reshape(nbatch, N, S5_W), s5_bmat, s5_avec, s5_cmat, l)
        o_s5 = _s5_glu(p_s5, yf.reshape(nbatch * N, S5_W), yb.reshape(nbatch * N, S5_W), s5_d, s5_w_glu, s5_b_glu, l)

        xbc, dtt, bmt = _m2_prep(p_m2.reshape(nbatch, N, M2_W), m2_conv_w, m2_conv_b, dt_bias, l)
        ssd_f, ssd_b = _ssd(xbc, dtt, bmt, a_col, l)
        o_m2 = _m2_norm(xbc.reshape(nbatch * N, 768), p_m2, ssd_f.reshape(nbatch * N, 256),
                        ssd_b.reshape(nbatch * N, 256), m2_dvec, m2_g_norm, l)

        x1, h2, aff = _merge(stream, o_mla, o_na.reshape(-1, 256), o_s5, o_m2, p_gate,
                             w_branch, w_out, g_post_mix, mod3, g_pre_ffn, w_router_t, nbatch, nb, l)

        slot_lat = _topk(aff, 0, LAT_BLKS, CAP_LAT)
        slot_ctx = _topk(aff, LAT_BLKS, 1, CAP_CTX) if has_ctx else None
        cnt = jnp.sum((slot_lat >= 0).astype(jnp.int32), axis=-1)
        first = jnp.cumsum(cnt, axis=1) - cnt
        win0 = jnp.minimum((first // 16) * 16, CAP_LAT - MOE_WIN)
        fits = jnp.all(first + cnt <= win0 + MOE_WIN, axis=(1, 2)).astype(jnp.int32)
        win0 = win0.reshape(-1)
        h2 = h2.reshape(nbatch, nb * ROW_BLK, D)
        ys = _experts(win0, fits, _gather(win0, fits, h2, slot_lat), h2, slot_lat, slot_ctx, aff, w_gate, w_up, w_down,
                      l, has_ctx)
        scol_lat = jnp.transpose(slot_lat, (0, 1, 3, 2)).reshape(nbatch, T, N_EXPERTS)
        scol_ctx = jnp.transpose(slot_ctx, (0, 1, 3, 2)).reshape(nbatch, LC, N_EXPERTS) if has_ctx else None
        xs = _combine(win0, fits, x1.reshape(nbatch, nb * ROW_BLK, D), ys[0], scol_lat, ys[1] if has_ctx else None,
                      scol_ctx, mod3, g_post_ffn, has_ctx, l)
        stream = (xs, xs, LAT_BLKS)
    return xs
```

```python
import functools
import math

import numpy as np
import jax
import jax.numpy as jnp
from jax import lax
from jax.experimental import pallas as pl
from jax.experimental.pallas import tpu as pltpu

F32 = jnp.float32
BF16 = jnp.bfloat16

D = 1024
T = 2048
LC = 256
N = T + LC
GRID_W = 64
ROW_BLK = 256
NBLK = N // ROW_BLK
LAT_BLKS = T // ROW_BLK
EPS = 1e-6
N_EXPERTS = 16
CAP_LAT = 2 * T // N_EXPERTS
CAP_CTX = 2 * LC // N_EXPERTS
ROPE_THETA = 10000.0
NEG = -1e30

MLA_W = 512
NA_W = 768
S5_W = 256
M2_W = 1152
GATE_W = 4096

MOE_WIN = 80
EXP_SB = 4
S5_CHUNK = 64
SSD_CHUNK = 128
SSD_SB = 8
NA_SB = 4

VMEM_LIMIT = 56 * 1024 * 1024


def _cp(*sem, vmem=VMEM_LIMIT):
    return pltpu.CompilerParams(dimension_semantics=sem, vmem_limit_bytes=vmem)


def _lspec(a, l, *lead):
    nd = a.ndim - 1 - len(lead)
    return pl.BlockSpec((None,) * (1 + len(lead)) + a.shape[1 + len(lead):], lambda *_: (l,) + lead + (0,) * nd)


def _dot(a, b):
    return jnp.dot(a.astype(BF16), b.astype(BF16), preferred_element_type=F32)


def _dot_nt(a, b):
    return lax.dot_general(a.astype(BF16), b.astype(BF16), (((1,), (1,)), ((), ())),
                           preferred_element_type=F32)


def _split3(a):
    hi = a.astype(BF16)
    r = a - hi.astype(F32)
    mid = r.astype(BF16)
    lo = (r - mid.astype(F32)).astype(BF16)
    return hi, mid, lo


def _dot_hi(a, b):
    ah, am, _ = _split3(a)
    bh, bm, _ = _split3(b)
    f = lambda x, y: jnp.dot(x, y, preferred_element_type=F32)
    return f(ah, bh) + (f(ah, bm) + f(am, bh))


def _dot_hi_nt(a, b):
    ah, am, _ = _split3(a)
    bh, bm, _ = _split3(b)
    f = lambda x, y: lax.dot_general(x, y, (((1,), (1,)), ((), ())), preferred_element_type=F32)
    return f(ah, bh) + (f(ah, bm) + f(am, bh))


def _dot_exact_lhs(m_bf16, a):
    h, m, l = _split3(a)
    f = lambda y: jnp.dot(m_bf16, y, preferred_element_type=F32)
    return f(h) + (f(m) + f(l))


def _dot_exact_rhs(a, m_bf16):
    h, m, l = _split3(a)
    f = lambda y: jnp.dot(y, m_bf16, preferred_element_type=F32)
    return f(h) + (f(m) + f(l))


def _sigmoid(x):
    return 0.5 * jnp.tanh(0.5 * x) + 0.5


def _silu(x):
    return x * _sigmoid(x)


def _rms(x, g):
    return x * lax.rsqrt(jnp.mean(x * x, axis=-1, keepdims=True) + EPS) * g


def _mod_row(nb):
    def f(i, nbatch):
        return jnp.where(i % nb == LAT_BLKS, nbatch, i // nb)
    return f


def _flat_blk(i, nb):
    return (i // nb) * NBLK + i % nb


def _stream_specs(stream, nb):
    lat, ctxa, cblk = stream
    w = lat.shape[-1]
    return [lat, ctxa], [pl.BlockSpec((1, ROW_BLK, w), lambda i: (i // nb, jnp.minimum(i % nb, LAT_BLKS - 1), 0)),
                         pl.BlockSpec((1, ROW_BLK, w), lambda i: (i // nb, cblk, 0))]


def _stream_block(xl_ref, xc_ref, nb):
    return jnp.where(pl.program_id(0) % nb == LAT_BLKS, xc_ref[0], xl_ref[0])


def _ada_kernel(c_ref, w_ref, b_ref, o_ref):
    c = c_ref[...]
    o_ref[...] = _dot_hi(_silu(c), w_ref[0]) + b_ref[0]


def _ada(cvec, w_ada, b_ada3, l):
    rows = cvec.shape[0]
    return pl.pallas_call(
        _ada_kernel,
        grid=(6,),
        in_specs=[pl.BlockSpec((rows, D), lambda k: (0, 0)),
                  pl.BlockSpec((1, D, D), lambda k: (l, 0, k)),
                  pl.BlockSpec((1, 1, D), lambda k: (l, 0, k))],
        out_specs=pl.BlockSpec((rows, D), lambda k: (0, k)),
        out_shape=jax.ShapeDtypeStruct((rows, 6 * D), F32),
        compiler_params=_cp("arbitrary"),
        name="ada",
    )(cvec, w_ada, b_ada3)


def _inproj_kernel(xl_ref, xc_ref, g_ref, sh_ref, sc_ref, wm, wn, ws, w2, wg, om, on, os_, o2, og, *, ctx_gates):
    x = _stream_block(xl_ref, xc_ref, NBLK)
    h = (_rms(x, g_ref[...]) * (1.0 + sc_ref[0]) + sh_ref[0]).astype(BF16)
    om[...] = jnp.dot(h, wm[...], preferred_element_type=F32)
    on[...] = jnp.dot(h, wn[...], preferred_element_type=F32).astype(BF16)
    os_[...] = jnp.dot(h, ws[...], preferred_element_type=F32)
    o2[...] = jnp.dot(h, w2[...], preferred_element_type=F32)
    if ctx_gates:
        og[...] = jnp.dot(h, wg[...], preferred_element_type=F32)
    else:
        is_ctx = pl.program_id(0) % NBLK == LAT_BLKS

        @pl.when(jnp.logical_not(is_ctx))
        def _():
            og[...] = jnp.dot(h, wg[...], preferred_element_type=F32)

        @pl.when(is_ctx)
        def _():
            og[...] = jnp.zeros_like(og)


def _inproj(stream, g, mod3, ws, nbatch, l, ctx_gates):
    rows = nbatch * N
    nblk = rows // ROW_BLK
    mrow = _mod_row(NBLK)
    xargs, xspecs = _stream_specs(stream, NBLK)
    full = lambda w: _lspec(w, l)
    widths = (MLA_W, NA_W, S5_W, M2_W, GATE_W)
    dts = (F32, BF16, F32, F32, F32)
    return pl.pallas_call(
        functools.partial(_inproj_kernel, ctx_gates=ctx_gates),
        grid=(nblk,),
        in_specs=xspecs + [
                  full(g),
                  pl.BlockSpec((1, 1, D), lambda i: (mrow(i, nbatch), 0, 0)),
                  pl.BlockSpec((1, 1, D), lambda i: (mrow(i, nbatch), 0, 1))]
                 + [full(w) for w in ws],
        out_specs=[pl.BlockSpec((ROW_BLK, w), lambda i: (i, 0)) for w in widths],
        out_shape=[jax.ShapeDtypeStruct((rows, w), dt) for w, dt in zip(widths, dts)],
        compiler_params=_cp("arbitrary"),
        name="inproj",
    )(*xargs, g, mod3, mod3, *ws)


def _mla_prep_kernel(p_ref, gq_ref, gkv_ref, wq_ref, wqr_ref, wk_ref, wv_ref, e_ref, vone_ref,
                     cos_ref, sin_ref, ck_ref, sk_ref, q_out, k_out, v_out):
    p = p_ref[...]
    cqn = _rms(p[:, :256], gq_ref[...]).astype(BF16)
    q = jnp.dot(cqn, wq_ref[...], preferred_element_type=F32)
    qr = jnp.dot(cqn, wqr_ref[...], preferred_element_type=F32)
    q_out[...] = (q * cos_ref[...] + qr * sin_ref[...]).astype(BF16)
    ckvn = _rms(p[:, 256:384], gkv_ref[...]).astype(BF16)
    kro = (p[:, 384:416] * ck_ref[...] + p[:, 416:448] * sk_ref[...]).astype(BF16)
    k = jnp.dot(ckvn, wk_ref[...], preferred_element_type=F32) + jnp.dot(kro, e_ref[...], preferred_element_type=F32)
    k_out[...] = k.astype(BF16)
    v_out[...] = (jnp.dot(ckvn, wv_ref[...], preferred_element_type=F32) + vone_ref[...]).astype(BF16)


def _mla_prep(p_mla, gq, gkv, wq, wqr, wk, wv, e, vone, cos_q, sin_q, cos_k, sin_k, l):
    rows = p_mla.shape[0]
    full = lambda w: _lspec(w, l)
    const = lambda w: pl.BlockSpec(w.shape, lambda j, b: (0, 0))
    rb = 3 * ROW_BLK
    row = lambda j, b: (b * (N // rb) + j, 0)
    tab = lambda w: pl.BlockSpec((rb, w), lambda j, b: (j, 0))
    return pl.pallas_call(
        _mla_prep_kernel,
        grid=(N // rb, rows // N),
        in_specs=[pl.BlockSpec((rb, MLA_W), row),
                  full(gq), full(gkv), full(wq), full(wqr), full(wk), full(wv), const(e), const(vone),
                  tab(512), tab(512), tab(32), tab(32)],
        out_specs=[pl.BlockSpec((rb, 512), row)] * 3,
        out_shape=[jax.ShapeDtypeStruct((rows, 512), BF16)] * 3,
        compiler_params=_cp("arbitrary", "arbitrary"),
        name="mla_prep",
    )(p_mla, gq, gkv, wq, wqr, wk, wv, e, vone, cos_q, sin_q, cos_k, sin_k)


def _mla_attn_kernel(q_ref, k_ref, v_ref, o_ref):
    lane = lax.broadcasted_iota(jnp.int32, (q_ref.shape[1], 128), 1)
    for hp in range(2):
        pv = []
        for h in (2 * hp, 2 * hp + 1):
            s = _dot_nt(q_ref[0, :, 128 * h:128 * h + 128], k_ref[0, :, 128 * h:128 * h + 128])
            p = jnp.exp2(s - jnp.max(s, axis=-1, keepdims=True)).astype(BF16)
            pv.append(jnp.dot(p, v_ref[0, :, 128 * h:128 * h + 128], preferred_element_type=F32))
        oa, ob = pv
        o = jnp.where(lane < 64, oa * (1.0 / oa[:, 64:65]), ob * (1.0 / ob[:, 0:1]))
        o_ref[0, :, 128 * hp:128 * hp + 128] = o.astype(o_ref.dtype)


def _mla_attn(q, k, v, has_ctx):
    nbatch = q.shape[0]
    qb = 2 * ROW_BLK

    def call(grid, qspec, kvspec, ospec, ntok):
        return pl.pallas_call(
            _mla_attn_kernel,
            grid=grid,
            in_specs=[qspec, kvspec, kvspec],
            out_specs=ospec,
            out_shape=jax.ShapeDtypeStruct((nbatch, ntok, 256), BF16),
            compiler_params=_cp("arbitrary", "arbitrary"),
            name="mla_attn",
        )(q, k, v)

    o_lat = call((nbatch, T // qb), pl.BlockSpec((1, qb, 512), lambda b, j: (b, j, 0)),
                 pl.BlockSpec((1, N, 512), lambda b, j: (b, 0, 0)),
                 pl.BlockSpec((1, qb, 256), lambda b, j: (b, j, 0)), T)
    if not has_ctx:
        return o_lat, o_lat
    ctx_blk = pl.BlockSpec((1, LC, 512), lambda b, j: (b, LAT_BLKS, 0))
    o_ctx = call((nbatch, 1), ctx_blk, ctx_blk, pl.BlockSpec((1, LC, 256), lambda b, j: (b, 0, 0)), LC)
    return o_lat, o_ctx


def _na_kernel(q_ref, kc_ref, k0_ref, k1_ref, k2_ref, vc_ref, v0_ref, v1_ref, v2_ref, b_ref, o_ref, *, has_ctx):
    scale = jnp.asarray(0.125, BF16)

    lane = lax.broadcasted_iota(jnp.int32, (ROW_BLK, 128), 1)
    lane64 = lax.broadcasted_iota(jnp.int32, (GRID_W, 128), 1) < 64
    grp = jnp.minimum(pl.program_id(0), LAT_BLKS - 1)
    krow0 = 4 * jnp.clip(grp - 1, 0, LAT_BLKS - 3)

    def bias_piece(h, i):
        rows = []
        for qr in range(4):
            r = 4 * grp + qr
            rs = jnp.clip(r - 4, 0, 24)
            tiles = []
            for pr in range(2):
                kr = krow0 + (4 * i + 2 * pr)
                t = b_ref[h, jnp.clip(kr - r + 8, 0, 15)]
                ok_a = jnp.logical_and(kr >= rs, kr < rs + 8).astype(jnp.int32)
                ok_b = jnp.logical_and(kr + 1 >= rs, kr + 1 < rs + 8).astype(jnp.int32)
                tiles.append(jnp.where(jnp.where(lane64, ok_a, ok_b) > 0, t, NEG))
            rows.append(jnp.concatenate(tiles, axis=1))
        return jnp.concatenate(rows, axis=0)

    def heads(win, sm):
        kws = (k0_ref, k1_ref, k2_ref)
        vws = (v0_ref, v1_ref, v2_ref)
        for hp in range(2):
            sl = slice(128 * hp, 128 * hp + 128)
            qp = q_ref[sm, :, sl] * scale
            outs = []
            for hh in range(2):
                h = 2 * hp + hh
                qh = jnp.where((lane < 64) if hh == 0 else (lane >= 64), qp, jnp.zeros_like(qp))
                s_c = _dot_nt(qh, kc_ref[sm, :, sl])
                s_w = []
                smax = s_c
                if win:
                    for i in range(3):
                        s = _dot_nt(qh, kws[i][sm, :, sl]) + bias[h][i]
                        s_w.append(s)
                        smax = jnp.maximum(smax, s)
                m = jnp.max(smax, axis=-1, keepdims=True)
                p = jnp.exp(s_c - m)
                psum = p
                o = _dot(p, vc_ref[sm, :, sl])
                for i, s in enumerate(s_w):
                    p = jnp.exp(s - m)
                    psum = psum + p
                    o = o + _dot(p, vws[i][sm, :, sl])
                outs.append(o * (1.0 / jnp.sum(psum, axis=-1, keepdims=True)))
            o_ref[sm, :, sl] = jnp.where(lane < 64, outs[0], outs[1]).astype(o_ref.dtype)

    def windowed():
        nonlocal bias
        bias = [[bias_piece(h, i) for i in range(3)] for h in range(4)]
        for sm in range(NA_SB):
            heads(True, sm)

    bias = None
    if has_ctx:
        g = pl.program_id(0)

        @pl.when(g < LAT_BLKS)
        def _():
            windowed()

        @pl.when(g == LAT_BLKS)
        def _():
            for sm in range(NA_SB):
                heads(False, sm)
    else:
        windowed()


def _na_attn(qkv, bias, has_ctx, l):
    nbatch = qkv.shape[0]
    ng = NBLK if has_ctx else LAT_BLKS
    j0 = lambda g: jnp.clip(g - 1, 0, LAT_BLKS - 3)
    blk = lambda f: pl.BlockSpec((NA_SB, ROW_BLK, 256), f)
    return pl.pallas_call(
        functools.partial(_na_kernel, has_ctx=has_ctx),
        grid=(ng, nbatch // NA_SB),
        in_specs=[blk(lambda g, b: (b, g, 0)),
                  blk(lambda g, b: (b, LAT_BLKS, 1)),
                  blk(lambda g, b: (b, j0(g), 1)),
                  blk(lambda g, b: (b, j0(g) + 1, 1)),
                  blk(lambda g, b: (b, j0(g) + 2, 1)),
                  blk(lambda g, b: (b, LAT_BLKS, 2)),
                  blk(lambda g, b: (b, j0(g), 2)),
                  blk(lambda g, b: (b, j0(g) + 1, 2)),
                  blk(lambda g, b: (b, j0(g) + 2, 2)),
                  _lspec(bias, l)],
        out_specs=blk(lambda g, b: (b, g, 0)),
        out_shape=jax.ShapeDtypeStruct((nbatch, ng * ROW_BLK, 256), BF16),
        compiler_params=_cp("arbitrary", "arbitrary"),
        name="na_attn",
    )(qkv, qkv, qkv, qkv, qkv, qkv, qkv, qkv, qkv, bias)


def _na_col_structure():
    onehot = np.zeros((31, 64, 64), np.float32)
    colmask = np.zeros((64, 64), bool)
    for c in range(64):
        cs = min(max(c - 8, 0), 48)
        for kc in range(cs, cs + 16):
            onehot[kc - c + 15, c, kc] = 1.0
            colmask[c, kc] = True
    return np.tile(onehot.reshape(31, 4096), (3, 1)), colmask


_NA_COL_ONEHOT, _NA_COLMASK = _na_col_structure()


def _na_tables(rpb):
    nl = rpb.shape[0]
    r = rpb.astype(F32)
    hi = lax.reduce_precision(r, 8, 7)
    mid = lax.reduce_precision(r - hi, 8, 7)
    parts = jnp.stack([hi, mid, r - hi - mid], axis=3).reshape(nl * 60, 93)
    tz = jnp.dot(parts, _NA_COL_ONEHOT, preferred_element_type=F32).reshape(nl, 4, 15, 64, 64)
    tz = jnp.where(_NA_COLMASK, tz, NEG)
    edge = jnp.full((nl, 4, 1, 64, 64), NEG, F32)
    tz17 = jnp.concatenate([edge, tz, edge], axis=2)
    return jnp.concatenate([tz17[:, :, :16], tz17[:, :, 1:]], axis=-1)


def _s5_kernel(uf_ref, ub_ref, bf_ref, bb_ref, af_ref, ab_ref, cf_ref, cb_ref, yf_ref, yb_ref,
               hf, hb, buf_f, buf_b, tm_f, tm_b):
    i = pl.program_id(0)
    half = 1024
    nb = uf_ref.shape[0]

    @pl.when(i == 0)
    def _():
        hf[...] = jnp.zeros_like(hf)
        hb[...] = jnp.zeros_like(hb)

    def expand(u_ref, tm, b_ref, buf):
        for b in range(nb):
            for c in range(2):
                tm[c, pl.ds(b, S5_CHUNK, stride=nb), :] = u_ref[b, :, 128 * c:128 * c + 128]
        buf[...] = _dot(jnp.concatenate([tm[0], tm[1]], axis=1), b_ref[...])

    def scan(buf, a_ref, h, order):
        hr, hi = h[:, :half], h[:, half:]
        ar, ai = a_ref[:, :half], a_ref[:, half:]
        for k in order:
            rows = slice(k * nb, (k + 1) * nb)
            nr = ar * hr - ai * hi + buf[rows, :half]
            ni = ar * hi + ai * hr + buf[rows, half:]
            buf[rows, :half] = nr
            buf[rows, half:] = ni
            hr, hi = nr, ni
        h[:, :half] = hr
        h[:, half:] = hi

    def readout(buf, c_ref, tm, y_ref):
        y = _dot(buf[...], c_ref[...])
        for c in range(2):
            tm[c] = y[:, 128 * c:128 * c + 128]
        for b in range(nb):
            for c in range(2):
                y_ref[b, :, 128 * c:128 * c + 128] = tm[c, pl.ds(b, S5_CHUNK, stride=nb), :]

    expand(uf_ref, tm_f, bf_ref, buf_f)
    expand(ub_ref, tm_b, bb_ref, buf_b)
    scan(buf_f, af_ref, hf, range(S5_CHUNK))
    readout(buf_f, cf_ref, tm_f, yf_ref)
    scan(buf_b, ab_ref, hb, range(S5_CHUNK - 1, -1, -1))
    readout(buf_b, cb_ref, tm_b, yb_ref)


def _s5_scan(u, bmat, avec, cmat, l):
    nbatch = u.shape[0]
    cr = S5_CHUNK * nbatch
    nch = N // S5_CHUNK
    nctx = LC // S5_CHUNK
    fidx = lambda i: jnp.where(i < nctx, nch - nctx + i, i - nctx)
    bidx = lambda i: nch - 1 - i
    ublk = lambda f: pl.BlockSpec((nbatch, S5_CHUNK, 256), lambda i: (0, f(i), 0))
    return pl.pallas_call(
        _s5_kernel,
        grid=(nch,),
        in_specs=[ublk(fidx), ublk(bidx),
                  _lspec(bmat, l, 0), _lspec(bmat, l, 1), _lspec(avec, l, 0), _lspec(avec, l, 1),
                  _lspec(cmat, l, 0), _lspec(cmat, l, 1)],
        out_specs=[ublk(fidx), ublk(bidx)],
        out_shape=[jax.ShapeDtypeStruct((nbatch, N, 256), F32)] * 2,
        scratch_shapes=[pltpu.VMEM((nbatch, 2048), F32), pltpu.VMEM((nbatch, 2048), F32),
                        pltpu.VMEM((cr, 2048), F32), pltpu.VMEM((cr, 2048), F32),
                        pltpu.VMEM((2, cr, 128), F32), pltpu.VMEM((2, cr, 128), F32)],
        compiler_params=_cp("arbitrary"),
        name="s5_scan",
    )(u, u, bmat, bmat, avec, avec, cmat, cmat)


def _s5_glu_kernel(u_ref, yf_ref, yb_ref, d_ref, w_ref, b_ref, o_ref):
    y = d_ref[...] * u_ref[...] + yf_ref[...] + yb_ref[...]
    z = y * (0.5 * (1.0 + jnp.tanh(math.sqrt(2.0 / math.pi) * (y + 0.044715 * (y * y * y)))))
    o_ref[...] = (z * _sigmoid(_dot(z, w_ref[...]) + b_ref[...])).astype(o_ref.dtype)


def _s5_glu(u_tb, yf, yb, d, w, b, l):
    rows = u_tb.shape[0]
    rb = 4 * ROW_BLK
    blk = pl.BlockSpec((rb, 256), lambda i: (i, 0))
    full = lambda a: _lspec(a, l)
    return pl.pallas_call(
        _s5_glu_kernel,
        grid=(rows // rb,),
        in_specs=[blk, blk, blk, full(d), full(w), full(b)],
        out_specs=blk,
        out_shape=jax.ShapeDtypeStruct((rows, 256), BF16),
        compiler_params=_cp("arbitrary"),
        name="s5_glu",
    )(u_tb, yf, yb, d, w, b)


def _s5_params(a_re, a_im, log_step, b_re, b_im, c_re, c_im, nbatch):
    nl = a_re.shape[0]
    same_group = (np.arange(256)[:, None] // 16 == np.arange(1024)[None, :] // 64).astype(np.float32)
    a = lax.complex(a_re.astype(F32), a_im.astype(F32))
    abar = jnp.exp(jnp.exp(log_step.astype(F32))[..., None] * a)
    bbar = ((abar - 1.0) / a)[..., None] * lax.complex(b_re.astype(F32), b_im.astype(F32))
    blk_in = lambda m: jnp.tile(jnp.swapaxes(m, -1, -2).reshape(nl, 2, 256, 64), (1, 1, 1, 16)) * same_group
    bmat = jnp.concatenate([blk_in(jnp.real(bbar)), blk_in(jnp.imag(bbar))], axis=-1).astype(BF16)
    avec = jnp.concatenate([jnp.real(abar).reshape(nl, 2, 1, 1024), jnp.imag(abar).reshape(nl, 2, 1, 1024)], axis=-1)
    avec = jnp.broadcast_to(avec, (nl, 2, nbatch, 2048))
    blk_out = lambda m: jnp.tile(jnp.swapaxes(m, -1, -2).reshape(nl, 2, 1024, 16), (1, 1, 1, 16)) * same_group.T
    cmat = jnp.concatenate([blk_out(c_re.astype(F32)), -blk_out(c_im.astype(F32))], axis=-2).astype(BF16)
    return bmat, avec, cmat


def _m2_prep_kernel(x_ref, dt_ref, w_ref, b_ref, dtb_ref, xo_ref, dtt_ref, bmt_ref):
    c = pl.program_id(1)
    x = x_ref[0]
    t = lax.broadcasted_iota(jnp.int32, x.shape, 0)
    m2 = ((t >= 2) & (t < T)) | (t >= T + 2)
    m1 = ((t >= 1) & (t < T)) | (t >= T + 1)
    p1 = (t <= T - 2) | ((t >= T) & (t <= N - 2))
    w = w_ref[...]
    y = (w[0:1] * jnp.where(m2, pltpu.roll(x, 2, 0), 0.0)
         + w[1:2] * jnp.where(m1, pltpu.roll(x, 1, 0), 0.0)
         + w[2:3] * x
         + w[3:4] * jnp.where(p1, pltpu.roll(x, N - 1, 0), 0.0)) + b_ref[...]
    act = _silu(y)
    xo_ref[0] = act

    @pl.when(c == 0)
    def _():
        v = dt_ref[0] + dtb_ref[...]
        dtt_ref[0] = (jnp.maximum(v, 0.0) + jnp.log1p(jnp.exp(-jnp.abs(v)))).T

    @pl.when(c == 1)
    def _():
        bmt_ref[0] = act.T.astype(BF16)


def _m2_prep(p_m2, conv_w, conv_b, dtb, l):
    nbatch = p_m2.shape[0]
    return pl.pallas_call(
        _m2_prep_kernel,
        grid=(nbatch, 3),
        in_specs=[pl.BlockSpec((1, N, 256), lambda b, c: (b, 0, 1 + c)),
                  pl.BlockSpec((1, N, 128), lambda b, c: (b, 0, 8)),
                  pl.BlockSpec((None, 4, 256), lambda b, c: (l, 0, c)),
                  pl.BlockSpec((None, 1, 256), lambda b, c: (l, 0, c)),
                  _lspec(dtb, l)],
        out_specs=[pl.BlockSpec((1, N, 256), lambda b, c: (b, 0, c)),
                   pl.BlockSpec((1, 128, N), lambda b, c: (b, 0, 0)),
                   pl.BlockSpec((1, 256, N), lambda b, c: (b, 0, 0))],
        out_shape=[jax.ShapeDtypeStruct((nbatch, N, 768), F32),
                   jax.ShapeDtypeStruct((nbatch, 128, N), F32),
                   jax.ShapeDtypeStruct((nbatch, 256, N), BF16)],
        compiler_params=_cp("arbitrary", "arbitrary"),
        name="m2_prep",
    )(p_m2, p_m2, conv_w, conv_b, dtb)


def _ssd_kernel(xf_ref, dttf_ref, bmtf_ref, xb_ref, dttb_ref, bmtb_ref, acol_ref, yf_ref, yb_ref, hs):
    i = pl.program_id(1)
    L = SSD_CHUNK

    @pl.when(i == 0)
    def _():
        hs[...] = jnp.zeros_like(hs)

    li = lax.broadcasted_iota(jnp.int32, (L, L), 0)
    si = lax.broadcasted_iota(jnp.int32, (L, L), 1)
    left = si < 64
    pick = lambda a, b: jnp.where(left, a, b)

    def direction(sm, x_ref, dtt_ref, bmt_ref, y_ref, d, causal, causal_t, last):
        cz = jnp.where(causal, 1.0, 0.0).astype(BF16)
        czt = jnp.where(causal_t, 1.0, 0.0).astype(BF16)
        xbc = x_ref[sm]
        dtt = dtt_ref[sm]
        parts = jnp.concatenate([p.astype(F32) for p in _split3(dtt * acol_ref[...])], axis=1)
        cumt = jnp.dot(parts.astype(BF16), jnp.concatenate([czt, czt, czt], axis=0),
                       preferred_element_type=F32)
        reps = jnp.concatenate([jnp.broadcast_to(parts[4 * d + h:4 * d + h + 1, :], (L, 3 * L)) for h in range(4)],
                               axis=0).astype(BF16)
        cumb = lax.dot_general(jnp.concatenate([cz, cz, cz], axis=1), reps, (((1,), (1,)), ((), ())),
                               preferred_element_type=F32)
        for g in range(2):
            xp = xbc[:, 128 * g:128 * g + 128].astype(BF16)
            cm = xbc[:, 512 + 128 * g:640 + 128 * g].astype(BF16)
            bmt = bmt_ref[sm, 128 * g:128 * g + 128, :]
            hprev = hs[sm, d, g]
            go = jnp.dot(cm, jnp.concatenate([bmt, hprev.astype(BF16)], axis=1), preferred_element_type=F32)
            gmat, yo = go[:, :L], go[:, L:]
            bmf = bmt.astype(F32)
            lhs, cbs, cls = [], [], []
            for hh in range(2):
                h = 2 * g + hh
                cb = cumb[:, L * h:L * h + L]
                crow = cumt[4 * d + h:4 * d + h + 1, :]
                dtrow = dtt[4 * d + h:4 * d + h + 1, :]
                cl = cb[last:last + 1, :]
                lhs.append((gmat * jnp.where(causal, jnp.exp(cb - crow), 0.0) * dtrow).astype(BF16))
                lhs.append((bmf * (jnp.exp(cl - crow) * dtrow)).astype(BF16))
                cbs.append(cb)
                cls.append(cl)
            big = jnp.dot(jnp.concatenate(lhs, axis=0), xp, preferred_element_type=F32)
            y_ref[sm, :, 128 * g:128 * g + 128] = pick(big[0:L], big[2 * L:3 * L]) + yo * jnp.exp(pick(cbs[0], cbs[1]))
            hs[sm, d, g] = hprev * jnp.exp(pick(cls[0], cls[1])) + pick(big[L:2 * L], big[3 * L:4 * L])

    for sm in range(SSD_SB):
        direction(sm, xf_ref, dttf_ref, bmtf_ref, yf_ref, 0, si <= li, li <= si, L - 1)
        direction(sm, xb_ref, dttb_ref, bmtb_ref, yb_ref, 1, si >= li, li >= si, 0)


def _ssd(xbc, dtt, bmt, a_col, l):
    nbatch = xbc.shape[0]
    nch = N // SSD_CHUNK
    nctx = LC // SSD_CHUNK
    fidx = lambda i: jnp.where(i < nctx, nch - nctx + i, i - nctx)
    bidx = lambda i: nch - 1 - i
    ins = lambda f: [pl.BlockSpec((SSD_SB, SSD_CHUNK, 768), lambda b, i: (b, f(i), 0)),
                     pl.BlockSpec((SSD_SB, 128, SSD_CHUNK), lambda b, i: (b, 0, f(i))),
                     pl.BlockSpec((SSD_SB, 256, SSD_CHUNK), lambda b, i: (b, 0, f(i)))]
    return pl.pallas_call(
        _ssd_kernel,
        grid=(nbatch // SSD_SB, nch),
        in_specs=ins(fidx) + ins(bidx) + [_lspec(a_col, l)],
        out_specs=[pl.BlockSpec((SSD_SB, SSD_CHUNK, 256), lambda b, i: (b, fidx(i), 0)),
                   pl.BlockSpec((SSD_SB, SSD_CHUNK, 256), lambda b, i: (b, bidx(i), 0))],
        out_shape=[jax.ShapeDtypeStruct((nbatch, N, 256), F32)] * 2,
        scratch_shapes=[pltpu.VMEM((SSD_SB, 2, 2, 128, 128), F32)],
        compiler_params=_cp("arbitrary", "arbitrary"),
        name="ssd",
    )(xbc, dtt, bmt, xbc, dtt, bmt, a_col)


def _m2_norm_kernel(x_ref, z_ref, yf_ref, yb_ref, d_ref, g_ref, o_ref):
    y = d_ref[...] * x_ref[...] + yf_ref[...] + yb_ref[...]
    o_ref[...] = _rms(y * _silu(z_ref[...]), g_ref[...]).astype(o_ref.dtype)


def _m2_norm(xbc2d, p_m2_2d, yf2d, yb2d, dvec, gn, l):
    rows = xbc2d.shape[0]
    rb = 4 * ROW_BLK
    blk = pl.BlockSpec((rb, 256), lambda i: (i, 0))
    full = lambda a: _lspec(a, l)
    return pl.pallas_call(
        _m2_norm_kernel,
        grid=(rows // rb,),
        in_specs=[blk, blk, blk, blk, full(dvec), full(gn)],
        out_specs=blk,
        out_shape=jax.ShapeDtypeStruct((rows, 256), BF16),
        compiler_params=_cp("arbitrary"),
        name="m2_norm",
    )(xbc2d, p_m2_2d, yf2d, yb2d, dvec, gn)


def _merge_kernel(xl_ref, xc_ref, oal_ref, oac_ref, on_ref, os_ref, om_ref, gl_ref, wb_ref, wo_ref, gpm_ref, gate1_ref,
                  gpf_ref, sh2_ref, sc2_ref, wr_ref, x1_ref, h2_ref, aff_ref, *, nb):
    outs = (_stream_block(oal_ref, oac_ref, nb), on_ref[...], os_ref[...], om_ref[...])
    y = None
    for j in range(4):
        d = jnp.dot(outs[j], wb_ref[j], preferred_element_type=F32)
        t = d * jnp.tanh(gl_ref[:, 1024 * j:1024 * j + 1024]) + d
        y = t if y is None else y + t
    y2 = _dot(y, wo_ref[...])
    x1 = _stream_block(xl_ref, xc_ref, nb) + gate1_ref[0] * _rms(y2, gpm_ref[...])
    x1_ref[...] = x1
    h2 = _rms(x1, gpf_ref[...]) * (1.0 + sc2_ref[0]) + sh2_ref[0]
    h2_ref[...] = h2.astype(BF16)
    logits = _dot_hi_nt(wr_ref[...], h2)
    m = jnp.max(logits, axis=0, keepdims=True)
    e = jnp.exp(logits - m)
    aff_ref[0, 0] = e / jnp.sum(e, axis=0, keepdims=True)


def _merge(stream, o_mla, o_na, o_s5, o_m2, gl, wb, wo, gpm, mod3, gpf, wrt, nbatch, nb, l):
    mrow = _mod_row(nb)
    xargs, xspecs = _stream_specs(stream, nb)
    aargs, aspecs = _stream_specs((o_mla[0], o_mla[1], 0), nb)
    fb = lambda i: (_flat_blk(i, nb), 0)
    blk = lambda w: pl.BlockSpec((ROW_BLK, w), fb)
    oblk = lambda w: pl.BlockSpec((ROW_BLK, w), lambda i: (i, 0))
    full = lambda a: _lspec(a, l)
    modspec = lambda k: pl.BlockSpec((1, 1, D), lambda i: (mrow(i, nbatch), 0, k))
    orows = nbatch * nb * ROW_BLK
    return pl.pallas_call(
        functools.partial(_merge_kernel, nb=nb),
        grid=(nbatch * nb,),
        in_specs=xspecs + aspecs + [oblk(256), blk(256), blk(256), blk(GATE_W),
                  full(wb), full(wo), full(gpm), modspec(2), full(gpf), modspec(3), modspec(4), full(wrt)],
        out_specs=[oblk(D), oblk(D),
                   pl.BlockSpec((1, 1, N_EXPERTS, ROW_BLK), lambda i: (i // nb, i % nb, 0, 0))],
        out_shape=[jax.ShapeDtypeStruct((orows, D), F32),
                   jax.ShapeDtypeStruct((orows, D), BF16),
                   jax.ShapeDtypeStruct((nbatch, nb, N_EXPERTS, ROW_BLK), F32)],
        compiler_params=_cp("arbitrary"),
        name="merge",
    )(*xargs, *aargs, o_na, o_s5, o_m2, gl, wb, wo, gpm, mod3, gpf, mod3, mod3, wrt)


def _topk_kernel(aff_ref, slot_ref, *, nk, cap):
    a = aff_ref[0]
    bits = lax.bitcast_convert_type(a, jnp.int32)
    count = lambda m: jnp.sum(jnp.sum(jnp.where(m, 1.0, 0.0), axis=2, keepdims=True), axis=0, keepdims=True)
    thr = jnp.where(count(bits >= (1 << 30)) >= cap, jnp.int32(1 << 30), jnp.zeros((1, N_EXPERTS, 1), jnp.int32))
    for bit in range(29, 0, -2):
        hi, lo = 1 << bit, 1 << (bit - 1)
        ok = [count(bits >= (thr | c)) >= cap for c in (hi | lo, hi, lo)]
        thr = thr | jnp.where(ok[0], hi | lo, jnp.where(ok[1], hi, jnp.where(ok[2], lo, 0)))
    gt = bits > thr
    eq = bits == thr
    need = cap - count(gt)

    tri = jnp.where(lax.broadcasted_iota(jnp.int32, (256, 256), 0) <= lax.broadcasted_iota(jnp.int32, (256, 256), 1),
                    1.0, 0.0).astype(BF16)

    def prefix_excl(m):
        incl = jnp.dot(m.reshape(nk * N_EXPERTS, 256).astype(BF16), tri,
                       preferred_element_type=F32).reshape(nk, N_EXPERTS, 256)
        offs = []
        run = jnp.zeros((1, N_EXPERTS, 1), F32)
        for k in range(nk):
            offs.append(run)
            run = run + incl[k:k + 1, :, 255:256]
        off = offs[0] if nk == 1 else jnp.concatenate(offs, axis=0)
        return incl - m + off

    eqf = jnp.where(eq, 1.0, 0.0)
    sel = jnp.where(gt, 1.0, jnp.where(eq & (prefix_excl(eqf) < need), 1.0, 0.0))
    slot = jnp.where(sel > 0.5, prefix_excl(sel), -1.0)
    slot_ref[0] = slot.astype(jnp.int32)


def _topk(aff, blk0, nk, cap):
    nbatch = aff.shape[0]
    return pl.pallas_call(
        functools.partial(_topk_kernel, nk=nk, cap=cap),
        grid=(nbatch,),
        in_specs=[pl.BlockSpec((1, nk, N_EXPERTS, ROW_BLK), lambda b: (b, blk0, 0, 0))],
        out_specs=pl.BlockSpec((1, nk, N_EXPERTS, ROW_BLK), lambda b: (b, 0, 0, 0)),
        out_shape=jax.ShapeDtypeStruct((nbatch, nk, N_EXPERTS, ROW_BLK), jnp.int32),
        compiler_params=_cp("arbitrary"),
        name="topk",
    )(aff)


def _gather_kernel(w0_ref, fits_ref, h_ref, sl_ref, xs_ref, acc):
    b = pl.program_id(0)
    k = pl.program_id(1)

    @pl.when(k == 0)
    def _():
        acc[...] = jnp.zeros_like(acc)

    h = h_ref[0]

    @pl.when(fits_ref[b] == 1)
    def _():
        r_iota = lax.broadcasted_iota(jnp.int32, (MOE_WIN, ROW_BLK), 0)
        w0s = [pl.multiple_of(w0_ref[(b * LAT_BLKS + k) * N_EXPERTS + e], 16) for e in range(N_EXPERTS)]
        onehot = jnp.concatenate([jnp.where((sl_ref[0, 0, e:e + 1, :] - w0s[e]) == r_iota, 1.0, 0.0).astype(BF16)
                                  for e in range(N_EXPERTS)], axis=0)
        rows = jnp.dot(onehot, h, preferred_element_type=F32)
        for e in range(N_EXPERTS):
            acc[e, pl.ds(w0s[e], MOE_WIN), :] += rows[MOE_WIN * e:MOE_WIN * e + MOE_WIN]

    @pl.when(fits_ref[b] != 1)
    def _():
        r_iota = lax.broadcasted_iota(jnp.int32, (CAP_LAT, ROW_BLK), 0)
        for e in range(N_EXPERTS):
            onehot = jnp.where(sl_ref[0, 0, e:e + 1, :] == r_iota, 1.0, 0.0).astype(BF16)
            acc[e] += jnp.dot(onehot, h, preferred_element_type=F32)

    @pl.when(k == pl.num_programs(1) - 1)
    def _():
        xs_ref[0] = acc[...].astype(xs_ref.dtype)


def _gather(win0, fits, h2, slot_lat):
    nbatch = h2.shape[0]
    smem = pl.BlockSpec(memory_space=pltpu.SMEM)
    return pl.pallas_call(
        _gather_kernel,
        grid=(nbatch, LAT_BLKS),
        in_specs=[smem, smem,
                  pl.BlockSpec((1, ROW_BLK, D), lambda b, k: (b, k, 0)),
                  pl.BlockSpec((1, 1, N_EXPERTS, ROW_BLK), lambda b, k: (b, k, 0, 0))],
        out_specs=pl.BlockSpec((1, N_EXPERTS, CAP_LAT, D), lambda b, k: (b, 0, 0, 0)),
        out_shape=jax.ShapeDtypeStruct((nbatch, N_EXPERTS, CAP_LAT, D), BF16),
        scratch_shapes=[pltpu.VMEM((N_EXPERTS, CAP_LAT, D), F32)],
        compiler_params=_cp("arbitrary", "arbitrary"),
        name="gather",
    )(win0, fits, h2, slot_lat)


def _expert_kernel(*refs, has_ctx):
    if has_ctx:
        (w0_ref, fits_ref, xl_ref, sl_ref, al_ref, hc_ref, sc_ref, ac_ref, wg_ref, wu_ref, wd_ref,
         yl_ref, yc_ref, wgb, wub, wdb, gs_acc, xc_all, gc_all) = refs
    else:
        (w0_ref, fits_ref, xl_ref, sl_ref, al_ref, wg_ref, wu_ref, wd_ref, yl_ref,
         wgb, wub, wdb, gs_acc) = refs
    e = pl.program_id(0)
    bp = pl.program_id(1)

    @pl.when(bp == 0)
    def _():
        wgb[...] = wg_ref[0, 0].astype(BF16)
        wub[...] = wu_ref[0, 0].astype(BF16)
        wdb[...] = wd_ref[0, 0].astype(BF16)

    def gather(h_ref, slot_ref, aff_ref, s, nk, cap):
        r_iota = lax.broadcasted_iota(jnp.int32, (cap, ROW_BLK), 0)
        xs = jnp.zeros((cap, D), F32)
        gs = jnp.zeros((cap, 1), F32)
        for k in range(nk):
            pm = slot_ref[s, k, pl.ds(e, 1), :] == r_iota
            xs = xs + jnp.dot(jnp.where(pm, 1.0, 0.0).astype(BF16), h_ref[s, ROW_BLK * k:ROW_BLK * k + ROW_BLK, :],
                              preferred_element_type=F32)
            gs = gs + jnp.sum(jnp.where(pm, aff_ref[s, k, pl.ds(e, 1), :], 0.0), axis=1, keepdims=True)
        return xs.astype(BF16), gs

    def ffn(xsb):
        gt = jnp.dot(xsb, wgb[...], preferred_element_type=F32)
        up = jnp.dot(xsb, wub[...], preferred_element_type=F32)
        return jnp.dot((_silu(gt) * up).astype(BF16), wdb[...], preferred_element_type=F32)

    for s in range(EXP_SB):
        b = bp * EXP_SB + s
        base = CAP_LAT * s

        @pl.when(fits_ref[b] == 1)
        def _():
            gs_acc[base:base + CAP_LAT, :] = jnp.zeros((CAP_LAT, 1), F32)
            r_iota = lax.broadcasted_iota(jnp.int32, (MOE_WIN, ROW_BLK), 0)
            for k in range(LAT_BLKS):
                w0 = pl.multiple_of(w0_ref[(b * LAT_BLKS + k) * N_EXPERTS + e], 16)
                pm = (sl_ref[s, k, pl.ds(e, 1), :] - w0) == r_iota
                gs_acc[pl.ds(base + w0, MOE_WIN), :] += jnp.sum(jnp.where(pm, al_ref[s, k, pl.ds(e, 1), :], 0.0),
                                                                axis=1, keepdims=True)

        @pl.when(fits_ref[b] != 1)
        def _():
            r_iota = lax.broadcasted_iota(jnp.int32, (CAP_LAT, ROW_BLK), 0)
            gs = jnp.zeros((CAP_LAT, 1), F32)
            for k in range(LAT_BLKS):
                pm = sl_ref[s, k, pl.ds(e, 1), :] == r_iota
                gs = gs + jnp.sum(jnp.where(pm, al_ref[s, k, pl.ds(e, 1), :], 0.0), axis=1, keepdims=True)
            gs_acc[base:base + CAP_LAT, :] = gs

    xs = xl_ref[...].reshape(EXP_SB * CAP_LAT, D)
    y = ffn(xs) * gs_acc[...]
    for s in range(EXP_SB):
        yl_ref[s, 0] = y[CAP_LAT * s:CAP_LAT * s + CAP_LAT].astype(yl_ref.dtype)
    if has_ctx:
        for s in range(EXP_SB):
            xs, gs = gather(hc_ref, sc_ref, ac_ref, s, 1, CAP_CTX)
            row = pl.multiple_of((bp * EXP_SB + s) * CAP_CTX, CAP_CTX)
            xc_all[pl.ds(row, CAP_CTX), :] = xs
            gc_all[pl.ds(row, CAP_CTX), :] = gs

        @pl.when(bp == pl.num_programs(1) - 1)
        def _():
            yc_ref[0] = (ffn(xc_all[...]) * gc_all[...]).astype(yc_ref.dtype)


def _experts(win0, fits, xs_lat, h2, slot_lat, slot_ctx, aff, w_gate, w_up, w_down, l, has_ctx):
    nbatch = h2.shape[0]
    assert nbatch % EXP_SB == 0
    idx4 = lambda e, b: (b, 0, 0, 0)
    smem = pl.BlockSpec(memory_space=pltpu.SMEM)
    in_specs = [smem, smem,
                pl.BlockSpec((EXP_SB, 1, CAP_LAT, D), lambda e, b: (b, e, 0, 0)),
                pl.BlockSpec((EXP_SB, LAT_BLKS, N_EXPERTS, ROW_BLK), idx4),
                pl.BlockSpec((EXP_SB, LAT_BLKS, N_EXPERTS, ROW_BLK), idx4)]
    args = [win0, fits, xs_lat, slot_lat, aff]
    out_specs = [pl.BlockSpec((EXP_SB, 1, CAP_LAT, D), lambda e, b: (b, e, 0, 0))]
    out_shape = [jax.ShapeDtypeStruct((nbatch, N_EXPERTS, CAP_LAT, D), BF16)]
    if has_ctx:
        in_specs += [pl.BlockSpec((EXP_SB, LC, D), lambda e, b: (b, LAT_BLKS, 0)),
                     pl.BlockSpec((EXP_SB, 1, N_EXPERTS, ROW_BLK), idx4),
                     pl.BlockSpec((EXP_SB, 1, N_EXPERTS, ROW_BLK), lambda e, b: (b, LAT_BLKS, 0, 0))]
        args += [h2, slot_ctx, aff]
        out_specs.append(pl.BlockSpec((1, nbatch * CAP_CTX, D), lambda e, b: (e, 0, 0)))
        out_shape.append(jax.ShapeDtypeStruct((N_EXPERTS, nbatch * CAP_CTX, D), BF16))
    wspec = pl.BlockSpec((1, 1, D, D), lambda e, b: (l, e, 0, 0))
    in_specs += [wspec, wspec, wspec]
    args += [w_gate, w_up, w_down]
    scratch = [pltpu.VMEM((D, D), BF16)] * 3 + [pltpu.VMEM((EXP_SB * CAP_LAT, 1), F32)]
    if has_ctx:
        scratch += [pltpu.VMEM((nbatch * CAP_CTX, D), BF16), pltpu.VMEM((nbatch * CAP_CTX, 1), F32)]
    return pl.pallas_call(
        functools.partial(_expert_kernel, has_ctx=has_ctx),
        grid=(N_EXPERTS, nbatch // EXP_SB),
        in_specs=in_specs,
        out_specs=out_specs,
        out_shape=out_shape,
        scratch_shapes=scratch,
        compiler_params=_cp("arbitrary", "arbitrary"),
        name="experts",
    )(*args)


def _combine_kernel(*refs, has_ctx):
    if has_ctx:
        w0_ref, fits_ref, x1_ref, yl_ref, scl_ref, yc_ref, scc_ref, gate2_ref, g_ref, o_ref, ycat = refs
    else:
        w0_ref, fits_ref, x1_ref, yl_ref, scl_ref, gate2_ref, g_ref, o_ref, ycat = refs
    b = pl.program_id(0)
    j = pl.program_id(1)

    def finish(acc):
        o_ref[0] = x1_ref[0] + gate2_ref[0] * _rms(acc, g_ref[...])

    def comb(sc_ref, y_of, cap):
        lane = lax.broadcasted_iota(jnp.int32, (ROW_BLK, cap), 1)
        sc = sc_ref[0]
        acc = jnp.zeros((ROW_BLK, D), F32)
        for e in range(N_EXPERTS):
            pt = jnp.where(sc[:, e:e + 1] == lane, 1.0, 0.0).astype(BF16)
            acc = acc + jnp.dot(pt, y_of(e), preferred_element_type=F32)
        finish(acc)

    def comb_windowed():
        lane = lax.broadcasted_iota(jnp.int32, (ROW_BLK, ROW_BLK), 1)
        sc = scl_ref[0]
        per = ROW_BLK // MOE_WIN
        if per * MOE_WIN < ROW_BLK:
            ycat[:, per * MOE_WIN:, :] = jnp.zeros((2, ROW_BLK - per * MOE_WIN, D), BF16)
        acc = jnp.zeros((ROW_BLK, D), F32)
        for gi, e0 in enumerate(range(0, N_EXPERTS, per)):
            hit = None
            for q, e in enumerate(range(e0, min(e0 + per, N_EXPERTS))):
                w0 = pl.multiple_of(w0_ref[(b * LAT_BLKS + j) * N_EXPERTS + e], 16)
                d = sc[:, e:e + 1] - w0
                d = jnp.where(jnp.logical_and(d >= 0, d < MOE_WIN), d + MOE_WIN * q, -1)
                hit = (d == lane) if hit is None else jnp.logical_or(hit, d == lane)
                ycat[gi % 2, MOE_WIN * q:MOE_WIN * q + MOE_WIN, :] = yl_ref[0, e, pl.ds(w0, MOE_WIN), :]
            acc = acc + jnp.dot(jnp.where(hit, 1.0, 0.0).astype(BF16), ycat[gi % 2], preferred_element_type=F32)
        finish(acc)

    lat_y = lambda e: yl_ref[0, e]
    fits = fits_ref[b] == 1

    @pl.when(jnp.logical_and(j < LAT_BLKS, fits))
    def _():
        comb_windowed()

    @pl.when(jnp.logical_and(j < LAT_BLKS, jnp.logical_not(fits)))
    def _():
        comb(scl_ref, lat_y, CAP_LAT)

    if has_ctx:
        @pl.when(j == LAT_BLKS)
        def _():
            comb(scc_ref, lambda e: yc_ref[e], CAP_CTX)


def _combine(win0, fits, x1, y_lat, scol_lat, y_ctx, scol_ctx, mod3, g, has_ctx, l):
    nbatch = x1.shape[0]
    nb = NBLK if has_ctx else LAT_BLKS
    ntok = N if has_ctx else T
    smem = pl.BlockSpec(memory_space=pltpu.SMEM)
    in_specs = [smem, smem,
                pl.BlockSpec((1, ROW_BLK, D), lambda b, j: (b, j, 0)),
                pl.BlockSpec((1, N_EXPERTS, CAP_LAT, D), lambda b, j: (b, 0, 0, 0)),
                pl.BlockSpec((1, ROW_BLK, N_EXPERTS), lambda b, j: (b, jnp.minimum(j, LAT_BLKS - 1), 0))]
    args = [win0, fits, x1, y_lat, scol_lat]
    if has_ctx:
        in_specs += [pl.BlockSpec((N_EXPERTS, CAP_CTX, D), lambda b, j: (0, b, 0)),
                     pl.BlockSpec((1, ROW_BLK, N_EXPERTS), lambda b, j: (b, 0, 0))]
        args += [y_ctx, scol_ctx]
    in_specs += [pl.BlockSpec((1, 1, D), lambda b, j: (jnp.where(j == LAT_BLKS, nbatch, b), 0, 5)),
                 _lspec(g, l)]
    args += [mod3, g]
    return pl.pallas_call(
        functools.partial(_combine_kernel, has_ctx=has_ctx),
        grid=(nbatch, nb),
        in_specs=in_specs,
        out_specs=pl.BlockSpec((1, ROW_BLK, D), lambda b, j: (b, j, 0)),
        out_shape=jax.ShapeDtypeStruct((nbatch, ntok, D), F32),
        scratch_shapes=[pltpu.VMEM((2, ROW_BLK, D), BF16)],
        compiler_params=_cp("arbitrary", "arbitrary"),
        name="combine",
    )(*args)


def _rope_tables():
    f = 1.0 / (ROPE_THETA ** (jnp.arange(0, 16, 2, dtype=F32) / 16))
    pos = jnp.arange(T)
    row, col = pos // GRID_W, pos % GRID_W
    ar = row.astype(F32)[:, None] * f[None, :]
    ac = col.astype(F32)[:, None] * f[None, :]
    cos32 = jnp.concatenate([jnp.cos(ar), jnp.cos(ar), jnp.cos(ac), jnp.cos(ac)], axis=-1)
    sin32 = jnp.concatenate([jnp.sin(ar), jnp.sin(ar), jnp.sin(ac), jnp.sin(ac)], axis=-1)
    cos32 = jnp.concatenate([cos32, jnp.ones((LC, 32), F32)], axis=0)
    sin32 = jnp.concatenate([sin32, jnp.zeros((LC, 32), F32)], axis=0)
    return cos32, sin32


def _rot_cols(w):
    a, b, c, d = w[..., 0:8], w[..., 8:16], w[..., 16:24], w[..., 24:32]
    return jnp.concatenate([-b, a, -d, c], axis=-1)


def _mla_weights(w_uq, w_ukv):
    nl, r, _ = w_uq.shape
    wq3 = w_uq.reshape(nl, r, 4, 96)
    z32 = jnp.zeros((nl, r, 4, 32), F32)
    wq = jnp.concatenate([wq3, z32], axis=-1).reshape(nl, r, 512)
    wqr = jnp.concatenate([jnp.zeros((nl, r, 4, 64), F32), _rot_cols(wq3[..., 64:96]), z32], axis=-1).reshape(nl, r, 512)
    rk = w_ukv.shape[1]
    wkv3 = w_ukv.reshape(nl, rk, 4, 128)
    wk = jnp.concatenate([wkv3[..., :64], jnp.zeros((nl, rk, 4, 64), F32)], axis=-1).reshape(nl, rk, 512)
    z64 = jnp.zeros((nl, rk, 64), F32)
    vh = [wkv3[:, :, h, 64:] for h in range(4)]
    wv = jnp.concatenate([vh[0], z64, z64, vh[1], vh[2], z64, z64, vh[3]], axis=-1)
    vone = np.tile(np.repeat(np.array([0.0, 1.0, 1.0, 0.0], np.float32), 64), 2)[None, :]
    e = np.zeros((32, 512), np.float32)
    for h in range(4):
        e[np.arange(32), 128 * h + 64 + np.arange(32)] = 1.0
    return (wq.astype(BF16), wqr.astype(BF16), wk.astype(BF16), wv.astype(BF16), jnp.asarray(e, BF16),
            jnp.asarray(vone))


def _inproj_weights(w):
    o = np.cumsum([0, 256, 128, 32, 768, 256, 256, 768, 8, 4096])
    w = w.astype(BF16)
    seg = lambda i: w[:, :, o[i]:o[i + 1]]
    nl = w.shape[0]
    kr = seg(2)
    wm = jnp.concatenate([seg(0), seg(1), kr, _rot_cols(kr), jnp.zeros((nl, D, 64), BF16)], axis=-1)
    w2 = jnp.concatenate([seg(5), seg(6), seg(7), jnp.zeros((nl, D, 120), BF16)], axis=-1)
    return [wm, seg(3), seg(4), w2, 0.5 * seg(8)]


def kernel(x, c, ctx, c_ctx, w_ada, b_ada, g_pre_mix, g_post_mix, g_pre_ffn, g_post_ffn, w_in, mla_g_cq, mla_g_ckv, mla_w_uq, mla_w_ukv, na_rpb, s5_a_re, s5_a_im, s5_log_step, s5_b_re, s5_b_im, s5_c_re, s5_c_im, s5_d, s5_w_glu, s5_b_glu, m2_conv_w, m2_conv_b, m2_a_log, m2_dt_bias, m2_d, m2_g_norm, w_branch, w_out, w_router, w_gate, w_up, w_down):
    nbatch = x.shape[0]
    depth = w_ada.shape[0]
    vec = lambda v: v.reshape(depth, 1, -1).astype(F32)

    stream = (x, ctx, 0)
    cvec = jnp.concatenate([c, c_ctx[None, :], jnp.zeros((7, D), F32)], axis=0)
    cvec = cvec[: ((nbatch + 1 + 7) // 8) * 8]
    b_ada3 = b_ada.reshape(depth, 1, 6 * D)

    cos32, sin32 = _rope_tables()
    qscale = (64 + 32) ** -0.5 * math.log2(math.e)
    lane_is_rope = np.tile(np.concatenate([np.zeros(64, bool), np.ones(32, bool), np.zeros(32, bool)]), 4)
    pick = lambda t32, fill: jnp.where(lane_is_rope[None, :], jnp.tile(jnp.pad(t32, ((0, 0), (64, 32))), (1, 4)), fill)
    cos_q = pick(cos32, 1.0) * qscale
    sin_q = pick(sin32, 0.0) * qscale

    g_pre_mix, g_post_mix, g_pre_ffn, g_post_ffn = vec(g_pre_mix), vec(g_post_mix), vec(g_pre_ffn), vec(g_post_ffn)
    w_in_segs = _inproj_weights(w_in)
    wq, wqr, wk, wv, e_mat, vone = _mla_weights(mla_w_uq, mla_w_ukv)
    mla_g_cq, mla_g_ckv = vec(mla_g_cq), vec(mla_g_ckv)
    na_tabs = _na_tables(na_rpb)
    s5_bmat, s5_avec, s5_cmat = _s5_params(s5_a_re, s5_a_im, s5_log_step, s5_b_re, s5_b_im, s5_c_re, s5_c_im, nbatch)
    s5_d, s5_b_glu, s5_w_glu = vec(s5_d), vec(s5_b_glu), s5_w_glu.astype(BF16)
    m2_conv_w, m2_conv_b = m2_conv_w.astype(F32), vec(m2_conv_b)
    dt_bias = jnp.pad(m2_dt_bias.reshape(depth, 1, 8).astype(F32), ((0, 0), (0, 0), (0, 120)))
    a_col = jnp.pad(-jnp.exp(m2_a_log.astype(F32)).reshape(depth, 8, 1), ((0, 0), (0, 120), (0, 0)))
    m2_dvec, m2_g_norm = vec(jnp.repeat(m2_d, 64, axis=-1)), vec(m2_g_norm)
    w_branch, w_out = (0.5 * w_branch).astype(BF16), w_out.astype(BF16)
    w_router_t = jnp.swapaxes(w_router, 1, 2).astype(F32)

    for l in range(depth):
        has_ctx = l < depth - 1
        nb = NBLK if has_ctx else LAT_BLKS
        mod = _ada(cvec, w_ada, b_ada3, l)
        mod3 = mod.reshape(mod.shape[0], 1, 6 * D)

        p_mla, p_na, p_s5, p_m2, p_gate = _inproj(stream, g_pre_mix, mod3, w_in_segs, nbatch, l, has_ctx)

        q, k, v = _mla_prep(p_mla, mla_g_cq, mla_g_ckv, wq, wqr, wk, wv, e_mat, vone, cos_q, sin_q, cos32, sin32, l)
        o_mla = _mla_attn(q.reshape(nbatch, N, 512), k.reshape(nbatch, N, 512), v.reshape(nbatch, N, 512), has_ctx)

        o_na = _na_attn(p_na.reshape(nbatch, N, NA_W), na_tabs, has_ctx, l)

        yf, yb = _s5_scan(p_s5.reshape(nbatch, N, S5_W), s5_bmat, s5_avec, s5_cmat, l)
        o_s5 = _s5_glu(p_s5, yf.reshape(nbatch * N, S5_W), yb.reshape(nbatch * N, S5_W), s5_d, s5_w_glu, s5_b_glu, l)

        xbc, dtt, bmt = _m2_prep(p_m2.reshape(nbatch, N, M2_W), m2_conv_w, m2_conv_b, dt_bias, l)
        ssd_f, ssd_b = _ssd(xbc, dtt, bmt, a_col, l)
        o_m2 = _m2_norm(xbc.reshape(nbatch * N, 768), p_m2, ssd_f.reshape(nbatch * N, 256),
                        ssd_b.reshape(nbatch * N, 256), m2_dvec, m2_g_norm, l)

        x1, h2, aff = _merge(stream, o_mla, o_na.reshape(-1, 256), o_s5, o_m2, p_gate,
                             w_branch, w_out, g_post_mix, mod3, g_pre_ffn, w_router_t, nbatch, nb, l)

        slot_lat = _topk(aff, 0, LAT_BLKS, CAP_LAT)
        slot_ctx = _topk(aff, LAT_BLKS, 1, CAP_CTX) if has_ctx else None
        cnt = jnp.sum((slot_lat >= 0).astype(jnp.int32), axis=-1)
        first = jnp.cumsum(cnt, axis=1) - cnt
        win0 = jnp.minimum((first // 16) * 16, CAP_LAT - MOE_WIN)
        fits = jnp.all(first + cnt <= win0 + MOE_WIN, axis=(1, 2)).astype(jnp.int32)
        win0 = win0.reshape(-1)
        h2 = h2.reshape(nbatch, nb * ROW_BLK, D)
        ys = _experts(win0, fits, _gather(win0, fits, h2, slot_lat), h2, slot_lat, slot_ctx, aff, w_gate, w_up, w_down,
                      l, has_ctx)
        scol_lat = jnp.transpose(slot_lat, (0, 1, 3, 2)).reshape(nbatch, T, N_EXPERTS)
        scol_ctx = jnp.transpose(slot_ctx, (0, 1, 3, 2)).reshape(nbatch, LC, N_EXPERTS) if has_ctx else None
        xs = _combine(win0, fits, x1.reshape(nbatch, nb * ROW_BLK, D), ys[0], scol_lat, ys[1] if has_ctx else None,
                      scol_ctx, mod3, g_post_ffn, has_ctx, l)
        stream = (xs, xs, LAT_BLKS)
    return xs
```

```python
import functools
import math

import numpy as np
import jax
import jax.numpy as jnp
from jax import lax
from jax.experimental import pallas as pl
from jax.experimental.pallas import tpu as pltpu

F32 = jnp.float32
BF16 = jnp.bfloat16

D = 1024
T = 2048
LC = 256
N = T + LC
GRID_W = 64
ROW_BLK = 256
NBLK = N // ROW_BLK
LAT_BLKS = T // ROW_BLK
EPS = 1e-6
N_EXPERTS = 16
CAP_LAT = 2 * T // N_EXPERTS
CAP_CTX = 2 * LC // N_EXPERTS
ROPE_THETA = 10000.0
NEG = -1e30

MLA_W = 512
NA_W = 768
S5_W = 256
M2_W = 1152
GATE_W = 4096

MOE_WIN = 64
EXP_SB = 4
S5_CHUNK = 64
SSD_CHUNK = 128
SSD_SB = 8
NA_SB = 4

VMEM_LIMIT = 56 * 1024 * 1024


def _cp(*sem, vmem=VMEM_LIMIT):
    return pltpu.CompilerParams(dimension_semantics=sem, vmem_limit_bytes=vmem)


def _lspec(a, l, *lead):
    nd = a.ndim - 1 - len(lead)
    return pl.BlockSpec((None,) * (1 + len(lead)) + a.shape[1 + len(lead):], lambda *_: (l,) + lead + (0,) * nd)


def _dot(a, b):
    return jnp.dot(a.astype(BF16), b.astype(BF16), preferred_element_type=F32)


def _dot_nt(a, b):
    return lax.dot_general(a.astype(BF16), b.astype(BF16), (((1,), (1,)), ((), ())),
                           preferred_element_type=F32)


def _split3(a):
    hi = a.astype(BF16)
    r = a - hi.astype(F32)
    mid = r.astype(BF16)
    lo = (r - mid.astype(F32)).astype(BF16)
    return hi, mid, lo


def _dot_hi(a, b):
    ah, am, _ = _split3(a)
    bh, bm, _ = _split3(b)
    f = lambda x, y: jnp.dot(x, y, preferred_element_type=F32)
    return f(ah, bh) + (f(ah, bm) + f(am, bh))


def _dot_hi_nt(a, b):
    ah, am, _ = _split3(a)
    bh, bm, _ = _split3(b)
    f = lambda x, y: lax.dot_general(x, y, (((1,), (1,)), ((), ())), preferred_element_type=F32)
    return f(ah, bh) + (f(ah, bm) + f(am, bh))


def _dot_exact_lhs(m_bf16, a):
    h, m, l = _split3(a)
    f = lambda y: jnp.dot(m_bf16, y, preferred_element_type=F32)
    return f(h) + (f(m) + f(l))


def _dot_exact_rhs(a, m_bf16):
    h, m, l = _split3(a)
    f = lambda y: jnp.dot(y, m_bf16, preferred_element_type=F32)
    return f(h) + (f(m) + f(l))


def _sigmoid(x):
    return 0.5 * jnp.tanh(0.5 * x) + 0.5


def _silu(x):
    return x * _sigmoid(x)


def _rms(x, g):
    return x * lax.rsqrt(jnp.mean(x * x, axis=-1, keepdims=True) + EPS) * g


def _mod_row(nb):
    def f(i, nbatch):
        return jnp.where(i % nb == LAT_BLKS, nbatch, i // nb)
    return f


def _flat_blk(i, nb):
    return (i // nb) * NBLK + i % nb


def _stream_specs(stream, nb):
    lat, ctxa, cblk = stream
    w = lat.shape[-1]
    return [lat, ctxa], [pl.BlockSpec((1, ROW_BLK, w), lambda i: (i // nb, jnp.minimum(i % nb, LAT_BLKS - 1), 0)),
                         pl.BlockSpec((1, ROW_BLK, w), lambda i: (i // nb, cblk, 0))]


def _stream_block(xl_ref, xc_ref, nb):
    return jnp.where(pl.program_id(0) % nb == LAT_BLKS, xc_ref[0], xl_ref[0])


def _ada_kernel(c_ref, w_ref, b_ref, o_ref):
    c = c_ref[...]
    o_ref[...] = _dot_hi(_silu(c), w_ref[0]) + b_ref[0]


def _ada(cvec, w_ada, b_ada3, l):
    rows = cvec.shape[0]
    return pl.pallas_call(
        _ada_kernel,
        grid=(6,),
        in_specs=[pl.BlockSpec((rows, D), lambda k: (0, 0)),
                  pl.BlockSpec((1, D, D), lambda k: (l, 0, k)),
                  pl.BlockSpec((1, 1, D), lambda k: (l, 0, k))],
        out_specs=pl.BlockSpec((rows, D), lambda k: (0, k)),
        out_shape=jax.ShapeDtypeStruct((rows, 6 * D), F32),
        compiler_params=_cp("arbitrary"),
        name="ada",
    )(cvec, w_ada, b_ada3)


def _inproj_kernel(xl_ref, xc_ref, g_ref, sh_ref, sc_ref, wm, wn, ws, w2, wg, om, on, os_, o2, og, *, ctx_gates):
    x = _stream_block(xl_ref, xc_ref, NBLK)
    h = (_rms(x, g_ref[...]) * (1.0 + sc_ref[0]) + sh_ref[0]).astype(BF16)
    om[...] = jnp.dot(h, wm[...], preferred_element_type=F32)
    on[...] = jnp.dot(h, wn[...], preferred_element_type=F32).astype(BF16)
    os_[...] = jnp.dot(h, ws[...], preferred_element_type=F32)
    o2[...] = jnp.dot(h, w2[...], preferred_element_type=F32)
    if ctx_gates:
        og[...] = jnp.dot(h, wg[...], preferred_element_type=F32)
    else:
        is_ctx = pl.program_id(0) % NBLK == LAT_BLKS

        @pl.when(jnp.logical_not(is_ctx))
        def _():
            og[...] = jnp.dot(h, wg[...], preferred_element_type=F32)

        @pl.when(is_ctx)
        def _():
            og[...] = jnp.zeros_like(og)


def _inproj(stream, g, mod3, ws, nbatch, l, ctx_gates):
    rows = nbatch * N
    nblk = rows // ROW_BLK
    mrow = _mod_row(NBLK)
    xargs, xspecs = _stream_specs(stream, NBLK)
    full = lambda w: _lspec(w, l)
    widths = (MLA_W, NA_W, S5_W, M2_W, GATE_W)
    dts = (F32, BF16, F32, F32, F32)
    return pl.pallas_call(
        functools.partial(_inproj_kernel, ctx_gates=ctx_gates),
        grid=(nblk,),
        in_specs=xspecs + [
                  full(g),
                  pl.BlockSpec((1, 1, D), lambda i: (mrow(i, nbatch), 0, 0)),
                  pl.BlockSpec((1, 1, D), lambda i: (mrow(i, nbatch), 0, 1))]
                 + [full(w) for w in ws],
        out_specs=[pl.BlockSpec((ROW_BLK, w), lambda i: (i, 0)) for w in widths],
        out_shape=[jax.ShapeDtypeStruct((rows, w), dt) for w, dt in zip(widths, dts)],
        compiler_params=_cp("arbitrary"),
        name="inproj",
    )(*xargs, g, mod3, mod3, *ws)


def _mla_prep_kernel(p_ref, gq_ref, gkv_ref, wq_ref, wqr_ref, wk_ref, wv_ref, e_ref, vone_ref,
                     cos_ref, sin_ref, ck_ref, sk_ref, q_out, k_out, v_out):
    p = p_ref[...]
    cqn = _rms(p[:, :256], gq_ref[...]).astype(BF16)
    q = jnp.dot(cqn, wq_ref[...], preferred_element_type=F32)
    qr = jnp.dot(cqn, wqr_ref[...], preferred_element_type=F32)
    q_out[...] = (q * cos_ref[...] + qr * sin_ref[...]).astype(BF16)
    ckvn = _rms(p[:, 256:384], gkv_ref[...]).astype(BF16)
    kro = (p[:, 384:416] * ck_ref[...] + p[:, 416:448] * sk_ref[...]).astype(BF16)
    k = jnp.dot(ckvn, wk_ref[...], preferred_element_type=F32) + jnp.dot(kro, e_ref[...], preferred_element_type=F32)
    k_out[...] = k.astype(BF16)
    v_out[...] = (jnp.dot(ckvn, wv_ref[...], preferred_element_type=F32) + vone_ref[...]).astype(BF16)


def _mla_prep(p_mla, gq, gkv, wq, wqr, wk, wv, e, vone, cos_q, sin_q, cos_k, sin_k, l):
    rows = p_mla.shape[0]
    full = lambda w: _lspec(w, l)
    const = lambda w: pl.BlockSpec(w.shape, lambda j, b: (0, 0))
    rb = 3 * ROW_BLK
    row = lambda j, b: (b * (N // rb) + j, 0)
    tab = lambda w: pl.BlockSpec((rb, w), lambda j, b: (j, 0))
    return pl.pallas_call(
        _mla_prep_kernel,
        grid=(N // rb, rows // N),
        in_specs=[pl.BlockSpec((rb, MLA_W), row),
                  full(gq), full(gkv), full(wq), full(wqr), full(wk), full(wv), const(e), const(vone),
                  tab(512), tab(512), tab(32), tab(32)],
        out_specs=[pl.BlockSpec((rb, 512), row)] * 3,
        out_shape=[jax.ShapeDtypeStruct((rows, 512), BF16)] * 3,
        compiler_params=_cp("arbitrary", "arbitrary"),
        name="mla_prep",
    )(p_mla, gq, gkv, wq, wqr, wk, wv, e, vone, cos_q, sin_q, cos_k, sin_k)


def _mla_attn_kernel(q_ref, k_ref, v_ref, o_ref):
    lane = lax.broadcasted_iota(jnp.int32, (q_ref.shape[1], 128), 1)
    for hp in range(2):
        pv = []
        for h in (2 * hp, 2 * hp + 1):
            s = _dot_nt(q_ref[0, :, 128 * h:128 * h + 128], k_ref[0, :, 128 * h:128 * h + 128])
            p = jnp.exp2(s - jnp.max(s, axis=-1, keepdims=True)).astype(BF16)
            pv.append(jnp.dot(p, v_ref[0, :, 128 * h:128 * h + 128], preferred_element_type=F32))
        oa, ob = pv
        o = jnp.where(lane < 64, oa * (1.0 / oa[:, 64:65]), ob * (1.0 / ob[:, 0:1]))
        o_ref[0, :, 128 * hp:128 * hp + 128] = o.astype(o_ref.dtype)


def _mla_attn(q, k, v, has_ctx):
    nbatch = q.shape[0]
    qb = 2 * ROW_BLK

    def call(grid, qspec, kvspec, ospec, ntok):
        return pl.pallas_call(
            _mla_attn_kernel,
            grid=grid,
            in_specs=[qspec, kvspec, kvspec],
            out_specs=ospec,
            out_shape=jax.ShapeDtypeStruct((nbatch, ntok, 256), BF16),
            compiler_params=_cp("arbitrary", "arbitrary"),
            name="mla_attn",
        )(q, k, v)

    o_lat = call((nbatch, T // qb), pl.BlockSpec((1, qb, 512), lambda b, j: (b, j, 0)),
                 pl.BlockSpec((1, N, 512), lambda b, j: (b, 0, 0)),
                 pl.BlockSpec((1, qb, 256), lambda b, j: (b, j, 0)), T)
    if not has_ctx:
        return o_lat, o_lat
    ctx_blk = pl.BlockSpec((1, LC, 512), lambda b, j: (b, LAT_BLKS, 0))
    o_ctx = call((nbatch, 1), ctx_blk, ctx_blk, pl.BlockSpec((1, LC, 256), lambda b, j: (b, 0, 0)), LC)
    return o_lat, o_ctx


def _na_kernel(q_ref, kc_ref, k0_ref, k1_ref, k2_ref, vc_ref, v0_ref, v1_ref, v2_ref, b_ref, o_ref, *, has_ctx):
    scale = jnp.asarray(0.125, BF16)

    lane = lax.broadcasted_iota(jnp.int32, (ROW_BLK, 128), 1)
    lane64 = lax.broadcasted_iota(jnp.int32, (GRID_W, 128), 1) < 64
    grp = jnp.minimum(pl.program_id(0), LAT_BLKS - 1)
    krow0 = 4 * jnp.clip(grp - 1, 0, LAT_BLKS - 3)

    def bias_piece(h, i):
        rows = []
        for qr in range(4):
            r = 4 * grp + qr
            rs = jnp.clip(r - 4, 0, 24)
            tiles = []
            for pr in range(2):
                kr = krow0 + (4 * i + 2 * pr)
                t = b_ref[h, jnp.clip(kr - r + 8, 0, 15)]
                ok_a = jnp.logical_and(kr >= rs, kr < rs + 8).astype(jnp.int32)
                ok_b = jnp.logical_and(kr + 1 >= rs, kr + 1 < rs + 8).astype(jnp.int32)
                tiles.append(jnp.where(jnp.where(lane64, ok_a, ok_b) > 0, t, NEG))
            rows.append(jnp.concatenate(tiles, axis=1))
        return jnp.concatenate(rows, axis=0)

    def heads(win, sm):
        kws = (k0_ref, k1_ref, k2_ref)
        vws = (v0_ref, v1_ref, v2_ref)
        for hp in range(2):
            sl = slice(128 * hp, 128 * hp + 128)
            qp = q_ref[sm, :, sl] * scale
            outs = []
            for hh in range(2):
                h = 2 * hp + hh
                qh = jnp.where((lane < 64) if hh == 0 else (lane >= 64), qp, jnp.zeros_like(qp))
                s_c = _dot_nt(qh, kc_ref[sm, :, sl])
                s_w = []
                smax = s_c
                if win:
                    for i in range(3):
                        s = _dot_nt(qh, kws[i][sm, :, sl]) + bias[h][i]
                        s_w.append(s)
                        smax = jnp.maximum(smax, s)
                m = jnp.max(smax, axis=-1, keepdims=True)
                p = jnp.exp(s_c - m)
                psum = p
                o = _dot(p, vc_ref[sm, :, sl])
                for i, s in enumerate(s_w):
                    p = jnp.exp(s - m)
                    psum = psum + p
                    o = o + _dot(p, vws[i][sm, :, sl])
                outs.append(o * (1.0 / jnp.sum(psum, axis=-1, keepdims=True)))
            o_ref[sm, :, sl] = jnp.where(lane < 64, outs[0], outs[1]).astype(o_ref.dtype)

    def windowed():
        nonlocal bias
        bias = [[bias_piece(h, i) for i in range(3)] for h in range(4)]
        for sm in range(NA_SB):
            heads(True, sm)

    bias = None
    if has_ctx:
        g = pl.program_id(0)

        @pl.when(g < LAT_BLKS)
        def _():
            windowed()

        @pl.when(g == LAT_BLKS)
        def _():
            for sm in range(NA_SB):
                heads(False, sm)
    else:
        windowed()


def _na_attn(qkv, bias, has_ctx, l):
    nbatch = qkv.shape[0]
    ng = NBLK if has_ctx else LAT_BLKS
    j0 = lambda g: jnp.clip(g - 1, 0, LAT_BLKS - 3)
    blk = lambda f: pl.BlockSpec((NA_SB, ROW_BLK, 256), f)
    return pl.pallas_call(
        functools.partial(_na_kernel, has_ctx=has_ctx),
        grid=(ng, nbatch // NA_SB),
        in_specs=[blk(lambda g, b: (b, g, 0)),
                  blk(lambda g, b: (b, LAT_BLKS, 1)),
                  blk(lambda g, b: (b, j0(g), 1)),
                  blk(lambda g, b: (b, j0(g) + 1, 1)),
                  blk(lambda g, b: (b, j0(g) + 2, 1)),
                  blk(lambda g, b: (b, LAT_BLKS, 2)),
                  blk(lambda g, b: (b, j0(g), 2)),
                  blk(lambda g, b: (b, j0(g) + 1, 2)),
                  blk(lambda g, b: (b, j0(g) + 2, 2)),
                  _lspec(bias, l)],
        out_specs=blk(lambda g, b: (b, g, 0)),
        out_shape=jax.ShapeDtypeStruct((nbatch, ng * ROW_BLK, 256), BF16),
        compiler_params=_cp("arbitrary", "arbitrary"),
        name="na_attn",
    )(qkv, qkv, qkv, qkv, qkv, qkv, qkv, qkv, qkv, bias)


def _na_col_structure():
    onehot = np.zeros((31, 64, 64), np.float32)
    colmask = np.zeros((64, 64), bool)
    for c in range(64):
        cs = min(max(c - 8, 0), 48)
        for kc in range(cs, cs + 16):
            onehot[kc - c + 15, c, kc] = 1.0
            colmask[c, kc] = True
    return np.tile(onehot.reshape(31, 4096), (3, 1)), colmask


_NA_COL_ONEHOT, _NA_COLMASK = _na_col_structure()


def _na_tables(rpb):
    nl = rpb.shape[0]
    r = rpb.astype(F32)
    hi = lax.reduce_precision(r, 8, 7)
    mid = lax.reduce_precision(r - hi, 8, 7)
    parts = jnp.stack([hi, mid, r - hi - mid], axis=3).reshape(nl * 60, 93)
    tz = jnp.dot(parts, _NA_COL_ONEHOT, preferred_element_type=F32).reshape(nl, 4, 15, 64, 64)
    tz = jnp.where(_NA_COLMASK, tz, NEG)
    edge = jnp.full((nl, 4, 1, 64, 64), NEG, F32)
    tz17 = jnp.concatenate([edge, tz, edge], axis=2)
    return jnp.concatenate([tz17[:, :, :16], tz17[:, :, 1:]], axis=-1)


def _s5_kernel(uf_ref, ub_ref, bf_ref, bb_ref, af_ref, ab_ref, cf_ref, cb_ref, yf_ref, yb_ref,
               hf, hb, buf_f, buf_b, tm_f, tm_b):
    i = pl.program_id(0)
    half = 1024
    nb = uf_ref.shape[0]

    @pl.when(i == 0)
    def _():
        hf[...] = jnp.zeros_like(hf)
        hb[...] = jnp.zeros_like(hb)

    def expand(u_ref, tm, b_ref, buf):
        for b in range(nb):
            for c in range(2):
                tm[c, pl.ds(b, S5_CHUNK, stride=nb), :] = u_ref[b, :, 128 * c:128 * c + 128]
        buf[...] = _dot(jnp.concatenate([tm[0], tm[1]], axis=1), b_ref[...])

    def scan(buf, a_ref, h, order):
        hr, hi = h[:, :half], h[:, half:]
        ar, ai = a_ref[:, :half], a_ref[:, half:]
        for k in order:
            rows = slice(k * nb, (k + 1) * nb)
            nr = ar * hr - ai * hi + buf[rows, :half]
            ni = ar * hi + ai * hr + buf[rows, half:]
            buf[rows, :half] = nr
            buf[rows, half:] = ni
            hr, hi = nr, ni
        h[:, :half] = hr
        h[:, half:] = hi

    def readout(buf, c_ref, tm, y_ref):
        y = _dot(buf[...], c_ref[...])
        for c in range(2):
            tm[c] = y[:, 128 * c:128 * c + 128]
        for b in range(nb):
            for c in range(2):
                y_ref[b, :, 128 * c:128 * c + 128] = tm[c, pl.ds(b, S5_CHUNK, stride=nb), :]

    expand(uf_ref, tm_f, bf_ref, buf_f)
    expand(ub_ref, tm_b, bb_ref, buf_b)
    scan(buf_f, af_ref, hf, range(S5_CHUNK))
    readout(buf_f, cf_ref, tm_f, yf_ref)
    scan(buf_b, ab_ref, hb, range(S5_CHUNK - 1, -1, -1))
    readout(buf_b, cb_ref, tm_b, yb_ref)


def _s5_scan(u, bmat, avec, cmat, l):
    nbatch = u.shape[0]
    cr = S5_CHUNK * nbatch
    nch = N // S5_CHUNK
    nctx = LC // S5_CHUNK
    fidx = lambda i: jnp.where(i < nctx, nch - nctx + i, i - nctx)
    bidx = lambda i: nch - 1 - i
    ublk = lambda f: pl.BlockSpec((nbatch, S5_CHUNK, 256), lambda i: (0, f(i), 0))
    return pl.pallas_call(
        _s5_kernel,
        grid=(nch,),
        in_specs=[ublk(fidx), ublk(bidx),
                  _lspec(bmat, l, 0), _lspec(bmat, l, 1), _lspec(avec, l, 0), _lspec(avec, l, 1),
                  _lspec(cmat, l, 0), _lspec(cmat, l, 1)],
        out_specs=[ublk(fidx), ublk(bidx)],
        out_shape=[jax.ShapeDtypeStruct((nbatch, N, 256), F32)] * 2,
        scratch_shapes=[pltpu.VMEM((nbatch, 2048), F32), pltpu.VMEM((nbatch, 2048), F32),
                        pltpu.VMEM((cr, 2048), F32), pltpu.VMEM((cr, 2048), F32),
                        pltpu.VMEM((2, cr, 128), F32), pltpu.VMEM((2, cr, 128), F32)],
        compiler_params=_cp("arbitrary"),
        name="s5_scan",
    )(u, u, bmat, bmat, avec, avec, cmat, cmat)


def _s5_glu_kernel(u_ref, yf_ref, yb_ref, d_ref, w_ref, b_ref, o_ref):
    y = d_ref[...] * u_ref[...] + yf_ref[...] + yb_ref[...]
    z = y * (0.5 * (1.0 + jnp.tanh(math.sqrt(2.0 / math.pi) * (y + 0.044715 * (y * y * y)))))
    o_ref[...] = (z * _sigmoid(_dot(z, w_ref[...]) + b_ref[...])).astype(o_ref.dtype)


def _s5_glu(u_tb, yf, yb, d, w, b, l):
    rows = u_tb.shape[0]
    rb = 4 * ROW_BLK
    blk = pl.BlockSpec((rb, 256), lambda i: (i, 0))
    full = lambda a: _lspec(a, l)
    return pl.pallas_call(
        _s5_glu_kernel,
        grid=(rows // rb,),
        in_specs=[blk, blk, blk, full(d), full(w), full(b)],
        out_specs=blk,
        out_shape=jax.ShapeDtypeStruct((rows, 256), BF16),
        compiler_params=_cp("arbitrary"),
        name="s5_glu",
    )(u_tb, yf, yb, d, w, b)


def _s5_params(a_re, a_im, log_step, b_re, b_im, c_re, c_im, nbatch):
    nl = a_re.shape[0]
    same_group = (np.arange(256)[:, None] // 16 == np.arange(1024)[None, :] // 64).astype(np.float32)
    a = lax.complex(a_re.astype(F32), a_im.astype(F32))
    abar = jnp.exp(jnp.exp(log_step.astype(F32))[..., None] * a)
    bbar = ((abar - 1.0) / a)[..., None] * lax.complex(b_re.astype(F32), b_im.astype(F32))
    blk_in = lambda m: jnp.tile(jnp.swapaxes(m, -1, -2).reshape(nl, 2, 256, 64), (1, 1, 1, 16)) * same_group
    bmat = jnp.concatenate([blk_in(jnp.real(bbar)), blk_in(jnp.imag(bbar))], axis=-1).astype(BF16)
    avec = jnp.concatenate([jnp.real(abar).reshape(nl, 2, 1, 1024), jnp.imag(abar).reshape(nl, 2, 1, 1024)], axis=-1)
    avec = jnp.broadcast_to(avec, (nl, 2, nbatch, 2048))
    blk_out = lambda m: jnp.tile(jnp.swapaxes(m, -1, -2).reshape(nl, 2, 1024, 16), (1, 1, 1, 16)) * same_group.T
    cmat = jnp.concatenate([blk_out(c_re.astype(F32)), -blk_out(c_im.astype(F32))], axis=-2).astype(BF16)
    return bmat, avec, cmat


def _m2_prep_kernel(x_ref, dt_ref, w_ref, b_ref, dtb_ref, xo_ref, dtt_ref, bmt_ref):
    c = pl.program_id(1)
    x = x_ref[0]
    t = lax.broadcasted_iota(jnp.int32, x.shape, 0)
    m2 = ((t >= 2) & (t < T)) | (t >= T + 2)
    m1 = ((t >= 1) & (t < T)) | (t >= T + 1)
    p1 = (t <= T - 2) | ((t >= T) & (t <= N - 2))
    w = w_ref[...]
    y = (w[0:1] * jnp.where(m2, pltpu.roll(x, 2, 0), 0.0)
         + w[1:2] * jnp.where(m1, pltpu.roll(x, 1, 0), 0.0)
         + w[2:3] * x
         + w[3:4] * jnp.where(p1, pltpu.roll(x, N - 1, 0), 0.0)) + b_ref[...]
    act = _silu(y)
    xo_ref[0] = act

    @pl.when(c == 0)
    def _():
        v = dt_ref[0] + dtb_ref[...]
        dtt_ref[0] = (jnp.maximum(v, 0.0) + jnp.log1p(jnp.exp(-jnp.abs(v)))).T

    @pl.when(c == 1)
    def _():
        bmt_ref[0] = act.T.astype(BF16)


def _m2_prep(p_m2, conv_w, conv_b, dtb, l):
    nbatch = p_m2.shape[0]
    return pl.pallas_call(
        _m2_prep_kernel,
        grid=(nbatch, 3),
        in_specs=[pl.BlockSpec((1, N, 256), lambda b, c: (b, 0, 1 + c)),
                  pl.BlockSpec((1, N, 128), lambda b, c: (b, 0, 8)),
                  pl.BlockSpec((None, 4, 256), lambda b, c: (l, 0, c)),
                  pl.BlockSpec((None, 1, 256), lambda b, c: (l, 0, c)),
                  _lspec(dtb, l)],
        out_specs=[pl.BlockSpec((1, N, 256), lambda b, c: (b, 0, c)),
                   pl.BlockSpec((1, 128, N), lambda b, c: (b, 0, 0)),
                   pl.BlockSpec((1, 256, N), lambda b, c: (b, 0, 0))],
        out_shape=[jax.ShapeDtypeStruct((nbatch, N, 768), F32),
                   jax.ShapeDtypeStruct((nbatch, 128, N), F32),
                   jax.ShapeDtypeStruct((nbatch, 256, N), BF16)],
        compiler_params=_cp("arbitrary", "arbitrary"),
        name="m2_prep",
    )(p_m2, p_m2, conv_w, conv_b, dtb)


def _ssd_kernel(xf_ref, dttf_ref, bmtf_ref, xb_ref, dttb_ref, bmtb_ref, acol_ref, yf_ref, yb_ref, hs):
    i = pl.program_id(1)
    L = SSD_CHUNK

    @pl.when(i == 0)
    def _():
        hs[...] = jnp.zeros_like(hs)

    li = lax.broadcasted_iota(jnp.int32, (L, L), 0)
    si = lax.broadcasted_iota(jnp.int32, (L, L), 1)
    left = si < 64
    pick = lambda a, b: jnp.where(left, a, b)

    def direction(sm, x_ref, dtt_ref, bmt_ref, y_ref, d, causal, causal_t, last):
        cz = jnp.where(causal, 1.0, 0.0).astype(BF16)
        czt = jnp.where(causal_t, 1.0, 0.0).astype(BF16)
        xbc = x_ref[sm]
        dtt = dtt_ref[sm]
        parts = jnp.concatenate([p.astype(F32) for p in _split3(dtt * acol_ref[...])], axis=1)
        cumt = jnp.dot(parts.astype(BF16), jnp.concatenate([czt, czt, czt], axis=0),
                       preferred_element_type=F32)
        reps = jnp.concatenate([jnp.broadcast_to(parts[4 * d + h:4 * d + h + 1, :], (L, 3 * L)) for h in range(4)],
                               axis=0).astype(BF16)
        cumb = lax.dot_general(jnp.concatenate([cz, cz, cz], axis=1), reps, (((1,), (1,)), ((), ())),
                               preferred_element_type=F32)
        for g in range(2):
            xp = xbc[:, 128 * g:128 * g + 128].astype(BF16)
            cm = xbc[:, 512 + 128 * g:640 + 128 * g].astype(BF16)
            bmt = bmt_ref[sm, 128 * g:128 * g + 128, :]
            hprev = hs[sm, d, g]
            go = jnp.dot(cm, jnp.concatenate([bmt, hprev.astype(BF16)], axis=1), preferred_element_type=F32)
            gmat, yo = go[:, :L], go[:, L:]
            bmf = bmt.astype(F32)
            lhs, cbs, cls = [], [], []
            for hh in range(2):
                h = 2 * g + hh
                cb = cumb[:, L * h:L * h + L]
                crow = cumt[4 * d + h:4 * d + h + 1, :]
                dtrow = dtt[4 * d + h:4 * d + h + 1, :]
                cl = cb[last:last + 1, :]
                lhs.append((gmat * jnp.where(causal, jnp.exp(cb - crow), 0.0) * dtrow).astype(BF16))
                lhs.append((bmf * (jnp.exp(cl - crow) * dtrow)).astype(BF16))
                cbs.append(cb)
                cls.append(cl)
            big = jnp.dot(jnp.concatenate(lhs, axis=0), xp, preferred_element_type=F32)
            y_ref[sm, :, 128 * g:128 * g + 128] = pick(big[0:L], big[2 * L:3 * L]) + yo * jnp.exp(pick(cbs[0], cbs[1]))
            hs[sm, d, g] = hprev * jnp.exp(pick(cls[0], cls[1])) + pick(big[L:2 * L], big[3 * L:4 * L])

    for sm in range(SSD_SB):
        direction(sm, xf_ref, dttf_ref, bmtf_ref, yf_ref, 0, si <= li, li <= si, L - 1)
        direction(sm, xb_ref, dttb_ref, bmtb_ref, yb_ref, 1, si >= li, li >= si, 0)


def _ssd(xbc, dtt, bmt, a_col, l):
    nbatch = xbc.shape[0]
    nch = N // SSD_CHUNK
    nctx = LC // SSD_CHUNK
    fidx = lambda i: jnp.where(i < nctx, nch - nctx + i, i - nctx)
    bidx = lambda i: nch - 1 - i
    ins = lambda f: [pl.BlockSpec((SSD_SB, SSD_CHUNK, 768), lambda b, i: (b, f(i), 0)),
                     pl.BlockSpec((SSD_SB, 128, SSD_CHUNK), lambda b, i: (b, 0, f(i))),
                     pl.BlockSpec((SSD_SB, 256, SSD_CHUNK), lambda b, i: (b, 0, f(i)))]
    return pl.pallas_call(
        _ssd_kernel,
        grid=(nbatch // SSD_SB, nch),
        in_specs=ins(fidx) + ins(bidx) + [_lspec(a_col, l)],
        out_specs=[pl.BlockSpec((SSD_SB, SSD_CHUNK, 256), lambda b, i: (b, fidx(i), 0)),
                   pl.BlockSpec((SSD_SB, SSD_CHUNK, 256), lambda b, i: (b, bidx(i), 0))],
        out_shape=[jax.ShapeDtypeStruct((nbatch, N, 256), F32)] * 2,
        scratch_shapes=[pltpu.VMEM((SSD_SB, 2, 2, 128, 128), F32)],
        compiler_params=_cp("arbitrary", "arbitrary"),
        name="ssd",
    )(xbc, dtt, bmt, xbc, dtt, bmt, a_col)


def _m2_norm_kernel(x_ref, z_ref, yf_ref, yb_ref, d_ref, g_ref, o_ref):
    y = d_ref[...] * x_ref[...] + yf_ref[...] + yb_ref[...]
    o_ref[...] = _rms(y * _silu(z_ref[...]), g_ref[...]).astype(o_ref.dtype)


def _m2_norm(xbc2d, p_m2_2d, yf2d, yb2d, dvec, gn, l):
    rows = xbc2d.shape[0]
    rb = 4 * ROW_BLK
    blk = pl.BlockSpec((rb, 256), lambda i: (i, 0))
    full = lambda a: _lspec(a, l)
    return pl.pallas_call(
        _m2_norm_kernel,
        grid=(rows // rb,),
        in_specs=[blk, blk, blk, blk, full(dvec), full(gn)],
        out_specs=blk,
        out_shape=jax.ShapeDtypeStruct((rows, 256), BF16),
        compiler_params=_cp("arbitrary"),
        name="m2_norm",
    )(xbc2d, p_m2_2d, yf2d, yb2d, dvec, gn)


def _merge_kernel(xl_ref, xc_ref, oal_ref, oac_ref, on_ref, os_ref, om_ref, gl_ref, wb_ref, wo_ref, gpm_ref, gate1_ref,
                  gpf_ref, sh2_ref, sc2_ref, wr_ref, x1_ref, h2_ref, aff_ref, *, nb):
    outs = (_stream_block(oal_ref, oac_ref, nb), on_ref[...], os_ref[...], om_ref[...])
    y = None
    for j in range(4):
        d = jnp.dot(outs[j], wb_ref[j], preferred_element_type=F32)
        t = d * jnp.tanh(gl_ref[:, 1024 * j:1024 * j + 1024]) + d
        y = t if y is None else y + t
    y2 = _dot(y, wo_ref[...])
    x1 = _stream_block(xl_ref, xc_ref, nb) + gate1_ref[0] * _rms(y2, gpm_ref[...])
    x1_ref[...] = x1
    h2 = _rms(x1, gpf_ref[...]) * (1.0 + sc2_ref[0]) + sh2_ref[0]
    h2_ref[...] = h2.astype(BF16)
    logits = _dot_hi_nt(wr_ref[...], h2)
    m = jnp.max(logits, axis=0, keepdims=True)
    e = jnp.exp(logits - m)
    aff_ref[0, 0] = e / jnp.sum(e, axis=0, keepdims=True)


def _merge(stream, o_mla, o_na, o_s5, o_m2, gl, wb, wo, gpm, mod3, gpf, wrt, nbatch, nb, l):
    mrow = _mod_row(nb)
    xargs, xspecs = _stream_specs(stream, nb)
    aargs, aspecs = _stream_specs((o_mla[0], o_mla[1], 0), nb)
    fb = lambda i: (_flat_blk(i, nb), 0)
    blk = lambda w: pl.BlockSpec((ROW_BLK, w), fb)
    oblk = lambda w: pl.BlockSpec((ROW_BLK, w), lambda i: (i, 0))
    full = lambda a: _lspec(a, l)
    modspec = lambda k: pl.BlockSpec((1, 1, D), lambda i: (mrow(i, nbatch), 0, k))
    orows = nbatch * nb * ROW_BLK
    return pl.pallas_call(
        functools.partial(_merge_kernel, nb=nb),
        grid=(nbatch * nb,),
        in_specs=xspecs + aspecs + [oblk(256), blk(256), blk(256), blk(GATE_W),
                  full(wb), full(wo), full(gpm), modspec(2), full(gpf), modspec(3), modspec(4), full(wrt)],
        out_specs=[oblk(D), oblk(D),
                   pl.BlockSpec((1, 1, N_EXPERTS, ROW_BLK), lambda i: (i // nb, i % nb, 0, 0))],
        out_shape=[jax.ShapeDtypeStruct((orows, D), F32),
                   jax.ShapeDtypeStruct((orows, D), BF16),
                   jax.ShapeDtypeStruct((nbatch, nb, N_EXPERTS, ROW_BLK), F32)],
        compiler_params=_cp("arbitrary"),
        name="merge",
    )(*xargs, *aargs, o_na, o_s5, o_m2, gl, wb, wo, gpm, mod3, gpf, mod3, mod3, wrt)


def _topk_kernel(aff_ref, slot_ref, *, nk, cap):
    a = aff_ref[0]
    bits = lax.bitcast_convert_type(a, jnp.int32)
    count = lambda m: jnp.sum(jnp.sum(jnp.where(m, 1.0, 0.0), axis=2, keepdims=True), axis=0, keepdims=True)
    thr = jnp.where(count(bits >= (1 << 30)) >= cap, jnp.int32(1 << 30), jnp.zeros((1, N_EXPERTS, 1), jnp.int32))
    for bit in range(29, 0, -2):
        hi, lo = 1 << bit, 1 << (bit - 1)
        ok = [count(bits >= (thr | c)) >= cap for c in (hi | lo, hi, lo)]
        thr = thr | jnp.where(ok[0], hi | lo, jnp.where(ok[1], hi, jnp.where(ok[2], lo, 0)))
    gt = bits > thr
    eq = bits == thr
    need = cap - count(gt)

    tri = jnp.where(lax.broadcasted_iota(jnp.int32, (256, 256), 0) <= lax.broadcasted_iota(jnp.int32, (256, 256), 1),
                    1.0, 0.0).astype(BF16)

    def prefix_excl(m):
        incl = jnp.dot(m.reshape(nk * N_EXPERTS, 256).astype(BF16), tri,
                       preferred_element_type=F32).reshape(nk, N_EXPERTS, 256)
        offs = []
        run = jnp.zeros((1, N_EXPERTS, 1), F32)
        for k in range(nk):
            offs.append(run)
            run = run + incl[k:k + 1, :, 255:256]
        off = offs[0] if nk == 1 else jnp.concatenate(offs, axis=0)
        return incl - m + off

    eqf = jnp.where(eq, 1.0, 0.0)
    sel = jnp.where(gt, 1.0, jnp.where(eq & (prefix_excl(eqf) < need), 1.0, 0.0))
    slot = jnp.where(sel > 0.5, prefix_excl(sel), -1.0)
    slot_ref[0] = slot.astype(jnp.int32)


def _topk(aff, blk0, nk, cap):
    nbatch = aff.shape[0]
    return pl.pallas_call(
        functools.partial(_topk_kernel, nk=nk, cap=cap),
        grid=(nbatch,),
        in_specs=[pl.BlockSpec((1, nk, N_EXPERTS, ROW_BLK), lambda b: (b, blk0, 0, 0))],
        out_specs=pl.BlockSpec((1, nk, N_EXPERTS, ROW_BLK), lambda b: (b, 0, 0, 0)),
        out_shape=jax.ShapeDtypeStruct((nbatch, nk, N_EXPERTS, ROW_BLK), jnp.int32),
        compiler_params=_cp("arbitrary"),
        name="topk",
    )(aff)


def _gather_kernel(w0_ref, fits_ref, h_ref, sl_ref, xs_ref, acc):
    b = pl.program_id(0)
    k = pl.program_id(1)

    @pl.when(k == 0)
    def _():
        acc[...] = jnp.zeros_like(acc)

    h = h_ref[0]

    @pl.when(fits_ref[b] == 1)
    def _():
        r_iota = lax.broadcasted_iota(jnp.int32, (MOE_WIN, ROW_BLK), 0)
        w0s = [pl.multiple_of(w0_ref[(b * LAT_BLKS + k) * N_EXPERTS + e], 16) for e in range(N_EXPERTS)]
        onehot = jnp.concatenate([jnp.where((sl_ref[0, 0, e:e + 1, :] - w0s[e]) == r_iota, 1.0, 0.0).astype(BF16)
                                  for e in range(N_EXPERTS)], axis=0)
        rows = jnp.dot(onehot, h, preferred_element_type=F32)
        for e in range(N_EXPERTS):
            acc[e, pl.ds(w0s[e], MOE_WIN), :] += rows[MOE_WIN * e:MOE_WIN * e + MOE_WIN]

    @pl.when(fits_ref[b] != 1)
    def _():
        r_iota = lax.broadcasted_iota(jnp.int32, (CAP_LAT, ROW_BLK), 0)
        for e in range(N_EXPERTS):
            onehot = jnp.where(sl_ref[0, 0, e:e + 1, :] == r_iota, 1.0, 0.0).astype(BF16)
            acc[e] += jnp.dot(onehot, h, preferred_element_type=F32)

    @pl.when(k == pl.num_programs(1) - 1)
    def _():
        xs_ref[0] = acc[...].astype(xs_ref.dtype)


def _gather(win0, fits, h2, slot_lat):
    nbatch = h2.shape[0]
    smem = pl.BlockSpec(memory_space=pltpu.SMEM)
    return pl.pallas_call(
        _gather_kernel,
        grid=(nbatch, LAT_BLKS),
        in_specs=[smem, smem,
                  pl.BlockSpec((1, ROW_BLK, D), lambda b, k: (b, k, 0)),
                  pl.BlockSpec((1, 1, N_EXPERTS, ROW_BLK), lambda b, k: (b, k, 0, 0))],
        out_specs=pl.BlockSpec((1, N_EXPERTS, CAP_LAT, D), lambda b, k: (b, 0, 0, 0)),
        out_shape=jax.ShapeDtypeStruct((nbatch, N_EXPERTS, CAP_LAT, D), BF16),
        scratch_shapes=[pltpu.VMEM((N_EXPERTS, CAP_LAT, D), F32)],
        compiler_params=_cp("arbitrary", "arbitrary"),
        name="gather",
    )(win0, fits, h2, slot_lat)


def _expert_kernel(*refs, has_ctx):
    if has_ctx:
        (w0_ref, fits_ref, xl_ref, sl_ref, al_ref, hc_ref, sc_ref, ac_ref, wg_ref, wu_ref, wd_ref,
         yl_ref, yc_ref, wgb, wub, wdb, gs_acc, xc_all, gc_all) = refs
    else:
        (w0_ref, fits_ref, xl_ref, sl_ref, al_ref, wg_ref, wu_ref, wd_ref, yl_ref,
         wgb, wub, wdb, gs_acc) = refs
    e = pl.program_id(0)
    bp = pl.program_id(1)

    @pl.when(bp == 0)
    def _():
        wgb[...] = wg_ref[0, 0].astype(BF16)
        wub[...] = wu_ref[0, 0].astype(BF16)
        wdb[...] = wd_ref[0, 0].astype(BF16)

    def gather(h_ref, slot_ref, aff_ref, s, nk, cap):
        r_iota = lax.broadcasted_iota(jnp.int32, (cap, ROW_BLK), 0)
        xs = jnp.zeros((cap, D), F32)
        gs = jnp.zeros((cap, 1), F32)
        for k in range(nk):
            pm = slot_ref[s, k, pl.ds(e, 1), :] == r_iota
            xs = xs + jnp.dot(jnp.where(pm, 1.0, 0.0).astype(BF16), h_ref[s, ROW_BLK * k:ROW_BLK * k + ROW_BLK, :],
                              preferred_element_type=F32)
            gs = gs + jnp.sum(jnp.where(pm, aff_ref[s, k, pl.ds(e, 1), :], 0.0), axis=1, keepdims=True)
        return xs.astype(BF16), gs

    def ffn(xsb):
        gt = jnp.dot(xsb, wgb[...], preferred_element_type=F32)
        up = jnp.dot(xsb, wub[...], preferred_element_type=F32)
        return jnp.dot((_silu(gt) * up).astype(BF16), wdb[...], preferred_element_type=F32)

    for s in range(EXP_SB):
        b = bp * EXP_SB + s
        base = CAP_LAT * s

        @pl.when(fits_ref[b] == 1)
        def _():
            gs_acc[base:base + CAP_LAT, :] = jnp.zeros((CAP_LAT, 1), F32)
            r_iota = lax.broadcasted_iota(jnp.int32, (MOE_WIN, ROW_BLK), 0)
            for k in range(LAT_BLKS):
                w0 = pl.multiple_of(w0_ref[(b * LAT_BLKS + k) * N_EXPERTS + e], 16)
                pm = (sl_ref[s, k, pl.ds(e, 1), :] - w0) == r_iota
                gs_acc[pl.ds(base + w0, MOE_WIN), :] += jnp.sum(jnp.where(pm, al_ref[s, k, pl.ds(e, 1), :], 0.0),
                                                                axis=1, keepdims=True)

        @pl.when(fits_ref[b] != 1)
        def _():
            r_iota = lax.broadcasted_iota(jnp.int32, (CAP_LAT, ROW_BLK), 0)
            gs = jnp.zeros((CAP_LAT, 1), F32)
            for k in range(LAT_BLKS):
                pm = sl_ref[s, k, pl.ds(e, 1), :] == r_iota
                gs = gs + jnp.sum(jnp.where(pm, al_ref[s, k, pl.ds(e, 1), :], 0.0), axis=1, keepdims=True)
            gs_acc[base:base + CAP_LAT, :] = gs

    xs = xl_ref[...].reshape(EXP_SB * CAP_LAT, D)
    y = ffn(xs) * gs_acc[...]
    for s in range(EXP_SB):
        yl_ref[s, 0] = y[CAP_LAT * s:CAP_LAT * s + CAP_LAT].astype(yl_ref.dtype)
    if has_ctx:
        for s in range(EXP_SB):
            xs, gs = gather(hc_ref, sc_ref, ac_ref, s, 1, CAP_CTX)
            row = pl.multiple_of((bp * EXP_SB + s) * CAP_CTX, CAP_CTX)
            xc_all[pl.ds(row, CAP_CTX), :] = xs
            gc_all[pl.ds(row, CAP_CTX), :] = gs

        @pl.when(bp == pl.num_programs(1) - 1)
        def _():
            yc_ref[0] = (ffn(xc_all[...]) * gc_all[...]).astype(yc_ref.dtype)


def _experts(win0, fits, xs_lat, h2, slot_lat, slot_ctx, aff, w_gate, w_up, w_down, l, has_ctx):
    nbatch = h2.shape[0]
    assert nbatch % EXP_SB == 0
    idx4 = lambda e, b: (b, 0, 0, 0)
    smem = pl.BlockSpec(memory_space=pltpu.SMEM)
    in_specs = [smem, smem,
                pl.BlockSpec((EXP_SB, 1, CAP_LAT, D), lambda e, b: (b, e, 0, 0)),
                pl.BlockSpec((EXP_SB, LAT_BLKS, N_EXPERTS, ROW_BLK), idx4),
                pl.BlockSpec((EXP_SB, LAT_BLKS, N_EXPERTS, ROW_BLK), idx4)]
    args = [win0, fits, xs_lat, slot_lat, aff]
    out_specs = [pl.BlockSpec((EXP_SB, 1, CAP_LAT, D), lambda e, b: (b, e, 0, 0))]
    out_shape = [jax.ShapeDtypeStruct((nbatch, N_EXPERTS, CAP_LAT, D), BF16)]
    if has_ctx:
        in_specs += [pl.BlockSpec((EXP_SB, LC, D), lambda e, b: (b, LAT_BLKS, 0)),
                     pl.BlockSpec((EXP_SB, 1, N_EXPERTS, ROW_BLK), idx4),
                     pl.BlockSpec((EXP_SB, 1, N_EXPERTS, ROW_BLK), lambda e, b: (b, LAT_BLKS, 0, 0))]
        args += [h2, slot_ctx, aff]
        out_specs.append(pl.BlockSpec((1, nbatch * CAP_CTX, D), lambda e, b: (e, 0, 0)))
        out_shape.append(jax.ShapeDtypeStruct((N_EXPERTS, nbatch * CAP_CTX, D), BF16))
    wspec = pl.BlockSpec((1, 1, D, D), lambda e, b: (l, e, 0, 0))
    in_specs += [wspec, wspec, wspec]
    args += [w_gate, w_up, w_down]
    scratch = [pltpu.VMEM((D, D), BF16)] * 3 + [pltpu.VMEM((EXP_SB * CAP_LAT, 1), F32)]
    if has_ctx:
        scratch += [pltpu.VMEM((nbatch * CAP_CTX, D), BF16), pltpu.VMEM((nbatch * CAP_CTX, 1), F32)]
    return pl.pallas_call(
        functools.partial(_expert_kernel, has_ctx=has_ctx),
        grid=(N_EXPERTS, nbatch // EXP_SB),
        in_specs=in_specs,
        out_specs=out_specs,
        out_shape=out_shape,
        scratch_shapes=scratch,
        compiler_params=_cp("arbitrary", "arbitrary"),
        name="experts",
    )(*args)


def _combine_kernel(*refs, has_ctx):
    if has_ctx:
        w0_ref, fits_ref, x1_ref, yl_ref, scl_ref, yc_ref, scc_ref, gate2_ref, g_ref, o_ref, ycat = refs
    else:
        w0_ref, fits_ref, x1_ref, yl_ref, scl_ref, gate2_ref, g_ref, o_ref, ycat = refs
    b = pl.program_id(0)
    j = pl.program_id(1)

    def finish(acc):
        o_ref[0] = x1_ref[0] + gate2_ref[0] * _rms(acc, g_ref[...])

    def comb(sc_ref, y_of, cap):
        lane = lax.broadcasted_iota(jnp.int32, (ROW_BLK, cap), 1)
        sc = sc_ref[0]
        acc = jnp.zeros((ROW_BLK, D), F32)
        for e in range(N_EXPERTS):
            pt = jnp.where(sc[:, e:e + 1] == lane, 1.0, 0.0).astype(BF16)
            acc = acc + jnp.dot(pt, y_of(e), preferred_element_type=F32)
        finish(acc)

    def comb_windowed():
        lane = lax.broadcasted_iota(jnp.int32, (ROW_BLK, ROW_BLK), 1)
        sc = scl_ref[0]
        per = ROW_BLK // MOE_WIN
        if per * MOE_WIN < ROW_BLK:
            ycat[:, per * MOE_WIN:, :] = jnp.zeros((2, ROW_BLK - per * MOE_WIN, D), BF16)
        acc = jnp.zeros((ROW_BLK, D), F32)
        for gi, e0 in enumerate(range(0, N_EXPERTS, per)):
            hit = None
            for q, e in enumerate(range(e0, min(e0 + per, N_EXPERTS))):
                w0 = pl.multiple_of(w0_ref[(b * LAT_BLKS + j) * N_EXPERTS + e], 16)
                d = sc[:, e:e + 1] - w0
                d = jnp.where(jnp.logical_and(d >= 0, d < MOE_WIN), d + MOE_WIN * q, -1)
                hit = (d == lane) if hit is None else jnp.logical_or(hit, d == lane)
                ycat[gi % 2, MOE_WIN * q:MOE_WIN * q + MOE_WIN, :] = yl_ref[0, e, pl.ds(w0, MOE_WIN), :]
            acc = acc + jnp.dot(jnp.where(hit, 1.0, 0.0).astype(BF16), ycat[gi % 2], preferred_element_type=F32)
        finish(acc)

    lat_y = lambda e: yl_ref[0, e]
    fits = fits_ref[b] == 1

    @pl.when(jnp.logical_and(j < LAT_BLKS, fits))
    def _():
        comb_windowed()

    @pl.when(jnp.logical_and(j < LAT_BLKS, jnp.logical_not(fits)))
    def _():
        comb(scl_ref, lat_y, CAP_LAT)

    if has_ctx:
        @pl.when(j == LAT_BLKS)
        def _():
            comb(scc_ref, lambda e: yc_ref[e], CAP_CTX)


def _combine(win0, fits, x1, y_lat, scol_lat, y_ctx, scol_ctx, mod3, g, has_ctx, l):
    nbatch = x1.shape[0]
    nb = NBLK if has_ctx else LAT_BLKS
    ntok = N if has_ctx else T
    smem = pl.BlockSpec(memory_space=pltpu.SMEM)
    in_specs = [smem, smem,
                pl.BlockSpec((1, ROW_BLK, D), lambda b, j: (b, j, 0)),
                pl.BlockSpec((1, N_EXPERTS, CAP_LAT, D), lambda b, j: (b, 0, 0, 0)),
                pl.BlockSpec((1, ROW_BLK, N_EXPERTS), lambda b, j: (b, jnp.minimum(j, LAT_BLKS - 1), 0))]
    args = [win0, fits, x1, y_lat, scol_lat]
    if has_ctx:
        in_specs += [pl.BlockSpec((N_EXPERTS, CAP_CTX, D), lambda b, j: (0, b, 0)),
                     pl.BlockSpec((1, ROW_BLK, N_EXPERTS), lambda b, j: (b, 0, 0))]
        args += [y_ctx, scol_ctx]
    in_specs += [pl.BlockSpec((1, 1, D), lambda b, j: (jnp.where(j == LAT_BLKS, nbatch, b), 0, 5)),
                 _lspec(g, l)]
    args += [mod3, g]
    return pl.pallas_call(
        functools.partial(_combine_kernel, has_ctx=has_ctx),
        grid=(nbatch, nb),
        in_specs=in_specs,
        out_specs=pl.BlockSpec((1, ROW_BLK, D), lambda b, j: (b, j, 0)),
        out_shape=jax.ShapeDtypeStruct((nbatch, ntok, D), F32),
        scratch_shapes=[pltpu.VMEM((2, ROW_BLK, D), BF16)],
        compiler_params=_cp("arbitrary", "arbitrary"),
        name="combine",
    )(*args)


def _rope_tables():
    f = 1.0 / (ROPE_THETA ** (jnp.arange(0, 16, 2, dtype=F32) / 16))
    pos = jnp.arange(T)
    row, col = pos // GRID_W, pos % GRID_W
    ar = row.astype(F32)[:, None] * f[None, :]
    ac = col.astype(F32)[:, None] * f[None, :]
    cos32 = jnp.concatenate([jnp.cos(ar), jnp.cos(ar), jnp.cos(ac), jnp.cos(ac)], axis=-1)
    sin32 = jnp.concatenate([jnp.sin(ar), jnp.sin(ar), jnp.sin(ac), jnp.sin(ac)], axis=-1)
    cos32 = jnp.concatenate([cos32, jnp.ones((LC, 32), F32)], axis=0)
    sin32 = jnp.concatenate([sin32, jnp.zeros((LC, 32), F32)], axis=0)
    return cos32, sin32


def _rot_cols(w):
    a, b, c, d = w[..., 0:8], w[..., 8:16], w[..., 16:24], w[..., 24:32]
    return jnp.concatenate([-b, a, -d, c], axis=-1)


def _mla_weights(w_uq, w_ukv):
    nl, r, _ = w_uq.shape
    wq3 = w_uq.reshape(nl, r, 4, 96)
    z32 = jnp.zeros((nl, r, 4, 32), F32)
    wq = jnp.concatenate([wq3, z32], axis=-1).reshape(nl, r, 512)
    wqr = jnp.concatenate([jnp.zeros((nl, r, 4, 64), F32), _rot_cols(wq3[..., 64:96]), z32], axis=-1).reshape(nl, r, 512)
    rk = w_ukv.shape[1]
    wkv3 = w_ukv.reshape(nl, rk, 4, 128)
    wk = jnp.concatenate([wkv3[..., :64], jnp.zeros((nl, rk, 4, 64), F32)], axis=-1).reshape(nl, rk, 512)
    z64 = jnp.zeros((nl, rk, 64), F32)
    vh = [wkv3[:, :, h, 64:] for h in range(4)]
    wv = jnp.concatenate([vh[0], z64, z64, vh[1], vh[2], z64, z64, vh[3]], axis=-1)
    vone = np.tile(np.repeat(np.array([0.0, 1.0, 1.0, 0.0], np.float32), 64), 2)[None, :]
    e = np.zeros((32, 512), np.float32)
    for h in range(4):
        e[np.arange(32), 128 * h + 64 + np.arange(32)] = 1.0
    return (wq.astype(BF16), wqr.astype(BF16), wk.astype(BF16), wv.astype(BF16), jnp.asarray(e, BF16),
            jnp.asarray(vone))


def _inproj_weights(w):
    o = np.cumsum([0, 256, 128, 32, 768, 256, 256, 768, 8, 4096])
    w = w.astype(BF16)
    seg = lambda i: w[:, :, o[i]:o[i + 1]]
    nl = w.shape[0]
    kr = seg(2)
    wm = jnp.concatenate([seg(0), seg(1), kr, _rot_cols(kr), jnp.zeros((nl, D, 64), BF16)], axis=-1)
    w2 = jnp.concatenate([seg(5), seg(6), seg(7), jnp.zeros((nl, D, 120), BF16)], axis=-1)
    return [wm, seg(3), seg(4), w2, 0.5 * seg(8)]


def kernel(x, c, ctx, c_ctx, w_ada, b_ada, g_pre_mix, g_post_mix, g_pre_ffn, g_post_ffn, w_in, mla_g_cq, mla_g_ckv, mla_w_uq, mla_w_ukv, na_rpb, s5_a_re, s5_a_im, s5_log_step, s5_b_re, s5_b_im, s5_c_re, s5_c_im, s5_d, s5_w_glu, s5_b_glu, m2_conv_w, m2_conv_b, m2_a_log, m2_dt_bias, m2_d, m2_g_norm, w_branch, w_out, w_router, w_gate, w_up, w_down):
    nbatch = x.shape[0]
    depth = w_ada.shape[0]
    vec = lambda v: v.reshape(depth, 1, -1).astype(F32)

    stream = (x, ctx, 0)
    cvec = jnp.concatenate([c, c_ctx[None, :], jnp.zeros((7, D), F32)], axis=0)
    cvec = cvec[: ((nbatch + 1 + 7) // 8) * 8]
    b_ada3 = b_ada.reshape(depth, 1, 6 * D)

    cos32, sin32 = _rope_tables()
    qscale = (64 + 32) ** -0.5 * math.log2(math.e)
    lane_is_rope = np.tile(np.concatenate([np.zeros(64, bool), np.ones(32, bool), np.zeros(32, bool)]), 4)
    pick = lambda t32, fill: jnp.where(lane_is_rope[None, :], jnp.tile(jnp.pad(t32, ((0, 0), (64, 32))), (1, 4)), fill)
    cos_q = pick(cos32, 1.0) * qscale
    sin_q = pick(sin32, 0.0) * qscale

    g_pre_mix, g_post_mix, g_pre_ffn, g_post_ffn = vec(g_pre_mix), vec(g_post_mix), vec(g_pre_ffn), vec(g_post_ffn)
    w_in_segs = _inproj_weights(w_in)
    wq, wqr, wk, wv, e_mat, vone = _mla_weights(mla_w_uq, mla_w_ukv)
    mla_g_cq, mla_g_ckv = vec(mla_g_cq), vec(mla_g_ckv)
    na_tabs = _na_tables(na_rpb)
    s5_bmat, s5_avec, s5_cmat = _s5_params(s5_a_re, s5_a_im, s5_log_step, s5_b_re, s5_b_im, s5_c_re, s5_c_im, nbatch)
    s5_d, s5_b_glu, s5_w_glu = vec(s5_d), vec(s5_b_glu), s5_w_glu.astype(BF16)
    m2_conv_w, m2_conv_b = m2_conv_w.astype(F32), vec(m2_conv_b)
    dt_bias = jnp.pad(m2_dt_bias.reshape(depth, 1, 8).astype(F32), ((0, 0), (0, 0), (0, 120)))
    a_col = jnp.pad(-jnp.exp(m2_a_log.astype(F32)).reshape(depth, 8, 1), ((0, 0), (0, 120), (0, 0)))
    m2_dvec, m2_g_norm = vec(jnp.repeat(m2_d, 64, axis=-1)), vec(m2_g_norm)
    w_branch, w_out = (0.5 * w_branch).astype(BF16), w_out.astype(BF16)
    w_router_t = jnp.swapaxes(w_router, 1, 2).astype(F32)

    for l in range(depth):
        has_ctx = l < depth - 1
        nb = NBLK if has_ctx else LAT_BLKS
        mod = _ada(cvec, w_ada, b_ada3, l)
        mod3 = mod.reshape(mod.shape[0], 1, 6 * D)

        p_mla, p_na, p_s5, p_m2, p_gate = _inproj(stream, g_pre_mix, mod3, w_in_segs, nbatch, l, has_ctx)

        q, k, v = _mla_prep(p_mla, mla_g_cq, mla_g_ckv, wq, wqr, wk, wv, e_mat, vone, cos_q, sin_q, cos32, sin32, l)
        o_mla = _mla_attn(q.reshape(nbatch, N, 512), k.reshape(nbatch, N, 512), v.reshape(nbatch, N, 512), has_ctx)

        o_na = _na_attn(p_na.reshape(nbatch, N, NA_W), na_tabs, has_ctx, l)

        yf, yb = _s5_scan(p_s5.reshape(nbatch, N, S5_W), s5_bmat, s5_avec, s5_cmat, l)
        o_s5 = _s5_glu(p_s5, yf.reshape(nbatch * N, S5_W), yb.reshape(nbatch * N, S5_W), s5_d, s5_w_glu, s5_b_glu, l)

        xbc, dtt, bmt = _m2_prep(p_m2.reshape(nbatch, N, M2_W), m2_conv_w, m2_conv_b, dt_bias, l)
        ssd_f, ssd_b = _ssd(xbc, dtt, bmt, a_col, l)
        o_m2 = _m2_norm(xbc.reshape(nbatch * N, 768), p_m2, ssd_f.reshape(nbatch * N, 256),
                        ssd_b.reshape(nbatch * N, 256), m2_dvec, m2_g_norm, l)

        x1, h2, aff = _merge(stream, o_mla, o_na.reshape(-1, 256), o_s5, o_m2, p_gate,
                             w_branch, w_out, g_post_mix, mod3, g_pre_ffn, w_router_t, nbatch, nb, l)

        slot_lat = _topk(aff, 0, LAT_BLKS, CAP_LAT)
        slot_ctx = _topk(aff, LAT_BLKS, 1, CAP_CTX) if has_ctx else None
        cnt = jnp.sum((slot_lat >= 0).astype(jnp.int32), axis=-1)
        first = jnp.cumsum(cnt, axis=1) - cnt
        win0 = jnp.minimum((first // 16) * 16, CAP_LAT - MOE_WIN)
        fits = jnp.all(first + cnt <= win0 + MOE_WIN, axis=(1, 2)).astype(jnp.int32)
        win0 = win0.reshape(-1)
        h2 = h2.reshape(nbatch, nb * ROW_BLK, D)
        ys = _experts(win0, fits, _gather(win0, fits, h2, slot_lat), h2, slot_lat, slot_ctx, aff, w_gate, w_up, w_down,
                      l, has_ctx)
        scol_lat = jnp.transpose(slot_lat, (0, 1, 3, 2)).reshape(nbatch, T, N_EXPERTS)
        scol_ctx = jnp.transpose(slot_ctx, (0, 1, 3, 2)).reshape(nbatch, LC, N_EXPERTS) if has_ctx else None
        xs = _combine(win0, fits, x1.reshape(nbatch, nb * ROW_BLK, D), ys[0], scol_lat, ys[1] if has_ctx else None,
                      scol_ctx, mod3, g_post_ffn, has_ctx, l)
        stream = (xs, xs, LAT_BLKS)
    return xs
```
